```python
import jax, jax.numpy as jnp
from jax import lax
import numpy as np

D_MODEL = 1024
BATCH = 8
SEQ = 4096
DEPTH = 2

GRID_W = 64
CTX_LEN = 256
NORM_EPS = 1e-6

MLA_HEADS = 8
Q_LORA = 384
KV_LORA = 256
QK_NOPE = 64
QK_ROPE = 32
V_HEAD = 64
ROPE_THETA = 10000.0
MLA_SCALE = (QK_NOPE + QK_ROPE) ** -0.5
Q_BLOCK = 128
MLA_WIDTH = MLA_HEADS * V_HEAD
MLA_SPLITS = (Q_LORA, KV_LORA, QK_ROPE)
MLA_IN = Q_LORA + KV_LORA + QK_ROPE

POOL_WINDOWS = (2, 4, 8, 16)
POOL_GROUPS = 4
POOL_CH = 64
POOL_WIDTH = POOL_GROUPS * POOL_CH

RWKV_HEADS = 4
RWKV_HEAD = 64
RWKV_WIDTH = RWKV_HEADS * RWKV_HEAD
DECAY_LORA = 32
ICLR_LORA = 32
GATE_LORA = 64
RWKV_GN_EPS = 64e-5
RWKV_SPLITS = (RWKV_WIDTH, RWKV_WIDTH, RWKV_WIDTH, DECAY_LORA, DECAY_LORA, ICLR_LORA, ICLR_LORA, GATE_LORA)
RWKV_IN = 3 * RWKV_WIDTH + 2 * DECAY_LORA + 2 * ICLR_LORA + GATE_LORA

MIX_WIDTH = MLA_WIDTH + POOL_WIDTH + RWKV_WIDTH
IN_WIDTH = MLA_IN + POOL_WIDTH + RWKV_IN

N_EXPERTS = 64
TOP_K = 6
EXPERT_FF = 256
SHARED_FF = 256
N_GROUPS = 8
TOPK_GROUPS = 4
ROUTED_SCALE = 2.5
MOE_BLOCK = 128

kernel_name = 'hybrid_mla_pool_rwkv7_moe_dit'


def _split(z, sizes):
    out, o = [], 0
    for s in sizes:
        out.append(z[..., o:o + s])
        o += s
    return out


def rmsnorm(x, g):
    xf = x.astype(jnp.float32)
    y = xf * lax.rsqrt(jnp.mean(xf * xf, -1, keepdims=True) + NORM_EPS)
    return (y * g.astype(jnp.float32)).astype(x.dtype)


def modulate(h, shift, scale):
    return h * (1 + scale) + shift


def axial_rope_tables(n_tokens):
    rows = n_tokens // GRID_W
    row = jnp.repeat(jnp.arange(rows, dtype=jnp.float32), GRID_W)
    col = jnp.tile(jnp.arange(GRID_W, dtype=jnp.float32), rows)
    n_freq = QK_ROPE // 4
    inv = ROPE_THETA ** (-jnp.arange(n_freq, dtype=jnp.float32) / n_freq)
    ang = jnp.concatenate([row[:, None] * inv, col[:, None] * inv], -1)
    return jnp.cos(ang), jnp.sin(ang)


def apply_rope(x, cos, sin):
    half = x.shape[-1] // 2
    x1, x2 = x[..., :half], x[..., half:]
    return jnp.concatenate([x1 * cos - x2 * sin, x2 * cos + x1 * sin], -1).astype(x.dtype)


def mla_keys(z, p, cos, sin, rotate):
    B, L, _ = z.shape
    _, kv_a, k_pe = _split(z, MLA_SPLITS)
    kv = (rmsnorm(kv_a, p['mla_kv_norm']) @ p['mla_w_kv_b']).reshape(B, L, MLA_HEADS, QK_NOPE + V_HEAD)
    if rotate:
        k_pe = apply_rope(k_pe, cos[None], sin[None])
    return kv[..., :QK_NOPE], k_pe, kv[..., QK_NOPE:]


def mla_queries(z, p, cos, sin, rotate):
    B, L, _ = z.shape
    q = (rmsnorm(z[..., :Q_LORA], p['mla_q_norm']) @ p['mla_w_q_b']).reshape(B, L, MLA_HEADS, QK_NOPE + QK_ROPE)
    q_pe = q[..., QK_NOPE:]
    if rotate:
        q_pe = apply_rope(q_pe, cos[None, :, None, :], sin[None, :, None, :])
    return q[..., :QK_NOPE], q_pe


def mla_attend(q_nope, q_pe, k_nope, k_pe, v):
    s = jnp.einsum('bqhd,bkhd->bhqk', q_nope, k_nope) + jnp.einsum('bqhr,bkr->bhqk', q_pe, k_pe)
    pr = jax.nn.softmax(s.astype(jnp.float32) * MLA_SCALE, axis=-1).astype(v.dtype)
    return jnp.einsum('bhqk,bkhd->bqhd', pr, v)


def mla_mixer(z_lat, z_ctx, p, cos, sin, need_ctx):
    B, L, _ = z_lat.shape
    kn, kp, v = mla_keys(z_lat, p, cos, sin, True)
    ckn, ckp, cv = mla_keys(z_ctx, p, None, None, False)
    keys_nope = jnp.concatenate([kn, ckn], 1)
    keys_pe = jnp.concatenate([kp, ckp], 1)
    vals = jnp.concatenate([v, cv], 1)
    qn, qp = mla_queries(z_lat, p, cos, sin, True)
    n_blk = L // Q_BLOCK
    to_blocks = lambda t: jnp.moveaxis(t.reshape(B, n_blk, Q_BLOCK, *t.shape[2:]), 1, 0)
    out = lax.map(lambda qb: mla_attend(qb[0], qb[1], keys_nope, keys_pe, vals), (to_blocks(qn), to_blocks(qp)))
    y_lat = jnp.moveaxis(out, 0, 1).reshape(B, L, MLA_WIDTH)
    y_ctx = None
    if need_ctx:
        cqn, cqp = mla_queries(z_ctx, p, None, None, False)
        y_ctx = mla_attend(cqn, cqp, ckn, ckp, cv).reshape(B, z_ctx.shape[1], MLA_WIDTH)
    return y_lat, y_ctx


def pool_mixer(z, pool_w, pool_scale):
    B, L, _ = z.shape
    zf = z.astype(jnp.float32)
    csum = jnp.concatenate([jnp.zeros((B, 1, POOL_WIDTH), jnp.float32), jnp.cumsum(zf, axis=1)], axis=1)
    csum = csum.reshape(B, L + 1, POOL_GROUPS, POOL_CH)
    t = jnp.arange(L)[:, None]
    half = jnp.array(POOL_WINDOWS, jnp.int32)[None, :] // 2
    lo = jnp.clip(t - half, 0, L)
    hi = jnp.clip(t + half, 0, L)
    grp = jnp.arange(POOL_GROUPS)[None, :]
    total = csum[:, hi, grp] - csum[:, lo, grp]
    count = (hi - lo).astype(jnp.float32)[None, :, :, None]
    diff = total / count - zf.reshape(B, L, POOL_GROUPS, POOL_CH)
    y = jnp.einsum('blgc,gcd->blgd', diff.astype(z.dtype), pool_w)
    return y.reshape(B, L, POOL_WIDTH) * pool_scale


def token_shift(z, mu):
    zp = jnp.pad(z, ((0, 0), (1, 1), (0, 0)))
    return z + mu * (0.5 * (zp[:, :-2] + zp[:, 2:]) - z)


def rwkv_prepare(z, p):
    B, L, _ = z.shape
    z = token_shift(z, p['rwkv_mu'])
    r, k, v, w_f, w_b, a_f, a_b, g_in = _split(z, RWKV_SPLITS)
    g = jax.nn.sigmoid(g_in) @ p['rwkv_g2']
    heads = lambda t: t.astype(jnp.float32).reshape(B, L, RWKV_HEADS, RWKV_HEAD)
    kk = heads(k * p['rwkv_k_k'])
    kk = kk / jnp.maximum(jnp.sqrt(jnp.sum(kk * kk, -1, keepdims=True)), 1e-12)
    dirs = []
    for d, (w_lo, a_lo) in enumerate(((w_f, a_f), (w_b, a_b))):
        log_w = -jax.nn.softplus(-(p['rwkv_w0'][d] + jnp.tanh(w_lo) @ p['rwkv_w2'][d])) - 0.5
        decay = jnp.exp(-jnp.exp(log_w.astype(jnp.float32)))
        a = jax.nn.sigmoid(p['rwkv_a0'][d] + a_lo @ p['rwkv_a2'][d])
        k_d = k * (1 + (a - 1) * p['rwkv_k_a'])
        dirs.append((heads(decay), kk * heads(a), heads(k_d)))
    return heads(r), kk, heads(v), g, dirs


def wkv_scan(state0, r, kk, v, decay, b, k, reverse):
    def step(S, inp):
        r_t, kk_t, v_t, w_t, b_t, k_t = inp
        sa = jnp.einsum('bhvk,bhk->bhv', S, kk_t)
        S = S * w_t[:, :, None, :] - sa[..., None] * b_t[:, :, None, :] + v_t[..., None] * k_t[:, :, None, :]
        return S, jnp.einsum('bhvk,bhk->bhv', S, r_t)
    xs = tuple(jnp.moveaxis(t, 1, 0) for t in (r, kk, v, decay, b, k))
    S, ys = lax.scan(step, state0, xs, reverse=reverse)
    return S, jnp.moveaxis(ys, 0, 1)


def rwkv_readout(y, prep, p):
    r, _, v, g, dirs = prep
    B, L = y.shape[:2]
    mean = jnp.mean(y, -1, keepdims=True)
    var = jnp.mean(jnp.square(y - mean), -1, keepdims=True)
    ln_w = p['rwkv_ln_w'].astype(jnp.float32).reshape(RWKV_HEADS, RWKV_HEAD)
    ln_b = p['rwkv_ln_b'].astype(jnp.float32).reshape(RWKV_HEADS, RWKV_HEAD)
    r_k = p['rwkv_r_k'].astype(jnp.float32)
    yn = (y - mean) * lax.rsqrt(var + RWKV_GN_EPS) * ln_w + ln_b
    bonus = sum(jnp.sum(r * k_d * r_k, -1, keepdims=True) for _, _, k_d in dirs) * v
    return (yn + bonus).reshape(B, L, RWKV_WIDTH).astype(g.dtype) * g


def rwkv_mixer(z_lat, z_ctx, p, need_ctx):
    lat = rwkv_prepare(z_lat, p)
    cx = rwkv_prepare(z_ctx, p)
    s0 = jnp.zeros((z_ctx.shape[0], RWKV_HEADS, RWKV_HEAD, RWKV_HEAD), jnp.float32)
    y_lat, y_ctx = 0.0, 0.0
    for d in range(2):
        reverse = d == 1
        s_ctx, yc = wkv_scan(s0, cx[0], cx[1], cx[2], *cx[4][d], reverse)
        _, yl = wkv_scan(s_ctx, lat[0], lat[1], lat[2], *lat[4][d], reverse)
        y_lat = y_lat + yl
        y_ctx = y_ctx + yc
    out_ctx = rwkv_readout(y_ctx, cx, p) if need_ctx else None
    return rwkv_readout(y_lat, lat, p), out_ctx


def swiglu(h, w_gate, w_up, w_down):
    return (jax.nn.silu(h @ w_gate) * (h @ w_up)) @ w_down


def moe_ffn(h, p):
    T, D = h.shape
    scores = jax.nn.sigmoid((h @ p['router_w']).astype(jnp.float32))
    sel = scores + p['router_bias'].astype(jnp.float32)
    per_group = N_EXPERTS // N_GROUPS
    group_score = lax.top_k(sel.reshape(T, N_GROUPS, per_group), 2)[0].sum(-1)
    _, group_idx = lax.top_k(group_score, TOPK_GROUPS)
    group_mask = jax.nn.one_hot(group_idx, N_GROUPS, dtype=jnp.float32).sum(1)
    expert_mask = jnp.repeat(group_mask, per_group, axis=1) > 0
    _, expert_idx = lax.top_k(jnp.where(expert_mask, sel, -jnp.inf), TOP_K)
    gate = jnp.take_along_axis(scores, expert_idx, axis=1)
    gate = gate / jnp.sum(gate, -1, keepdims=True) * ROUTED_SCALE

    n_assign = T * TOP_K
    n_slots = -(-(n_assign + N_EXPERTS * (MOE_BLOCK - 1)) // MOE_BLOCK) * MOE_BLOCK
    n_blocks = n_slots // MOE_BLOCK
    flat_e = expert_idx.reshape(-1)
    flat_tok = jnp.repeat(jnp.arange(T, dtype=jnp.int32), TOP_K)
    flat_gate = gate.reshape(-1)
    order = jnp.argsort(flat_e)
    sorted_e = flat_e[order]
    counts = jnp.bincount(flat_e, length=N_EXPERTS)
    starts = jnp.cumsum(counts) - counts
    padded = (counts + MOE_BLOCK - 1) // MOE_BLOCK * MOE_BLOCK
    padded_ends = jnp.cumsum(padded)
    padded_starts = padded_ends - padded
    dest = padded_starts[sorted_e] + jnp.arange(n_assign, dtype=jnp.int32) - starts[sorted_e]
    slot_tok = jnp.full((n_slots,), T, jnp.int32).at[dest].set(flat_tok[order])
    slot_gate = jnp.zeros((n_slots,), jnp.float32).at[dest].set(flat_gate[order])
    block_expert = jnp.minimum(jnp.searchsorted(padded_ends, jnp.arange(n_blocks, dtype=jnp.int32) * MOE_BLOCK, side='right'), N_EXPERTS - 1)
    h_pad = jnp.concatenate([h, jnp.zeros((1, D), h.dtype)], 0)

    def expert_block(acc, blk):
        tok, g, e = blk
        yb = swiglu(h_pad[tok], p['exp_w_gate'][e], p['exp_w_up'][e], p['exp_w_down'][e])
        return acc.at[tok].add(yb.astype(jnp.float32) * g[:, None]), None

    acc, _ = lax.scan(expert_block, jnp.zeros((T + 1, D), jnp.float32),
                      (slot_tok.reshape(n_blocks, MOE_BLOCK), slot_gate.reshape(n_blocks, MOE_BLOCK), block_expert))
    shared = swiglu(h, p['sh_w_gate'], p['sh_w_up'], p['sh_w_down'])
    return (acc[:T] + shared.astype(jnp.float32)).astype(h.dtype)


def trunk_layer(x, ctx, mod, mod_ctx, p, cos, sin, is_last):
    B, L, D = x.shape
    need_ctx = not is_last
    sh1, sc1, gt1, sh2, sc2, gt2 = jnp.split(mod[:, None, :], 6, axis=-1)
    csh1, csc1, cgt1, csh2, csc2, cgt2 = jnp.split(mod_ctx[None, None, :], 6, axis=-1)

    z_lat = modulate(rmsnorm(x, p['pre_mix_g']), sh1, sc1) @ p['w_in']
    z_ctx = modulate(rmsnorm(ctx, p['pre_mix_g']), csh1, csc1) @ p['w_in']
    m_lat, pl_lat, r_lat = _split(z_lat, (MLA_IN, POOL_WIDTH, RWKV_IN))
    m_ctx, pl_ctx, r_ctx = _split(z_ctx, (MLA_IN, POOL_WIDTH, RWKV_IN))
    a_lat, a_ctx = mla_mixer(m_lat, m_ctx, p, cos, sin, need_ctx)
    w_lat, w_ctx = rwkv_mixer(r_lat, r_ctx, p, need_ctx)
    y = jnp.concatenate([a_lat, pool_mixer(pl_lat, p['pool_w'], p['pool_scale']), w_lat], -1) @ p['w_out']
    x = x + gt1 * rmsnorm(y, p['post_mix_g'])

    if is_last:
        h = modulate(rmsnorm(x, p['pre_ffn_g']), sh2, sc2)
        f = moe_ffn(h.reshape(B * L, D), p).reshape(B, L, D)
        return x + gt2 * rmsnorm(f, p['post_ffn_g']), None

    yc = jnp.concatenate([a_ctx, pool_mixer(pl_ctx, p['pool_w'], p['pool_scale']), w_ctx], -1) @ p['w_out']
    ctx = ctx + cgt1 * rmsnorm(yc, p['post_mix_g'])

    h = modulate(rmsnorm(x, p['pre_ffn_g']), sh2, sc2)
    hc = modulate(rmsnorm(ctx, p['pre_ffn_g']), csh2, csc2)
    f = moe_ffn(jnp.concatenate([h.reshape(B * L, D), hc.reshape(-1, D)], 0), p)
    x = x + gt2 * rmsnorm(f[:B * L].reshape(B, L, D), p['post_ffn_g'])
    ctx = ctx + cgt2 * rmsnorm(f[B * L:].reshape(ctx.shape), p['post_ffn_g'])
    return x, ctx


def setup_inputs(seed: int = 0) -> dict:
    key = jax.random.key(seed)
    ks = iter(jax.random.split(key, 48))
    nrm = lambda shape, scale: jax.random.normal(next(ks), shape, jnp.float32) * scale
    D = D_MODEL
    return {
        'x': nrm((BATCH, SEQ, D), 1.0),
        'c': nrm((BATCH, D), 1.0),
        'ctx': nrm((BATCH, CTX_LEN, D), 1.0),
        'c_ctx': nrm((D,), 1.0),
        'ada_w': nrm((DEPTH, D, 6 * D), 0.5 * D ** -0.5),
        'ada_b': nrm((DEPTH, 6 * D), 0.02),
        'pre_mix_g': 1.0 + nrm((DEPTH, D), 0.05),
        'post_mix_g': 1.0 + nrm((DEPTH, D), 0.05),
        'pre_ffn_g': 1.0 + nrm((DEPTH, D), 0.05),
        'post_ffn_g': 1.0 + nrm((DEPTH, D), 0.05),
        'w_in': nrm((DEPTH, D, IN_WIDTH), D ** -0.5),
        'w_out': nrm((DEPTH, MIX_WIDTH, D), MIX_WIDTH ** -0.5),
        'mla_q_norm': 1.0 + nrm((DEPTH, Q_LORA), 0.05),
        'mla_w_q_b': nrm((DEPTH, Q_LORA, MLA_HEADS * (QK_NOPE + QK_ROPE)), Q_LORA ** -0.5),
        'mla_kv_norm': 1.0 + nrm((DEPTH, KV_LORA), 0.05),
        'mla_w_kv_b': nrm((DEPTH, KV_LORA, MLA_HEADS * (QK_NOPE + V_HEAD)), KV_LORA ** -0.5),
        'pool_w': nrm((DEPTH, POOL_GROUPS, POOL_CH, POOL_CH), POOL_CH ** -0.5),
        'pool_scale': 1.0 + nrm((DEPTH, POOL_WIDTH), 0.1),
        'rwkv_mu': jax.random.uniform(next(ks), (DEPTH, RWKV_IN), jnp.float32),
        'rwkv_w0': -2.0 + nrm((DEPTH, 2, RWKV_WIDTH), 1.0),
        'rwkv_w2': nrm((DEPTH, 2, DECAY_LORA, RWKV_WIDTH), 0.5 * DECAY_LORA ** -0.5),
        'rwkv_a0': nrm((DEPTH, 2, RWKV_WIDTH), 0.5),
        'rwkv_a2': nrm((DEPTH, 2, ICLR_LORA, RWKV_WIDTH), ICLR_LORA ** -0.5),
        'rwkv_g2': nrm((DEPTH, GATE_LORA, RWKV_WIDTH), GATE_LORA ** -0.5),
        'rwkv_k_k': 0.85 + nrm((DEPTH, RWKV_WIDTH), 0.05),
        'rwkv_k_a': 1.0 + nrm((DEPTH, RWKV_WIDTH), 0.05),
        'rwkv_r_k': nrm((DEPTH, RWKV_HEADS, RWKV_HEAD), 0.1),
        'rwkv_ln_w': 1.0 + nrm((DEPTH, RWKV_WIDTH), 0.05),
        'rwkv_ln_b': nrm((DEPTH, RWKV_WIDTH), 0.02),
        'router_w': nrm((DEPTH, D, N_EXPERTS), D ** -0.5),
        'router_bias': nrm((DEPTH, N_EXPERTS), 0.01),
        'exp_w_gate': nrm((DEPTH, N_EXPERTS, D, EXPERT_FF), D ** -0.5),
        'exp_w_up': nrm((DEPTH, N_EXPERTS, D, EXPERT_FF), D ** -0.5),
        'exp_w_down': nrm((DEPTH, N_EXPERTS, EXPERT_FF, D), EXPERT_FF ** -0.5),
        'sh_w_gate': nrm((DEPTH, D, SHARED_FF), D ** -0.5),
        'sh_w_up': nrm((DEPTH, D, SHARED_FF), D ** -0.5),
        'sh_w_down': nrm((DEPTH, SHARED_FF, D), SHARED_FF ** -0.5),
    }


def reference(x, c, ctx, c_ctx, ada_w, ada_b, pre_mix_g, post_mix_g, pre_ffn_g, post_ffn_g, w_in, w_out,
              mla_q_norm, mla_w_q_b, mla_kv_norm, mla_w_kv_b, pool_w, pool_scale,
              rwkv_mu, rwkv_w0, rwkv_w2, rwkv_a0, rwkv_a2, rwkv_g2, rwkv_k_k, rwkv_k_a, rwkv_r_k, rwkv_ln_w, rwkv_ln_b,
              router_w, router_bias, exp_w_gate, exp_w_up, exp_w_down, sh_w_gate, sh_w_up, sh_w_down):
    cos, sin = axial_rope_tables(x.shape[1])
    silu_c = jax.nn.silu(c)
    silu_cc = jax.nn.silu(c_ctx)
    for i in range(DEPTH):
        p = {
            'pre_mix_g': pre_mix_g[i], 'post_mix_g': post_mix_g[i], 'pre_ffn_g': pre_ffn_g[i], 'post_ffn_g': post_ffn_g[i],
            'w_in': w_in[i], 'w_out': w_out[i],
            'mla_q_norm': mla_q_norm[i], 'mla_w_q_b': mla_w_q_b[i], 'mla_kv_norm': mla_kv_norm[i], 'mla_w_kv_b': mla_w_kv_b[i],
            'pool_w': pool_w[i], 'pool_scale': pool_scale[i],
            'rwkv_mu': rwkv_mu[i], 'rwkv_w0': rwkv_w0[i], 'rwkv_w2': rwkv_w2[i], 'rwkv_a0': rwkv_a0[i], 'rwkv_a2': rwkv_a2[i],
            'rwkv_g2': rwkv_g2[i], 'rwkv_k_k': rwkv_k_k[i], 'rwkv_k_a': rwkv_k_a[i], 'rwkv_r_k': rwkv_r_k[i],
            'rwkv_ln_w': rwkv_ln_w[i], 'rwkv_ln_b': rwkv_ln_b[i],
            'router_w': router_w[i], 'router_bias': router_bias[i],
            'exp_w_gate': exp_w_gate[i], 'exp_w_up': exp_w_up[i], 'exp_w_down': exp_w_down[i],
            'sh_w_gate': sh_w_gate[i], 'sh_w_up': sh_w_up[i], 'sh_w_down': sh_w_down[i],
        }
        mod = silu_c @ ada_w[i] + ada_b[i]
        mod_ctx = silu_cc @ ada_w[i] + ada_b[i]
        x, ctx = trunk_layer(x, ctx, mod, mod_ctx, p, cos, sin, i == DEPTH - 1)
    return x
```

```python
import functools

import numpy as np
import jax
import jax.numpy as jnp
from jax import lax
from jax.experimental import pallas as pl
from jax.experimental.pallas import tpu as pltpu

f32 = jnp.float32
bf16 = jnp.bfloat16
HIGHEST = lax.Precision.HIGHEST

DEPTH = 2
GRID_W = 64
NORM_EPS = 1e-6
MLA_HEADS = 8
Q_LORA = 384
KV_LORA = 256
QK_NOPE = 64
QK_ROPE = 32
V_HEAD = 64
ROPE_THETA = 10000.0
MLA_SCALE = (QK_NOPE + QK_ROPE) ** -0.5
MLA_IN = Q_LORA + KV_LORA + QK_ROPE
HEAD_PAD = 128
POOL_WINDOWS = (2, 4, 8, 16)
POOL_WIDTH = 256
POOL_HALO = 8
RWKV_HEADS = 4
RWKV_HEAD = 64
RWKV_WIDTH = 256
RWKV_IN = 960
RWKV_PAD = 1024
RWKV_GN_EPS = 64e-5
CHUNK = 64
N_EXPERTS = 64
TOP_K = 6
N_GROUPS = 8
TOPK_GROUPS = 4
ROUTED_SCALE = 2.5
EXPERT_FF = 256
TM = 256
MOE_BM = 256
Z_COLS = 2048
VMEM_LIMIT = 48 * 1024 * 1024
LOG2E = 1.4426950408889634
EXP_M05 = 0.6065306597126334


def _cparams(sem, vmem=VMEM_LIMIT):
    return pltpu.CompilerParams(dimension_semantics=sem, vmem_limit_bytes=vmem)


def _mm(a, b):
    return jnp.dot(a.astype(bf16), b.astype(bf16), preferred_element_type=f32)


def _mm_nt(a, b):
    return lax.dot_general(a.astype(bf16), b.astype(bf16), (((1,), (1,)), ((), ())), preferred_element_type=f32)


def _mm_tn(a, b):
    return lax.dot_general(a.astype(bf16), b.astype(bf16), (((0,), (0,)), ((), ())), preferred_element_type=f32)


def _mmf(a, b):
    return jnp.dot(a, b, precision=HIGHEST, preferred_element_type=f32)


def _rms(x):
    return x * lax.rsqrt(jnp.mean(x * x, axis=-1, keepdims=True) + NORM_EPS)


def _sigmoid(x):
    return 1.0 / (1.0 + jnp.exp(-x))


def _ada_kernel(c_ref, w_ref, b_ref, o_ref):
    c = c_ref[...]
    s = c * _sigmoid(c)
    o_ref[0] = _mm(s, w_ref[0]) + b_ref[0]


def _ada_mod(c_all, ada_w, ada_b):
    depth, d, n = ada_w.shape
    tn = 1024
    return pl.pallas_call(
        _ada_kernel,
        out_shape=jax.ShapeDtypeStruct((depth, 16, n), f32),
        grid=(depth, n // tn),
        in_specs=[pl.BlockSpec((16, d), lambda i, j: (0, 0)),
                  pl.BlockSpec((1, d, tn), lambda i, j: (i, 0, j)),
                  pl.BlockSpec((1, 1, tn), lambda i, j: (i, 0, j))],
        out_specs=pl.BlockSpec((1, 16, tn), lambda i, j: (i, 0, j)),
        compiler_params=_cparams(("arbitrary", "arbitrary")),
        name="ada_mod",
    )(c_all, ada_w, ada_b.reshape(depth, 1, n))


def _inproj_kernel(x_ref, m_ref, rope_ref, g_ref, win_ref, qg_ref, wq_ref, wqs_ref, kvg_ref, wk_ref, wv_ref,
                   e_ref, es_ref, q_ref, k_ref, v_ref, zp_ref, zr_ref):
    m = m_ref[0]
    h = _rms(x_ref[...]) * g_ref[...]
    h = h * (1.0 + m[1:2]) + m[0:1]
    z = _mm(h, win_ref[...])
    zp_ref[...] = z[:, 768:1024]
    zr_ref[...] = z[:, 1024:2048]
    tile8 = lambda t: jnp.concatenate([t] * MLA_HEADS, axis=1)
    qn = (_rms(z[:, 0:Q_LORA]) * qg_ref[...]).astype(bf16)
    q = _mm(qn, wq_ref[...]) * tile8(rope_ref[0]) + _mm(qn, wqs_ref[...]) * tile8(rope_ref[1])
    q_ref[...] = q.astype(bf16)
    kvn = (_rms(z[:, Q_LORA:Q_LORA + KV_LORA]) * kvg_ref[...]).astype(bf16)
    kpe = z[:, 640:768].astype(bf16)
    k = _mm(kvn, wk_ref[...]) + _mm(kpe, e_ref[...]) * tile8(rope_ref[2]) + _mm(kpe, es_ref[...]) * tile8(rope_ref[3])
    k_ref[...] = k.astype(bf16)
    v_ref[...] = _mm(kvn, wv_ref[...]).astype(bf16)


def _inproj(xa, mt, rope, rope_idx, p, nt):
    t_all, d = xa.shape
    full = lambda a: pl.BlockSpec(a.shape, lambda i: (0,) * a.ndim)
    ws = [p["pre_mix_g"], p["w_in_p"], p["mla_q_norm"], p["wq_p"], p["wq_s"], p["mla_kv_norm"], p["wk_p"], p["wv_p"],
          p["e_p"], p["e_s"]]
    return pl.pallas_call(
        _inproj_kernel,
        out_shape=(jax.ShapeDtypeStruct((t_all, MLA_HEADS * HEAD_PAD), bf16),
                   jax.ShapeDtypeStruct((t_all, MLA_HEADS * HEAD_PAD), bf16),
                   jax.ShapeDtypeStruct((t_all, MLA_HEADS * V_HEAD), bf16),
                   jax.ShapeDtypeStruct((t_all, POOL_WIDTH), f32),
                   jax.ShapeDtypeStruct((t_all, RWKV_PAD), f32)),
        grid=(nt,),
        in_specs=[pl.BlockSpec((TM, d), lambda i: (i, 0)),
                  pl.BlockSpec((1, 8, d), lambda i: (i, 0, 0)),
                  pl.BlockSpec((4, TM, HEAD_PAD), lambda i: (0, rope_idx(i), 0))] + [full(w) for w in ws],
        out_specs=(pl.BlockSpec((TM, 1024), lambda i: (i, 0)),
                   pl.BlockSpec((TM, 1024), lambda i: (i, 0)),
                   pl.BlockSpec((TM, 512), lambda i: (i, 0)),
                   pl.BlockSpec((TM, POOL_WIDTH), lambda i: (i, 0)),
                   pl.BlockSpec((TM, RWKV_PAD), lambda i: (i, 0))),
        compiler_params=_cparams(("arbitrary",)),
        name="in_proj",
    )(xa, mt, rope, *ws)


def _attn_kernel(*refs, seg_tiles):
    q_ref, o_ref = refs[0], refs[-1]
    kv = refs[1:-1]
    tq = q_ref.shape[0]
    outs = []
    for h in range(MLA_HEADS):
        qh = q_ref[:, h * HEAD_PAD:(h + 1) * HEAD_PAD]
        carry = (jnp.full((tq, 1), -jnp.inf, f32), jnp.zeros((tq, 1), f32), jnp.zeros((tq, V_HEAD), f32))
        for si, (nk, tk) in enumerate(seg_tiles):
            k_ref, v_ref = kv[2 * si], kv[2 * si + 1]

            def body(j, c, k_ref=k_ref, v_ref=v_ref, tk=tk, qh=qh, h=h):
                m, l, acc = c
                r0 = pl.multiple_of(j * tk, tk)
                kh = k_ref[pl.ds(r0, tk), h * HEAD_PAD:(h + 1) * HEAD_PAD]
                vh = v_ref[pl.ds(r0, tk), h * V_HEAD:(h + 1) * V_HEAD]
                s = lax.dot_general(qh, kh, (((1,), (1,)), ((), ())), preferred_element_type=f32)
                m_new = jnp.maximum(m, jnp.max(s, axis=-1, keepdims=True))
                pr = jnp.exp2(s - m_new)
                alpha = jnp.exp2(m - m_new)
                l = alpha * l + jnp.sum(pr, axis=-1, keepdims=True)
                acc = alpha * acc + jnp.dot(pr.astype(bf16), vh, preferred_element_type=f32)
                return m_new, l, acc

            carry = body(0, carry) if nk == 1 else lax.fori_loop(0, nk, body, carry)
        outs.append(carry[2] / carry[1])
    for pr2 in range(MLA_HEADS // 2):
        o_ref[:, pr2 * 128:(pr2 + 1) * 128] = jnp.concatenate([outs[2 * pr2], outs[2 * pr2 + 1]], axis=1).astype(bf16)


def _attention(q, k, v, n_q_tiles, q_tile0, segs):
    in_specs = [pl.BlockSpec((TM, 1024), lambda i: (i + q_tile0, 0))]
    args = [q]
    seg_tiles = []
    for rows, tk, bidx in segs:
        in_specs.append(pl.BlockSpec((rows, 1024), lambda i, bidx=bidx: (bidx(i), 0)))
        in_specs.append(pl.BlockSpec((rows, 512), lambda i, bidx=bidx: (bidx(i), 0)))
        args += [k, v]
        seg_tiles.append((rows // tk, tk))
    return pl.pallas_call(
        functools.partial(_attn_kernel, seg_tiles=tuple(seg_tiles)),
        out_shape=jax.ShapeDtypeStruct((n_q_tiles * TM, MLA_HEADS * V_HEAD), bf16),
        grid=(n_q_tiles,),
        in_specs=in_specs,
        out_specs=pl.BlockSpec((TM, 512), lambda i: (i, 0)),
        compiler_params=_cparams(("arbitrary",)),
        name="mla_attention",
    )(*args)


def _pool_kernel(z_ref, zp_ref, zn_ref, band_ref, cnt_ref, pw_ref, ps_ref, o_ref, *, nlat, lt, ct):
    i = pl.program_id(0)
    is_lat = i < nlat
    j = jnp.where(is_lat, i % lt, (i - nlat) % ct)
    n = jnp.where(is_lat, lt, ct)
    z = z_ref[...]
    prev = zp_ref[...] * jnp.where(j == 0, 0.0, 1.0)
    nxt = zn_ref[...] * jnp.where(j == n - 1, 0.0, 1.0)
    zh = jnp.concatenate([prev, z, nxt], axis=0)
    lane_grp = lax.broadcasted_iota(jnp.int32, (1, POOL_WIDTH), 1) // 64
    tot = jnp.zeros_like(z)
    for g in range(len(POOL_WINDOWS)):
        tot = tot + _mmf(band_ref[g], zh * jnp.where(lane_grp == g, 1.0, 0.0))
    diff = tot / cnt_ref[...] - z
    o_ref[...] = (_mm(diff, pw_ref[...]) * ps_ref[...]).astype(bf16)


def _pool(zp, band, cnt, cnt_idx, pw_bd, pscale, nt, nlat, lt, ct):
    t_all = zp.shape[0]
    nb8 = t_all // POOL_HALO
    r = TM // POOL_HALO
    return pl.pallas_call(
        functools.partial(_pool_kernel, nlat=nlat, lt=lt, ct=ct),
        out_shape=jax.ShapeDtypeStruct((nt * TM, POOL_WIDTH), bf16),
        grid=(nt,),
        in_specs=[pl.BlockSpec((TM, POOL_WIDTH), lambda i: (i, 0)),
                  pl.BlockSpec((POOL_HALO, POOL_WIDTH), lambda i: (jnp.maximum(i * r - 1, 0), 0)),
                  pl.BlockSpec((POOL_HALO, POOL_WIDTH), lambda i: (jnp.minimum((i + 1) * r, nb8 - 1), 0)),
                  pl.BlockSpec(band.shape, lambda i: (0, 0, 0)),
                  pl.BlockSpec((TM, POOL_WIDTH), lambda i: (cnt_idx(i), 0)),
                  pl.BlockSpec((POOL_WIDTH, POOL_WIDTH), lambda i: (0, 0)),
                  pl.BlockSpec((1, POOL_WIDTH), lambda i: (0, 0))],
        out_specs=pl.BlockSpec((TM, POOL_WIDTH), lambda i: (i, 0)),
        compiler_params=_cparams(("arbitrary",)),
        name="pool_mixer",
    )(zp, zp, zp, band, cnt, pw_bd, pscale)


def _rwkv_blk(d, b, s, *, ncc, nlc, cbase):
    in_ctx = s < ncc
    jc = jnp.where(d == 0, s, ncc - 1 - s)
    jl = jnp.where(d == 0, s - ncc, nlc - 1 - (s - ncc))
    blk = jnp.where(in_ctx, cbase + b * ncc + jc, b * nlc + jl)
    first = jnp.where(in_ctx, jc == 0, jl == 0)
    last = jnp.where(in_ctx, jc == ncc - 1, jl == nlc - 1)
    return blk, first, last


def _tri_inverse(lm, tri_ref, eye):
    n = -(lm * tri_ref[0, 2])
    n2 = _mm(n, n)
    n4 = _mm(n2, n2)
    t = _mm(_mm(eye + n, eye + n2), eye + n4)
    for lvl in range(3):
        t = t - _mm(_mm(t, lm * tri_ref[0, 3 + lvl]), t)
    return t


def _rwkv_kernel(z_ref, zp_ref, zn_ref, bd_ref, eye_ref, tri_ref, csi_ref, mu_ref, w0_ref, w2_ref, a0_ref, a2_ref,
                 g2_ref, kk_ref, ka_ref, rk_ref, y_ref, bv_ref, g_ref, s_ref, *, ncc, nlc, cbase):
    d, b, s = pl.program_id(0), pl.program_id(1), pl.program_id(2)
    _, first, last = _rwkv_blk(d, b, s, ncc=ncc, nlc=nlc, cbase=cbase)

    @pl.when(s == 0)
    def _():
        s_ref[...] = jnp.zeros_like(s_ref)

    c = CHUNK
    z = z_ref[...]
    row = lax.broadcasted_iota(jnp.int32, (c, 1), 0)
    prev_row = zp_ref[7:8, :] * jnp.where(first, 0.0, 1.0)
    next_row = zn_ref[0:1, :] * jnp.where(last, 0.0, 1.0)
    zp = jnp.where(row == 0, prev_row, pltpu.roll(z, 1, 0))
    zn = jnp.where(row == c - 1, next_row, pltpu.roll(z, c - 1, 0))
    zs = z + mu_ref[...] * (0.5 * (zp + zn) - z)
    r, k, v = zs[:, 0:256], zs[:, 256:512], zs[:, 512:768]
    lora, gate_in = zs[:, 768:896], zs[:, 896:1024]
    bd = bd_ref[...]
    eye = eye_ref[...]

    g_ref[0] = _mmf(_sigmoid(gate_in), g2_ref[...])
    e = EXP_M05 * _sigmoid(w0_ref[0] + _mmf(jnp.tanh(lora), w2_ref[0]))
    a = _sigmoid(a0_ref[0] + _mmf(lora, a2_ref[0]))
    kd = k * (1.0 + (a - 1.0) * ka_ref[...])
    kkr = k * kk_ref[...]
    kk = kkr / jnp.maximum(jnp.sqrt(_mmf(kkr * kkr, bd)), 1e-12)
    bv_ref[0] = _mmf(r * kd * rk_ref[...], bd) * v
    bb = kk * a

    cs = _mmf(csi_ref[0], e)
    tot = _mmf(jnp.ones((c, c), f32), e)
    a_s = jnp.exp(e - cs) * kk
    b_s = bb * jnp.exp(cs)
    k_s = kd * jnp.exp(cs)
    r_s = r * jnp.exp(-cs)
    b_e = bb * jnp.exp(cs - tot)
    k_e = kd * jnp.exp(cs - tot)
    g_end = jnp.exp(-tot[0:1, :])

    rep4 = lambda t: jnp.concatenate([t] * RWKV_HEADS, axis=0)
    fold4 = lambda t: t[0:c] + t[c:2 * c] + t[2 * c:3 * c] + t[3 * c:4 * c]
    a4, r4, v4 = rep4(a_s) * bd, rep4(r_s) * bd, rep4(v) * bd
    b4, k4 = rep4(b_s), rep4(k_s)
    tri_s, tri_i = tri_ref[0, 0], tri_ref[0, 1]
    lm = _mm_nt(a4, b4) * tri_s
    akm = _mm_nt(a4, k4) * tri_s
    rbm = _mm_nt(r4, b4) * tri_i
    rkm = _mm_nt(r4, k4) * tri_i
    t = _tri_inverse(lm, tri_ref, eye)
    w4 = _mm(t, a4)
    u4 = _mm(t, _mm(akm, v4))
    q4 = r4 - _mm(rbm, w4)
    y4 = _mm(rkm, v4) - _mm(rbm, u4)
    w_all, u_all, q_all, y0 = fold4(w4), fold4(u4), fold4(q4), fold4(y4)
    g_bd = eye * g_end - bd * _mm_tn(w_all, b_e)
    h_bd = bd * (_mm_tn(v, k_e) - _mm_tn(u_all, b_e))
    st = s_ref[...]
    y_ref[0] = _mm_nt(q_all, st) + y0
    s_ref[...] = _mm(st, g_bd) + h_bd


def _rwkv(zr, consts, p, b_sz, ncc, nlc):
    t_all = zr.shape[0]
    cbase = b_sz * nlc
    nb8 = t_all // 8
    kw = dict(ncc=ncc, nlc=nlc, cbase=cbase)
    blk = lambda d, b, s: _rwkv_blk(d, b, s, **kw)[0]
    full = lambda a: pl.BlockSpec(a.shape, lambda d, b, s: (0,) * a.ndim)
    by_dir = lambda a: pl.BlockSpec((1,) + a.shape[1:], lambda d, b, s: (d,) + (0,) * (a.ndim - 1))
    out = jax.ShapeDtypeStruct((2, t_all, RWKV_WIDTH), f32)
    ospec = pl.BlockSpec((1, CHUNK, RWKV_WIDTH), lambda d, b, s: (d, blk(d, b, s), 0))
    return pl.pallas_call(
        functools.partial(_rwkv_kernel, **kw),
        out_shape=(out, out, out),
        grid=(2, b_sz, ncc + nlc),
        in_specs=[pl.BlockSpec((CHUNK, RWKV_PAD), lambda d, b, s: (blk(d, b, s), 0)),
                  pl.BlockSpec((8, RWKV_PAD), lambda d, b, s: (jnp.maximum(blk(d, b, s) * 8 - 1, 0), 0)),
                  pl.BlockSpec((8, RWKV_PAD), lambda d, b, s: (jnp.minimum(blk(d, b, s) * 8 + 8, nb8 - 1), 0)),
                  full(consts["bd"]), full(consts["eye"]), by_dir(consts["tri"]), by_dir(consts["csi"]),
                  full(p["mu_p"]), by_dir(p["w0"]), by_dir(p["w2_p"]), by_dir(p["a0"]), by_dir(p["a2_p"]),
                  full(p["g2_p"]), full(p["k_k"]), full(p["k_a"]), full(p["r_k"])],
        out_specs=(ospec, ospec, ospec),
        scratch_shapes=[pltpu.VMEM((RWKV_WIDTH, RWKV_WIDTH), f32)],
        compiler_params=_cparams(("arbitrary", "arbitrary", "arbitrary")),
        name="rwkv7_chunked",
    )(zr, zr, zr, consts["bd"], consts["eye"], consts["tri"], consts["csi"], p["mu_p"], p["w0"], p["w2_p"], p["a0"],
      p["a2_p"], p["g2_p"], p["k_k"], p["k_a"], p["r_k"])


def _pack_bf16_pairs(lo, hi):
    lo_b = pltpu.bitcast(lo.astype(bf16).astype(f32), jnp.uint32)
    hi_b = pltpu.bitcast(hi.astype(bf16).astype(f32), jnp.uint32)
    return (hi_b & jnp.uint32(0xFFFF0000)) | (lo_b >> 16)


def _unpack_bf16_pairs(w):
    lo = pltpu.bitcast(w << 16, f32).astype(bf16)
    hi = pltpu.bitcast(w & jnp.uint32(0xFFFF0000), f32).astype(bf16)
    return lo, hi


def _outproj_kernel(a_ref, py_ref, y_ref, bv_ref, g_ref, x_ref, m_ref, avg_ref, lnw_ref, lnb_ref, wo_ref, pmg_ref,
                    pfg_ref, x1_ref, hp_ref):
    m = m_ref[0]
    ysum = y_ref[0] + y_ref[1]
    avg = avg_ref[...]
    dev = ysum - _mmf(ysum, avg)
    var = _mmf(dev * dev, avg)
    yn = dev * lax.rsqrt(var + RWKV_GN_EPS) * lnw_ref[...] + lnb_ref[...]
    rw = (yn + bv_ref[0] + bv_ref[1]) * g_ref[0]
    o = (jnp.dot(a_ref[...], wo_ref[0:512, :], preferred_element_type=f32)
         + jnp.dot(py_ref[...], wo_ref[512:768, :], preferred_element_type=f32)
         + _mm(rw, wo_ref[768:1024, :]))
    x1 = x_ref[...] + m[2:3] * (_rms(o) * pmg_ref[...])
    x1_ref[...] = x1
    h = (_rms(x1) * pfg_ref[...]) * (1.0 + m[4:5]) + m[3:4]
    hp_ref[...] = _pack_bf16_pairs(h[:, 0:512], h[:, 512:1024])


def _outproj(att, py, y, bv, g, xa, mt, consts, p, nt):
    d = xa.shape[1]
    full = lambda a: pl.BlockSpec(a.shape, lambda i: (0,) * a.ndim)
    ws = [consts["avg"], p["ln_w"], p["ln_b"], p["w_out"], p["post_mix_g"], p["pre_ffn_g"]]
    return pl.pallas_call(
        _outproj_kernel,
        out_shape=(jax.ShapeDtypeStruct((nt * TM, d), f32), jax.ShapeDtypeStruct((nt * TM, d // 2), jnp.uint32)),
        grid=(nt,),
        in_specs=[pl.BlockSpec((TM, 512), lambda i: (i, 0)),
                  pl.BlockSpec((TM, POOL_WIDTH), lambda i: (i, 0)),
                  pl.BlockSpec((2, TM, RWKV_WIDTH), lambda i: (0, i, 0)),
                  pl.BlockSpec((2, TM, RWKV_WIDTH), lambda i: (0, i, 0)),
                  pl.BlockSpec((1, TM, RWKV_WIDTH), lambda i: (0, i, 0)),
                  pl.BlockSpec((TM, d), lambda i: (i, 0)),
                  pl.BlockSpec((1, 8, d), lambda i: (i, 0, 0))] + [full(w) for w in ws],
        out_specs=(pl.BlockSpec((TM, d), lambda i: (i, 0)), pl.BlockSpec((TM, d // 2), lambda i: (i, 0))),
        compiler_params=_cparams(("arbitrary",)),
        name="out_proj",
    )(att, py, y, bv, g, xa, mt, *ws)


def _router_kernel(hp_ref, rw_ref, rb_ref, ut_ref, lt_ref, ei_ref, pos_ref, gt_ref, cnt_ref, run_ref):
    i = pl.program_id(0)

    @pl.when(i == 0)
    def _():
        run_ref[...] = jnp.zeros_like(run_ref)

    tm = hp_ref.shape[0]
    ne, ng = N_EXPERTS, N_GROUPS
    pg = ne // ng
    lo, hi = _unpack_bf16_pairs(hp_ref[...])
    logits = (lax.dot_general(rw_ref[:, 0:512], lo, (((1,), (1,)), ((), ())), preferred_element_type=f32)
              + lax.dot_general(rw_ref[:, 512:1024], hi, (((1,), (1,)), ((), ())), preferred_element_type=f32))
    scores = _sigmoid(logits)
    sel = scores + rb_ref[...]
    neg = -jnp.inf

    s3 = sel.reshape(ng, pg, tm)
    io = lax.broadcasted_iota(jnp.int32, (ng, pg, tm), 1)
    m1 = jnp.max(s3, axis=1, keepdims=True)
    i1 = jnp.min(jnp.where(s3 == m1, io, pg), axis=1, keepdims=True)
    m2 = jnp.max(jnp.where(io == i1, neg, s3), axis=1, keepdims=True)
    gs = (m1 + m2).reshape(ng, tm)
    gi = lax.broadcasted_iota(jnp.int32, (ng, tm), 0)
    grank = jnp.zeros((ng, tm), f32)
    for j in range(ng):
        rj = gs[j:j + 1, :]
        grank = grank + jnp.where((rj > gs) | ((rj == gs) & (j < gi)), 1.0, 0.0)
    gsel = jnp.where(grank < TOPK_GROUPS, 1.0, 0.0)
    gsel3 = jnp.broadcast_to(gsel.reshape(ng, 1, tm), (ng, pg, tm)).reshape(ne, tm)
    msk = jnp.where(gsel3 > 0.5, sel, neg)
    ei = lax.broadcasted_iota(jnp.int32, (ne, tm), 0)
    erank = jnp.zeros((ne, tm), f32)
    for j in range(ne):
        rj = msk[j:j + 1, :]
        erank = erank + jnp.where((rj > msk) | ((rj == msk) & (j < ei)), 1.0, 0.0)
    chosen = erank < TOP_K
    chf = jnp.where(chosen, 1.0, 0.0)
    graw = jnp.where(chosen, scores, 0.0)
    gate = graw / jnp.sum(graw, axis=0, keepdims=True) * ROUTED_SCALE

    pos = run_ref[...] + _mm(chf, ut_ref[...])
    tot = jnp.sum(chf, axis=1, keepdims=True)
    run_new = run_ref[...] + tot
    run_ref[...] = run_new
    cnt_ref[...] = run_new[:, 0:128]
    rk = _mm(lt_ref[...], chf)
    eif = ei.astype(f32)
    rows_e, rows_p, rows_g = [], [], []
    for kq in range(TOP_K):
        mk = chosen & (rk == float(kq))
        rows_e.append(jnp.sum(jnp.where(mk, eif, 0.0), axis=0, keepdims=True))
        rows_p.append(jnp.sum(jnp.where(mk, pos, 0.0), axis=0, keepdims=True))
        rows_g.append(jnp.sum(jnp.where(mk, gate, 0.0), axis=0, keepdims=True))
    zrow = jnp.zeros((8 - TOP_K, tm), f32)
    ei_ref[...] = jnp.concatenate(rows_e + [zrow], axis=0).astype(jnp.int32)
    pos_ref[...] = jnp.concatenate(rows_p + [zrow], axis=0).astype(jnp.int32)
    gpad = jnp.concatenate(rows_g + [jnp.zeros((128 - TOP_K, tm), f32)], axis=0)
    gt_ref[...] = gpad.T


def _router(hp, rwt, rb, consts, nt):
    full = lambda a: pl.BlockSpec(a.shape, lambda i: (0,) * a.ndim)
    return pl.pallas_call(
        _router_kernel,
        out_shape=(jax.ShapeDtypeStruct((8, nt * TM), jnp.int32), jax.ShapeDtypeStruct((8, nt * TM), jnp.int32),
                   jax.ShapeDtypeStruct((nt * TM, 128), f32), jax.ShapeDtypeStruct((N_EXPERTS, 128), f32)),
        grid=(nt,),
        in_specs=[pl.BlockSpec((TM, 512), lambda i: (i, 0)), full(rwt), full(rb), full(consts["ut"]),
                  full(consts["lt"])],
        out_specs=(pl.BlockSpec((8, TM), lambda i: (0, i)), pl.BlockSpec((8, TM), lambda i: (0, i)),
                   pl.BlockSpec((TM, 128), lambda i: (i, 0)), pl.BlockSpec((N_EXPERTS, 128), lambda i: (0, 0))),
        scratch_shapes=[pltpu.VMEM((N_EXPERTS, TM), f32)],
        compiler_params=_cparams(("arbitrary",)),
        name="moe_router",
    )(hp, rwt, rb, consts["ut"], consts["lt"])


def _row_copy(src_ref, src_row, dst_ref, dst_row, sem):
    return pltpu.make_async_copy(src_ref.at[pl.ds(src_row, 1)], dst_ref.at[pl.ds(dst_row, 1)], sem)


def _dispatch_kernel(dest_ref, hp_ref, xz_ref, xs_ref, idx_ref, isem, rsem):
    del xz_ref
    i = pl.program_id(0)
    cp = pltpu.make_async_copy(dest_ref.at[i], idx_ref, isem)
    cp.start()
    cp.wait()
    base = i * TM

    def issue(t, c):
        for kq in range(TOP_K):
            _row_copy(hp_ref, base + t, xs_ref, idx_ref[kq * TM + t], rsem).start()
        return c

    def drain(t, c):
        for kq in range(TOP_K):
            _row_copy(hp_ref, 0, xs_ref, 0, rsem).wait()
        return c

    lax.fori_loop(0, TM, issue, 0)
    lax.fori_loop(0, TM, drain, 0)


def _dispatch(dest, hp, n_slots, nt):
    xz = jnp.zeros((n_slots, hp.shape[1]), jnp.uint32)
    anyspec = pl.BlockSpec(memory_space=pl.ANY)
    return pl.pallas_call(
        _dispatch_kernel,
        out_shape=jax.ShapeDtypeStruct(xz.shape, xz.dtype),
        grid=(nt,),
        in_specs=[anyspec, anyspec, anyspec],
        out_specs=anyspec,
        scratch_shapes=[pltpu.SMEM((8 * TM,), jnp.int32), pltpu.SemaphoreType.DMA, pltpu.SemaphoreType.DMA],
        input_output_aliases={2: 0},
        compiler_params=pltpu.CompilerParams(dimension_semantics=("arbitrary",), has_side_effects=True),
        name="moe_dispatch",
    )(dest, hp, xz)


def _expert_kernel(be_ref, nb_ref, xs_ref, wgu_ref, wd_ref, ys_ref):
    i = pl.program_id(0)

    @pl.when(i < nb_ref[0])
    def _():
        lo, hi = _unpack_bf16_pairs(xs_ref[...])
        gu = (jnp.dot(lo, wgu_ref[0, 0:512, :], preferred_element_type=f32)
              + jnp.dot(hi, wgu_ref[0, 512:1024, :], preferred_element_type=f32))
        gg, uu = gu[:, 0:EXPERT_FF], gu[:, EXPERT_FF:2 * EXPERT_FF]
        act = gg * _sigmoid(gg) * uu
        ys_ref[...] = _mm(act, wd_ref[0])

    @pl.when(i >= nb_ref[0])
    def _():
        ys_ref[...] = jnp.zeros_like(ys_ref)


def _experts(block_expert, nb_used, xs, wgu, wd):
    n_slots = xs.shape[0]
    d = wd.shape[2]
    return pl.pallas_call(
        _expert_kernel,
        out_shape=jax.ShapeDtypeStruct((n_slots, d), f32),
        grid_spec=pltpu.PrefetchScalarGridSpec(
            num_scalar_prefetch=2,
            grid=(n_slots // MOE_BM,),
            in_specs=[pl.BlockSpec((MOE_BM, xs.shape[1]), lambda i, be, nb: (i, 0)),
                      pl.BlockSpec((1,) + wgu.shape[1:], lambda i, be, nb: (be[i], 0, 0)),
                      pl.BlockSpec((1,) + wd.shape[1:], lambda i, be, nb: (be[i], 0, 0))],
            out_specs=pl.BlockSpec((MOE_BM, d), lambda i, be, nb: (i, 0))),
        compiler_params=_cparams(("arbitrary",)),
        name="moe_experts",
    )(block_expert, nb_used, xs, wgu, wd)


def _combine_kernel(dest_ref, ys_ref, hp_ref, gt_ref, x1_ref, m_ref, wsgu_ref, wsd_ref, pg_ref, o_ref,
                    idx_ref, buf_ref, isem, rsem):
    i = pl.program_id(0)
    cp = pltpu.make_async_copy(dest_ref.at[i], idx_ref, isem)
    cp.start()
    cp.wait()

    def issue(t, c):
        for kq in range(TOP_K):
            _row_copy(ys_ref, idx_ref[kq * TM + t], buf_ref.at[kq], t, rsem).start()
        return c

    def drain(t, c):
        for kq in range(TOP_K):
            _row_copy(ys_ref, 0, buf_ref.at[kq], 0, rsem).wait()
        return c

    lax.fori_loop(0, TM, issue, 0)
    lo, hi = _unpack_bf16_pairs(hp_ref[...])
    gu = (jnp.dot(lo, wsgu_ref[0:512, :], preferred_element_type=f32)
          + jnp.dot(hi, wsgu_ref[512:1024, :], preferred_element_type=f32))
    gg, uu = gu[:, 0:EXPERT_FF], gu[:, EXPERT_FF:2 * EXPERT_FF]
    f = _mm(gg * _sigmoid(gg) * uu, wsd_ref[...])
    lax.fori_loop(0, TM, drain, 0)
    gt = gt_ref[...]
    routed = jnp.zeros_like(f)
    for kq in range(TOP_K):
        routed = routed + buf_ref[kq] * gt[:, kq:kq + 1]
    f = routed + f
    m = m_ref[0]
    o_ref[...] = x1_ref[...] + m[5:6] * (_rms(f) * pg_ref[...])


def _combine(dest, ys, hp, gt, x1, mt, p, nt):
    d = x1.shape[1]
    anyspec = pl.BlockSpec(memory_space=pl.ANY)
    full = lambda a: pl.BlockSpec(a.shape, lambda i: (0,) * a.ndim)
    ws = [p["sh_wgu"], p["sh_wd"], p["post_ffn_g"]]
    return pl.pallas_call(
        _combine_kernel,
        out_shape=jax.ShapeDtypeStruct((nt * TM, d), f32),
        grid=(nt,),
        in_specs=[anyspec, anyspec,
                  pl.BlockSpec((TM, d // 2), lambda i: (i, 0)),
                  pl.BlockSpec((TM, 128), lambda i: (i, 0)),
                  pl.BlockSpec((TM, d), lambda i: (i, 0)),
                  pl.BlockSpec((1, 8, d), lambda i: (i, 0, 0))] + [full(w) for w in ws],
        out_specs=pl.BlockSpec((TM, d), lambda i: (i, 0)),
        scratch_shapes=[pltpu.SMEM((8 * TM,), jnp.int32), pltpu.VMEM((TOP_K, TM, d), f32),
                        pltpu.SemaphoreType.DMA, pltpu.SemaphoreType.DMA],
        compiler_params=_cparams(("arbitrary",)),
        name="moe_combine",
    )(dest, ys, hp, gt, x1, mt, *ws)


def _moe(hp, x1, mt, consts, p, nt):
    t = nt * TM
    ei, pos, gt, cnt = _router(hp, p["router_wt"], p["router_b"], consts, nt)
    n_assign = t * TOP_K
    n_blocks = -(-(n_assign + N_EXPERTS * (MOE_BM - 1)) // MOE_BM)
    n_slots = n_blocks * MOE_BM
    counts = cnt[:, 0].astype(jnp.int32)
    padded = (counts + MOE_BM - 1) // MOE_BM * MOE_BM
    pend = jnp.cumsum(padded)
    pstart = pend - padded
    nb_used = (pend[-1:] // MOE_BM).astype(jnp.int32)
    block_row0 = jnp.arange(n_blocks, dtype=jnp.int32) * MOE_BM
    block_expert = jnp.minimum(jnp.sum((pend[None, :] <= block_row0[:, None]).astype(jnp.int32), axis=1),
                               N_EXPERTS - 1)
    dest = pstart[ei] + pos
    dest = dest.reshape(8, nt, TM).transpose(1, 0, 2).reshape(nt, 8 * TM)
    xs = _dispatch(dest, hp, n_slots, nt)
    ys = _experts(block_expert, nb_used, xs, p["wgu"], p["wd"])
    return _combine(dest, ys, hp, gt, x1, mt, p, nt)


def _np_consts(l_lat, l_ctx):
    n = RWKV_WIDTH
    i = np.arange(n)
    bd = (i[:, None] // 64 == i[None, :] // 64).astype(np.float32)
    t_r, t_c = (i % 64)[:, None], (i % 64)[None, :]
    tri = np.zeros((2, 6, n, n), np.float32)
    for d in range(2):
        before = (t_c < t_r) if d == 0 else (t_c > t_r)
        tri[d, 0] = bd * before
        tri[d, 1] = bd * (before | (t_c == t_r))
        tri[d, 2] = bd * before * (t_r // 8 == t_c // 8)
        for lvl, blk in enumerate((8, 16, 32)):
            tri[d, 3 + lvl] = bd * before * (t_r // (2 * blk) == t_c // (2 * blk)) * (t_r // blk != t_c // blk)
    j = np.arange(CHUNK)
    csi = np.stack([(j[None, :] <= j[:, None]), (j[None, :] >= j[:, None])]).astype(np.float32)
    tt = np.arange(TM)
    ut = (tt[:, None] < tt[None, :]).astype(np.float32)
    ee = np.arange(N_EXPERTS)
    lt = (ee[None, :] < ee[:, None]).astype(np.float32)
    jj = np.arange(TM + 2 * POOL_HALO)[None, :]
    band = np.stack([((jj >= tt[:, None] + POOL_HALO - w // 2) & (jj <= tt[:, None] + POOL_HALO + w // 2 - 1))
                     for w in POOL_WINDOWS]).astype(np.float32)

    def counts(length):
        t = np.arange(length)[:, None]
        half = np.repeat(np.array(POOL_WINDOWS) // 2, 64)[None, :]
        return (np.clip(t + half, 0, length) - np.clip(t - half, 0, length)).astype(np.float32)

    cnt = np.concatenate([counts(l_lat), counts(l_ctx)], axis=0)
    return dict(bd=jnp.asarray(bd), eye=jnp.eye(n, dtype=f32), tri=jnp.asarray(tri), csi=jnp.asarray(csi),
                avg=jnp.asarray(bd / 64.0), ut=jnp.asarray(ut, dtype=bf16), lt=jnp.asarray(lt, dtype=bf16),
                band=jnp.asarray(band), cnt=jnp.asarray(cnt))


def _rope_tables(l_lat):
    rows = l_lat // GRID_W
    row = jnp.repeat(jnp.arange(rows, dtype=f32), GRID_W)
    col = jnp.tile(jnp.arange(GRID_W, dtype=f32), rows)
    n_freq = QK_ROPE // 4
    inv = ROPE_THETA ** (-jnp.arange(n_freq, dtype=f32) / n_freq)
    ang = jnp.concatenate([row[:, None] * inv, col[:, None] * inv], -1)
    cos = jnp.concatenate([jnp.cos(ang), jnp.ones((TM, 16), f32)], 0)
    sin = jnp.concatenate([jnp.sin(ang), jnp.zeros((TM, 16), f32)], 0)
    n = cos.shape[0]
    ct = jnp.concatenate([jnp.ones((n, QK_NOPE), f32), cos, cos, jnp.zeros((n, 32), f32)], 1)
    st = jnp.concatenate([jnp.zeros((n, QK_NOPE), f32), -sin, sin, jnp.zeros((n, 32), f32)], 1)
    qs = MLA_SCALE * LOG2E
    return jnp.stack([ct * qs, st * qs, ct, st])


def _layer_params(i, a):
    d = a["w_in"].shape[1]
    p = {}
    row = lambda v: v.reshape(1, -1).astype(f32)
    for name in ("pre_mix_g", "post_mix_g", "pre_ffn_g", "post_ffn_g", "mla_q_norm", "mla_kv_norm"):
        p[name] = row(a[name][i])
    w_in = a["w_in"][i]
    zc = lambda n: jnp.zeros((d, n), f32)
    p["w_in_p"] = jnp.concatenate(
        [w_in[:, 0:MLA_IN], zc(128 - QK_ROPE), w_in[:, MLA_IN:MLA_IN + POOL_WIDTH],
         w_in[:, MLA_IN + POOL_WIDTH:], zc(RWKV_PAD - RWKV_IN)], 1).astype(bf16)
    wq = a["mla_w_q_b"][i].reshape(Q_LORA, MLA_HEADS, QK_NOPE + QK_ROPE)
    zq = jnp.zeros((Q_LORA, MLA_HEADS, 32), f32)
    half = QK_ROPE // 2
    p["wq_p"] = jnp.concatenate([wq, zq], 2).reshape(Q_LORA, -1).astype(bf16)
    p["wq_s"] = jnp.concatenate([jnp.zeros_like(wq[:, :, :QK_NOPE]), wq[:, :, QK_NOPE + half:],
                                 wq[:, :, QK_NOPE:QK_NOPE + half], zq], 2).reshape(Q_LORA, -1).astype(bf16)
    wkv = a["mla_w_kv_b"][i].reshape(KV_LORA, MLA_HEADS, QK_NOPE + V_HEAD)
    p["wk_p"] = jnp.concatenate([wkv[:, :, :QK_NOPE], jnp.zeros((KV_LORA, MLA_HEADS, 64), f32)], 2
                                ).reshape(KV_LORA, -1).astype(bf16)
    p["wv_p"] = wkv[:, :, QK_NOPE:].reshape(KV_LORA, -1).astype(bf16)
    e_p = np.zeros((128, MLA_HEADS * HEAD_PAD), np.float32)
    e_s = np.zeros_like(e_p)
    for h in range(MLA_HEADS):
        for j in range(QK_ROPE):
            e_p[j, h * HEAD_PAD + QK_NOPE + j] = 1.0
            e_s[(j + half) % QK_ROPE, h * HEAD_PAD + QK_NOPE + j] = 1.0
    p["e_p"], p["e_s"] = jnp.asarray(e_p, dtype=bf16), jnp.asarray(e_s, dtype=bf16)
    pw = a["pool_w"][i]
    p["pool_w_bd"] = jax.scipy.linalg.block_diag(*[pw[g] for g in range(pw.shape[0])]).astype(bf16)
    p["pool_scale"] = row(a["pool_scale"][i])
    p["mu_p"] = jnp.pad(a["rwkv_mu"][i], (0, RWKV_PAD - RWKV_IN)).reshape(1, -1)
    p["w0"] = a["rwkv_w0"][i].reshape(2, 1, RWKV_WIDTH)
    p["a0"] = a["rwkv_a0"][i].reshape(2, 1, RWKV_WIDTH)
    z32 = jnp.zeros((32, RWKV_WIDTH), f32)
    w2, a2 = a["rwkv_w2"][i], a["rwkv_a2"][i]
    p["w2_p"] = jnp.stack([jnp.concatenate([w2[0], z32, z32, z32]), jnp.concatenate([z32, w2[1], z32, z32])])
    p["a2_p"] = jnp.stack([jnp.concatenate([z32, z32, a2[0], z32]), jnp.concatenate([z32, z32, z32, a2[1]])])
    p["g2_p"] = jnp.concatenate([a["rwkv_g2"][i], jnp.zeros((64, RWKV_WIDTH), f32)])
    for name in ("k_k", "k_a", "r_k"):
        p[name] = row(a["rwkv_" + name][i])
    p["ln_w"], p["ln_b"] = row(a["rwkv_ln_w"][i]), row(a["rwkv_ln_b"][i])
    p["w_out"] = a["w_out"][i].astype(bf16)
    p["router_wt"] = a["router_w"][i].T.astype(bf16)
    p["router_b"] = a["router_bias"][i].reshape(-1, 1).astype(f32)
    p["wgu"] = jnp.concatenate([a["exp_w_gate"][i], a["exp_w_up"][i]], 2).astype(bf16)
    p["wd"] = a["exp_w_down"][i].astype(bf16)
    p["sh_wgu"] = jnp.concatenate([a["sh_w_gate"][i], a["sh_w_up"][i]], 1).astype(bf16)
    p["sh_wd"] = a["sh_w_down"][i].astype(bf16)
    return p


def kernel(x, c, ctx, c_ctx, ada_w, ada_b, pre_mix_g, post_mix_g, pre_ffn_g, post_ffn_g, w_in, w_out, mla_q_norm, mla_w_q_b, mla_kv_norm, mla_w_kv_b, pool_w, pool_scale, rwkv_mu, rwkv_w0, rwkv_w2, rwkv_a0, rwkv_a2, rwkv_g2, rwkv_k_k, rwkv_k_a, rwkv_r_k, rwkv_ln_w, rwkv_ln_b, router_w, router_bias, exp_w_gate, exp_w_up, exp_w_down, sh_w_gate, sh_w_up, sh_w_down):
    arrs = dict(pre_mix_g=pre_mix_g, post_mix_g=post_mix_g, pre_ffn_g=pre_ffn_g, post_ffn_g=post_ffn_g, w_in=w_in,
                w_out=w_out, mla_q_norm=mla_q_norm, mla_w_q_b=mla_w_q_b, mla_kv_norm=mla_kv_norm,
                mla_w_kv_b=mla_w_kv_b, pool_w=pool_w, pool_scale=pool_scale, rwkv_mu=rwkv_mu, rwkv_w0=rwkv_w0,
                rwkv_w2=rwkv_w2, rwkv_a0=rwkv_a0, rwkv_a2=rwkv_a2, rwkv_g2=rwkv_g2, rwkv_k_k=rwkv_k_k,
                rwkv_k_a=rwkv_k_a, rwkv_r_k=rwkv_r_k, rwkv_ln_w=rwkv_ln_w, rwkv_ln_b=rwkv_ln_b, router_w=router_w,
                router_bias=router_bias, exp_w_gate=exp_w_gate, exp_w_up=exp_w_up, exp_w_down=exp_w_down,
                sh_w_gate=sh_w_gate, sh_w_up=sh_w_up, sh_w_down=sh_w_down)
    b_sz, l_lat, d = x.shape
    l_ctx = ctx.shape[1]
    assert l_lat % TM == 0 and l_ctx % TM == 0 and l_lat % GRID_W == 0 and b_sz < 16
    lt, ct = l_lat // TM, l_ctx // TM
    nlat, nctx = b_sz * lt, b_sz * ct
    nall = nlat + nctx
    ncc, nlc = l_ctx // CHUNK, l_lat // CHUNK
    consts = _np_consts(l_lat, l_ctx)
    rope = _rope_tables(l_lat)
    rope_idx = lambda i: jnp.where(i < nlat, i % lt, lt)
    cnt_idx = lambda i: jnp.where(i < nlat, i % lt, lt + (i - nlat) % ct)

    c_all = jnp.zeros((16, d), f32).at[:b_sz].set(c).at[b_sz].set(c_ctx)
    mods = _ada_mod(c_all, ada_w, ada_b)
    tile_row = np.concatenate([np.repeat(np.arange(b_sz), lt), np.full(nctx, b_sz)])
    xa = jnp.concatenate([x.reshape(b_sz * l_lat, d), ctx.reshape(b_sz * l_ctx, d)], 0)

    for i in range(DEPTH):
        last = i == DEPTH - 1
        p = _layer_params(i, arrs)
        mt = jnp.pad(mods[i][tile_row].reshape(nall, 6, d), ((0, 0), (0, 2), (0, 0)))
        nt = nlat if last else nall
        q, k, v, zp, zr = _inproj(xa, mt, rope, rope_idx, p, nall)
        segs_lat = [(l_lat, min(512, l_lat), lambda t: t // lt), (l_ctx, l_ctx if l_ctx <= 512 else TM,
                                                                   lambda t: nlat * TM // l_ctx + t // lt)]
        att = _attention(q, k, v, nlat, 0, segs_lat)
        if not last:
            segs_ctx = [(l_ctx, l_ctx if l_ctx <= 512 else TM, lambda t: nlat * TM // l_ctx + t // ct)]
            att = jnp.concatenate([att, _attention(q, k, v, nctx, nlat, segs_ctx)], 0)
        py = _pool(zp, consts["band"], consts["cnt"], cnt_idx, p["pool_w_bd"], p["pool_scale"], nt, nlat, lt, ct)
        y, bv, g = _rwkv(zr, consts, p, b_sz, ncc, nlc)
        x1, hp = _outproj(att, py, y, bv, g, xa, mt, consts, p, nt)
        xa = _moe(hp, x1, mt, consts, p, nt)
    return xa[:b_sz * l_lat].reshape(b_sz, l_lat, d)
```

```python
import functools

import numpy as np
import jax
import jax.numpy as jnp
from jax import lax
from jax.experimental import pallas as pl
from jax.experimental.pallas import tpu as pltpu

f32 = jnp.float32
bf16 = jnp.bfloat16
HIGHEST = lax.Precision.HIGHEST

DEPTH = 2
GRID_W = 64
NORM_EPS = 1e-6
MLA_HEADS = 8
Q_LORA = 384
KV_LORA = 256
QK_NOPE = 64
QK_ROPE = 32
V_HEAD = 64
ROPE_THETA = 10000.0
MLA_SCALE = (QK_NOPE + QK_ROPE) ** -0.5
MLA_IN = Q_LORA + KV_LORA + QK_ROPE
HEAD_PAD = 128
POOL_WINDOWS = (2, 4, 8, 16)
POOL_WIDTH = 256
POOL_HALO = 8
RWKV_HEADS = 4
RWKV_HEAD = 64
RWKV_WIDTH = 256
RWKV_IN = 960
RWKV_PAD = 1024
RWKV_GN_EPS = 64e-5
CHUNK = 64
N_EXPERTS = 64
TOP_K = 6
N_GROUPS = 8
TOPK_GROUPS = 4
ROUTED_SCALE = 2.5
EXPERT_FF = 256
TM = 256
MOE_BM = 256
Z_COLS = 2048
VMEM_LIMIT = 48 * 1024 * 1024
LOG2E = 1.4426950408889634
EXP_M05 = 0.6065306597126334


def _cparams(sem, vmem=VMEM_LIMIT):
    return pltpu.CompilerParams(dimension_semantics=sem, vmem_limit_bytes=vmem)


def _mm(a, b):
    return jnp.dot(a.astype(bf16), b.astype(bf16), preferred_element_type=f32)


def _mm_nt(a, b):
    return lax.dot_general(a.astype(bf16), b.astype(bf16), (((1,), (1,)), ((), ())), preferred_element_type=f32)


def _mm_tn(a, b):
    return lax.dot_general(a.astype(bf16), b.astype(bf16), (((0,), (0,)), ((), ())), preferred_element_type=f32)


def _mmf(a, b):
    return jnp.dot(a, b, precision=HIGHEST, preferred_element_type=f32)


def _rms(x):
    return x * lax.rsqrt(jnp.mean(x * x, axis=-1, keepdims=True) + NORM_EPS)


def _sigmoid(x):
    return 1.0 / (1.0 + jnp.exp(-x))


def _ada_kernel(c_ref, w_ref, b_ref, o_ref):
    c = c_ref[...]
    s = c * _sigmoid(c)
    o_ref[0] = _mm(s, w_ref[0]) + b_ref[0]


def _ada_mod(c_all, ada_w, ada_b):
    depth, d, n = ada_w.shape
    tn = 1024
    return pl.pallas_call(
        _ada_kernel,
        out_shape=jax.ShapeDtypeStruct((depth, 16, n), f32),
        grid=(depth, n // tn),
        in_specs=[pl.BlockSpec((16, d), lambda i, j: (0, 0)),
                  pl.BlockSpec((1, d, tn), lambda i, j: (i, 0, j)),
                  pl.BlockSpec((1, 1, tn), lambda i, j: (i, 0, j))],
        out_specs=pl.BlockSpec((1, 16, tn), lambda i, j: (i, 0, j)),
        compiler_params=_cparams(("arbitrary", "arbitrary")),
        name="ada_mod",
    )(c_all, ada_w, ada_b.reshape(depth, 1, n))


def _inproj_kernel(x_ref, m_ref, rope_ref, g_ref, win_ref, qg_ref, wq_ref, wqs_ref, kvg_ref, wk_ref, wv_ref,
                   e_ref, es_ref, q_ref, k_ref, v_ref, zp_ref, zr_ref):
    m = m_ref[0]
    h = _rms(x_ref[...]) * g_ref[...]
    h = h * (1.0 + m[1:2]) + m[0:1]
    z = _mm(h, win_ref[...])
    zp_ref[...] = z[:, 768:1024]
    zr_ref[...] = z[:, 1024:2048]
    tile8 = lambda t: jnp.concatenate([t] * MLA_HEADS, axis=1)
    qn = (_rms(z[:, 0:Q_LORA]) * qg_ref[...]).astype(bf16)
    q = _mm(qn, wq_ref[...]) * tile8(rope_ref[0]) + _mm(qn, wqs_ref[...]) * tile8(rope_ref[1])
    q_ref[...] = q.astype(bf16)
    kvn = (_rms(z[:, Q_LORA:Q_LORA + KV_LORA]) * kvg_ref[...]).astype(bf16)
    kpe = z[:, 640:768].astype(bf16)
    k = _mm(kvn, wk_ref[...]) + _mm(kpe, e_ref[...]) * tile8(rope_ref[2]) + _mm(kpe, es_ref[...]) * tile8(rope_ref[3])
    k_ref[...] = k.astype(bf16)
    lane = lax.broadcasted_iota(jnp.int32, (1, MLA_HEADS * HEAD_PAD), 1)
    v_ref[...] = (_mm(kvn, wv_ref[...]) + jnp.where(lane % HEAD_PAD == V_HEAD, 1.0, 0.0)).astype(bf16)


def _inproj(xa, mt, rope, rope_idx, p, nt):
    t_all, d = xa.shape
    full = lambda a: pl.BlockSpec(a.shape, lambda i: (0,) * a.ndim)
    ws = [p["pre_mix_g"], p["w_in_p"], p["mla_q_norm"], p["wq_p"], p["wq_s"], p["mla_kv_norm"], p["wk_p"], p["wv_p"],
          p["e_p"], p["e_s"]]
    return pl.pallas_call(
        _inproj_kernel,
        out_shape=(jax.ShapeDtypeStruct((t_all, MLA_HEADS * HEAD_PAD), bf16),
                   jax.ShapeDtypeStruct((t_all, MLA_HEADS * HEAD_PAD), bf16),
                   jax.ShapeDtypeStruct((t_all, MLA_HEADS * HEAD_PAD), bf16),
                   jax.ShapeDtypeStruct((t_all, POOL_WIDTH), f32),
                   jax.ShapeDtypeStruct((t_all, RWKV_PAD), f32)),
        grid=(nt,),
        in_specs=[pl.BlockSpec((TM, d), lambda i: (i, 0)),
                  pl.BlockSpec((1, 8, d), lambda i: (i, 0, 0)),
                  pl.BlockSpec((4, TM, HEAD_PAD), lambda i: (0, rope_idx(i), 0))] + [full(w) for w in ws],
        out_specs=(pl.BlockSpec((TM, 1024), lambda i: (i, 0)),
                   pl.BlockSpec((TM, 1024), lambda i: (i, 0)),
                   pl.BlockSpec((TM, 1024), lambda i: (i, 0)),
                   pl.BlockSpec((TM, POOL_WIDTH), lambda i: (i, 0)),
                   pl.BlockSpec((TM, RWKV_PAD), lambda i: (i, 0))),
        compiler_params=_cparams(("arbitrary",)),
        name="in_proj",
    )(xa, mt, rope, *ws)


def _attn_kernel(*refs, seg_tiles):
    n_in = 1 + 2 * len(seg_tiles)
    q_ref, kv, o_ref, m_ref, acc_ref = refs[0], refs[1:n_in], refs[n_in], refs[n_in + 1], refs[n_in + 2]
    m_ref[...] = jnp.full(m_ref.shape, -jnp.inf, f32)
    acc_ref[...] = jnp.zeros(acc_ref.shape, f32)

    def step(k_ref, v_ref, r0, tk):
        for h in range(MLA_HEADS):
            hs = slice(h * HEAD_PAD, (h + 1) * HEAD_PAD)
            s = lax.dot_general(q_ref[:, hs], k_ref[pl.ds(r0, tk), hs], (((1,), (1,)), ((), ())),
                                preferred_element_type=f32)
            m_prev = m_ref[h]
            m_new = jnp.maximum(m_prev, jnp.max(s, axis=-1, keepdims=True))
            pr = jnp.exp2(s - jnp.concatenate([m_new] * (tk // HEAD_PAD), axis=1))
            acc_ref[h] = (jnp.exp2(m_prev - m_new) * acc_ref[h]
                          + jnp.dot(pr.astype(bf16), v_ref[pl.ds(r0, tk), hs], preferred_element_type=f32))
            m_ref[h] = m_new

    for si, (nk, tk) in enumerate(seg_tiles):
        k_ref, v_ref = kv[2 * si], kv[2 * si + 1]
        if nk == 1:
            step(k_ref, v_ref, 0, tk)
        else:
            def body(j, c, k_ref=k_ref, v_ref=v_ref, tk=tk):
                step(k_ref, v_ref, pl.multiple_of(j * tk, tk), tk)
                return c
            lax.fori_loop(0, nk, body, 0)
    for pr2 in range(MLA_HEADS // 2):
        a0, a1 = acc_ref[2 * pr2], acc_ref[2 * pr2 + 1]
        o_ref[:, pr2 * 128:(pr2 + 1) * 128] = jnp.concatenate(
            [a0[:, :V_HEAD] / a0[:, V_HEAD:V_HEAD + 1], a1[:, :V_HEAD] / a1[:, V_HEAD:V_HEAD + 1]], axis=1).astype(bf16)


def _attention(q, k, v, n_q_tiles, q_tile0, segs):
    in_specs = [pl.BlockSpec((TM, 1024), lambda i: (i + q_tile0, 0))]
    args = [q]
    seg_tiles = []
    for rows, tk, bidx in segs:
        in_specs.append(pl.BlockSpec((rows, 1024), lambda i, bidx=bidx: (bidx(i), 0), pipeline_mode=pl.Buffered(1)))
        in_specs.append(pl.BlockSpec((rows, 1024), lambda i, bidx=bidx: (bidx(i), 0), pipeline_mode=pl.Buffered(1)))
        args += [k, v]
        seg_tiles.append((rows // tk, tk))
    return pl.pallas_call(
        functools.partial(_attn_kernel, seg_tiles=tuple(seg_tiles)),
        out_shape=jax.ShapeDtypeStruct((n_q_tiles * TM, MLA_HEADS * V_HEAD), bf16),
        grid=(n_q_tiles,),
        in_specs=in_specs,
        out_specs=pl.BlockSpec((TM, 512), lambda i: (i, 0)),
        scratch_shapes=[pltpu.VMEM((MLA_HEADS, TM, HEAD_PAD), f32), pltpu.VMEM((MLA_HEADS, TM, HEAD_PAD), f32)],
        compiler_params=_cparams(("arbitrary",)),
        name="mla_attention",
    )(*args)


def _pool_kernel(z_ref, zp_ref, zn_ref, band_ref, cnt_ref, pw_ref, ps_ref, o_ref, *, nlat, lt, ct):
    i = pl.program_id(0)
    is_lat = i < nlat
    j = jnp.where(is_lat, i % lt, (i - nlat) % ct)
    n = jnp.where(is_lat, lt, ct)
    z = z_ref[...]
    prev = zp_ref[...] * jnp.where(j == 0, 0.0, 1.0)
    nxt = zn_ref[...] * jnp.where(j == n - 1, 0.0, 1.0)
    zh = jnp.concatenate([prev, z, nxt], axis=0)
    lane_grp = lax.broadcasted_iota(jnp.int32, (1, POOL_WIDTH), 1) // 64
    tot = jnp.zeros_like(z)
    for g in range(len(POOL_WINDOWS)):
        tot = tot + _mmf(band_ref[g], zh * jnp.where(lane_grp == g, 1.0, 0.0))
    diff = tot / cnt_ref[...] - z
    o_ref[...] = (_mm(diff, pw_ref[...]) * ps_ref[...]).astype(bf16)


def _pool(zp, band, cnt, cnt_idx, pw_bd, pscale, nt, nlat, lt, ct):
    t_all = zp.shape[0]
    nb8 = t_all // POOL_HALO
    r = TM // POOL_HALO
    return pl.pallas_call(
        functools.partial(_pool_kernel, nlat=nlat, lt=lt, ct=ct),
        out_shape=jax.ShapeDtypeStruct((nt * TM, POOL_WIDTH), bf16),
        grid=(nt,),
        in_specs=[pl.BlockSpec((TM, POOL_WIDTH), lambda i: (i, 0)),
                  pl.BlockSpec((POOL_HALO, POOL_WIDTH), lambda i: (jnp.maximum(i * r - 1, 0), 0)),
                  pl.BlockSpec((POOL_HALO, POOL_WIDTH), lambda i: (jnp.minimum((i + 1) * r, nb8 - 1), 0)),
                  pl.BlockSpec(band.shape, lambda i: (0, 0, 0)),
                  pl.BlockSpec((TM, POOL_WIDTH), lambda i: (cnt_idx(i), 0)),
                  pl.BlockSpec((POOL_WIDTH, POOL_WIDTH), lambda i: (0, 0)),
                  pl.BlockSpec((1, POOL_WIDTH), lambda i: (0, 0))],
        out_specs=pl.BlockSpec((TM, POOL_WIDTH), lambda i: (i, 0)),
        compiler_params=_cparams(("arbitrary",)),
        name="pool_mixer",
    )(zp, zp, zp, band, cnt, pw_bd, pscale)


def _rwkv_blk(d, b, s, *, ncc, nlc, cbase):
    in_ctx = s < ncc
    jc = jnp.where(d == 0, s, ncc - 1 - s)
    jl = jnp.where(d == 0, s - ncc, nlc - 1 - (s - ncc))
    blk = jnp.where(in_ctx, cbase + b * ncc + jc, b * nlc + jl)
    first = jnp.where(in_ctx, jc == 0, jl == 0)
    last = jnp.where(in_ctx, jc == ncc - 1, jl == nlc - 1)
    return blk, first, last


def _tri_inverse(lm, tri_ref, eye):
    n = -(lm * tri_ref[0, 2])
    n2 = _mm(n, n)
    n4 = _mm(n2, n2)
    t = _mm(_mm(eye + n, eye + n2), eye + n4)
    for lvl in range(3):
        t = t - _mm(_mm(t, lm * tri_ref[0, 3 + lvl]), t)
    return t


def _rwkv_kernel(z_ref, zp_ref, zn_ref, bd_ref, eye_ref, tri_ref, csi_ref, mu_ref, w0_ref, w2_ref, a0_ref, a2_ref,
                 g2_ref, kk_ref, ka_ref, rk_ref, y_ref, bv_ref, g_ref, s_ref, *, ncc, nlc, cbase):
    d, b, s = pl.program_id(0), pl.program_id(1), pl.program_id(2)
    _, first, last = _rwkv_blk(d, b, s, ncc=ncc, nlc=nlc, cbase=cbase)

    @pl.when(s == 0)
    def _():
        s_ref[...] = jnp.zeros_like(s_ref)

    c = CHUNK
    z = z_ref[...]
    row = lax.broadcasted_iota(jnp.int32, (c, 1), 0)
    prev_row = zp_ref[7:8, :] * jnp.where(first, 0.0, 1.0)
    next_row = zn_ref[0:1, :] * jnp.where(last, 0.0, 1.0)
    zp = jnp.where(row == 0, prev_row, pltpu.roll(z, 1, 0))
    zn = jnp.where(row == c - 1, next_row, pltpu.roll(z, c - 1, 0))
    zs = z + mu_ref[...] * (0.5 * (zp + zn) - z)
    r, k, v = zs[:, 0:256], zs[:, 256:512], zs[:, 512:768]
    lora, gate_in = zs[:, 768:896], zs[:, 896:1024]
    bd = bd_ref[...]
    eye = eye_ref[...]

    g_ref[0] = _mmf(_sigmoid(gate_in), g2_ref[...])
    e = EXP_M05 * _sigmoid(w0_ref[0] + _mmf(jnp.tanh(lora), w2_ref[0]))
    a = _sigmoid(a0_ref[0] + _mmf(lora, a2_ref[0]))
    kd = k * (1.0 + (a - 1.0) * ka_ref[...])
    kkr = k * kk_ref[...]
    kk = kkr / jnp.maximum(jnp.sqrt(_mmf(kkr * kkr, bd)), 1e-12)
    bv_ref[0] = _mmf(r * kd * rk_ref[...], bd) * v
    bb = kk * a

    cs = _mmf(csi_ref[0], e)
    tot = _mmf(jnp.ones((c, c), f32), e)
    a_s = jnp.exp(e - cs) * kk
    b_s = bb * jnp.exp(cs)
    k_s = kd * jnp.exp(cs)
    r_s = r * jnp.exp(-cs)
    b_e = bb * jnp.exp(cs - tot)
    k_e = kd * jnp.exp(cs - tot)
    g_end = jnp.exp(-tot[0:1, :])

    rep4 = lambda t: jnp.concatenate([t] * RWKV_HEADS, axis=0)
    fold4 = lambda t: t[0:c] + t[c:2 * c] + t[2 * c:3 * c] + t[3 * c:4 * c]
    a4, r4, v4 = rep4(a_s) * bd, rep4(r_s) * bd, rep4(v) * bd
    b4, k4 = rep4(b_s), rep4(k_s)
    tri_s, tri_i = tri_ref[0, 0], tri_ref[0, 1]
    lm = _mm_nt(a4, b4) * tri_s
    akm = _mm_nt(a4, k4) * tri_s
    rbm = _mm_nt(r4, b4) * tri_i
    rkm = _mm_nt(r4, k4) * tri_i
    t = _tri_inverse(lm, tri_ref, eye)
    w4 = _mm(t, a4)
    u4 = _mm(t, _mm(akm, v4))
    q4 = r4 - _mm(rbm, w4)
    y4 = _mm(rkm, v4) - _mm(rbm, u4)
    w_all, u_all, q_all, y0 = fold4(w4), fold4(u4), fold4(q4), fold4(y4)
    g_bd = eye * g_end - bd * _mm_tn(w_all, b_e)
    h_bd = bd * (_mm_tn(v, k_e) - _mm_tn(u_all, b_e))
    st = s_ref[...]
    y_ref[0] = _mm_nt(q_all, st) + y0
    s_ref[...] = _mm(st, g_bd) + h_bd


def _rwkv(zr, consts, p, b_sz, ncc, nlc):
    t_all = zr.shape[0]
    cbase = b_sz * nlc
    nb8 = t_all // 8
    kw = dict(ncc=ncc, nlc=nlc, cbase=cbase)
    blk = lambda d, b, s: _rwkv_blk(d, b, s, **kw)[0]
    full = lambda a: pl.BlockSpec(a.shape, lambda d, b, s: (0,) * a.ndim)
    by_dir = lambda a: pl.BlockSpec((1,) + a.shape[1:], lambda d, b, s: (d,) + (0,) * (a.ndim - 1))
    out = jax.ShapeDtypeStruct((2, t_all, RWKV_WIDTH), f32)
    ospec = pl.BlockSpec((1, CHUNK, RWKV_WIDTH), lambda d, b, s: (d, blk(d, b, s), 0))
    return pl.pallas_call(
        functools.partial(_rwkv_kernel, **kw),
        out_shape=(out, out, out),
        grid=(2, b_sz, ncc + nlc),
        in_specs=[pl.BlockSpec((CHUNK, RWKV_PAD), lambda d, b, s: (blk(d, b, s), 0)),
                  pl.BlockSpec((8, RWKV_PAD), lambda d, b, s: (jnp.maximum(blk(d, b, s) * 8 - 1, 0), 0)),
                  pl.BlockSpec((8, RWKV_PAD), lambda d, b, s: (jnp.minimum(blk(d, b, s) * 8 + 8, nb8 - 1), 0)),
                  full(consts["bd"]), full(consts["eye"]), by_dir(consts["tri"]), by_dir(consts["csi"]),
                  full(p["mu_p"]), by_dir(p["w0"]), by_dir(p["w2_p"]), by_dir(p["a0"]), by_dir(p["a2_p"]),
                  full(p["g2_p"]), full(p["k_k"]), full(p["k_a"]), full(p["r_k"])],
        out_specs=(ospec, ospec, ospec),
        scratch_shapes=[pltpu.VMEM((RWKV_WIDTH, RWKV_WIDTH), f32)],
        compiler_params=_cparams(("arbitrary", "arbitrary", "arbitrary")),
        name="rwkv7_chunked",
    )(zr, zr, zr, consts["bd"], consts["eye"], consts["tri"], consts["csi"], p["mu_p"], p["w0"], p["w2_p"], p["a0"],
      p["a2_p"], p["g2_p"], p["k_k"], p["k_a"], p["r_k"])


def _pack_bf16_pairs(lo, hi):
    lo_b = pltpu.bitcast(lo.astype(bf16).astype(f32), jnp.uint32)
    hi_b = pltpu.bitcast(hi.astype(bf16).astype(f32), jnp.uint32)
    return (hi_b & jnp.uint32(0xFFFF0000)) | (lo_b >> 16)


def _unpack_bf16_pairs(w):
    lo = pltpu.bitcast(w << 16, f32).astype(bf16)
    hi = pltpu.bitcast(w & jnp.uint32(0xFFFF0000), f32).astype(bf16)
    return lo, hi


def _outproj_kernel(a_ref, py_ref, y_ref, bv_ref, g_ref, x_ref, m_ref, avg_ref, lnw_ref, lnb_ref, wo_ref, pmg_ref,
                    pfg_ref, x1_ref, hp_ref):
    m = m_ref[0]
    ysum = y_ref[0] + y_ref[1]
    avg = avg_ref[...]
    dev = ysum - _mmf(ysum, avg)
    var = _mmf(dev * dev, avg)
    yn = dev * lax.rsqrt(var + RWKV_GN_EPS) * lnw_ref[...] + lnb_ref[...]
    rw = (yn + bv_ref[0] + bv_ref[1]) * g_ref[0]
    o = (jnp.dot(a_ref[...], wo_ref[0:512, :], preferred_element_type=f32)
         + jnp.dot(py_ref[...], wo_ref[512:768, :], preferred_element_type=f32)
         + _mm(rw, wo_ref[768:1024, :]))
    x1 = x_ref[...] + m[2:3] * (_rms(o) * pmg_ref[...])
    x1_ref[...] = x1
    h = (_rms(x1) * pfg_ref[...]) * (1.0 + m[4:5]) + m[3:4]
    hp_ref[...] = _pack_bf16_pairs(h[:, 0:512], h[:, 512:1024])


def _outproj(att, py, y, bv, g, xa, mt, consts, p, nt):
    d = xa.shape[1]
    full = lambda a: pl.BlockSpec(a.shape, lambda i: (0,) * a.ndim)
    ws = [consts["avg"], p["ln_w"], p["ln_b"], p["w_out"], p["post_mix_g"], p["pre_ffn_g"]]
    return pl.pallas_call(
        _outproj_kernel,
        out_shape=(jax.ShapeDtypeStruct((nt * TM, d), f32), jax.ShapeDtypeStruct((nt * TM, d // 2), jnp.uint32)),
        grid=(nt,),
        in_specs=[pl.BlockSpec((TM, 512), lambda i: (i, 0)),
                  pl.BlockSpec((TM, POOL_WIDTH), lambda i: (i, 0)),
                  pl.BlockSpec((2, TM, RWKV_WIDTH), lambda i: (0, i, 0)),
                  pl.BlockSpec((2, TM, RWKV_WIDTH), lambda i: (0, i, 0)),
                  pl.BlockSpec((1, TM, RWKV_WIDTH), lambda i: (0, i, 0)),
                  pl.BlockSpec((TM, d), lambda i: (i, 0)),
                  pl.BlockSpec((1, 8, d), lambda i: (i, 0, 0))] + [full(w) for w in ws],
        out_specs=(pl.BlockSpec((TM, d), lambda i: (i, 0)), pl.BlockSpec((TM, d // 2), lambda i: (i, 0))),
        compiler_params=_cparams(("arbitrary",)),
        name="out_proj",
    )(att, py, y, bv, g, xa, mt, *ws)


def _router_kernel(hp_ref, rw_ref, rb_ref, ut_ref, lt_ref, ei_ref, pos_ref, gt_ref, cnt_ref, run_ref):
    i = pl.program_id(0)

    @pl.when(i == 0)
    def _():
        run_ref[...] = jnp.zeros_like(run_ref)

    tm = hp_ref.shape[0]
    ne, ng = N_EXPERTS, N_GROUPS
    pg = ne // ng
    lo, hi = _unpack_bf16_pairs(hp_ref[...])
    logits = (lax.dot_general(rw_ref[:, 0:512], lo, (((1,), (1,)), ((), ())), preferred_element_type=f32)
              + lax.dot_general(rw_ref[:, 512:1024], hi, (((1,), (1,)), ((), ())), preferred_element_type=f32))
    scores = _sigmoid(logits)
    sel = scores + rb_ref[...]
    neg = -jnp.inf

    s3 = sel.reshape(ng, pg, tm)
    io = lax.broadcasted_iota(jnp.int32, (ng, pg, tm), 1)
    m1 = jnp.max(s3, axis=1, keepdims=True)
    i1 = jnp.min(jnp.where(s3 == m1, io, pg), axis=1, keepdims=True)
    m2 = jnp.max(jnp.where(io == i1, neg, s3), axis=1, keepdims=True)
    gs = (m1 + m2).reshape(ng, tm)
    gi = lax.broadcasted_iota(jnp.int32, (ng, tm), 0)
    grank = jnp.zeros((ng, tm), f32)
    for j in range(ng):
        rj = gs[j:j + 1, :]
        grank = grank + jnp.where((rj > gs) | ((rj == gs) & (j < gi)), 1.0, 0.0)
    gsel = jnp.where(grank < TOPK_GROUPS, 1.0, 0.0)
    gsel3 = jnp.broadcast_to(gsel.reshape(ng, 1, tm), (ng, pg, tm)).reshape(ne, tm)
    msk = jnp.where(gsel3 > 0.5, sel, neg)
    ei = lax.broadcasted_iota(jnp.int32, (ne, tm), 0)
    erank = jnp.zeros((ne, tm), f32)
    for j in range(ne):
        rj = msk[j:j + 1, :]
        erank = erank + jnp.where((rj > msk) | ((rj == msk) & (j < ei)), 1.0, 0.0)
    chosen = erank < TOP_K
    chf = jnp.where(chosen, 1.0, 0.0)
    graw = jnp.where(chosen, scores, 0.0)
    gate = graw / jnp.sum(graw, axis=0, keepdims=True) * ROUTED_SCALE

    pos = run_ref[...] + _mm(chf, ut_ref[...])
    tot = jnp.sum(chf, axis=1, keepdims=True)
    run_new = run_ref[...] + tot
    run_ref[...] = run_new
    cnt_ref[...] = run_new[:, 0:128]
    rk = _mm(lt_ref[...], chf)
    eif = ei.astype(f32)
    rows_e, rows_p, rows_g = [], [], []
    for kq in range(TOP_K):
        mk = chosen & (rk == float(kq))
        rows_e.append(jnp.sum(jnp.where(mk, eif, 0.0), axis=0, keepdims=True))
        rows_p.append(jnp.sum(jnp.where(mk, pos, 0.0), axis=0, keepdims=True))
        rows_g.append(jnp.sum(jnp.where(mk, gate, 0.0), axis=0, keepdims=True))
    zrow = jnp.zeros((8 - TOP_K, tm), f32)
    ei_ref[0] = jnp.concatenate(rows_e + [zrow], axis=0).astype(jnp.int32)
    pos_ref[0] = jnp.concatenate(rows_p + [zrow], axis=0).astype(jnp.int32)
    gpad = jnp.concatenate(rows_g + [jnp.zeros((128 - TOP_K, tm), f32)], axis=0)
    gt_ref[...] = gpad.T


def _router(hp, rwt, rb, consts, nt):
    full = lambda a: pl.BlockSpec(a.shape, lambda i: (0,) * a.ndim)
    return pl.pallas_call(
        _router_kernel,
        out_shape=(jax.ShapeDtypeStruct((nt, 8, TM), jnp.int32), jax.ShapeDtypeStruct((nt, 8, TM), jnp.int32),
                   jax.ShapeDtypeStruct((nt * TM, 128), f32), jax.ShapeDtypeStruct((N_EXPERTS, 128), f32)),
        grid=(nt,),
        in_specs=[pl.BlockSpec((TM, 512), lambda i: (i, 0)), full(rwt), full(rb), full(consts["ut"]),
                  full(consts["lt"])],
        out_specs=(pl.BlockSpec((1, 8, TM), lambda i: (i, 0, 0)), pl.BlockSpec((1, 8, TM), lambda i: (i, 0, 0)),
                   pl.BlockSpec((TM, 128), lambda i: (i, 0)), pl.BlockSpec((N_EXPERTS, 128), lambda i: (0, 0))),
        scratch_shapes=[pltpu.VMEM((N_EXPERTS, TM), f32)],
        compiler_params=_cparams(("arbitrary",)),
        name="moe_router",
    )(hp, rwt, rb, consts["ut"], consts["lt"])


def _row_copy(src_ref, src_row, dst_ref, dst_row, sem):
    return pltpu.make_async_copy(src_ref.at[pl.ds(src_row, 1)], dst_ref.at[pl.ds(dst_row, 1)], sem)


def _dispatch_kernel(ps_ref, ei_ref, pos_ref, hp_ref, xz_ref, xs_ref, dest_ref, ei_s, pos_s, dest_s, isem, rsem):
    del xz_ref
    i = pl.program_id(0)
    c1 = pltpu.make_async_copy(ei_ref.at[i], ei_s, isem.at[0])
    c2 = pltpu.make_async_copy(pos_ref.at[i], pos_s, isem.at[1])
    c1.start()
    c2.start()
    c1.wait()
    c2.wait()

    def issue(t, c):
        for kq in range(TOP_K):
            slot = ps_ref[ei_s[kq * TM + t]] + pos_s[kq * TM + t]
            dest_s[kq * TM + t] = slot
            _row_copy(hp_ref, t, xs_ref, slot, rsem).start()
        return c

    def drain(t, c):
        for kq in range(TOP_K):
            _row_copy(hp_ref, 0, xs_ref, 0, rsem).wait()
        return c

    @pl.when(i == 0)
    def _():
        def zero(j, c):
            dest_s[j] = 0
            return c
        lax.fori_loop(TOP_K * TM, 8 * TM, zero, 0)

    lax.fori_loop(0, TM, issue, 0)
    c3 = pltpu.make_async_copy(dest_s, dest_ref.at[i], isem.at[0])
    c3.start()
    lax.fori_loop(0, TM, drain, 0)
    c3.wait()


def _dispatch(pstart, ei, pos, hp, n_slots, nt):
    xz = jnp.zeros((n_slots, hp.shape[1]), jnp.uint32)
    anyspec = pl.BlockSpec(memory_space=pl.ANY)
    return pl.pallas_call(
        _dispatch_kernel,
        out_shape=(jax.ShapeDtypeStruct(xz.shape, xz.dtype), jax.ShapeDtypeStruct((nt, 8 * TM), jnp.int32)),
        grid_spec=pltpu.PrefetchScalarGridSpec(
            num_scalar_prefetch=1,
            grid=(nt,),
            in_specs=[anyspec, anyspec, pl.BlockSpec((TM, hp.shape[1]), lambda i, ps: (i, 0)), anyspec],
            out_specs=(anyspec, anyspec),
            scratch_shapes=[pltpu.SMEM((8 * TM,), jnp.int32), pltpu.SMEM((8 * TM,), jnp.int32),
                            pltpu.SMEM((8 * TM,), jnp.int32), pltpu.SemaphoreType.DMA((2,)),
                            pltpu.SemaphoreType.DMA]),
        input_output_aliases={4: 0},
        compiler_params=_cparams(("arbitrary",)),
        name="moe_dispatch",
    )(pstart, ei, pos, hp, xz)


def _expert_kernel(be_ref, nb_ref, xs_ref, wgu_ref, wd_ref, ys_ref):
    i = pl.program_id(0)

    @pl.when(i < nb_ref[0])
    def _():
        lo, hi = _unpack_bf16_pairs(xs_ref[...])
        gu = (jnp.dot(lo, wgu_ref[0, 0:512, :], preferred_element_type=f32)
              + jnp.dot(hi, wgu_ref[0, 512:1024, :], preferred_element_type=f32))
        gg, uu = gu[:, 0:EXPERT_FF], gu[:, EXPERT_FF:2 * EXPERT_FF]
        act = gg * _sigmoid(gg) * uu
        ys_ref[...] = _mm(act, wd_ref[0])

    @pl.when(i >= nb_ref[0])
    def _():
        ys_ref[...] = jnp.zeros_like(ys_ref)


def _experts(block_expert, nb_used, xs, wgu, wd):
    n_slots = xs.shape[0]
    d = wd.shape[2]
    return pl.pallas_call(
        _expert_kernel,
        out_shape=jax.ShapeDtypeStruct((n_slots, d), f32),
        grid_spec=pltpu.PrefetchScalarGridSpec(
            num_scalar_prefetch=2,
            grid=(n_slots // MOE_BM,),
            in_specs=[pl.BlockSpec((MOE_BM, xs.shape[1]), lambda i, be, nb: (i, 0)),
                      pl.BlockSpec((1,) + wgu.shape[1:], lambda i, be, nb: (be[i], 0, 0)),
                      pl.BlockSpec((1,) + wd.shape[1:], lambda i, be, nb: (be[i], 0, 0))],
            out_specs=pl.BlockSpec((MOE_BM, d), lambda i, be, nb: (i, 0))),
        compiler_params=_cparams(("arbitrary",)),
        name="moe_experts",
    )(block_expert, nb_used, xs, wgu, wd)


def _combine_kernel(dest_ref, ys_ref, hp_ref, gt_ref, x1_ref, m_ref, wsgu_ref, wsd_ref, pg_ref, o_ref,
                    idx_ref, buf_ref, isem, rsem):
    i = pl.program_id(0)
    cp = pltpu.make_async_copy(dest_ref.at[i], idx_ref, isem)
    cp.start()
    cp.wait()

    def issue(t, c):
        for kq in range(TOP_K):
            _row_copy(ys_ref, idx_ref[kq * TM + t], buf_ref.at[kq], t, rsem).start()
        return c

    def drain(t, c):
        for kq in range(TOP_K):
            _row_copy(ys_ref, 0, buf_ref.at[kq], 0, rsem).wait()
        return c

    lax.fori_loop(0, TM, issue, 0)
    lo, hi = _unpack_bf16_pairs(hp_ref[...])
    gu = (jnp.dot(lo, wsgu_ref[0:512, :], preferred_element_type=f32)
          + jnp.dot(hi, wsgu_ref[512:1024, :], preferred_element_type=f32))
    gg, uu = gu[:, 0:EXPERT_FF], gu[:, EXPERT_FF:2 * EXPERT_FF]
    f = _mm(gg * _sigmoid(gg) * uu, wsd_ref[...])
    lax.fori_loop(0, TM, drain, 0)
    gt = gt_ref[...]
    routed = jnp.zeros_like(f)
    for kq in range(TOP_K):
        routed = routed + buf_ref[kq] * gt[:, kq:kq + 1]
    f = routed + f
    m = m_ref[0]
    o_ref[...] = x1_ref[...] + m[5:6] * (_rms(f) * pg_ref[...])


def _combine(dest, ys, hp, gt, x1, mt, p, nt):
    d = x1.shape[1]
    anyspec = pl.BlockSpec(memory_space=pl.ANY)
    full = lambda a: pl.BlockSpec(a.shape, lambda i: (0,) * a.ndim)
    ws = [p["sh_wgu"], p["sh_wd"], p["post_ffn_g"]]
    return pl.pallas_call(
        _combine_kernel,
        out_shape=jax.ShapeDtypeStruct((nt * TM, d), f32),
        grid=(nt,),
        in_specs=[anyspec, anyspec,
                  pl.BlockSpec((TM, d // 2), lambda i: (i, 0)),
                  pl.BlockSpec((TM, 128), lambda i: (i, 0)),
                  pl.BlockSpec((TM, d), lambda i: (i, 0)),
                  pl.BlockSpec((1, 8, d), lambda i: (i, 0, 0))] + [full(w) for w in ws],
        out_specs=pl.BlockSpec((TM, d), lambda i: (i, 0)),
        scratch_shapes=[pltpu.SMEM((8 * TM,), jnp.int32), pltpu.VMEM((TOP_K, TM, d), f32),
                        pltpu.SemaphoreType.DMA, pltpu.SemaphoreType.DMA],
        compiler_params=_cparams(("arbitrary",)),
        name="moe_combine",
    )(dest, ys, hp, gt, x1, mt, *ws)


def _moe(hp, x1, mt, consts, p, nt):
    t = nt * TM
    ei, pos, gt, cnt = _router(hp, p["router_wt"], p["router_b"], consts, nt)
    n_assign = t * TOP_K
    n_blocks = -(-(n_assign + N_EXPERTS * (MOE_BM - 1)) // MOE_BM)
    n_slots = n_blocks * MOE_BM
    counts = cnt[:, 0].astype(jnp.int32)
    padded = (counts + MOE_BM - 1) // MOE_BM * MOE_BM
    pend = jnp.cumsum(padded)
    pstart = pend - padded
    nb_used = (pend[-1:] // MOE_BM).astype(jnp.int32)
    block_row0 = jnp.arange(n_blocks, dtype=jnp.int32) * MOE_BM
    block_expert = jnp.minimum(jnp.sum((pend[None, :] <= block_row0[:, None]).astype(jnp.int32), axis=1),
                               N_EXPERTS - 1)
    xs, dest = _dispatch(pstart.astype(jnp.int32), ei.reshape(nt, 8 * TM), pos.reshape(nt, 8 * TM), hp, n_slots, nt)
    ys = _experts(block_expert, nb_used, xs, p["wgu"], p["wd"])
    return _combine(dest, ys, hp, gt, x1, mt, p, nt)


def _np_consts(l_lat, l_ctx):
    n = RWKV_WIDTH
    i = np.arange(n)
    bd = (i[:, None] // 64 == i[None, :] // 64).astype(np.float32)
    t_r, t_c = (i % 64)[:, None], (i % 64)[None, :]
    tri = np.zeros((2, 6, n, n), np.float32)
    for d in range(2):
        before = (t_c < t_r) if d == 0 else (t_c > t_r)
        tri[d, 0] = bd * before
        tri[d, 1] = bd * (before | (t_c == t_r))
        tri[d, 2] = bd * before * (t_r // 8 == t_c // 8)
        for lvl, blk in enumerate((8, 16, 32)):
            tri[d, 3 + lvl] = bd * before * (t_r // (2 * blk) == t_c // (2 * blk)) * (t_r // blk != t_c // blk)
    j = np.arange(CHUNK)
    csi = np.stack([(j[None, :] <= j[:, None]), (j[None, :] >= j[:, None])]).astype(np.float32)
    tt = np.arange(TM)
    ut = (tt[:, None] < tt[None, :]).astype(np.float32)
    ee = np.arange(N_EXPERTS)
    lt = (ee[None, :] < ee[:, None]).astype(np.float32)
    jj = np.arange(TM + 2 * POOL_HALO)[None, :]
    band = np.stack([((jj >= tt[:, None] + POOL_HALO - w // 2) & (jj <= tt[:, None] + POOL_HALO + w // 2 - 1))
                     for w in POOL_WINDOWS]).astype(np.float32)

    def counts(length):
        t = np.arange(length)[:, None]
        half = np.repeat(np.array(POOL_WINDOWS) // 2, 64)[None, :]
        return (np.clip(t + half, 0, length) - np.clip(t - half, 0, length)).astype(np.float32)

    cnt = np.concatenate([counts(l_lat), counts(l_ctx)], axis=0)
    return dict(bd=jnp.asarray(bd), eye=jnp.eye(n, dtype=f32), tri=jnp.asarray(tri), csi=jnp.asarray(csi),
                avg=jnp.asarray(bd / 64.0), ut=jnp.asarray(ut, dtype=bf16), lt=jnp.asarray(lt, dtype=bf16),
                band=jnp.asarray(band), cnt=jnp.asarray(cnt))


def _rope_tables(l_lat):
    rows = l_lat // GRID_W
    row = jnp.repeat(jnp.arange(rows, dtype=f32), GRID_W)
    col = jnp.tile(jnp.arange(GRID_W, dtype=f32), rows)
    n_freq = QK_ROPE // 4
    inv = ROPE_THETA ** (-jnp.arange(n_freq, dtype=f32) / n_freq)
    ang = jnp.concatenate([row[:, None] * inv, col[:, None] * inv], -1)
    cos = jnp.concatenate([jnp.cos(ang), jnp.ones((TM, 16), f32)], 0)
    sin = jnp.concatenate([jnp.sin(ang), jnp.zeros((TM, 16), f32)], 0)
    n = cos.shape[0]
    ct = jnp.concatenate([jnp.ones((n, QK_NOPE), f32), cos, cos, jnp.zeros((n, 32), f32)], 1)
    st = jnp.concatenate([jnp.zeros((n, QK_NOPE), f32), -sin, sin, jnp.zeros((n, 32), f32)], 1)
    qs = MLA_SCALE * LOG2E
    return jnp.stack([ct * qs, st * qs, ct, st])


def _layer_params(i, a):
    d = a["w_in"].shape[1]
    p = {}
    row = lambda v: v.reshape(1, -1).astype(f32)
    for name in ("pre_mix_g", "post_mix_g", "pre_ffn_g", "post_ffn_g", "mla_q_norm", "mla_kv_norm"):
        p[name] = row(a[name][i])
    w_in = a["w_in"][i]
    zc = lambda n: jnp.zeros((d, n), f32)
    p["w_in_p"] = jnp.concatenate(
        [w_in[:, 0:MLA_IN], zc(128 - QK_ROPE), w_in[:, MLA_IN:MLA_IN + POOL_WIDTH],
         w_in[:, MLA_IN + POOL_WIDTH:], zc(RWKV_PAD - RWKV_IN)], 1).astype(bf16)
    wq = a["mla_w_q_b"][i].reshape(Q_LORA, MLA_HEADS, QK_NOPE + QK_ROPE)
    zq = jnp.zeros((Q_LORA, MLA_HEADS, 32), f32)
    half = QK_ROPE // 2
    p["wq_p"] = jnp.concatenate([wq, zq], 2).reshape(Q_LORA, -1).astype(bf16)
    p["wq_s"] = jnp.concatenate([jnp.zeros_like(wq[:, :, :QK_NOPE]), wq[:, :, QK_NOPE + half:],
                                 wq[:, :, QK_NOPE:QK_NOPE + half], zq], 2).reshape(Q_LORA, -1).astype(bf16)
    wkv = a["mla_w_kv_b"][i].reshape(KV_LORA, MLA_HEADS, QK_NOPE + V_HEAD)
    p["wk_p"] = jnp.concatenate([wkv[:, :, :QK_NOPE], jnp.zeros((KV_LORA, MLA_HEADS, 64), f32)], 2
                                ).reshape(KV_LORA, -1).astype(bf16)
    p["wv_p"] = jnp.concatenate([wkv[:, :, QK_NOPE:], jnp.zeros((KV_LORA, MLA_HEADS, 64), f32)], 2
                                ).reshape(KV_LORA, -1).astype(bf16)
    e_p = np.zeros((128, MLA_HEADS * HEAD_PAD), np.float32)
    e_s = np.zeros_like(e_p)
    for h in range(MLA_HEADS):
        for j in range(QK_ROPE):
            e_p[j, h * HEAD_PAD + QK_NOPE + j] = 1.0
            e_s[(j + half) % QK_ROPE, h * HEAD_PAD + QK_NOPE + j] = 1.0
    p["e_p"], p["e_s"] = jnp.asarray(e_p, dtype=bf16), jnp.asarray(e_s, dtype=bf16)
    pw = a["pool_w"][i]
    p["pool_w_bd"] = jax.scipy.linalg.block_diag(*[pw[g] for g in range(pw.shape[0])]).astype(bf16)
    p["pool_scale"] = row(a["pool_scale"][i])
    p["mu_p"] = jnp.pad(a["rwkv_mu"][i], (0, RWKV_PAD - RWKV_IN)).reshape(1, -1)
    p["w0"] = a["rwkv_w0"][i].reshape(2, 1, RWKV_WIDTH)
    p["a0"] = a["rwkv_a0"][i].reshape(2, 1, RWKV_WIDTH)
    z32 = jnp.zeros((32, RWKV_WIDTH), f32)
    w2, a2 = a["rwkv_w2"][i], a["rwkv_a2"][i]
    p["w2_p"] = jnp.stack([jnp.concatenate([w2[0], z32, z32, z32]), jnp.concatenate([z32, w2[1], z32, z32])])
    p["a2_p"] = jnp.stack([jnp.concatenate([z32, z32, a2[0], z32]), jnp.concatenate([z32, z32, z32, a2[1]])])
    p["g2_p"] = jnp.concatenate([a["rwkv_g2"][i], jnp.zeros((64, RWKV_WIDTH), f32)])
    for name in ("k_k", "k_a", "r_k"):
        p[name] = row(a["rwkv_" + name][i])
    p["ln_w"], p["ln_b"] = row(a["rwkv_ln_w"][i]), row(a["rwkv_ln_b"][i])
    p["w_out"] = a["w_out"][i].astype(bf16)
    p["router_wt"] = a["router_w"][i].T.astype(bf16)
    p["router_b"] = a["router_bias"][i].reshape(-1, 1).astype(f32)
    p["wgu"] = jnp.concatenate([a["exp_w_gate"][i], a["exp_w_up"][i]], 2).astype(bf16)
    p["wd"] = a["exp_w_down"][i].astype(bf16)
    p["sh_wgu"] = jnp.concatenate([a["sh_w_gate"][i], a["sh_w_up"][i]], 1).astype(bf16)
    p["sh_wd"] = a["sh_w_down"][i].astype(bf16)
    return p


def kernel(x, c, ctx, c_ctx, ada_w, ada_b, pre_mix_g, post_mix_g, pre_ffn_g, post_ffn_g, w_in, w_out, mla_q_norm, mla_w_q_b, mla_kv_norm, mla_w_kv_b, pool_w, pool_scale, rwkv_mu, rwkv_w0, rwkv_w2, rwkv_a0, rwkv_a2, rwkv_g2, rwkv_k_k, rwkv_k_a, rwkv_r_k, rwkv_ln_w, rwkv_ln_b, router_w, router_bias, exp_w_gate, exp_w_up, exp_w_down, sh_w_gate, sh_w_up, sh_w_down):
    arrs = dict(pre_mix_g=pre_mix_g, post_mix_g=post_mix_g, pre_ffn_g=pre_ffn_g, post_ffn_g=post_ffn_g, w_in=w_in,
                w_out=w_out, mla_q_norm=mla_q_norm, mla_w_q_b=mla_w_q_b, mla_kv_norm=mla_kv_norm,
                mla_w_kv_b=mla_w_kv_b, pool_w=pool_w, pool_scale=pool_scale, rwkv_mu=rwkv_mu, rwkv_w0=rwkv_w0,
                rwkv_w2=rwkv_w2, rwkv_a0=rwkv_a0, rwkv_a2=rwkv_a2, rwkv_g2=rwkv_g2, rwkv_k_k=rwkv_k_k,
                rwkv_k_a=rwkv_k_a, rwkv_r_k=rwkv_r_k, rwkv_ln_w=rwkv_ln_w, rwkv_ln_b=rwkv_ln_b, router_w=router_w,
                router_bias=router_bias, exp_w_gate=exp_w_gate, exp_w_up=exp_w_up, exp_w_down=exp_w_down,
                sh_w_gate=sh_w_gate, sh_w_up=sh_w_up, sh_w_down=sh_w_down)
    b_sz, l_lat, d = x.shape
    l_ctx = ctx.shape[1]
    assert l_lat % TM == 0 and l_ctx % TM == 0 and l_lat % GRID_W == 0 and b_sz < 16
    lt, ct = l_lat // TM, l_ctx // TM
    nlat, nctx = b_sz * lt, b_sz * ct
    nall = nlat + nctx
    ncc, nlc = l_ctx // CHUNK, l_lat // CHUNK
    consts = _np_consts(l_lat, l_ctx)
    rope = _rope_tables(l_lat)
    rope_idx = lambda i: jnp.where(i < nlat, i % lt, lt)
    cnt_idx = lambda i: jnp.where(i < nlat, i % lt, lt + (i - nlat) % ct)

    c_all = jnp.zeros((16, d), f32).at[:b_sz].set(c).at[b_sz].set(c_ctx)
    mods = _ada_mod(c_all, ada_w, ada_b)
    tile_row = np.concatenate([np.repeat(np.arange(b_sz), lt), np.full(nctx, b_sz)])
    xa = jnp.concatenate([x.reshape(b_sz * l_lat, d), ctx.reshape(b_sz * l_ctx, d)], 0)

    for i in range(DEPTH):
        last = i == DEPTH - 1
        p = _layer_params(i, arrs)
        mt = jnp.pad(mods[i][tile_row].reshape(nall, 6, d), ((0, 0), (0, 2), (0, 0)))
        nt = nlat if last else nall
        q, k, v, zp, zr = _inproj(xa, mt, rope, rope_idx, p, nall)
        segs_lat = [(l_lat, min(512, l_lat), lambda t: t // lt), (l_ctx, l_ctx if l_ctx <= 512 else TM,
                                                                   lambda t: nlat * TM // l_ctx + t // lt)]
        att = _attention(q, k, v, nlat, 0, segs_lat)
        if not last:
            segs_ctx = [(l_ctx, l_ctx if l_ctx <= 512 else TM, lambda t: nlat * TM // l_ctx + t // ct)]
            att = jnp.concatenate([att, _attention(q, k, v, nctx, nlat, segs_ctx)], 0)
        py = _pool(zp, consts["band"], consts["cnt"], cnt_idx, p["pool_w_bd"], p["pool_scale"], nt, nlat, lt, ct)
        y, bv, g = _rwkv(zr, consts, p, b_sz, ncc, nlc)
        x1, hp = _outproj(att, py, y, bv, g, xa, mt, consts, p, nt)
        xa = _moe(hp, x1, mt, consts, p, nt)
    return xa[:b_sz * l_lat].reshape(b_sz, l_lat, d)
```

```python
import functools

import numpy as np
import jax
import jax.numpy as jnp
from jax import lax
from jax.experimental import pallas as pl
from jax.experimental.pallas import tpu as pltpu

f32 = jnp.float32
bf16 = jnp.bfloat16
HIGHEST = lax.Precision.HIGHEST

DEPTH = 2
GRID_W = 64
NORM_EPS = 1e-6
MLA_HEADS = 8
Q_LORA = 384
KV_LORA = 256
QK_NOPE = 64
QK_ROPE = 32
V_HEAD = 64
ROPE_THETA = 10000.0
MLA_SCALE = (QK_NOPE + QK_ROPE) ** -0.5
MLA_IN = Q_LORA + KV_LORA + QK_ROPE
HEAD_PAD = 128
POOL_WINDOWS = (2, 4, 8, 16)
POOL_WIDTH = 256
POOL_HALO = 8
RWKV_HEADS = 4
RWKV_HEAD = 64
RWKV_WIDTH = 256
RWKV_IN = 960
RWKV_PAD = 1024
RWKV_GN_EPS = 64e-5
CHUNK = 64
RWKV_BATCHES_PER_STEP = 2
N_EXPERTS = 64
TOP_K = 6
N_GROUPS = 8
TOPK_GROUPS = 4
ROUTED_SCALE = 2.5
EXPERT_FF = 256
TM = 256
MOE_BM = 512
Z_COLS = 2048
VMEM_LIMIT = 48 * 1024 * 1024
LOG2E = 1.4426950408889634
EXP_M05 = 0.6065306597126334


def _cparams(sem, vmem=VMEM_LIMIT):
    return pltpu.CompilerParams(dimension_semantics=sem, vmem_limit_bytes=vmem)


def _mm(a, b):
    return jnp.dot(a.astype(bf16), b.astype(bf16), preferred_element_type=f32)


def _mm_nt(a, b):
    return lax.dot_general(a.astype(bf16), b.astype(bf16), (((1,), (1,)), ((), ())), preferred_element_type=f32)


def _mm_tn(a, b):
    return lax.dot_general(a.astype(bf16), b.astype(bf16), (((0,), (0,)), ((), ())), preferred_element_type=f32)


def _mmf(a, b):
    return jnp.dot(a, b, precision=HIGHEST, preferred_element_type=f32)


def _split_hi_lo(a):
    hi = a.astype(bf16)
    return hi, (a - hi.astype(f32)).astype(bf16)


def _mm_x01(a, w01):
    hi, lo = _split_hi_lo(a)
    w = w01.astype(bf16)
    return jnp.dot(hi, w, preferred_element_type=f32) + jnp.dot(lo, w, preferred_element_type=f32)


def _mm_01x(w01, a):
    hi, lo = _split_hi_lo(a)
    w = w01.astype(bf16)
    return jnp.dot(w, hi, preferred_element_type=f32) + jnp.dot(w, lo, preferred_element_type=f32)


def _rms(x):
    return x * lax.rsqrt(jnp.mean(x * x, axis=-1, keepdims=True) + NORM_EPS)


def _sigmoid(x):
    return 1.0 / (1.0 + jnp.exp(-x))


def _ada_kernel(c_ref, w_ref, b_ref, o_ref):
    c = c_ref[...]
    s = c * _sigmoid(c)
    o_ref[0] = _mm(s, w_ref[0]) + b_ref[0]


def _ada_mod(c_all, ada_w, ada_b):
    depth, d, n = ada_w.shape
    tn = 1024
    return pl.pallas_call(
        _ada_kernel,
        out_shape=jax.ShapeDtypeStruct((depth, 16, n), f32),
        grid=(depth, n // tn),
        in_specs=[pl.BlockSpec((16, d), lambda i, j: (0, 0)),
                  pl.BlockSpec((1, d, tn), lambda i, j: (i, 0, j)),
                  pl.BlockSpec((1, 1, tn), lambda i, j: (i, 0, j))],
        out_specs=pl.BlockSpec((1, 16, tn), lambda i, j: (i, 0, j)),
        compiler_params=_cparams(("arbitrary", "arbitrary")),
        name="ada_mod",
    )(c_all, ada_w, ada_b.reshape(depth, 1, n))


def _inproj_kernel(x_ref, m_ref, rope_ref, g_ref, win_ref, qg_ref, wq_ref, wqs_ref, kvg_ref, wk_ref, wv_ref,
                   e_ref, es_ref, q_ref, k_ref, v_ref, zp_ref, zr_ref):
    m = m_ref[0]
    h = _rms(x_ref[...]) * g_ref[...]
    h = h * (1.0 + m[1:2]) + m[0:1]
    z = _mm(h, win_ref[...])
    zp_ref[...] = z[:, 768:1024]
    for c4 in range(TM // CHUNK):
        zr_ref[c4, 0] = z[c4 * CHUNK:(c4 + 1) * CHUNK, 1024:2048]
    tile8 = lambda t: jnp.concatenate([t] * MLA_HEADS, axis=1)
    qn = (_rms(z[:, 0:Q_LORA]) * qg_ref[...]).astype(bf16)
    q = _mm(qn, wq_ref[...]) * tile8(rope_ref[0]) + _mm(qn, wqs_ref[...]) * tile8(rope_ref[1])
    q_ref[...] = q.astype(bf16)
    kvn = (_rms(z[:, Q_LORA:Q_LORA + KV_LORA]) * kvg_ref[...]).astype(bf16)
    kpe = z[:, 640:768].astype(bf16)
    k = _mm(kvn, wk_ref[...]) + _mm(kpe, e_ref[...]) * tile8(rope_ref[2]) + _mm(kpe, es_ref[...]) * tile8(rope_ref[3])
    k_ref[...] = k.astype(bf16)
    lane = lax.broadcasted_iota(jnp.int32, (1, MLA_HEADS * HEAD_PAD), 1)
    v_ref[...] = (_mm(kvn, wv_ref[...]) + jnp.where(lane % HEAD_PAD == V_HEAD, 1.0, 0.0)).astype(bf16)


def _inproj(xa, mt, rope, rope_idx, chunk_idx, npos, b_sz, p, nt):
    t_all, d = xa.shape
    cpt = TM // CHUNK
    full = lambda a: pl.BlockSpec(a.shape, lambda i: (0,) * a.ndim)
    ws = [p["pre_mix_g"], p["w_in_p"], p["mla_q_norm"], p["wq_p"], p["wq_s"], p["mla_kv_norm"], p["wk_p"], p["wv_p"],
          p["e_p"], p["e_s"]]
    return pl.pallas_call(
        _inproj_kernel,
        out_shape=(jax.ShapeDtypeStruct((t_all, MLA_HEADS * HEAD_PAD), bf16),
                   jax.ShapeDtypeStruct((t_all, MLA_HEADS * HEAD_PAD), bf16),
                   jax.ShapeDtypeStruct((t_all, MLA_HEADS * HEAD_PAD), bf16),
                   jax.ShapeDtypeStruct((t_all, POOL_WIDTH), f32),
                   jax.ShapeDtypeStruct((npos, b_sz, CHUNK, RWKV_PAD), f32)),
        grid=(nt,),
        in_specs=[pl.BlockSpec((TM, d), lambda i: (i, 0)),
                  pl.BlockSpec((1, 8, d), lambda i: (i, 0, 0)),
                  pl.BlockSpec((4, TM, HEAD_PAD), lambda i: (0, rope_idx(i), 0))] + [full(w) for w in ws],
        out_specs=(pl.BlockSpec((TM, 1024), lambda i: (i, 0)),
                   pl.BlockSpec((TM, 1024), lambda i: (i, 0)),
                   pl.BlockSpec((TM, 1024), lambda i: (i, 0)),
                   pl.BlockSpec((TM, POOL_WIDTH), lambda i: (i, 0)),
                   pl.BlockSpec((cpt, 1, CHUNK, RWKV_PAD), lambda i: chunk_idx(i) + (0, 0))),
        compiler_params=_cparams(("arbitrary",)),
        name="in_proj",
    )(xa, mt, rope, *ws)


def _attn_kernel(*refs, seg_tiles):
    n_in = 1 + 2 * len(seg_tiles)
    q_ref, kv, o_ref, m_ref, acc_ref = refs[0], refs[1:n_in], refs[n_in], refs[n_in + 1], refs[n_in + 2]
    m_ref[...] = jnp.full(m_ref.shape, -jnp.inf, f32)
    acc_ref[...] = jnp.zeros(acc_ref.shape, f32)

    def step(k_ref, v_ref, r0, tk):
        for h in range(MLA_HEADS):
            hs = slice(h * HEAD_PAD, (h + 1) * HEAD_PAD)
            s = lax.dot_general(q_ref[:, hs], k_ref[pl.ds(r0, tk), hs], (((1,), (1,)), ((), ())),
                                preferred_element_type=f32)
            m_prev = m_ref[h]
            m_new = jnp.maximum(m_prev, jnp.max(s, axis=-1, keepdims=True))
            pr = jnp.exp2(s - jnp.concatenate([m_new] * (tk // HEAD_PAD), axis=1))
            acc_ref[h] = (jnp.exp2(m_prev - m_new) * acc_ref[h]
                          + jnp.dot(pr.astype(bf16), v_ref[pl.ds(r0, tk), hs], preferred_element_type=f32))
            m_ref[h] = m_new

    for si, (nk, tk) in enumerate(seg_tiles):
        k_ref, v_ref = kv[2 * si], kv[2 * si + 1]
        if nk == 1:
            step(k_ref, v_ref, 0, tk)
        else:
            def body(j, c, k_ref=k_ref, v_ref=v_ref, tk=tk):
                step(k_ref, v_ref, pl.multiple_of(j * tk, tk), tk)
                return c
            lax.fori_loop(0, nk, body, 0)
    for pr2 in range(MLA_HEADS // 2):
        a0, a1 = acc_ref[2 * pr2], acc_ref[2 * pr2 + 1]
        o_ref[:, pr2 * 128:(pr2 + 1) * 128] = jnp.concatenate(
            [a0[:, :V_HEAD] / a0[:, V_HEAD:V_HEAD + 1], a1[:, :V_HEAD] / a1[:, V_HEAD:V_HEAD + 1]], axis=1).astype(bf16)


def _attention(q, k, v, n_q_tiles, q_tile0, segs):
    in_specs = [pl.BlockSpec((TM, 1024), lambda i: (i + q_tile0, 0))]
    args = [q]
    seg_tiles = []
    for rows, tk, bidx in segs:
        in_specs.append(pl.BlockSpec((rows, 1024), lambda i, bidx=bidx: (bidx(i), 0), pipeline_mode=pl.Buffered(1)))
        in_specs.append(pl.BlockSpec((rows, 1024), lambda i, bidx=bidx: (bidx(i), 0), pipeline_mode=pl.Buffered(1)))
        args += [k, v]
        seg_tiles.append((rows // tk, tk))
    return pl.pallas_call(
        functools.partial(_attn_kernel, seg_tiles=tuple(seg_tiles)),
        out_shape=jax.ShapeDtypeStruct((n_q_tiles * TM, MLA_HEADS * V_HEAD), bf16),
        grid=(n_q_tiles,),
        in_specs=in_specs,
        out_specs=pl.BlockSpec((TM, 512), lambda i: (i, 0)),
        scratch_shapes=[pltpu.VMEM((MLA_HEADS, TM, HEAD_PAD), f32), pltpu.VMEM((MLA_HEADS, TM, HEAD_PAD), f32)],
        compiler_params=_cparams(("arbitrary",)),
        name="mla_attention",
    )(*args)


def _pool_kernel(z_ref, zp_ref, zn_ref, band_ref, cnt_ref, pw_ref, ps_ref, o_ref, *, nlat, lt, ct):
    i = pl.program_id(0)
    is_lat = i < nlat
    j = jnp.where(is_lat, i % lt, (i - nlat) % ct)
    n = jnp.where(is_lat, lt, ct)
    z = z_ref[...]
    prev = zp_ref[...] * jnp.where(j == 0, 0.0, 1.0)
    nxt = zn_ref[...] * jnp.where(j == n - 1, 0.0, 1.0)
    zh = jnp.concatenate([prev, z, nxt], axis=0)
    lane_grp = lax.broadcasted_iota(jnp.int32, (1, POOL_WIDTH), 1) // 64
    tot = jnp.zeros_like(z)
    for g in range(len(POOL_WINDOWS)):
        tot = tot + _mm_01x(band_ref[g], zh * jnp.where(lane_grp == g, 1.0, 0.0))
    diff = tot / cnt_ref[...] - z
    o_ref[...] = (_mm(diff, pw_ref[...]) * ps_ref[...]).astype(bf16)


def _pool(zp, band, cnt, cnt_idx, pw_bd, pscale, nt, nlat, lt, ct):
    t_all = zp.shape[0]
    nb8 = t_all // POOL_HALO
    r = TM // POOL_HALO
    return pl.pallas_call(
        functools.partial(_pool_kernel, nlat=nlat, lt=lt, ct=ct),
        out_shape=jax.ShapeDtypeStruct((nt * TM, POOL_WIDTH), bf16),
        grid=(nt,),
        in_specs=[pl.BlockSpec((TM, POOL_WIDTH), lambda i: (i, 0)),
                  pl.BlockSpec((POOL_HALO, POOL_WIDTH), lambda i: (jnp.maximum(i * r - 1, 0), 0)),
                  pl.BlockSpec((POOL_HALO, POOL_WIDTH), lambda i: (jnp.minimum((i + 1) * r, nb8 - 1), 0)),
                  pl.BlockSpec(band.shape, lambda i: (0, 0, 0)),
                  pl.BlockSpec((TM, POOL_WIDTH), lambda i: (cnt_idx(i), 0)),
                  pl.BlockSpec((POOL_WIDTH, POOL_WIDTH), lambda i: (0, 0)),
                  pl.BlockSpec((1, POOL_WIDTH), lambda i: (0, 0))],
        out_specs=pl.BlockSpec((TM, POOL_WIDTH), lambda i: (i, 0)),
        compiler_params=_cparams(("arbitrary",)),
        name="pool_mixer",
    )(zp, zp, zp, band, cnt, pw_bd, pscale)


def _rwkv_pos(d, s, *, ncc, nlc):
    in_ctx = s < ncc
    jc = jnp.where(d == 0, s, ncc - 1 - s)
    jl = jnp.where(d == 0, s - ncc, nlc - 1 - (s - ncc))
    pos = jnp.where(in_ctx, nlc + jc, jl)
    first = jnp.where(in_ctx, jc == 0, jl == 0)
    last = jnp.where(in_ctx, jc == ncc - 1, jl == nlc - 1)
    return pos, first, last


def _tri_inverse(lm, tri_ref, eye):
    n = -(lm * tri_ref[0, 2])
    n2 = _mm(n, n)
    n4 = _mm(n2, n2)
    t = _mm(_mm(eye + n, eye + n2), eye + n4)
    for lvl in range(3):
        t = t - _mm(_mm(t, lm * tri_ref[0, 3 + lvl]), t)
    return t


def _rwkv_kernel(z_ref, zp_ref, zn_ref, bd_ref, eye_ref, tri_ref, csi_ref, mu_ref, w0_ref, w2_ref, a0_ref, a2_ref,
                 g2_ref, kk_ref, ka_ref, rk_ref, y_ref, bv_ref, g_ref, s_ref, *, ncc, nlc):
    d, s = pl.program_id(0), pl.program_id(2)
    _, first, last = _rwkv_pos(d, s, ncc=ncc, nlc=nlc)

    @pl.when(s == 0)
    def _():
        s_ref[...] = jnp.zeros_like(s_ref)

    for nb in range(z_ref.shape[1]):
        prev_row = zp_ref[0, nb, 7:8, :] * jnp.where(first, 0.0, 1.0)
        next_row = zn_ref[0, nb, 0:1, :] * jnp.where(last, 0.0, 1.0)
        y, bv, g, st = _rwkv_chunk(z_ref[0, nb], prev_row, next_row, s_ref[nb], d, bd_ref, eye_ref, tri_ref, csi_ref,
                                   mu_ref, w0_ref, w2_ref, a0_ref, a2_ref, g2_ref, kk_ref, ka_ref, rk_ref)
        y_ref[0, 0, nb] = y
        bv_ref[0, 0, nb] = bv
        g_ref[0, 0, nb] = g
        s_ref[nb] = st


def _rwkv_chunk(z, prev_row, next_row, st, d, bd_ref, eye_ref, tri_ref, csi_ref, mu_ref, w0_ref, w2_ref, a0_ref,
                a2_ref, g2_ref, kk_ref, ka_ref, rk_ref):
    c = CHUNK
    row = lax.broadcasted_iota(jnp.int32, (c, 1), 0)
    zp = jnp.where(row == 0, prev_row, pltpu.roll(z, 1, 0))
    zn = jnp.where(row == c - 1, next_row, pltpu.roll(z, c - 1, 0))
    zs = z + mu_ref[...] * (0.5 * (zp + zn) - z)
    r, k, v = zs[:, 0:256], zs[:, 256:512], zs[:, 512:768]
    lora, gate_in = zs[:, 768:896], zs[:, 896:1024]
    bd = bd_ref[...]
    eye = eye_ref[...]

    g = _mm(_sigmoid(gate_in), g2_ref[...])
    e = EXP_M05 * _sigmoid(w0_ref[0] + _mm(jnp.tanh(lora), w2_ref[0]))
    a = _sigmoid(a0_ref[0] + _mm(lora, a2_ref[0]))
    kd = k * (1.0 + (a - 1.0) * ka_ref[...])
    kkr = k * kk_ref[...]
    kk = kkr / jnp.maximum(jnp.sqrt(_mm_x01(kkr * kkr, bd)), 1e-12)
    bv = _mm_x01(r * kd * rk_ref[...], bd) * v
    bb = kk * a

    cs = _mm_01x(csi_ref[0], e)
    tot = jnp.where(d == 0, cs[c - 1:c, :], cs[0:1, :])
    a_s = jnp.exp(e - cs) * kk
    b_s = bb * jnp.exp(cs)
    k_s = kd * jnp.exp(cs)
    r_s = r * jnp.exp(-cs)
    b_e = bb * jnp.exp(cs - tot)
    k_e = kd * jnp.exp(cs - tot)
    g_end = jnp.exp(-tot)

    rep4 = lambda t: jnp.concatenate([t] * RWKV_HEADS, axis=0)
    fold4 = lambda t: t[0:c] + t[c:2 * c] + t[2 * c:3 * c] + t[3 * c:4 * c]
    a4, r4, v4 = rep4(a_s) * bd, rep4(r_s) * bd, rep4(v) * bd
    b4, k4 = rep4(b_s), rep4(k_s)
    tri_s, tri_i = tri_ref[0, 0], tri_ref[0, 1]
    lm = _mm_nt(a4, b4) * tri_s
    akm = _mm_nt(a4, k4) * tri_s
    rbm = _mm_nt(r4, b4) * tri_i
    rkm = _mm_nt(r4, k4) * tri_i
    t = _tri_inverse(lm, tri_ref, eye)
    w4 = _mm(t, a4)
    u4 = _mm(t, _mm(akm, v4))
    q4 = r4 - _mm(rbm, w4)
    y4 = _mm(rkm, v4) - _mm(rbm, u4)
    w_all, u_all, q_all, y0 = fold4(w4), fold4(u4), fold4(q4), fold4(y4)
    g_bd = eye * g_end - bd * _mm_tn(w_all, b_e)
    h_bd = bd * (_mm_tn(v, k_e) - _mm_tn(u_all, b_e))
    return _mm_nt(q_all, st) + y0, bv, g, _mm(st, g_bd) + h_bd


def _rwkv(zr, consts, p, ncc, nlc):
    npos, b_sz = zr.shape[0], zr.shape[1]
    nbat = RWKV_BATCHES_PER_STEP
    kw = dict(ncc=ncc, nlc=nlc)
    pos = lambda d, s: _rwkv_pos(d, s, **kw)[0]
    full = lambda a: pl.BlockSpec(a.shape, lambda d, b, s: (0,) * a.ndim)
    by_dir = lambda a: pl.BlockSpec((1,) + a.shape[1:], lambda d, b, s: (d,) + (0,) * (a.ndim - 1))
    out = jax.ShapeDtypeStruct((2, npos, b_sz, CHUNK, RWKV_WIDTH), f32)
    ospec = pl.BlockSpec((1, 1, nbat, CHUNK, RWKV_WIDTH), lambda d, b, s: (d, pos(d, s), b, 0, 0))
    last8 = CHUNK // 8 - 1
    return pl.pallas_call(
        functools.partial(_rwkv_kernel, **kw),
        out_shape=(out, out, out),
        grid=(2, b_sz // nbat, npos),
        in_specs=[pl.BlockSpec((1, nbat, CHUNK, RWKV_PAD), lambda d, b, s: (pos(d, s), b, 0, 0)),
                  pl.BlockSpec((1, nbat, 8, RWKV_PAD), lambda d, b, s: (jnp.maximum(pos(d, s) - 1, 0), b, last8, 0)),
                  pl.BlockSpec((1, nbat, 8, RWKV_PAD), lambda d, b, s: (jnp.minimum(pos(d, s) + 1, npos - 1), b, 0, 0)),
                  full(consts["bd"]), full(consts["eye"]), by_dir(consts["tri"]), by_dir(consts["csi"]),
                  full(p["mu_p"]), by_dir(p["w0"]), by_dir(p["w2_p"]), by_dir(p["a0"]), by_dir(p["a2_p"]),
                  full(p["g2_p"]), full(p["k_k"]), full(p["k_a"]), full(p["r_k"])],
        out_specs=(ospec, ospec, ospec),
        scratch_shapes=[pltpu.VMEM((nbat, RWKV_WIDTH, RWKV_WIDTH), f32)],
        compiler_params=_cparams(("arbitrary", "arbitrary", "arbitrary")),
        name="rwkv7_chunked",
    )(zr, zr, zr, consts["bd"], consts["eye"], consts["tri"], consts["csi"], p["mu_p"], p["w0"], p["w2_p"], p["a0"],
      p["a2_p"], p["g2_p"], p["k_k"], p["k_a"], p["r_k"])


def _pack_bf16_pairs(lo, hi):
    lo_b = pltpu.bitcast(lo.astype(bf16).astype(f32), jnp.uint32)
    hi_b = pltpu.bitcast(hi.astype(bf16).astype(f32), jnp.uint32)
    return (hi_b & jnp.uint32(0xFFFF0000)) | (lo_b >> 16)


def _unpack_bf16_pairs(w):
    lo = pltpu.bitcast(w << 16, f32).astype(bf16)
    hi = pltpu.bitcast(w & jnp.uint32(0xFFFF0000), f32).astype(bf16)
    return lo, hi


def _outproj_kernel(a_ref, py_ref, y_ref, bv_ref, g_ref, x_ref, m_ref, avg_ref, lnw_ref, lnb_ref, wo_ref, pmg_ref,
                    pfg_ref, x1_ref, hp_ref):
    m = m_ref[0]
    rows = lambda ref, dd: jnp.concatenate([ref[dd, c4, 0] for c4 in range(TM // CHUNK)], axis=0)
    ysum = rows(y_ref, 0) + rows(y_ref, 1)
    avg = avg_ref[...]
    dev = ysum - _mm_x01(ysum, avg)
    var = _mm_x01(dev * dev, avg)
    yn = dev * lax.rsqrt(var + RWKV_GN_EPS) * lnw_ref[...] + lnb_ref[...]
    rw = (yn + rows(bv_ref, 0) + rows(bv_ref, 1)) * rows(g_ref, 0)
    o = (jnp.dot(a_ref[...], wo_ref[0:512, :], preferred_element_type=f32)
         + jnp.dot(py_ref[...], wo_ref[512:768, :], preferred_element_type=f32)
         + _mm(rw, wo_ref[768:1024, :]))
    x1 = x_ref[...] + m[2:3] * (_rms(o) * pmg_ref[...])
    x1_ref[...] = x1
    h = (_rms(x1) * pfg_ref[...]) * (1.0 + m[4:5]) + m[3:4]
    hp_ref[...] = _pack_bf16_pairs(h[:, 0:512], h[:, 512:1024])


def _outproj(att, py, y, bv, g, xa, mt, chunk_idx, consts, p, nt):
    d = xa.shape[1]
    cpt = TM // CHUNK
    cspec = lambda nd: pl.BlockSpec((nd, cpt, 1, CHUNK, RWKV_WIDTH), lambda i: (0,) + chunk_idx(i) + (0, 0))
    full = lambda a: pl.BlockSpec(a.shape, lambda i: (0,) * a.ndim)
    ws = [consts["avg"], p["ln_w"], p["ln_b"], p["w_out"], p["post_mix_g"], p["pre_ffn_g"]]
    return pl.pallas_call(
        _outproj_kernel,
        out_shape=(jax.ShapeDtypeStruct((nt * TM, d), f32), jax.ShapeDtypeStruct((nt * TM, d // 2), jnp.uint32)),
        grid=(nt,),
        in_specs=[pl.BlockSpec((TM, 512), lambda i: (i, 0)),
                  pl.BlockSpec((TM, POOL_WIDTH), lambda i: (i, 0)),
                  cspec(2), cspec(2), cspec(1),
                  pl.BlockSpec((TM, d), lambda i: (i, 0)),
                  pl.BlockSpec((1, 8, d), lambda i: (i, 0, 0))] + [full(w) for w in ws],
        out_specs=(pl.BlockSpec((TM, d), lambda i: (i, 0)), pl.BlockSpec((TM, d // 2), lambda i: (i, 0))),
        compiler_params=_cparams(("arbitrary",)),
        name="out_proj",
    )(att, py, y, bv, g, xa, mt, *ws)


def _router_kernel(hp_ref, rw_ref, rb_ref, ut_ref, lt_ref, ei_ref, pos_ref, gt_ref, cnt_ref, run_ref):
    i = pl.program_id(0)

    @pl.when(i == 0)
    def _():
        run_ref[...] = jnp.zeros_like(run_ref)

    tm = hp_ref.shape[0]
    ne, ng = N_EXPERTS, N_GROUPS
    pg = ne // ng
    lo, hi = _unpack_bf16_pairs(hp_ref[...])
    logits = (lax.dot_general(rw_ref[:, 0:512], lo, (((1,), (1,)), ((), ())), preferred_element_type=f32)
              + lax.dot_general(rw_ref[:, 512:1024], hi, (((1,), (1,)), ((), ())), preferred_element_type=f32))
    scores = _sigmoid(logits)
    sel = scores + rb_ref[...]
    neg = -jnp.inf

    s3 = sel.reshape(ng, pg, tm)
    io = lax.broadcasted_iota(jnp.int32, (ng, pg, tm), 1)
    m1 = jnp.max(s3, axis=1, keepdims=True)
    i1 = jnp.min(jnp.where(s3 == m1, io, pg), axis=1, keepdims=True)
    m2 = jnp.max(jnp.where(io == i1, neg, s3), axis=1, keepdims=True)
    gs = (m1 + m2).reshape(ng, tm)
    gi = lax.broadcasted_iota(jnp.int32, (ng, tm), 0)
    grank = jnp.zeros((ng, tm), f32)
    for j in range(ng):
        rj = gs[j:j + 1, :]
        grank = grank + jnp.where((rj > gs) | ((rj == gs) & (j < gi)), 1.0, 0.0)
    gsel = jnp.where(grank < TOPK_GROUPS, 1.0, 0.0)
    gsel3 = jnp.broadcast_to(gsel.reshape(ng, 1, tm), (ng, pg, tm)).reshape(ne, tm)
    msk = jnp.where(gsel3 > 0.5, sel, neg)
    ei = lax.broadcasted_iota(jnp.int32, (ne, tm), 0)
    erank = jnp.zeros((ne, tm), f32)
    for j in range(ne):
        rj = msk[j:j + 1, :]
        erank = erank + jnp.where((rj > msk) | ((rj == msk) & (j < ei)), 1.0, 0.0)
    chosen = erank < TOP_K
    chf = jnp.where(chosen, 1.0, 0.0)
    graw = jnp.where(chosen, scores, 0.0)
    gate = graw / jnp.sum(graw, axis=0, keepdims=True) * ROUTED_SCALE

    pos = run_ref[...] + _mm(chf, ut_ref[...])
    tot = jnp.sum(chf, axis=1, keepdims=True)
    run_new = run_ref[...] + tot
    run_ref[...] = run_new
    cnt_ref[...] = run_new[:, 0:128]
    rk = _mm(lt_ref[...], chf)
    eif = ei.astype(f32)
    rows_e, rows_p, rows_g = [], [], []
    for kq in range(TOP_K):
        mk = chosen & (rk == float(kq))
        rows_e.append(jnp.sum(jnp.where(mk, eif, 0.0), axis=0, keepdims=True))
        rows_p.append(jnp.sum(jnp.where(mk, pos, 0.0), axis=0, keepdims=True))
        rows_g.append(jnp.sum(jnp.where(mk, gate, 0.0), axis=0, keepdims=True))
    zrow = jnp.zeros((8 - TOP_K, tm), f32)
    ei_ref[0] = jnp.concatenate(rows_e + [zrow], axis=0).astype(jnp.int32)
    pos_ref[0] = jnp.concatenate(rows_p + [zrow], axis=0).astype(jnp.int32)
    gpad = jnp.concatenate(rows_g + [jnp.zeros((128 - TOP_K, tm), f32)], axis=0)
    gt_ref[...] = gpad.T


def _router(hp, rwt, rb, consts, nt):
    full = lambda a: pl.BlockSpec(a.shape, lambda i: (0,) * a.ndim)
    return pl.pallas_call(
        _router_kernel,
        out_shape=(jax.ShapeDtypeStruct((nt, 8, TM), jnp.int32), jax.ShapeDtypeStruct((nt, 8, TM), jnp.int32),
                   jax.ShapeDtypeStruct((nt * TM, 128), f32), jax.ShapeDtypeStruct((N_EXPERTS, 128), f32)),
        grid=(nt,),
        in_specs=[pl.BlockSpec((TM, 512), lambda i: (i, 0)), full(rwt), full(rb), full(consts["ut"]),
                  full(consts["lt"])],
        out_specs=(pl.BlockSpec((1, 8, TM), lambda i: (i, 0, 0)), pl.BlockSpec((1, 8, TM), lambda i: (i, 0, 0)),
                   pl.BlockSpec((TM, 128), lambda i: (i, 0)), pl.BlockSpec((N_EXPERTS, 128), lambda i: (0, 0))),
        scratch_shapes=[pltpu.VMEM((N_EXPERTS, TM), f32)],
        compiler_params=_cparams(("arbitrary",)),
        name="moe_router",
    )(hp, rwt, rb, consts["ut"], consts["lt"])


def _row_copy(src_ref, src_row, dst_ref, dst_row, sem):
    return pltpu.make_async_copy(src_ref.at[pl.ds(src_row, 1)], dst_ref.at[pl.ds(dst_row, 1)], sem)


def _dispatch_kernel(ps_ref, ei_ref, pos_ref, hp_ref, xz_ref, xs_ref, dest_ref, ei_s, pos_s, dest_s, isem, rsem):
    del xz_ref
    i = pl.program_id(0)
    c1 = pltpu.make_async_copy(ei_ref.at[i], ei_s, isem.at[0])
    c2 = pltpu.make_async_copy(pos_ref.at[i], pos_s, isem.at[1])
    c1.start()
    c2.start()
    c1.wait()
    c2.wait()

    def issue(t, c):
        for kq in range(TOP_K):
            slot = ps_ref[ei_s[kq * TM + t]] + pos_s[kq * TM + t]
            dest_s[kq * TM + t] = slot
            _row_copy(hp_ref, t, xs_ref, slot, rsem).start()
        return c

    def drain(t, c):
        for kq in range(TOP_K):
            _row_copy(hp_ref, 0, xs_ref, 0, rsem).wait()
        return c

    @pl.when(i == 0)
    def _():
        def zero(j, c):
            dest_s[j] = 0
            return c
        lax.fori_loop(TOP_K * TM, 8 * TM, zero, 0)

    lax.fori_loop(0, TM, issue, 0)
    c3 = pltpu.make_async_copy(dest_s, dest_ref.at[i], isem.at[0])
    c3.start()
    lax.fori_loop(0, TM, drain, 0)
    c3.wait()


def _dispatch(pstart, ei, pos, hp, n_slots, nt):
    xz = jnp.zeros((n_slots, hp.shape[1]), jnp.uint32)
    anyspec = pl.BlockSpec(memory_space=pl.ANY)
    return pl.pallas_call(
        _dispatch_kernel,
        out_shape=(jax.ShapeDtypeStruct(xz.shape, xz.dtype), jax.ShapeDtypeStruct((nt, 8 * TM), jnp.int32)),
        grid_spec=pltpu.PrefetchScalarGridSpec(
            num_scalar_prefetch=1,
            grid=(nt,),
            in_specs=[anyspec, anyspec, pl.BlockSpec((TM, hp.shape[1]), lambda i, ps: (i, 0)), anyspec],
            out_specs=(anyspec, anyspec),
            scratch_shapes=[pltpu.SMEM((8 * TM,), jnp.int32), pltpu.SMEM((8 * TM,), jnp.int32),
                            pltpu.SMEM((8 * TM,), jnp.int32), pltpu.SemaphoreType.DMA((2,)),
                            pltpu.SemaphoreType.DMA]),
        input_output_aliases={4: 0},
        compiler_params=_cparams(("arbitrary",)),
        name="moe_dispatch",
    )(pstart, ei, pos, hp, xz)


def _expert_kernel(be_ref, nb_ref, xs_ref, wgu_ref, wd_ref, ys_ref):
    i = pl.program_id(0)

    @pl.when(i < nb_ref[0])
    def _():
        lo, hi = _unpack_bf16_pairs(xs_ref[...])
        gu = (jnp.dot(lo, wgu_ref[0, 0:512, :], preferred_element_type=f32)
              + jnp.dot(hi, wgu_ref[0, 512:1024, :], preferred_element_type=f32))
        gg, uu = gu[:, 0:EXPERT_FF], gu[:, EXPERT_FF:2 * EXPERT_FF]
        act = gg * _sigmoid(gg) * uu
        ys_ref[...] = _mm(act, wd_ref[0])

    @pl.when(i >= nb_ref[0])
    def _():
        ys_ref[...] = jnp.zeros_like(ys_ref)


def _experts(block_expert, nb_used, xs, wgu, wd):
    n_slots = xs.shape[0]
    d = wd.shape[2]
    return pl.pallas_call(
        _expert_kernel,
        out_shape=jax.ShapeDtypeStruct((n_slots, d), f32),
        grid_spec=pltpu.PrefetchScalarGridSpec(
            num_scalar_prefetch=2,
            grid=(n_slots // MOE_BM,),
            in_specs=[pl.BlockSpec((MOE_BM, xs.shape[1]), lambda i, be, nb: (i, 0)),
                      pl.BlockSpec((1,) + wgu.shape[1:], lambda i, be, nb: (be[i], 0, 0)),
                      pl.BlockSpec((1,) + wd.shape[1:], lambda i, be, nb: (be[i], 0, 0))],
            out_specs=pl.BlockSpec((MOE_BM, d), lambda i, be, nb: (i, 0))),
        compiler_params=_cparams(("arbitrary",)),
        name="moe_experts",
    )(block_expert, nb_used, xs, wgu, wd)


def _combine_kernel(dest_ref, ys_ref, hp_ref, gt_ref, x1_ref, m_ref, wsgu_ref, wsd_ref, pg_ref, o_ref,
                    idx_ref, buf_ref, isem, rsem):
    i = pl.program_id(0)
    cp = pltpu.make_async_copy(dest_ref.at[i], idx_ref, isem)
    cp.start()
    cp.wait()

    def issue(t, c):
        for kq in range(TOP_K):
            _row_copy(ys_ref, idx_ref[kq * TM + t], buf_ref.at[kq], t, rsem).start()
        return c

    def drain(t, c):
        for kq in range(TOP_K):
            _row_copy(ys_ref, 0, buf_ref.at[kq], 0, rsem).wait()
        return c

    lax.fori_loop(0, TM, issue, 0)
    lo, hi = _unpack_bf16_pairs(hp_ref[...])
    gu = (jnp.dot(lo, wsgu_ref[0:512, :], preferred_element_type=f32)
          + jnp.dot(hi, wsgu_ref[512:1024, :], preferred_element_type=f32))
    gg, uu = gu[:, 0:EXPERT_FF], gu[:, EXPERT_FF:2 * EXPERT_FF]
    f = _mm(gg * _sigmoid(gg) * uu, wsd_ref[...])
    lax.fori_loop(0, TM, drain, 0)
    gt = gt_ref[...]
    routed = jnp.zeros_like(f)
    for kq in range(TOP_K):
        routed = routed + buf_ref[kq] * gt[:, kq:kq + 1]
    f = routed + f
    m = m_ref[0]
    o_ref[...] = x1_ref[...] + m[5:6] * (_rms(f) * pg_ref[...])


def _combine(dest, ys, hp, gt, x1, mt, p, nt):
    d = x1.shape[1]
    anyspec = pl.BlockSpec(memory_space=pl.ANY)
    full = lambda a: pl.BlockSpec(a.shape, lambda i: (0,) * a.ndim)
    ws = [p["sh_wgu"], p["sh_wd"], p["post_ffn_g"]]
    return pl.pallas_call(
        _combine_kernel,
        out_shape=jax.ShapeDtypeStruct((nt * TM, d), f32),
        grid=(nt,),
        in_specs=[anyspec, anyspec,
                  pl.BlockSpec((TM, d // 2), lambda i: (i, 0)),
                  pl.BlockSpec((TM, 128), lambda i: (i, 0)),
                  pl.BlockSpec((TM, d), lambda i: (i, 0)),
                  pl.BlockSpec((1, 8, d), lambda i: (i, 0, 0))] + [full(w) for w in ws],
        out_specs=pl.BlockSpec((TM, d), lambda i: (i, 0)),
        scratch_shapes=[pltpu.SMEM((8 * TM,), jnp.int32), pltpu.VMEM((TOP_K, TM, d), f32),
                        pltpu.SemaphoreType.DMA, pltpu.SemaphoreType.DMA],
        compiler_params=_cparams(("arbitrary",)),
        name="moe_combine",
    )(dest, ys, hp, gt, x1, mt, *ws)


def _moe(hp, x1, mt, consts, p, nt):
    t = nt * TM
    ei, pos, gt, cnt = _router(hp, p["router_wt"], p["router_b"], consts, nt)
    n_assign = t * TOP_K
    n_blocks = -(-(n_assign + N_EXPERTS * (MOE_BM - 1)) // MOE_BM)
    n_slots = n_blocks * MOE_BM
    counts = cnt[:, 0].astype(jnp.int32)
    padded = (counts + MOE_BM - 1) // MOE_BM * MOE_BM
    pend = jnp.cumsum(padded)
    pstart = pend - padded
    nb_used = (pend[-1:] // MOE_BM).astype(jnp.int32)
    block_row0 = jnp.arange(n_blocks, dtype=jnp.int32) * MOE_BM
    block_expert = jnp.minimum(jnp.sum((pend[None, :] <= block_row0[:, None]).astype(jnp.int32), axis=1),
                               N_EXPERTS - 1)
    xs, dest = _dispatch(pstart.astype(jnp.int32), ei.reshape(nt, 8 * TM), pos.reshape(nt, 8 * TM), hp, n_slots, nt)
    ys = _experts(block_expert, nb_used, xs, p["wgu"], p["wd"])
    return _combine(dest, ys, hp, gt, x1, mt, p, nt)


def _np_consts(l_lat, l_ctx):
    n = RWKV_WIDTH
    i = np.arange(n)
    bd = (i[:, None] // 64 == i[None, :] // 64).astype(np.float32)
    t_r, t_c = (i % 64)[:, None], (i % 64)[None, :]
    tri = np.zeros((2, 6, n, n), np.float32)
    for d in range(2):
        before = (t_c < t_r) if d == 0 else (t_c > t_r)
        tri[d, 0] = bd * before
        tri[d, 1] = bd * (before | (t_c == t_r))
        tri[d, 2] = bd * before * (t_r // 8 == t_c // 8)
        for lvl, blk in enumerate((8, 16, 32)):
            tri[d, 3 + lvl] = bd * before * (t_r // (2 * blk) == t_c // (2 * blk)) * (t_r // blk != t_c // blk)
    j = np.arange(CHUNK)
    csi = np.stack([(j[None, :] <= j[:, None]), (j[None, :] >= j[:, None])]).astype(np.float32)
    tt = np.arange(TM)
    ut = (tt[:, None] < tt[None, :]).astype(np.float32)
    ee = np.arange(N_EXPERTS)
    lt = (ee[None, :] < ee[:, None]).astype(np.float32)
    jj = np.arange(TM + 2 * POOL_HALO)[None, :]
    band = np.stack([((jj >= tt[:, None] + POOL_HALO - w // 2) & (jj <= tt[:, None] + POOL_HALO + w // 2 - 1))
                     for w in POOL_WINDOWS]).astype(np.float32)

    def counts(length):
        t = np.arange(length)[:, None]
        half = np.repeat(np.array(POOL_WINDOWS) // 2, 64)[None, :]
        return (np.clip(t + half, 0, length) - np.clip(t - half, 0, length)).astype(np.float32)

    cnt = np.concatenate([counts(l_lat), counts(l_ctx)], axis=0)
    return dict(bd=jnp.asarray(bd), eye=jnp.eye(n, dtype=f32), tri=jnp.asarray(tri), csi=jnp.asarray(csi),
                avg=jnp.asarray(bd / 64.0), ut=jnp.asarray(ut, dtype=bf16), lt=jnp.asarray(lt, dtype=bf16),
                band=jnp.asarray(band), cnt=jnp.asarray(cnt))


def _rope_tables(l_lat):
    rows = l_lat // GRID_W
    row = jnp.repeat(jnp.arange(rows, dtype=f32), GRID_W)
    col = jnp.tile(jnp.arange(GRID_W, dtype=f32), rows)
    n_freq = QK_ROPE // 4
    inv = ROPE_THETA ** (-jnp.arange(n_freq, dtype=f32) / n_freq)
    ang = jnp.concatenate([row[:, None] * inv, col[:, None] * inv], -1)
    cos = jnp.concatenate([jnp.cos(ang), jnp.ones((TM, 16), f32)], 0)
    sin = jnp.concatenate([jnp.sin(ang), jnp.zeros((TM, 16), f32)], 0)
    n = cos.shape[0]
    ct = jnp.concatenate([jnp.ones((n, QK_NOPE), f32), cos, cos, jnp.zeros((n, 32), f32)], 1)
    st = jnp.concatenate([jnp.zeros((n, QK_NOPE), f32), -sin, sin, jnp.zeros((n, 32), f32)], 1)
    qs = MLA_SCALE * LOG2E
    return jnp.stack([ct * qs, st * qs, ct, st])


def _layer_params(i, a):
    d = a["w_in"].shape[1]
    p = {}
    row = lambda v: v.reshape(1, -1).astype(f32)
    for name in ("pre_mix_g", "post_mix_g", "pre_ffn_g", "post_ffn_g", "mla_q_norm", "mla_kv_norm"):
        p[name] = row(a[name][i])
    w_in = a["w_in"][i]
    zc = lambda n: jnp.zeros((d, n), f32)
    p["w_in_p"] = jnp.concatenate(
        [w_in[:, 0:MLA_IN], zc(128 - QK_ROPE), w_in[:, MLA_IN:MLA_IN + POOL_WIDTH],
         w_in[:, MLA_IN + POOL_WIDTH:], zc(RWKV_PAD - RWKV_IN)], 1).astype(bf16)
    wq = a["mla_w_q_b"][i].reshape(Q_LORA, MLA_HEADS, QK_NOPE + QK_ROPE)
    zq = jnp.zeros((Q_LORA, MLA_HEADS, 32), f32)
    half = QK_ROPE // 2
    p["wq_p"] = jnp.concatenate([wq, zq], 2).reshape(Q_LORA, -1).astype(bf16)
    p["wq_s"] = jnp.concatenate([jnp.zeros_like(wq[:, :, :QK_NOPE]), wq[:, :, QK_NOPE + half:],
                                 wq[:, :, QK_NOPE:QK_NOPE + half], zq], 2).reshape(Q_LORA, -1).astype(bf16)
    wkv = a["mla_w_kv_b"][i].reshape(KV_LORA, MLA_HEADS, QK_NOPE + V_HEAD)
    p["wk_p"] = jnp.concatenate([wkv[:, :, :QK_NOPE], jnp.zeros((KV_LORA, MLA_HEADS, 64), f32)], 2
                                ).reshape(KV_LORA, -1).astype(bf16)
    p["wv_p"] = jnp.concatenate([wkv[:, :, QK_NOPE:], jnp.zeros((KV_LORA, MLA_HEADS, 64), f32)], 2
                                ).reshape(KV_LORA, -1).astype(bf16)
    e_p = np.zeros((128, MLA_HEADS * HEAD_PAD), np.float32)
    e_s = np.zeros_like(e_p)
    for h in range(MLA_HEADS):
        for j in range(QK_ROPE):
            e_p[j, h * HEAD_PAD + QK_NOPE + j] = 1.0
            e_s[(j + half) % QK_ROPE, h * HEAD_PAD + QK_NOPE + j] = 1.0
    p["e_p"], p["e_s"] = jnp.asarray(e_p, dtype=bf16), jnp.asarray(e_s, dtype=bf16)
    pw = a["pool_w"][i]
    p["pool_w_bd"] = jax.scipy.linalg.block_diag(*[pw[g] for g in range(pw.shape[0])]).astype(bf16)
    p["pool_scale"] = row(a["pool_scale"][i])
    p["mu_p"] = jnp.pad(a["rwkv_mu"][i], (0, RWKV_PAD - RWKV_IN)).reshape(1, -1)
    p["w0"] = a["rwkv_w0"][i].reshape(2, 1, RWKV_WIDTH)
    p["a0"] = a["rwkv_a0"][i].reshape(2, 1, RWKV_WIDTH)
    z32 = jnp.zeros((32, RWKV_WIDTH), f32)
    w2, a2 = a["rwkv_w2"][i], a["rwkv_a2"][i]
    p["w2_p"] = jnp.stack([jnp.concatenate([w2[0], z32, z32, z32]), jnp.concatenate([z32, w2[1], z32, z32])])
    p["a2_p"] = jnp.stack([jnp.concatenate([z32, z32, a2[0], z32]), jnp.concatenate([z32, z32, z32, a2[1]])])
    p["g2_p"] = jnp.concatenate([a["rwkv_g2"][i], jnp.zeros((64, RWKV_WIDTH), f32)])
    for name in ("k_k", "k_a", "r_k"):
        p[name] = row(a["rwkv_" + name][i])
    p["ln_w"], p["ln_b"] = row(a["rwkv_ln_w"][i]), row(a["rwkv_ln_b"][i])
    p["w_out"] = a["w_out"][i].astype(bf16)
    p["router_wt"] = a["router_w"][i].T.astype(bf16)
    p["router_b"] = a["router_bias"][i].reshape(-1, 1).astype(f32)
    p["wgu"] = jnp.concatenate([a["exp_w_gate"][i], a["exp_w_up"][i]], 2).astype(bf16)
    p["wd"] = a["exp_w_down"][i].astype(bf16)
    p["sh_wgu"] = jnp.concatenate([a["sh_w_gate"][i], a["sh_w_up"][i]], 1).astype(bf16)
    p["sh_wd"] = a["sh_w_down"][i].astype(bf16)
    return p


def kernel(x, c, ctx, c_ctx, ada_w, ada_b, pre_mix_g, post_mix_g, pre_ffn_g, post_ffn_g, w_in, w_out, mla_q_norm, mla_w_q_b, mla_kv_norm, mla_w_kv_b, pool_w, pool_scale, rwkv_mu, rwkv_w0, rwkv_w2, rwkv_a0, rwkv_a2, rwkv_g2, rwkv_k_k, rwkv_k_a, rwkv_r_k, rwkv_ln_w, rwkv_ln_b, router_w, router_bias, exp_w_gate, exp_w_up, exp_w_down, sh_w_gate, sh_w_up, sh_w_down):
    arrs = dict(pre_mix_g=pre_mix_g, post_mix_g=post_mix_g, pre_ffn_g=pre_ffn_g, post_ffn_g=post_ffn_g, w_in=w_in,
                w_out=w_out, mla_q_norm=mla_q_norm, mla_w_q_b=mla_w_q_b, mla_kv_norm=mla_kv_norm,
                mla_w_kv_b=mla_w_kv_b, pool_w=pool_w, pool_scale=pool_scale, rwkv_mu=rwkv_mu, rwkv_w0=rwkv_w0,
                rwkv_w2=rwkv_w2, rwkv_a0=rwkv_a0, rwkv_a2=rwkv_a2, rwkv_g2=rwkv_g2, rwkv_k_k=rwkv_k_k,
                rwkv_k_a=rwkv_k_a, rwkv_r_k=rwkv_r_k, rwkv_ln_w=rwkv_ln_w, rwkv_ln_b=rwkv_ln_b, router_w=router_w,
                router_bias=router_bias, exp_w_gate=exp_w_gate, exp_w_up=exp_w_up, exp_w_down=exp_w_down,
                sh_w_gate=sh_w_gate, sh_w_up=sh_w_up, sh_w_down=sh_w_down)
    b_sz, l_lat, d = x.shape
    l_ctx = ctx.shape[1]
    assert l_lat % TM == 0 and l_ctx % TM == 0 and l_lat % GRID_W == 0 and b_sz < 16
    lt, ct = l_lat // TM, l_ctx // TM
    nlat, nctx = b_sz * lt, b_sz * ct
    nall = nlat + nctx
    ncc, nlc = l_ctx // CHUNK, l_lat // CHUNK
    consts = _np_consts(l_lat, l_ctx)
    rope = _rope_tables(l_lat)
    rope_idx = lambda i: jnp.where(i < nlat, i % lt, lt)
    cnt_idx = lambda i: jnp.where(i < nlat, i % lt, lt + (i - nlat) % ct)
    chunk_idx = lambda i: (jnp.where(i < nlat, i % lt, nlc * CHUNK // TM + (i - nlat) % ct),
                           jnp.where(i < nlat, i // lt, (i - nlat) // ct))

    c_all = jnp.zeros((16, d), f32).at[:b_sz].set(c).at[b_sz].set(c_ctx)
    mods = _ada_mod(c_all, ada_w, ada_b)
    tile_row = np.concatenate([np.repeat(np.arange(b_sz), lt), np.full(nctx, b_sz)])
    xa = jnp.concatenate([x.reshape(b_sz * l_lat, d), ctx.reshape(b_sz * l_ctx, d)], 0)

    for i in range(DEPTH):
        last = i == DEPTH - 1
        p = _layer_params(i, arrs)
        mt = jnp.pad(mods[i][tile_row].reshape(nall, 6, d), ((0, 0), (0, 2), (0, 0)))
        nt = nlat if last else nall
        q, k, v, zp, zr = _inproj(xa, mt, rope, rope_idx, chunk_idx, ncc + nlc, b_sz, p, nall)
        segs_lat = [(l_lat, min(512, l_lat), lambda t: t // lt), (l_ctx, l_ctx if l_ctx <= 512 else TM,
                                                                   lambda t: nlat * TM // l_ctx + t // lt)]
        att = _attention(q, k, v, nlat, 0, segs_lat)
        if not last:
            segs_ctx = [(l_ctx, l_ctx if l_ctx <= 512 else TM, lambda t: nlat * TM // l_ctx + t // ct)]
            att = jnp.concatenate([att, _attention(q, k, v, nctx, nlat, segs_ctx)], 0)
        py = _pool(zp, consts["band"], consts["cnt"], cnt_idx, p["pool_w_bd"], p["pool_scale"], nt, nlat, lt, ct)
        y, bv, g = _rwkv(zr, consts, p, ncc, nlc)
        x1, hp = _outproj(att, py, y, bv, g, xa, mt, chunk_idx, consts, p, nt)
        xa = _moe(hp, x1, mt, consts, p, nt)
    return xa[:b_sz * l_lat].reshape(b_sz, l_lat, d)
```

```python
import functools

import numpy as np
import jax
import jax.numpy as jnp
from jax import lax
from jax.experimental import pallas as pl
from jax.experimental.pallas import tpu as pltpu
from jax.experimental.pallas import tpu_sc as plsc

f32 = jnp.float32
bf16 = jnp.bfloat16
HIGHEST = lax.Precision.HIGHEST

DEPTH = 2
GRID_W = 64
NORM_EPS = 1e-6
MLA_HEADS = 8
Q_LORA = 384
KV_LORA = 256
QK_NOPE = 64
QK_ROPE = 32
V_HEAD = 64
ROPE_THETA = 10000.0
MLA_SCALE = (QK_NOPE + QK_ROPE) ** -0.5
MLA_IN = Q_LORA + KV_LORA + QK_ROPE
HEAD_PAD = 128
POOL_WINDOWS = (2, 4, 8, 16)
POOL_WIDTH = 256
POOL_HALO = 8
RWKV_HEADS = 4
RWKV_HEAD = 64
RWKV_WIDTH = 256
RWKV_IN = 960
RWKV_PAD = 1024
RWKV_GN_EPS = 64e-5
CHUNK = 64
RWKV_BATCHES_PER_STEP = 2
N_EXPERTS = 64
TOP_K = 6
N_GROUPS = 8
TOPK_GROUPS = 4
ROUTED_SCALE = 2.5
EXPERT_FF = 256
TM = 256
MOE_BM = 512
SC_CORES, SC_SUBCORES = 2, 16
SC_WINDOW = 128
Z_COLS = 2048
VMEM_LIMIT = 48 * 1024 * 1024
LOG2E = 1.4426950408889634
EXP_M05 = 0.6065306597126334


def _cparams(sem, vmem=VMEM_LIMIT):
    return pltpu.CompilerParams(dimension_semantics=sem, vmem_limit_bytes=vmem)


def _mm(a, b):
    return jnp.dot(a.astype(bf16), b.astype(bf16), preferred_element_type=f32)


def _mm_nt(a, b):
    return lax.dot_general(a.astype(bf16), b.astype(bf16), (((1,), (1,)), ((), ())), preferred_element_type=f32)


def _mm_tn(a, b):
    return lax.dot_general(a.astype(bf16), b.astype(bf16), (((0,), (0,)), ((), ())), preferred_element_type=f32)


def _mmf(a, b):
    return jnp.dot(a, b, precision=HIGHEST, preferred_element_type=f32)


def _split_hi_lo(a):
    hi = a.astype(bf16)
    return hi, (a - hi.astype(f32)).astype(bf16)


def _mm_x01(a, w01):
    hi, lo = _split_hi_lo(a)
    w = w01.astype(bf16)
    return jnp.dot(hi, w, preferred_element_type=f32) + jnp.dot(lo, w, preferred_element_type=f32)


def _mm_01x(w01, a):
    hi, lo = _split_hi_lo(a)
    w = w01.astype(bf16)
    return jnp.dot(w, hi, preferred_element_type=f32) + jnp.dot(w, lo, preferred_element_type=f32)


def _rms(x):
    return x * lax.rsqrt(jnp.mean(x * x, axis=-1, keepdims=True) + NORM_EPS)


def _sigmoid(x):
    return 1.0 / (1.0 + jnp.exp(-x))


def _ada_kernel(c_ref, w_ref, b_ref, o_ref):
    c = c_ref[...]
    s = c * _sigmoid(c)
    o_ref[0] = _mm(s, w_ref[0]) + b_ref[0]


def _ada_mod(c_all, ada_w, ada_b):
    depth, d, n = ada_w.shape
    tn = 1024
    return pl.pallas_call(
        _ada_kernel,
        out_shape=jax.ShapeDtypeStruct((depth, 16, n), f32),
        grid=(depth, n // tn),
        in_specs=[pl.BlockSpec((16, d), lambda i, j: (0, 0)),
                  pl.BlockSpec((1, d, tn), lambda i, j: (i, 0, j)),
                  pl.BlockSpec((1, 1, tn), lambda i, j: (i, 0, j))],
        out_specs=pl.BlockSpec((1, 16, tn), lambda i, j: (i, 0, j)),
        compiler_params=_cparams(("arbitrary", "arbitrary")),
        name="ada_mod",
    )(c_all, ada_w, ada_b.reshape(depth, 1, n))


def _inproj_kernel(x_ref, m_ref, rope_ref, g_ref, win_ref, qg_ref, wq_ref, wqs_ref, kvg_ref, wk_ref, wv_ref,
                   e_ref, es_ref, q_ref, k_ref, v_ref, zp_ref, zr_ref):
    m = m_ref[0]
    h = _rms(x_ref[...]) * g_ref[...]
    h = h * (1.0 + m[1:2]) + m[0:1]
    z = _mm(h, win_ref[...])
    zp_ref[...] = z[:, 768:1024]
    for c4 in range(TM // CHUNK):
        zr_ref[c4, 0] = z[c4 * CHUNK:(c4 + 1) * CHUNK, 1024:2048]
    tile8 = lambda t: jnp.concatenate([t] * MLA_HEADS, axis=1)
    qn = (_rms(z[:, 0:Q_LORA]) * qg_ref[...]).astype(bf16)
    q = _mm(qn, wq_ref[...]) * tile8(rope_ref[0]) + _mm(qn, wqs_ref[...]) * tile8(rope_ref[1])
    q_ref[...] = q.astype(bf16)
    kvn = (_rms(z[:, Q_LORA:Q_LORA + KV_LORA]) * kvg_ref[...]).astype(bf16)
    kpe = z[:, 640:768].astype(bf16)
    k = _mm(kvn, wk_ref[...]) + _mm(kpe, e_ref[...]) * tile8(rope_ref[2]) + _mm(kpe, es_ref[...]) * tile8(rope_ref[3])
    k_ref[...] = k.astype(bf16)
    lane = lax.broadcasted_iota(jnp.int32, (1, MLA_HEADS * HEAD_PAD), 1)
    v_ref[...] = (_mm(kvn, wv_ref[...]) + jnp.where(lane % HEAD_PAD == V_HEAD, 1.0, 0.0)).astype(bf16)


def _inproj(xa, mt, rope, rope_idx, chunk_idx, npos, b_sz, p, nt):
    t_all, d = xa.shape
    cpt = TM // CHUNK
    full = lambda a: pl.BlockSpec(a.shape, lambda i: (0,) * a.ndim)
    ws = [p["pre_mix_g"], p["w_in_p"], p["mla_q_norm"], p["wq_p"], p["wq_s"], p["mla_kv_norm"], p["wk_p"], p["wv_p"],
          p["e_p"], p["e_s"]]
    return pl.pallas_call(
        _inproj_kernel,
        out_shape=(jax.ShapeDtypeStruct((t_all, MLA_HEADS * HEAD_PAD), bf16),
                   jax.ShapeDtypeStruct((t_all, MLA_HEADS * HEAD_PAD), bf16),
                   jax.ShapeDtypeStruct((t_all, MLA_HEADS * HEAD_PAD), bf16),
                   jax.ShapeDtypeStruct((t_all, POOL_WIDTH), f32),
                   jax.ShapeDtypeStruct((npos, b_sz, CHUNK, RWKV_PAD), f32)),
        grid=(nt,),
        in_specs=[pl.BlockSpec((TM, d), lambda i: (i, 0)),
                  pl.BlockSpec((1, 8, d), lambda i: (i, 0, 0)),
                  pl.BlockSpec((4, TM, HEAD_PAD), lambda i: (0, rope_idx(i), 0))] + [full(w) for w in ws],
        out_specs=(pl.BlockSpec((TM, 1024), lambda i: (i, 0)),
                   pl.BlockSpec((TM, 1024), lambda i: (i, 0)),
                   pl.BlockSpec((TM, 1024), lambda i: (i, 0)),
                   pl.BlockSpec((TM, POOL_WIDTH), lambda i: (i, 0)),
                   pl.BlockSpec((cpt, 1, CHUNK, RWKV_PAD), lambda i: chunk_idx(i) + (0, 0))),
        compiler_params=_cparams(("arbitrary",)),
        name="in_proj",
    )(xa, mt, rope, *ws)


def _attn_kernel(*refs, seg_tiles):
    n_in = 1 + 2 * len(seg_tiles)
    q_ref, kv, o_ref, m_ref, acc_ref = refs[0], refs[1:n_in], refs[n_in], refs[n_in + 1], refs[n_in + 2]
    m_ref[...] = jnp.full(m_ref.shape, -jnp.inf, f32)
    acc_ref[...] = jnp.zeros(acc_ref.shape, f32)

    def step(k_ref, v_ref, r0, tk):
        for h in range(MLA_HEADS):
            hs = slice(h * HEAD_PAD, (h + 1) * HEAD_PAD)
            s = lax.dot_general(q_ref[:, hs], k_ref[pl.ds(r0, tk), hs], (((1,), (1,)), ((), ())),
                                preferred_element_type=f32)
            m_prev = m_ref[h]
            m_new = jnp.maximum(m_prev, jnp.max(s, axis=-1, keepdims=True))
            pr = jnp.exp2(s - jnp.concatenate([m_new] * (tk // HEAD_PAD), axis=1))
            acc_ref[h] = (jnp.exp2(m_prev - m_new) * acc_ref[h]
                          + jnp.dot(pr.astype(bf16), v_ref[pl.ds(r0, tk), hs], preferred_element_type=f32))
            m_ref[h] = m_new

    for si, (nk, tk) in enumerate(seg_tiles):
        k_ref, v_ref = kv[2 * si], kv[2 * si + 1]
        if nk == 1:
            step(k_ref, v_ref, 0, tk)
        else:
            def body(j, c, k_ref=k_ref, v_ref=v_ref, tk=tk):
                step(k_ref, v_ref, pl.multiple_of(j * tk, tk), tk)
                return c
            lax.fori_loop(0, nk, body, 0)
    for pr2 in range(MLA_HEADS // 2):
        a0, a1 = acc_ref[2 * pr2], acc_ref[2 * pr2 + 1]
        o_ref[:, pr2 * 128:(pr2 + 1) * 128] = jnp.concatenate(
            [a0[:, :V_HEAD] / a0[:, V_HEAD:V_HEAD + 1], a1[:, :V_HEAD] / a1[:, V_HEAD:V_HEAD + 1]], axis=1).astype(bf16)


def _attention(q, k, v, n_q_tiles, q_tile0, segs):
    in_specs = [pl.BlockSpec((TM, 1024), lambda i: (i + q_tile0, 0))]
    args = [q]
    seg_tiles = []
    for rows, tk, bidx in segs:
        in_specs.append(pl.BlockSpec((rows, 1024), lambda i, bidx=bidx: (bidx(i), 0), pipeline_mode=pl.Buffered(1)))
        in_specs.append(pl.BlockSpec((rows, 1024), lambda i, bidx=bidx: (bidx(i), 0), pipeline_mode=pl.Buffered(1)))
        args += [k, v]
        seg_tiles.append((rows // tk, tk))
    return pl.pallas_call(
        functools.partial(_attn_kernel, seg_tiles=tuple(seg_tiles)),
        out_shape=jax.ShapeDtypeStruct((n_q_tiles * TM, MLA_HEADS * V_HEAD), bf16),
        grid=(n_q_tiles,),
        in_specs=in_specs,
        out_specs=pl.BlockSpec((TM, 512), lambda i: (i, 0)),
        scratch_shapes=[pltpu.VMEM((MLA_HEADS, TM, HEAD_PAD), f32), pltpu.VMEM((MLA_HEADS, TM, HEAD_PAD), f32)],
        compiler_params=_cparams(("arbitrary",)),
        name="mla_attention",
    )(*args)


def _pool_kernel(z_ref, zp_ref, zn_ref, band_ref, cnt_ref, pw_ref, ps_ref, o_ref, *, nlat, lt, ct):
    i = pl.program_id(0)
    is_lat = i < nlat
    j = jnp.where(is_lat, i % lt, (i - nlat) % ct)
    n = jnp.where(is_lat, lt, ct)
    z = z_ref[...]
    prev = zp_ref[...] * jnp.where(j == 0, 0.0, 1.0)
    nxt = zn_ref[...] * jnp.where(j == n - 1, 0.0, 1.0)
    zh = jnp.concatenate([prev, z, nxt], axis=0)
    lane_grp = lax.broadcasted_iota(jnp.int32, (1, POOL_WIDTH), 1) // 64
    tot = jnp.zeros_like(z)
    for g in range(len(POOL_WINDOWS)):
        tot = tot + _mm_01x(band_ref[g], zh * jnp.where(lane_grp == g, 1.0, 0.0))
    diff = tot / cnt_ref[...] - z
    o_ref[...] = (_mm(diff, pw_ref[...]) * ps_ref[...]).astype(bf16)


def _pool(zp, band, cnt, cnt_idx, pw_bd, pscale, nt, nlat, lt, ct):
    t_all = zp.shape[0]
    nb8 = t_all // POOL_HALO
    r = TM // POOL_HALO
    return pl.pallas_call(
        functools.partial(_pool_kernel, nlat=nlat, lt=lt, ct=ct),
        out_shape=jax.ShapeDtypeStruct((nt * TM, POOL_WIDTH), bf16),
        grid=(nt,),
        in_specs=[pl.BlockSpec((TM, POOL_WIDTH), lambda i: (i, 0)),
                  pl.BlockSpec((POOL_HALO, POOL_WIDTH), lambda i: (jnp.maximum(i * r - 1, 0), 0)),
                  pl.BlockSpec((POOL_HALO, POOL_WIDTH), lambda i: (jnp.minimum((i + 1) * r, nb8 - 1), 0)),
                  pl.BlockSpec(band.shape, lambda i: (0, 0, 0)),
                  pl.BlockSpec((TM, POOL_WIDTH), lambda i: (cnt_idx(i), 0)),
                  pl.BlockSpec((POOL_WIDTH, POOL_WIDTH), lambda i: (0, 0)),
                  pl.BlockSpec((1, POOL_WIDTH), lambda i: (0, 0))],
        out_specs=pl.BlockSpec((TM, POOL_WIDTH), lambda i: (i, 0)),
        compiler_params=_cparams(("arbitrary",)),
        name="pool_mixer",
    )(zp, zp, zp, band, cnt, pw_bd, pscale)


def _rwkv_pos(d, s, *, ncc, nlc):
    in_ctx = s < ncc
    jc = jnp.where(d == 0, s, ncc - 1 - s)
    jl = jnp.where(d == 0, s - ncc, nlc - 1 - (s - ncc))
    pos = jnp.where(in_ctx, nlc + jc, jl)
    first = jnp.where(in_ctx, jc == 0, jl == 0)
    last = jnp.where(in_ctx, jc == ncc - 1, jl == nlc - 1)
    return pos, first, last


def _tri_inverse(lm, tri_ref, eye):
    n = -(lm * tri_ref[0, 2])
    n2 = _mm(n, n)
    n4 = _mm(n2, n2)
    t = _mm(_mm(eye + n, eye + n2), eye + n4)
    for lvl in range(3):
        t = t - _mm(_mm(t, lm * tri_ref[0, 3 + lvl]), t)
    return t


def _rwkv_kernel(z_ref, zp_ref, zn_ref, bd_ref, eye_ref, tri_ref, csi_ref, mu_ref, w0_ref, w2_ref, a0_ref, a2_ref,
                 g2_ref, kk_ref, ka_ref, rk_ref, y_ref, bv_ref, g_ref, s_ref, *, ncc, nlc):
    d, s = pl.program_id(0), pl.program_id(2)
    _, first, last = _rwkv_pos(d, s, ncc=ncc, nlc=nlc)

    @pl.when(s == 0)
    def _():
        s_ref[...] = jnp.zeros_like(s_ref)

    for nb in range(z_ref.shape[1]):
        prev_row = zp_ref[0, nb, 7:8, :] * jnp.where(first, 0.0, 1.0)
        next_row = zn_ref[0, nb, 0:1, :] * jnp.where(last, 0.0, 1.0)
        y, bv, g, st = _rwkv_chunk(z_ref[0, nb], prev_row, next_row, s_ref[nb], d, bd_ref, eye_ref, tri_ref, csi_ref,
                                   mu_ref, w0_ref, w2_ref, a0_ref, a2_ref, g2_ref, kk_ref, ka_ref, rk_ref)
        y_ref[0, 0, nb] = y
        bv_ref[0, 0, nb] = bv
        g_ref[0, 0, nb] = g
        s_ref[nb] = st


def _rwkv_chunk(z, prev_row, next_row, st, d, bd_ref, eye_ref, tri_ref, csi_ref, mu_ref, w0_ref, w2_ref, a0_ref,
                a2_ref, g2_ref, kk_ref, ka_ref, rk_ref):
    c = CHUNK
    row = lax.broadcasted_iota(jnp.int32, (c, 1), 0)
    zp = jnp.where(row == 0, prev_row, pltpu.roll(z, 1, 0))
    zn = jnp.where(row == c - 1, next_row, pltpu.roll(z, c - 1, 0))
    zs = z + mu_ref[...] * (0.5 * (zp + zn) - z)
    r, k, v = zs[:, 0:256], zs[:, 256:512], zs[:, 512:768]
    lora, gate_in = zs[:, 768:896], zs[:, 896:1024]
    bd = bd_ref[...]
    eye = eye_ref[...]

    g = _mm(_sigmoid(gate_in), g2_ref[...])
    e = EXP_M05 * _sigmoid(w0_ref[0] + _mm(jnp.tanh(lora), w2_ref[0]))
    a = _sigmoid(a0_ref[0] + _mm(lora, a2_ref[0]))
    kd = k * (1.0 + (a - 1.0) * ka_ref[...])
    kkr = k * kk_ref[...]
    kk = kkr / jnp.maximum(jnp.sqrt(_mm_x01(kkr * kkr, bd)), 1e-12)
    bv = _mm_x01(r * kd * rk_ref[...], bd) * v
    bb = kk * a

    cs = _mm_01x(csi_ref[0], e)
    tot = jnp.where(d == 0, cs[c - 1:c, :], cs[0:1, :])
    a_s = jnp.exp(e - cs) * kk
    b_s = bb * jnp.exp(cs)
    k_s = kd * jnp.exp(cs)
    r_s = r * jnp.exp(-cs)
    b_e = bb * jnp.exp(cs - tot)
    k_e = kd * jnp.exp(cs - tot)
    g_end = jnp.exp(-tot)

    rep4 = lambda t: jnp.concatenate([t] * RWKV_HEADS, axis=0)
    fold4 = lambda t: t[0:c] + t[c:2 * c] + t[2 * c:3 * c] + t[3 * c:4 * c]
    a4, r4, v4 = rep4(a_s) * bd, rep4(r_s) * bd, rep4(v) * bd
    b4, k4 = rep4(b_s), rep4(k_s)
    tri_s, tri_i = tri_ref[0, 0], tri_ref[0, 1]
    lm = _mm_nt(a4, b4) * tri_s
    akm = _mm_nt(a4, k4) * tri_s
    rbm = _mm_nt(r4, b4) * tri_i
    rkm = _mm_nt(r4, k4) * tri_i
    t = _tri_inverse(lm, tri_ref, eye)
    w4 = _mm(t, a4)
    u4 = _mm(t, _mm(akm, v4))
    q4 = r4 - _mm(rbm, w4)
    y4 = _mm(rkm, v4) - _mm(rbm, u4)
    w_all, u_all, q_all, y0 = fold4(w4), fold4(u4), fold4(q4), fold4(y4)
    g_bd = eye * g_end - bd * _mm_tn(w_all, b_e)
    h_bd = bd * (_mm_tn(v, k_e) - _mm_tn(u_all, b_e))
    return _mm_nt(q_all, st) + y0, bv, g, _mm(st, g_bd) + h_bd


def _rwkv(zr, consts, p, ncc, nlc):
    npos, b_sz = zr.shape[0], zr.shape[1]
    nbat = RWKV_BATCHES_PER_STEP
    kw = dict(ncc=ncc, nlc=nlc)
    pos = lambda d, s: _rwkv_pos(d, s, **kw)[0]
    full = lambda a: pl.BlockSpec(a.shape, lambda d, b, s: (0,) * a.ndim)
    by_dir = lambda a: pl.BlockSpec((1,) + a.shape[1:], lambda d, b, s: (d,) + (0,) * (a.ndim - 1))
    out = jax.ShapeDtypeStruct((2, npos, b_sz, CHUNK, RWKV_WIDTH), f32)
    ospec = pl.BlockSpec((1, 1, nbat, CHUNK, RWKV_WIDTH), lambda d, b, s: (d, pos(d, s), b, 0, 0))
    last8 = CHUNK // 8 - 1
    return pl.pallas_call(
        functools.partial(_rwkv_kernel, **kw),
        out_shape=(out, out, out),
        grid=(2, b_sz // nbat, npos),
        in_specs=[pl.BlockSpec((1, nbat, CHUNK, RWKV_PAD), lambda d, b, s: (pos(d, s), b, 0, 0)),
                  pl.BlockSpec((1, nbat, 8, RWKV_PAD), lambda d, b, s: (jnp.maximum(pos(d, s) - 1, 0), b, last8, 0)),
                  pl.BlockSpec((1, nbat, 8, RWKV_PAD), lambda d, b, s: (jnp.minimum(pos(d, s) + 1, npos - 1), b, 0, 0)),
                  full(consts["bd"]), full(consts["eye"]), by_dir(consts["tri"]), by_dir(consts["csi"]),
                  full(p["mu_p"]), by_dir(p["w0"]), by_dir(p["w2_p"]), by_dir(p["a0"]), by_dir(p["a2_p"]),
                  full(p["g2_p"]), full(p["k_k"]), full(p["k_a"]), full(p["r_k"])],
        out_specs=(ospec, ospec, ospec),
        scratch_shapes=[pltpu.VMEM((nbat, RWKV_WIDTH, RWKV_WIDTH), f32)],
        compiler_params=_cparams(("arbitrary", "arbitrary", "arbitrary")),
        name="rwkv7_chunked",
    )(zr, zr, zr, consts["bd"], consts["eye"], consts["tri"], consts["csi"], p["mu_p"], p["w0"], p["w2_p"], p["a0"],
      p["a2_p"], p["g2_p"], p["k_k"], p["k_a"], p["r_k"])


def _pack_bf16_pairs(lo, hi):
    lo_b = pltpu.bitcast(lo.astype(bf16).astype(f32), jnp.uint32)
    hi_b = pltpu.bitcast(hi.astype(bf16).astype(f32), jnp.uint32)
    return (hi_b & jnp.uint32(0xFFFF0000)) | (lo_b >> 16)


def _unpack_bf16_pairs(w):
    lo = pltpu.bitcast(w << 16, f32).astype(bf16)
    hi = pltpu.bitcast(w & jnp.uint32(0xFFFF0000), f32).astype(bf16)
    return lo, hi


def _outproj_kernel(a_ref, py_ref, y_ref, bv_ref, g_ref, x_ref, m_ref, avg_ref, lnw_ref, lnb_ref, wo_ref, pmg_ref,
                    pfg_ref, x1_ref, hp_ref):
    m = m_ref[0]
    rows = lambda ref, dd: jnp.concatenate([ref[dd, c4, 0] for c4 in range(TM // CHUNK)], axis=0)
    ysum = rows(y_ref, 0) + rows(y_ref, 1)
    avg = avg_ref[...]
    dev = ysum - _mm_x01(ysum, avg)
    var = _mm_x01(dev * dev, avg)
    yn = dev * lax.rsqrt(var + RWKV_GN_EPS) * lnw_ref[...] + lnb_ref[...]
    rw = (yn + rows(bv_ref, 0) + rows(bv_ref, 1)) * rows(g_ref, 0)
    o = (jnp.dot(a_ref[...], wo_ref[0:512, :], preferred_element_type=f32)
         + jnp.dot(py_ref[...], wo_ref[512:768, :], preferred_element_type=f32)
         + _mm(rw, wo_ref[768:1024, :]))
    x1 = x_ref[...] + m[2:3] * (_rms(o) * pmg_ref[...])
    x1_ref[...] = x1
    h = (_rms(x1) * pfg_ref[...]) * (1.0 + m[4:5]) + m[3:4]
    hp_ref[...] = _pack_bf16_pairs(h[:, 0:512], h[:, 512:1024])


def _outproj(att, py, y, bv, g, xa, mt, chunk_idx, consts, p, nt):
    d = xa.shape[1]
    cpt = TM // CHUNK
    cspec = lambda nd: pl.BlockSpec((nd, cpt, 1, CHUNK, RWKV_WIDTH), lambda i: (0,) + chunk_idx(i) + (0, 0))
    full = lambda a: pl.BlockSpec(a.shape, lambda i: (0,) * a.ndim)
    ws = [consts["avg"], p["ln_w"], p["ln_b"], p["w_out"], p["post_mix_g"], p["pre_ffn_g"]]
    return pl.pallas_call(
        _outproj_kernel,
        out_shape=(jax.ShapeDtypeStruct((nt * TM, d), f32), jax.ShapeDtypeStruct((nt * TM, d // 2), jnp.uint32)),
        grid=(nt,),
        in_specs=[pl.BlockSpec((TM, 512), lambda i: (i, 0)),
                  pl.BlockSpec((TM, POOL_WIDTH), lambda i: (i, 0)),
                  cspec(2), cspec(2), cspec(1),
                  pl.BlockSpec((TM, d), lambda i: (i, 0)),
                  pl.BlockSpec((1, 8, d), lambda i: (i, 0, 0))] + [full(w) for w in ws],
        out_specs=(pl.BlockSpec((TM, d), lambda i: (i, 0)), pl.BlockSpec((TM, d // 2), lambda i: (i, 0))),
        compiler_params=_cparams(("arbitrary",)),
        name="out_proj",
    )(att, py, y, bv, g, xa, mt, *ws)


def _router_kernel(hp_ref, rw_ref, rb_ref, ut_ref, lt_ref, ei_ref, pos_ref, gt_ref, cnt_ref, run_ref):
    i = pl.program_id(0)

    @pl.when(i == 0)
    def _():
        run_ref[...] = jnp.zeros_like(run_ref)

    tm = hp_ref.shape[0]
    ne, ng = N_EXPERTS, N_GROUPS
    pg = ne // ng
    lo, hi = _unpack_bf16_pairs(hp_ref[...])
    logits = (lax.dot_general(rw_ref[:, 0:512], lo, (((1,), (1,)), ((), ())), preferred_element_type=f32)
              + lax.dot_general(rw_ref[:, 512:1024], hi, (((1,), (1,)), ((), ())), preferred_element_type=f32))
    scores = _sigmoid(logits)
    sel = scores + rb_ref[...]
    neg = -jnp.inf

    s3 = sel.reshape(ng, pg, tm)
    io = lax.broadcasted_iota(jnp.int32, (ng, pg, tm), 1)
    m1 = jnp.max(s3, axis=1, keepdims=True)
    i1 = jnp.min(jnp.where(s3 == m1, io, pg), axis=1, keepdims=True)
    m2 = jnp.max(jnp.where(io == i1, neg, s3), axis=1, keepdims=True)
    gs = (m1 + m2).reshape(ng, tm)
    gi = lax.broadcasted_iota(jnp.int32, (ng, tm), 0)
    grank = jnp.zeros((ng, tm), f32)
    for j in range(ng):
        rj = gs[j:j + 1, :]
        grank = grank + jnp.where((rj > gs) | ((rj == gs) & (j < gi)), 1.0, 0.0)
    gsel = jnp.where(grank < TOPK_GROUPS, 1.0, 0.0)
    gsel3 = jnp.broadcast_to(gsel.reshape(ng, 1, tm), (ng, pg, tm)).reshape(ne, tm)
    msk = jnp.where(gsel3 > 0.5, sel, neg)
    ei = lax.broadcasted_iota(jnp.int32, (ne, tm), 0)
    erank = jnp.zeros((ne, tm), f32)
    for j in range(ne):
        rj = msk[j:j + 1, :]
        erank = erank + jnp.where((rj > msk) | ((rj == msk) & (j < ei)), 1.0, 0.0)
    chosen = erank < TOP_K
    chf = jnp.where(chosen, 1.0, 0.0)
    graw = jnp.where(chosen, scores, 0.0)
    gate = graw / jnp.sum(graw, axis=0, keepdims=True) * ROUTED_SCALE

    pos = run_ref[...] + _mm(chf, ut_ref[...])
    tot = jnp.sum(chf, axis=1, keepdims=True)
    run_new = run_ref[...] + tot
    run_ref[...] = run_new
    cnt_ref[...] = run_new[:, 0:128]
    rk = _mm(lt_ref[...], chf)
    eif = ei.astype(f32)
    rows_e, rows_p, rows_g = [], [], []
    for kq in range(TOP_K):
        mk = chosen & (rk == float(kq))
        rows_e.append(jnp.sum(jnp.where(mk, eif, 0.0), axis=0, keepdims=True))
        rows_p.append(jnp.sum(jnp.where(mk, pos, 0.0), axis=0, keepdims=True))
        rows_g.append(jnp.sum(jnp.where(mk, gate, 0.0), axis=0, keepdims=True))
    zrow = jnp.zeros((8 - TOP_K, tm), f32)
    ei_ref[0] = jnp.concatenate(rows_e + [zrow], axis=0).astype(jnp.int32)
    pos_ref[0] = jnp.concatenate(rows_p + [zrow], axis=0).astype(jnp.int32)
    gpad = jnp.concatenate(rows_g + [jnp.zeros((128 - TOP_K, tm), f32)], axis=0)
    gt_ref[...] = gpad.T


def _router(hp, rwt, rb, consts, nt):
    full = lambda a: pl.BlockSpec(a.shape, lambda i: (0,) * a.ndim)
    return pl.pallas_call(
        _router_kernel,
        out_shape=(jax.ShapeDtypeStruct((nt, 8, TM), jnp.int32), jax.ShapeDtypeStruct((nt, 8, TM), jnp.int32),
                   jax.ShapeDtypeStruct((nt * TM, 128), f32), jax.ShapeDtypeStruct((N_EXPERTS, 128), f32)),
        grid=(nt,),
        in_specs=[pl.BlockSpec((TM, 512), lambda i: (i, 0)), full(rwt), full(rb), full(consts["ut"]),
                  full(consts["lt"])],
        out_specs=(pl.BlockSpec((1, 8, TM), lambda i: (i, 0, 0)), pl.BlockSpec((1, 8, TM), lambda i: (i, 0, 0)),
                   pl.BlockSpec((TM, 128), lambda i: (i, 0)), pl.BlockSpec((N_EXPERTS, 128), lambda i: (0, 0))),
        scratch_shapes=[pltpu.VMEM((N_EXPERTS, TM), f32)],
        compiler_params=_cparams(("arbitrary",)),
        name="moe_router",
    )(hp, rwt, rb, consts["ut"], consts["lt"])


def _row_copy(src_ref, src_row, dst_ref, dst_row, sem):
    return pltpu.make_async_copy(src_ref.at[pl.ds(src_row, 1)], dst_ref.at[pl.ds(dst_row, 1)], sem)


def _dispatch_kernel(ps_ref, ei_ref, pos_ref, hp_ref, xz_ref, xs_ref, dest_ref, ei_s, pos_s, dest_s, isem, rsem):
    del xz_ref
    i = pl.program_id(0)
    c1 = pltpu.make_async_copy(ei_ref.at[i], ei_s, isem.at[0])
    c2 = pltpu.make_async_copy(pos_ref.at[i], pos_s, isem.at[1])
    c1.start()
    c2.start()
    c1.wait()
    c2.wait()

    def issue(t, c):
        for kq in range(TOP_K):
            slot = ps_ref[ei_s[kq * TM + t]] + pos_s[kq * TM + t]
            dest_s[kq * TM + t] = slot
            _row_copy(hp_ref, t, xs_ref, slot, rsem).start()
        return c

    def drain(t, c):
        for kq in range(TOP_K):
            _row_copy(hp_ref, 0, xs_ref, 0, rsem).wait()
        return c

    @pl.when(i == 0)
    def _():
        def zero(j, c):
            dest_s[j] = 0
            return c
        lax.fori_loop(TOP_K * TM, 8 * TM, zero, 0)

    lax.fori_loop(0, TM, issue, 0)
    c3 = pltpu.make_async_copy(dest_s, dest_ref.at[i], isem.at[0])
    c3.start()
    lax.fori_loop(0, TM, drain, 0)
    c3.wait()


def _dispatch(pstart, ei, pos, hp, n_slots, nt):
    xz = jnp.zeros((n_slots, hp.shape[1]), jnp.uint32)
    anyspec = pl.BlockSpec(memory_space=pl.ANY)
    return pl.pallas_call(
        _dispatch_kernel,
        out_shape=(jax.ShapeDtypeStruct(xz.shape, xz.dtype), jax.ShapeDtypeStruct((nt, 8 * TM), jnp.int32)),
        grid_spec=pltpu.PrefetchScalarGridSpec(
            num_scalar_prefetch=1,
            grid=(nt,),
            in_specs=[anyspec, anyspec, pl.BlockSpec((TM, hp.shape[1]), lambda i, ps: (i, 0)), anyspec],
            out_specs=(anyspec, anyspec),
            scratch_shapes=[pltpu.SMEM((8 * TM,), jnp.int32), pltpu.SMEM((8 * TM,), jnp.int32),
                            pltpu.SMEM((8 * TM,), jnp.int32), pltpu.SemaphoreType.DMA((2,)),
                            pltpu.SemaphoreType.DMA]),
        input_output_aliases={4: 0},
        compiler_params=_cparams(("arbitrary",)),
        name="moe_dispatch",
    )(pstart, ei, pos, hp, xz)


def _slots_kernel(ps_ref, ei_ref, pos_ref, d_ref, dt_ref):
    ei = ei_ref[0]
    slot = pos_ref[0]
    for e in range(N_EXPERTS):
        slot = slot + jnp.where(ei == e, ps_ref[e], 0)
    d_ref[...] = slot
    dt_ref[0] = slot


def _slots(pstart, ei, pos, nt):
    return pl.pallas_call(
        _slots_kernel,
        out_shape=(jax.ShapeDtypeStruct((8, nt * TM), jnp.int32), jax.ShapeDtypeStruct((nt, 8, TM), jnp.int32)),
        grid_spec=pltpu.PrefetchScalarGridSpec(
            num_scalar_prefetch=1,
            grid=(nt,),
            in_specs=[pl.BlockSpec((1, 8, TM), lambda i, ps: (i, 0, 0)), pl.BlockSpec((1, 8, TM), lambda i, ps: (i, 0, 0))],
            out_specs=(pl.BlockSpec((8, TM), lambda i, ps: (0, i)), pl.BlockSpec((1, 8, TM), lambda i, ps: (i, 0, 0)))),
        compiler_params=_cparams(("arbitrary",)),
        name="moe_slots",
    )(pstart, ei, pos)


def _sc_dispatch(hp, dest, n_slots):
    t, w = hp.shape
    wh = w // 2
    mesh = plsc.VectorSubcoreMesh(core_axis_name="core", subcore_axis_name="subcore", num_cores=SC_CORES,
                                  num_subcores=SC_SUBCORES)
    idx = [dest[kq:kq + 1] for kq in range(TOP_K)]
    half = jax.ShapeDtypeStruct((n_slots, wh), hp.dtype)

    @pl.kernel(out_type=(half, half), mesh=mesh, scratch_types=[])
    def scatter_rows(hp_hbm, *rest):
        idx_hbm, xs_hbm = rest[:TOP_K], rest[TOP_K:]
        for c in range(2):
            def body(x_vmem, *i_vmem, c=c):
                for iv in i_vmem:
                    pltpu.sync_copy(x_vmem, xs_hbm[c].at[iv.at[0]])

            pltpu.emit_pipeline(
                body,
                grid=(t // SC_WINDOW,),
                in_specs=[pl.BlockSpec((SC_WINDOW, wh), lambda i, c=c: (i, c))]
                + [pl.BlockSpec((1, SC_WINDOW), lambda i: (0, i))] * TOP_K,
                out_specs=[],
                core_axis_name=("core", "subcore"),
                dimension_semantics=(pltpu.PARALLEL,),
            )(hp_hbm, *idx_hbm)

    return scatter_rows(hp, *idx)


def _expert_kernel(be_ref, nb_ref, bv_ref, xa_ref, xb_ref, wgu_ref, wd_ref, ys_ref):
    i = pl.program_id(0)

    @pl.when(i < nb_ref[0])
    def _():
        live = lax.broadcasted_iota(jnp.int32, (MOE_BM, 1), 0) < bv_ref[i]
        la, ha = _unpack_bf16_pairs(jnp.where(live, xa_ref[...], jnp.uint32(0)))
        lb, hb = _unpack_bf16_pairs(jnp.where(live, xb_ref[...], jnp.uint32(0)))
        gu = (jnp.dot(la, wgu_ref[0, 0:256, :], preferred_element_type=f32)
              + jnp.dot(lb, wgu_ref[0, 256:512, :], preferred_element_type=f32)
              + jnp.dot(ha, wgu_ref[0, 512:768, :], preferred_element_type=f32)
              + jnp.dot(hb, wgu_ref[0, 768:1024, :], preferred_element_type=f32))
        gg, uu = gu[:, 0:EXPERT_FF], gu[:, EXPERT_FF:2 * EXPERT_FF]
        act = gg * _sigmoid(gg) * uu
        ys_ref[...] = _mm(act, wd_ref[0])

    @pl.when(i >= nb_ref[0])
    def _():
        ys_ref[...] = jnp.zeros_like(ys_ref)


def _experts(block_expert, nb_used, block_valid, xs, wgu, wd):
    xa, xb = xs
    n_slots = xa.shape[0]
    d = wd.shape[2]
    return pl.pallas_call(
        _expert_kernel,
        out_shape=jax.ShapeDtypeStruct((n_slots, d), f32),
        grid_spec=pltpu.PrefetchScalarGridSpec(
            num_scalar_prefetch=3,
            grid=(n_slots // MOE_BM,),
            in_specs=[pl.BlockSpec((MOE_BM, xa.shape[1]), lambda i, be, nb, bv: (i, 0)),
                      pl.BlockSpec((MOE_BM, xb.shape[1]), lambda i, be, nb, bv: (i, 0)),
                      pl.BlockSpec((1,) + wgu.shape[1:], lambda i, be, nb, bv: (be[i], 0, 0)),
                      pl.BlockSpec((1,) + wd.shape[1:], lambda i, be, nb, bv: (be[i], 0, 0))],
            out_specs=pl.BlockSpec((MOE_BM, d), lambda i, be, nb, bv: (i, 0))),
        compiler_params=_cparams(("arbitrary",)),
        name="moe_experts",
    )(block_expert, nb_used, block_valid, xa, xb, wgu, wd)


def _combine_kernel(dest_ref, ys_ref, hp_ref, gt_ref, x1_ref, m_ref, wsgu_ref, wsd_ref, pg_ref, o_ref,
                    idx_ref, buf_ref, isem, rsem):
    i = pl.program_id(0)
    cp = pltpu.make_async_copy(dest_ref.at[i], idx_ref, isem)
    cp.start()
    cp.wait()

    def issue(t, c):
        for kq in range(TOP_K):
            _row_copy(ys_ref, idx_ref[kq * TM + t], buf_ref.at[kq], t, rsem).start()
        return c

    def drain(t, c):
        for kq in range(TOP_K):
            _row_copy(ys_ref, 0, buf_ref.at[kq], 0, rsem).wait()
        return c

    lax.fori_loop(0, TM, issue, 0)
    lo, hi = _unpack_bf16_pairs(hp_ref[...])
    gu = (jnp.dot(lo, wsgu_ref[0:512, :], preferred_element_type=f32)
          + jnp.dot(hi, wsgu_ref[512:1024, :], preferred_element_type=f32))
    gg, uu = gu[:, 0:EXPERT_FF], gu[:, EXPERT_FF:2 * EXPERT_FF]
    f = _mm(gg * _sigmoid(gg) * uu, wsd_ref[...])
    lax.fori_loop(0, TM, drain, 0)
    gt = gt_ref[...]
    routed = jnp.zeros_like(f)
    for kq in range(TOP_K):
        routed = routed + buf_ref[kq] * gt[:, kq:kq + 1]
    f = routed + f
    m = m_ref[0]
    o_ref[...] = x1_ref[...] + m[5:6] * (_rms(f) * pg_ref[...])


def _combine(dest, ys, hp, gt, x1, mt, p, nt):
    d = x1.shape[1]
    anyspec = pl.BlockSpec(memory_space=pl.ANY)
    full = lambda a: pl.BlockSpec(a.shape, lambda i: (0,) * a.ndim)
    ws = [p["sh_wgu"], p["sh_wd"], p["post_ffn_g"]]
    return pl.pallas_call(
        _combine_kernel,
        out_shape=jax.ShapeDtypeStruct((nt * TM, d), f32),
        grid=(nt,),
        in_specs=[anyspec, anyspec,
                  pl.BlockSpec((TM, d // 2), lambda i: (i, 0)),
                  pl.BlockSpec((TM, 128), lambda i: (i, 0)),
                  pl.BlockSpec((TM, d), lambda i: (i, 0)),
                  pl.BlockSpec((1, 8, d), lambda i: (i, 0, 0))] + [full(w) for w in ws],
        out_specs=pl.BlockSpec((TM, d), lambda i: (i, 0)),
        scratch_shapes=[pltpu.SMEM((8 * TM,), jnp.int32), pltpu.VMEM((TOP_K, TM, d), f32),
                        pltpu.SemaphoreType.DMA, pltpu.SemaphoreType.DMA],
        compiler_params=_cparams(("arbitrary",)),
        name="moe_combine",
    )(dest, ys, hp, gt, x1, mt, *ws)


def _moe(hp, x1, mt, consts, p, nt):
    t = nt * TM
    ei, pos, gt, cnt = _router(hp, p["router_wt"], p["router_b"], consts, nt)
    n_assign = t * TOP_K
    n_blocks = -(-(n_assign + N_EXPERTS * (MOE_BM - 1)) // MOE_BM)
    n_slots = n_blocks * MOE_BM
    counts = cnt[:, 0].astype(jnp.int32)
    padded = (counts + MOE_BM - 1) // MOE_BM * MOE_BM
    pend = jnp.cumsum(padded)
    pstart = pend - padded
    nb_used = (pend[-1:] // MOE_BM).astype(jnp.int32)
    block_row0 = jnp.arange(n_blocks, dtype=jnp.int32) * MOE_BM
    block_expert = jnp.minimum(jnp.sum((pend[None, :] <= block_row0[:, None]).astype(jnp.int32), axis=1),
                               N_EXPERTS - 1)
    run_end = (pstart + counts)[block_expert]
    block_valid = jnp.clip(run_end - block_row0, 0, MOE_BM).astype(jnp.int32)
    dest, dest_tiles = _slots(pstart.astype(jnp.int32), ei, pos, nt)
    xs = _sc_dispatch(hp[:t], dest, n_slots)
    ys = _experts(block_expert, nb_used, block_valid, xs, p["wgu"], p["wd"])
    return _combine(dest_tiles.reshape(nt, 8 * TM), ys, hp, gt, x1, mt, p, nt)


def _np_consts(l_lat, l_ctx):
    n = RWKV_WIDTH
    i = np.arange(n)
    bd = (i[:, None] // 64 == i[None, :] // 64).astype(np.float32)
    t_r, t_c = (i % 64)[:, None], (i % 64)[None, :]
    tri = np.zeros((2, 6, n, n), np.float32)
    for d in range(2):
        before = (t_c < t_r) if d == 0 else (t_c > t_r)
        tri[d, 0] = bd * before
        tri[d, 1] = bd * (before | (t_c == t_r))
        tri[d, 2] = bd * before * (t_r // 8 == t_c // 8)
        for lvl, blk in enumerate((8, 16, 32)):
            tri[d, 3 + lvl] = bd * before * (t_r // (2 * blk) == t_c // (2 * blk)) * (t_r // blk != t_c // blk)
    j = np.arange(CHUNK)
    csi = np.stack([(j[None, :] <= j[:, None]), (j[None, :] >= j[:, None])]).astype(np.float32)
    tt = np.arange(TM)
    ut = (tt[:, None] < tt[None, :]).astype(np.float32)
    ee = np.arange(N_EXPERTS)
    lt = (ee[None, :] < ee[:, None]).astype(np.float32)
    jj = np.arange(TM + 2 * POOL_HALO)[None, :]
    band = np.stack([((jj >= tt[:, None] + POOL_HALO - w // 2) & (jj <= tt[:, None] + POOL_HALO + w // 2 - 1))
                     for w in POOL_WINDOWS]).astype(np.float32)

    def counts(length):
        t = np.arange(length)[:, None]
        half = np.repeat(np.array(POOL_WINDOWS) // 2, 64)[None, :]
        return (np.clip(t + half, 0, length) - np.clip(t - half, 0, length)).astype(np.float32)

    cnt = np.concatenate([counts(l_lat), counts(l_ctx)], axis=0)
    return dict(bd=jnp.asarray(bd), eye=jnp.eye(n, dtype=f32), tri=jnp.asarray(tri), csi=jnp.asarray(csi),
                avg=jnp.asarray(bd / 64.0), ut=jnp.asarray(ut, dtype=bf16), lt=jnp.asarray(lt, dtype=bf16),
                band=jnp.asarray(band), cnt=jnp.asarray(cnt))


def _rope_tables(l_lat):
    rows = l_lat // GRID_W
    row = jnp.repeat(jnp.arange(rows, dtype=f32), GRID_W)
    col = jnp.tile(jnp.arange(GRID_W, dtype=f32), rows)
    n_freq = QK_ROPE // 4
    inv = ROPE_THETA ** (-jnp.arange(n_freq, dtype=f32) / n_freq)
    ang = jnp.concatenate([row[:, None] * inv, col[:, None] * inv], -1)
    cos = jnp.concatenate([jnp.cos(ang), jnp.ones((TM, 16), f32)], 0)
    sin = jnp.concatenate([jnp.sin(ang), jnp.zeros((TM, 16), f32)], 0)
    n = cos.shape[0]
    ct = jnp.concatenate([jnp.ones((n, QK_NOPE), f32), cos, cos, jnp.zeros((n, 32), f32)], 1)
    st = jnp.concatenate([jnp.zeros((n, QK_NOPE), f32), -sin, sin, jnp.zeros((n, 32), f32)], 1)
    qs = MLA_SCALE * LOG2E
    return jnp.stack([ct * qs, st * qs, ct, st])


def _layer_params(i, a):
    d = a["w_in"].shape[1]
    p = {}
    row = lambda v: v.reshape(1, -1).astype(f32)
    for name in ("pre_mix_g", "post_mix_g", "pre_ffn_g", "post_ffn_g", "mla_q_norm", "mla_kv_norm"):
        p[name] = row(a[name][i])
    w_in = a["w_in"][i]
    zc = lambda n: jnp.zeros((d, n), f32)
    p["w_in_p"] = jnp.concatenate(
        [w_in[:, 0:MLA_IN], zc(128 - QK_ROPE), w_in[:, MLA_IN:MLA_IN + POOL_WIDTH],
         w_in[:, MLA_IN + POOL_WIDTH:], zc(RWKV_PAD - RWKV_IN)], 1).astype(bf16)
    wq = a["mla_w_q_b"][i].reshape(Q_LORA, MLA_HEADS, QK_NOPE + QK_ROPE)
    zq = jnp.zeros((Q_LORA, MLA_HEADS, 32), f32)
    half = QK_ROPE // 2
    p["wq_p"] = jnp.concatenate([wq, zq], 2).reshape(Q_LORA, -1).astype(bf16)
    p["wq_s"] = jnp.concatenate([jnp.zeros_like(wq[:, :, :QK_NOPE]), wq[:, :, QK_NOPE + half:],
                                 wq[:, :, QK_NOPE:QK_NOPE + half], zq], 2).reshape(Q_LORA, -1).astype(bf16)
    wkv = a["mla_w_kv_b"][i].reshape(KV_LORA, MLA_HEADS, QK_NOPE + V_HEAD)
    p["wk_p"] = jnp.concatenate([wkv[:, :, :QK_NOPE], jnp.zeros((KV_LORA, MLA_HEADS, 64), f32)], 2
                                ).reshape(KV_LORA, -1).astype(bf16)
    p["wv_p"] = jnp.concatenate([wkv[:, :, QK_NOPE:], jnp.zeros((KV_LORA, MLA_HEADS, 64), f32)], 2
                                ).reshape(KV_LORA, -1).astype(bf16)
    e_p = np.zeros((128, MLA_HEADS * HEAD_PAD), np.float32)
    e_s = np.zeros_like(e_p)
    for h in range(MLA_HEADS):
        for j in range(QK_ROPE):
            e_p[j, h * HEAD_PAD + QK_NOPE + j] = 1.0
            e_s[(j + half) % QK_ROPE, h * HEAD_PAD + QK_NOPE + j] = 1.0
    p["e_p"], p["e_s"] = jnp.asarray(e_p, dtype=bf16), jnp.asarray(e_s, dtype=bf16)
    pw = a["pool_w"][i]
    p["pool_w_bd"] = jax.scipy.linalg.block_diag(*[pw[g] for g in range(pw.shape[0])]).astype(bf16)
    p["pool_scale"] = row(a["pool_scale"][i])
    p["mu_p"] = jnp.pad(a["rwkv_mu"][i], (0, RWKV_PAD - RWKV_IN)).reshape(1, -1)
    p["w0"] = a["rwkv_w0"][i].reshape(2, 1, RWKV_WIDTH)
    p["a0"] = a["rwkv_a0"][i].reshape(2, 1, RWKV_WIDTH)
    z32 = jnp.zeros((32, RWKV_WIDTH), f32)
    w2, a2 = a["rwkv_w2"][i], a["rwkv_a2"][i]
    p["w2_p"] = jnp.stack([jnp.concatenate([w2[0], z32, z32, z32]), jnp.concatenate([z32, w2[1], z32, z32])])
    p["a2_p"] = jnp.stack([jnp.concatenate([z32, z32, a2[0], z32]), jnp.concatenate([z32, z32, z32, a2[1]])])
    p["g2_p"] = jnp.concatenate([a["rwkv_g2"][i], jnp.zeros((64, RWKV_WIDTH), f32)])
    for name in ("k_k", "k_a", "r_k"):
        p[name] = row(a["rwkv_" + name][i])
    p["ln_w"], p["ln_b"] = row(a["rwkv_ln_w"][i]), row(a["rwkv_ln_b"][i])
    p["w_out"] = a["w_out"][i].astype(bf16)
    p["router_wt"] = a["router_w"][i].T.astype(bf16)
    p["router_b"] = a["router_bias"][i].reshape(-1, 1).astype(f32)
    p["wgu"] = jnp.concatenate([a["exp_w_gate"][i], a["exp_w_up"][i]], 2).astype(bf16)
    p["wd"] = a["exp_w_down"][i].astype(bf16)
    p["sh_wgu"] = jnp.concatenate([a["sh_w_gate"][i], a["sh_w_up"][i]], 1).astype(bf16)
    p["sh_wd"] = a["sh_w_down"][i].astype(bf16)
    return p


def kernel(x, c, ctx, c_ctx, ada_w, ada_b, pre_mix_g, post_mix_g, pre_ffn_g, post_ffn_g, w_in, w_out, mla_q_norm, mla_w_q_b, mla_kv_norm, mla_w_kv_b, pool_w, pool_scale, rwkv_mu, rwkv_w0, rwkv_w2, rwkv_a0, rwkv_a2, rwkv_g2, rwkv_k_k, rwkv_k_a, rwkv_r_k, rwkv_ln_w, rwkv_ln_b, router_w, router_bias, exp_w_gate, exp_w_up, exp_w_down, sh_w_gate, sh_w_up, sh_w_down):
    arrs = dict(pre_mix_g=pre_mix_g, post_mix_g=post_mix_g, pre_ffn_g=pre_ffn_g, post_ffn_g=post_ffn_g, w_in=w_in,
                w_out=w_out, mla_q_norm=mla_q_norm, mla_w_q_b=mla_w_q_b, mla_kv_norm=mla_kv_norm,
                mla_w_kv_b=mla_w_kv_b, pool_w=pool_w, pool_scale=pool_scale, rwkv_mu=rwkv_mu, rwkv_w0=rwkv_w0,
                rwkv_w2=rwkv_w2, rwkv_a0=rwkv_a0, rwkv_a2=rwkv_a2, rwkv_g2=rwkv_g2, rwkv_k_k=rwkv_k_k,
                rwkv_k_a=rwkv_k_a, rwkv_r_k=rwkv_r_k, rwkv_ln_w=rwkv_ln_w, rwkv_ln_b=rwkv_ln_b, router_w=router_w,
                router_bias=router_bias, exp_w_gate=exp_w_gate, exp_w_up=exp_w_up, exp_w_down=exp_w_down,
                sh_w_gate=sh_w_gate, sh_w_up=sh_w_up, sh_w_down=sh_w_down)
    b_sz, l_lat, d = x.shape
    l_ctx = ctx.shape[1]
    assert l_lat % TM == 0 and l_ctx % TM == 0 and l_lat % GRID_W == 0 and b_sz < 16
    lt, ct = l_lat // TM, l_ctx // TM
    nlat, nctx = b_sz * lt, b_sz * ct
    nall = nlat + nctx
    ncc, nlc = l_ctx // CHUNK, l_lat // CHUNK
    consts = _np_consts(l_lat, l_ctx)
    rope = _rope_tables(l_lat)
    rope_idx = lambda i: jnp.where(i < nlat, i % lt, lt)
    cnt_idx = lambda i: jnp.where(i < nlat, i % lt, lt + (i - nlat) % ct)
    chunk_idx = lambda i: (jnp.where(i < nlat, i % lt, nlc * CHUNK // TM + (i - nlat) % ct),
                           jnp.where(i < nlat, i // lt, (i - nlat) // ct))

    c_all = jnp.zeros((16, d), f32).at[:b_sz].set(c).at[b_sz].set(c_ctx)
    mods = _ada_mod(c_all, ada_w, ada_b)
    tile_row = np.concatenate([np.repeat(np.arange(b_sz), lt), np.full(nctx, b_sz)])
    xa = jnp.concatenate([x.reshape(b_sz * l_lat, d), ctx.reshape(b_sz * l_ctx, d)], 0)

    for i in range(DEPTH):
        last = i == DEPTH - 1
        p = _layer_params(i, arrs)
        mt = jnp.pad(mods[i][tile_row].reshape(nall, 6, d), ((0, 0), (0, 2), (0, 0)))
        nt = nlat if last else nall
        q, k, v, zp, zr = _inproj(xa, mt, rope, rope_idx, chunk_idx, ncc + nlc, b_sz, p, nall)
        segs_lat = [(l_lat, min(512, l_lat), lambda t: t // lt), (l_ctx, l_ctx if l_ctx <= 512 else TM,
                                                                   lambda t: nlat * TM // l_ctx + t // lt)]
        att = _attention(q, k, v, nlat, 0, segs_lat)
        if not last:
            segs_ctx = [(l_ctx, l_ctx if l_ctx <= 512 else TM, lambda t: nlat * TM // l_ctx + t // ct)]
            att = jnp.concatenate([att, _attention(q, k, v, nctx, nlat, segs_ctx)], 0)
        py = _pool(zp, consts["band"], consts["cnt"], cnt_idx, p["pool_w_bd"], p["pool_scale"], nt, nlat, lt, ct)
        y, bv, g = _rwkv(zr, consts, p, ncc, nlc)
        x1, hp = _outproj(att, py, y, bv, g, xa, mt, chunk_idx, consts, p, nt)
        xa = _moe(hp, x1, mt, consts, p, nt)
    return xa[:b_sz * l_lat].reshape(b_sz, l_lat, d)
```

```python
import functools

import numpy as np
import jax
import jax.numpy as jnp
from jax import lax
from jax.experimental import pallas as pl
from jax.experimental.pallas import tpu as pltpu
from jax.experimental.pallas import tpu_sc as plsc

f32 = jnp.float32
bf16 = jnp.bfloat16
HIGHEST = lax.Precision.HIGHEST

DEPTH = 2
GRID_W = 64
NORM_EPS = 1e-6
MLA_HEADS = 8
Q_LORA = 384
KV_LORA = 256
QK_NOPE = 64
QK_ROPE = 32
V_HEAD = 64
ROPE_THETA = 10000.0
MLA_SCALE = (QK_NOPE + QK_ROPE) ** -0.5
MLA_IN = Q_LORA + KV_LORA + QK_ROPE
HEAD_PAD = 128
POOL_WINDOWS = (2, 4, 8, 16)
POOL_WIDTH = 256
POOL_HALO = 8
RWKV_HEADS = 4
RWKV_HEAD = 64
RWKV_WIDTH = 256
RWKV_IN = 960
RWKV_PAD = 1024
RWKV_GN_EPS = 64e-5
CHUNK = 64
RWKV_BATCHES_PER_STEP = 2
N_EXPERTS = 64
TOP_K = 6
N_GROUPS = 8
TOPK_GROUPS = 4
ROUTED_SCALE = 2.5
EXPERT_FF = 256
TM = 256
MOE_BM = 512
SC_CORES, SC_SUBCORES = 2, 16
SC_WINDOW = 128
Z_COLS = 2048
VMEM_LIMIT = 48 * 1024 * 1024
LOG2E = 1.4426950408889634
EXP_M05 = 0.6065306597126334


def _cparams(sem, vmem=VMEM_LIMIT):
    return pltpu.CompilerParams(dimension_semantics=sem, vmem_limit_bytes=vmem)


def _mm(a, b):
    return jnp.dot(a.astype(bf16), b.astype(bf16), preferred_element_type=f32)


def _mm_nt(a, b):
    return lax.dot_general(a.astype(bf16), b.astype(bf16), (((1,), (1,)), ((), ())), preferred_element_type=f32)


def _mm_tn(a, b):
    return lax.dot_general(a.astype(bf16), b.astype(bf16), (((0,), (0,)), ((), ())), preferred_element_type=f32)


def _mmf(a, b):
    return jnp.dot(a, b, precision=HIGHEST, preferred_element_type=f32)


def _split_hi_lo(a):
    hi = a.astype(bf16)
    return hi, (a - hi.astype(f32)).astype(bf16)


def _mm_x01(a, w01):
    hi, lo = _split_hi_lo(a)
    w = w01.astype(bf16)
    return jnp.dot(hi, w, preferred_element_type=f32) + jnp.dot(lo, w, preferred_element_type=f32)


def _mm_01x(w01, a):
    hi, lo = _split_hi_lo(a)
    w = w01.astype(bf16)
    return jnp.dot(w, hi, preferred_element_type=f32) + jnp.dot(w, lo, preferred_element_type=f32)


def _rms(x):
    return x * lax.rsqrt(jnp.mean(x * x, axis=-1, keepdims=True) + NORM_EPS)


def _sigmoid(x):
    return 1.0 / (1.0 + jnp.exp(-x))


def _ada_kernel(c_ref, w_ref, b_ref, o_ref):
    c = c_ref[...]
    s = c * _sigmoid(c)
    o_ref[0] = _mm(s, w_ref[0]) + b_ref[0]


def _ada_mod(c_all, ada_w, ada_b):
    depth, d, n = ada_w.shape
    tn = 1024
    return pl.pallas_call(
        _ada_kernel,
        out_shape=jax.ShapeDtypeStruct((depth, 16, n), f32),
        grid=(depth, n // tn),
        in_specs=[pl.BlockSpec((16, d), lambda i, j: (0, 0)),
                  pl.BlockSpec((1, d, tn), lambda i, j: (i, 0, j)),
                  pl.BlockSpec((1, 1, tn), lambda i, j: (i, 0, j))],
        out_specs=pl.BlockSpec((1, 16, tn), lambda i, j: (i, 0, j)),
        compiler_params=_cparams(("arbitrary", "arbitrary")),
        name="ada_mod",
    )(c_all, ada_w, ada_b.reshape(depth, 1, n))


def _inproj_kernel(x_ref, m_ref, rope_ref, g_ref, win_ref, qg_ref, wq_ref, wqs_ref, kvg_ref, wk_ref, wv_ref,
                   e_ref, es_ref, q_ref, k_ref, v_ref, zp_ref, zr_ref):
    m = m_ref[0]
    h = _rms(x_ref[...]) * g_ref[...]
    h = h * (1.0 + m[1:2]) + m[0:1]
    z = _mm(h, win_ref[...])
    zp_ref[...] = z[:, 768:1024]
    for c4 in range(TM // CHUNK):
        zr_ref[c4, 0] = z[c4 * CHUNK:(c4 + 1) * CHUNK, 1024:2048]
    tile8 = lambda t: jnp.concatenate([t] * MLA_HEADS, axis=1)
    qn = (_rms(z[:, 0:Q_LORA]) * qg_ref[...]).astype(bf16)
    q = _mm(qn, wq_ref[...]) * tile8(rope_ref[0]) + _mm(qn, wqs_ref[...]) * tile8(rope_ref[1])
    q_ref[...] = q.astype(bf16)
    kvn = (_rms(z[:, Q_LORA:Q_LORA + KV_LORA]) * kvg_ref[...]).astype(bf16)
    kpe = z[:, 640:768].astype(bf16)
    k = _mm(kvn, wk_ref[...]) + _mm(kpe, e_ref[...]) * tile8(rope_ref[2]) + _mm(kpe, es_ref[...]) * tile8(rope_ref[3])
    k_ref[...] = k.astype(bf16)
    lane = lax.broadcasted_iota(jnp.int32, (1, MLA_HEADS * HEAD_PAD), 1)
    v_ref[...] = (_mm(kvn, wv_ref[...]) + jnp.where(lane % HEAD_PAD == V_HEAD, 1.0, 0.0)).astype(bf16)


def _inproj(xa, mt, rope, rope_idx, chunk_idx, npos, b_sz, p, nt):
    t_all, d = xa.shape
    cpt = TM // CHUNK
    full = lambda a: pl.BlockSpec(a.shape, lambda i: (0,) * a.ndim)
    ws = [p["pre_mix_g"], p["w_in_p"], p["mla_q_norm"], p["wq_p"], p["wq_s"], p["mla_kv_norm"], p["wk_p"], p["wv_p"],
          p["e_p"], p["e_s"]]
    return pl.pallas_call(
        _inproj_kernel,
        out_shape=(jax.ShapeDtypeStruct((t_all, MLA_HEADS * HEAD_PAD), bf16),
                   jax.ShapeDtypeStruct((t_all, MLA_HEADS * HEAD_PAD), bf16),
                   jax.ShapeDtypeStruct((t_all, MLA_HEADS * HEAD_PAD), bf16),
                   jax.ShapeDtypeStruct((t_all, POOL_WIDTH), f32),
                   jax.ShapeDtypeStruct((npos, b_sz, CHUNK, RWKV_PAD), f32)),
        grid=(nt,),
        in_specs=[pl.BlockSpec((TM, d), lambda i: (i, 0)),
                  pl.BlockSpec((1, 8, d), lambda i: (i, 0, 0)),
                  pl.BlockSpec((4, TM, HEAD_PAD), lambda i: (0, rope_idx(i), 0))] + [full(w) for w in ws],
        out_specs=(pl.BlockSpec((TM, 1024), lambda i: (i, 0)),
                   pl.BlockSpec((TM, 1024), lambda i: (i, 0)),
                   pl.BlockSpec((TM, 1024), lambda i: (i, 0)),
                   pl.BlockSpec((TM, POOL_WIDTH), lambda i: (i, 0)),
                   pl.BlockSpec((cpt, 1, CHUNK, RWKV_PAD), lambda i: chunk_idx(i) + (0, 0))),
        compiler_params=_cparams(("arbitrary",)),
        name="in_proj",
    )(xa, mt, rope, *ws)


def _attn_kernel(*refs, seg_tiles):
    n_in = 1 + 2 * len(seg_tiles)
    q_ref, kv, o_ref, m_ref, acc_ref = refs[0], refs[1:n_in], refs[n_in], refs[n_in + 1], refs[n_in + 2]
    m_ref[...] = jnp.full(m_ref.shape, -jnp.inf, f32)
    acc_ref[...] = jnp.zeros(acc_ref.shape, f32)

    def step(k_ref, v_ref, r0, tk):
        for h in range(MLA_HEADS):
            hs = slice(h * HEAD_PAD, (h + 1) * HEAD_PAD)
            s = lax.dot_general(q_ref[:, hs], k_ref[pl.ds(r0, tk), hs], (((1,), (1,)), ((), ())),
                                preferred_element_type=f32)
            m_prev = m_ref[h]
            m_new = jnp.maximum(m_prev, jnp.max(s, axis=-1, keepdims=True))
            pr = jnp.exp2(s - jnp.concatenate([m_new] * (tk // HEAD_PAD), axis=1))
            acc_ref[h] = (jnp.exp2(m_prev - m_new) * acc_ref[h]
                          + jnp.dot(pr.astype(bf16), v_ref[pl.ds(r0, tk), hs], preferred_element_type=f32))
            m_ref[h] = m_new

    for si, (nk, tk) in enumerate(seg_tiles):
        k_ref, v_ref = kv[2 * si], kv[2 * si + 1]
        if nk == 1:
            step(k_ref, v_ref, 0, tk)
        else:
            def body(j, c, k_ref=k_ref, v_ref=v_ref, tk=tk):
                step(k_ref, v_ref, pl.multiple_of(j * tk, tk), tk)
                return c
            lax.fori_loop(0, nk, body, 0)
    for pr2 in range(MLA_HEADS // 2):
        a0, a1 = acc_ref[2 * pr2], acc_ref[2 * pr2 + 1]
        o_ref[:, pr2 * 128:(pr2 + 1) * 128] = jnp.concatenate(
            [a0[:, :V_HEAD] / a0[:, V_HEAD:V_HEAD + 1], a1[:, :V_HEAD] / a1[:, V_HEAD:V_HEAD + 1]], axis=1).astype(bf16)


def _attention(q, k, v, n_q_tiles, q_tile0, segs):
    in_specs = [pl.BlockSpec((TM, 1024), lambda i: (i + q_tile0, 0))]
    args = [q]
    seg_tiles = []
    for rows, tk, bidx in segs:
        in_specs.append(pl.BlockSpec((rows, 1024), lambda i, bidx=bidx: (bidx(i), 0), pipeline_mode=pl.Buffered(1)))
        in_specs.append(pl.BlockSpec((rows, 1024), lambda i, bidx=bidx: (bidx(i), 0), pipeline_mode=pl.Buffered(1)))
        args += [k, v]
        seg_tiles.append((rows // tk, tk))
    return pl.pallas_call(
        functools.partial(_attn_kernel, seg_tiles=tuple(seg_tiles)),
        out_shape=jax.ShapeDtypeStruct((n_q_tiles * TM, MLA_HEADS * V_HEAD), bf16),
        grid=(n_q_tiles,),
        in_specs=in_specs,
        out_specs=pl.BlockSpec((TM, 512), lambda i: (i, 0)),
        scratch_shapes=[pltpu.VMEM((MLA_HEADS, TM, HEAD_PAD), f32), pltpu.VMEM((MLA_HEADS, TM, HEAD_PAD), f32)],
        compiler_params=_cparams(("arbitrary",)),
        name="mla_attention",
    )(*args)


def _pool_kernel(z_ref, zp_ref, zn_ref, band_ref, cnt_ref, pw_ref, ps_ref, o_ref, *, nlat, lt, ct):
    i = pl.program_id(0)
    is_lat = i < nlat
    j = jnp.where(is_lat, i % lt, (i - nlat) % ct)
    n = jnp.where(is_lat, lt, ct)
    z = z_ref[...]
    prev = zp_ref[...] * jnp.where(j == 0, 0.0, 1.0)
    nxt = zn_ref[...] * jnp.where(j == n - 1, 0.0, 1.0)
    zh = jnp.concatenate([prev, z, nxt], axis=0)
    lane_grp = lax.broadcasted_iota(jnp.int32, (1, POOL_WIDTH), 1) // 64
    tot = jnp.zeros_like(z)
    for g in range(len(POOL_WINDOWS)):
        tot = tot + _mm_01x(band_ref[g], zh * jnp.where(lane_grp == g, 1.0, 0.0))
    diff = tot / cnt_ref[...] - z
    o_ref[...] = (_mm(diff, pw_ref[...]) * ps_ref[...]).astype(bf16)


def _pool(zp, band, cnt, cnt_idx, pw_bd, pscale, nt, nlat, lt, ct):
    t_all = zp.shape[0]
    nb8 = t_all // POOL_HALO
    r = TM // POOL_HALO
    return pl.pallas_call(
        functools.partial(_pool_kernel, nlat=nlat, lt=lt, ct=ct),
        out_shape=jax.ShapeDtypeStruct((nt * TM, POOL_WIDTH), bf16),
        grid=(nt,),
        in_specs=[pl.BlockSpec((TM, POOL_WIDTH), lambda i: (i, 0)),
                  pl.BlockSpec((POOL_HALO, POOL_WIDTH), lambda i: (jnp.maximum(i * r - 1, 0), 0)),
                  pl.BlockSpec((POOL_HALO, POOL_WIDTH), lambda i: (jnp.minimum((i + 1) * r, nb8 - 1), 0)),
                  pl.BlockSpec(band.shape, lambda i: (0, 0, 0)),
                  pl.BlockSpec((TM, POOL_WIDTH), lambda i: (cnt_idx(i), 0)),
                  pl.BlockSpec((POOL_WIDTH, POOL_WIDTH), lambda i: (0, 0)),
                  pl.BlockSpec((1, POOL_WIDTH), lambda i: (0, 0))],
        out_specs=pl.BlockSpec((TM, POOL_WIDTH), lambda i: (i, 0)),
        compiler_params=_cparams(("arbitrary",)),
        name="pool_mixer",
    )(zp, zp, zp, band, cnt, pw_bd, pscale)


def _rwkv_pos(d, s, *, ncc, nlc):
    in_ctx = s < ncc
    jc = jnp.where(d == 0, s, ncc - 1 - s)
    jl = jnp.where(d == 0, s - ncc, nlc - 1 - (s - ncc))
    pos = jnp.where(in_ctx, nlc + jc, jl)
    first = jnp.where(in_ctx, jc == 0, jl == 0)
    last = jnp.where(in_ctx, jc == ncc - 1, jl == nlc - 1)
    return pos, first, last


def _tri_inverse(lm, tri_ref, eye):
    n = -(lm * tri_ref[0, 2])
    n2 = _mm(n, n)
    n4 = _mm(n2, n2)
    t = _mm(_mm(eye + n, eye + n2), eye + n4)
    for lvl in range(3):
        t = t - _mm(_mm(t, lm * tri_ref[0, 3 + lvl]), t)
    return t


def _rwkv_kernel(z_ref, zp_ref, zn_ref, bd_ref, eye_ref, tri_ref, csi_ref, mu_ref, w0_ref, w2_ref, a0_ref, a2_ref,
                 g2_ref, kk_ref, ka_ref, rk_ref, y_ref, bv_ref, g_ref, s_ref, *, ncc, nlc):
    d, s = pl.program_id(0), pl.program_id(2)
    _, first, last = _rwkv_pos(d, s, ncc=ncc, nlc=nlc)

    @pl.when(s == 0)
    def _():
        s_ref[...] = jnp.zeros_like(s_ref)

    for nb in range(z_ref.shape[1]):
        prev_row = zp_ref[0, nb, 7:8, :] * jnp.where(first, 0.0, 1.0)
        next_row = zn_ref[0, nb, 0:1, :] * jnp.where(last, 0.0, 1.0)
        y, bv, g, st = _rwkv_chunk(z_ref[0, nb], prev_row, next_row, s_ref[nb], d, bd_ref, eye_ref, tri_ref, csi_ref,
                                   mu_ref, w0_ref, w2_ref, a0_ref, a2_ref, g2_ref, kk_ref, ka_ref, rk_ref)
        y_ref[0, 0, nb] = y
        bv_ref[0, 0, nb] = bv
        g_ref[0, 0, nb] = g
        s_ref[nb] = st


def _rwkv_chunk(z, prev_row, next_row, st, d, bd_ref, eye_ref, tri_ref, csi_ref, mu_ref, w0_ref, w2_ref, a0_ref,
                a2_ref, g2_ref, kk_ref, ka_ref, rk_ref):
    c = CHUNK
    row = lax.broadcasted_iota(jnp.int32, (c, 1), 0)
    zp = jnp.where(row == 0, prev_row, pltpu.roll(z, 1, 0))
    zn = jnp.where(row == c - 1, next_row, pltpu.roll(z, c - 1, 0))
    zs = z + mu_ref[...] * (0.5 * (zp + zn) - z)
    r, k, v = zs[:, 0:256], zs[:, 256:512], zs[:, 512:768]
    lora, gate_in = zs[:, 768:896], zs[:, 896:1024]
    bd = bd_ref[...]
    eye = eye_ref[...]

    g = _mm(_sigmoid(gate_in), g2_ref[...])
    e = EXP_M05 * _sigmoid(w0_ref[0] + _mm(jnp.tanh(lora), w2_ref[0]))
    a = _sigmoid(a0_ref[0] + _mm(lora, a2_ref[0]))
    kd = k * (1.0 + (a - 1.0) * ka_ref[...])
    kkr = k * kk_ref[...]
    kk = kkr / jnp.maximum(jnp.sqrt(_mm_x01(kkr * kkr, bd)), 1e-12)
    bv = _mm_x01(r * kd * rk_ref[...], bd) * v
    bb = kk * a

    cs = _mm_01x(csi_ref[0], e)
    tot = jnp.where(d == 0, cs[c - 1:c, :], cs[0:1, :])
    a_s = jnp.exp(e - cs) * kk
    b_s = bb * jnp.exp(cs)
    k_s = kd * jnp.exp(cs)
    r_s = r * jnp.exp(-cs)
    b_e = bb * jnp.exp(cs - tot)
    k_e = kd * jnp.exp(cs - tot)
    g_end = jnp.exp(-tot)

    rep4 = lambda t: jnp.concatenate([t] * RWKV_HEADS, axis=0)
    fold4 = lambda t: t[0:c] + t[c:2 * c] + t[2 * c:3 * c] + t[3 * c:4 * c]
    a4, r4, v4 = rep4(a_s) * bd, rep4(r_s) * bd, rep4(v) * bd
    b4, k4 = rep4(b_s), rep4(k_s)
    tri_s, tri_i = tri_ref[0, 0], tri_ref[0, 1]
    lm = _mm_nt(a4, b4) * tri_s
    akm = _mm_nt(a4, k4) * tri_s
    rbm = _mm_nt(r4, b4) * tri_i
    rkm = _mm_nt(r4, k4) * tri_i
    t = _tri_inverse(lm, tri_ref, eye)
    w4 = _mm(t, a4)
    u4 = _mm(t, _mm(akm, v4))
    q4 = r4 - _mm(rbm, w4)
    y4 = _mm(rkm, v4) - _mm(rbm, u4)
    w_all, u_all, q_all, y0 = fold4(w4), fold4(u4), fold4(q4), fold4(y4)
    g_bd = eye * g_end - bd * _mm_tn(w_all, b_e)
    h_bd = bd * (_mm_tn(v, k_e) - _mm_tn(u_all, b_e))
    return _mm_nt(q_all, st) + y0, bv, g, _mm(st, g_bd) + h_bd


def _rwkv(zr, consts, p, ncc, nlc):
    npos, b_sz = zr.shape[0], zr.shape[1]
    nbat = RWKV_BATCHES_PER_STEP
    kw = dict(ncc=ncc, nlc=nlc)
    pos = lambda d, s: _rwkv_pos(d, s, **kw)[0]
    full = lambda a: pl.BlockSpec(a.shape, lambda d, b, s: (0,) * a.ndim)
    by_dir = lambda a: pl.BlockSpec((1,) + a.shape[1:], lambda d, b, s: (d,) + (0,) * (a.ndim - 1))
    out = jax.ShapeDtypeStruct((2, npos, b_sz, CHUNK, RWKV_WIDTH), f32)
    ospec = pl.BlockSpec((1, 1, nbat, CHUNK, RWKV_WIDTH), lambda d, b, s: (d, pos(d, s), b, 0, 0))
    last8 = CHUNK // 8 - 1
    return pl.pallas_call(
        functools.partial(_rwkv_kernel, **kw),
        out_shape=(out, out, out),
        grid=(2, b_sz // nbat, npos),
        in_specs=[pl.BlockSpec((1, nbat, CHUNK, RWKV_PAD), lambda d, b, s: (pos(d, s), b, 0, 0)),
                  pl.BlockSpec((1, nbat, 8, RWKV_PAD), lambda d, b, s: (jnp.maximum(pos(d, s) - 1, 0), b, last8, 0)),
                  pl.BlockSpec((1, nbat, 8, RWKV_PAD), lambda d, b, s: (jnp.minimum(pos(d, s) + 1, npos - 1), b, 0, 0)),
                  full(consts["bd"]), full(consts["eye"]), by_dir(consts["tri"]), by_dir(consts["csi"]),
                  full(p["mu_p"]), by_dir(p["w0"]), by_dir(p["w2_p"]), by_dir(p["a0"]), by_dir(p["a2_p"]),
                  full(p["g2_p"]), full(p["k_k"]), full(p["k_a"]), full(p["r_k"])],
        out_specs=(ospec, ospec, ospec),
        scratch_shapes=[pltpu.VMEM((nbat, RWKV_WIDTH, RWKV_WIDTH), f32)],
        compiler_params=_cparams(("arbitrary", "arbitrary", "arbitrary")),
        name="rwkv7_chunked",
    )(zr, zr, zr, consts["bd"], consts["eye"], consts["tri"], consts["csi"], p["mu_p"], p["w0"], p["w2_p"], p["a0"],
      p["a2_p"], p["g2_p"], p["k_k"], p["k_a"], p["r_k"])


def _pack_bf16_pairs(lo, hi):
    lo_b = pltpu.bitcast(lo.astype(bf16).astype(f32), jnp.uint32)
    hi_b = pltpu.bitcast(hi.astype(bf16).astype(f32), jnp.uint32)
    return (hi_b & jnp.uint32(0xFFFF0000)) | (lo_b >> 16)


def _unpack_bf16_pairs(w):
    lo = pltpu.bitcast(w << 16, f32).astype(bf16)
    hi = pltpu.bitcast(w & jnp.uint32(0xFFFF0000), f32).astype(bf16)
    return lo, hi


def _outproj_kernel(a_ref, py_ref, y_ref, bv_ref, g_ref, x_ref, m_ref, avg_ref, lnw_ref, lnb_ref, wo_ref, pmg_ref,
                    pfg_ref, x1_ref, hp_ref):
    m = m_ref[0]
    rows = lambda ref, dd: jnp.concatenate([ref[dd, c4, 0] for c4 in range(TM // CHUNK)], axis=0)
    ysum = rows(y_ref, 0) + rows(y_ref, 1)
    avg = avg_ref[...]
    dev = ysum - _mm_x01(ysum, avg)
    var = _mm_x01(dev * dev, avg)
    yn = dev * lax.rsqrt(var + RWKV_GN_EPS) * lnw_ref[...] + lnb_ref[...]
    rw = (yn + rows(bv_ref, 0) + rows(bv_ref, 1)) * rows(g_ref, 0)
    o = (jnp.dot(a_ref[...], wo_ref[0:512, :], preferred_element_type=f32)
         + jnp.dot(py_ref[...], wo_ref[512:768, :], preferred_element_type=f32)
         + _mm(rw, wo_ref[768:1024, :]))
    x1 = x_ref[...] + m[2:3] * (_rms(o) * pmg_ref[...])
    x1_ref[...] = x1
    h = (_rms(x1) * pfg_ref[...]) * (1.0 + m[4:5]) + m[3:4]
    hp_ref[...] = _pack_bf16_pairs(h[:, 0:512], h[:, 512:1024])


def _outproj(att, py, y, bv, g, xa, mt, chunk_idx, consts, p, nt):
    d = xa.shape[1]
    cpt = TM // CHUNK
    cspec = lambda nd: pl.BlockSpec((nd, cpt, 1, CHUNK, RWKV_WIDTH), lambda i: (0,) + chunk_idx(i) + (0, 0))
    full = lambda a: pl.BlockSpec(a.shape, lambda i: (0,) * a.ndim)
    ws = [consts["avg"], p["ln_w"], p["ln_b"], p["w_out"], p["post_mix_g"], p["pre_ffn_g"]]
    return pl.pallas_call(
        _outproj_kernel,
        out_shape=(jax.ShapeDtypeStruct((nt * TM, d), f32), jax.ShapeDtypeStruct((nt * TM, d // 2), jnp.uint32)),
        grid=(nt,),
        in_specs=[pl.BlockSpec((TM, 512), lambda i: (i, 0)),
                  pl.BlockSpec((TM, POOL_WIDTH), lambda i: (i, 0)),
                  cspec(2), cspec(2), cspec(1),
                  pl.BlockSpec((TM, d), lambda i: (i, 0)),
                  pl.BlockSpec((1, 8, d), lambda i: (i, 0, 0))] + [full(w) for w in ws],
        out_specs=(pl.BlockSpec((TM, d), lambda i: (i, 0)), pl.BlockSpec((TM, d // 2), lambda i: (i, 0))),
        compiler_params=_cparams(("arbitrary",)),
        name="out_proj",
    )(att, py, y, bv, g, xa, mt, *ws)


def _router_kernel(hp_ref, rw_ref, rb_ref, ut_ref, lt_ref, ei_ref, pos_ref, gt_ref, cnt_ref, run_ref):
    i = pl.program_id(0)

    @pl.when(i == 0)
    def _():
        run_ref[...] = jnp.zeros_like(run_ref)

    tm = hp_ref.shape[0]
    ne, ng = N_EXPERTS, N_GROUPS
    pg = ne // ng
    lo, hi = _unpack_bf16_pairs(hp_ref[...])
    logits = (lax.dot_general(rw_ref[:, 0:512], lo, (((1,), (1,)), ((), ())), preferred_element_type=f32)
              + lax.dot_general(rw_ref[:, 512:1024], hi, (((1,), (1,)), ((), ())), preferred_element_type=f32))
    scores = _sigmoid(logits)
    sel = scores + rb_ref[...]
    neg = -jnp.inf

    s3 = sel.reshape(ng, pg, tm)
    io = lax.broadcasted_iota(jnp.int32, (ng, pg, tm), 1)
    m1 = jnp.max(s3, axis=1, keepdims=True)
    i1 = jnp.min(jnp.where(s3 == m1, io, pg), axis=1, keepdims=True)
    m2 = jnp.max(jnp.where(io == i1, neg, s3), axis=1, keepdims=True)
    gs = (m1 + m2).reshape(ng, tm)
    gi = lax.broadcasted_iota(jnp.int32, (ng, tm), 0)
    grank = jnp.zeros((ng, tm), f32)
    for j in range(ng):
        rj = gs[j:j + 1, :]
        grank = grank + jnp.where((rj > gs) | ((rj == gs) & (j < gi)), 1.0, 0.0)
    gsel = jnp.where(grank < TOPK_GROUPS, 1.0, 0.0)
    gsel3 = jnp.broadcast_to(gsel.reshape(ng, 1, tm), (ng, pg, tm)).reshape(ne, tm)
    msk = jnp.where(gsel3 > 0.5, sel, neg)
    ei = lax.broadcasted_iota(jnp.int32, (ne, tm), 0)
    erank = jnp.zeros((ne, tm), f32)
    for j in range(ne):
        rj = msk[j:j + 1, :]
        erank = erank + jnp.where((rj > msk) | ((rj == msk) & (j < ei)), 1.0, 0.0)
    chosen = erank < TOP_K
    chf = jnp.where(chosen, 1.0, 0.0)
    graw = jnp.where(chosen, scores, 0.0)
    gate = graw / jnp.sum(graw, axis=0, keepdims=True) * ROUTED_SCALE

    pos = run_ref[...] + _mm(chf, ut_ref[...])
    tot = jnp.sum(chf, axis=1, keepdims=True)
    run_new = run_ref[...] + tot
    run_ref[...] = run_new
    cnt_ref[...] = run_new[:, 0:128]
    rk = _mm(lt_ref[...], chf)
    eif = ei.astype(f32)
    rows_e, rows_p, rows_g = [], [], []
    for kq in range(TOP_K):
        mk = chosen & (rk == float(kq))
        rows_e.append(jnp.sum(jnp.where(mk, eif, 0.0), axis=0, keepdims=True))
        rows_p.append(jnp.sum(jnp.where(mk, pos, 0.0), axis=0, keepdims=True))
        rows_g.append(jnp.sum(jnp.where(mk, gate, 0.0), axis=0, keepdims=True))
    zrow = jnp.zeros((8 - TOP_K, tm), f32)
    ei_ref[0] = jnp.concatenate(rows_e + [zrow], axis=0).astype(jnp.int32)
    pos_ref[0] = jnp.concatenate(rows_p + [zrow], axis=0).astype(jnp.int32)
    gpad = jnp.concatenate(rows_g + [jnp.zeros((128 - TOP_K, tm), f32)], axis=0)
    gt_ref[...] = gpad.T


def _router(hp, rwt, rb, consts, nt):
    full = lambda a: pl.BlockSpec(a.shape, lambda i: (0,) * a.ndim)
    return pl.pallas_call(
        _router_kernel,
        out_shape=(jax.ShapeDtypeStruct((nt, 8, TM), jnp.int32), jax.ShapeDtypeStruct((nt, 8, TM), jnp.int32),
                   jax.ShapeDtypeStruct((nt * TM, 128), f32), jax.ShapeDtypeStruct((N_EXPERTS, 128), f32)),
        grid=(nt,),
        in_specs=[pl.BlockSpec((TM, 512), lambda i: (i, 0)), full(rwt), full(rb), full(consts["ut"]),
                  full(consts["lt"])],
        out_specs=(pl.BlockSpec((1, 8, TM), lambda i: (i, 0, 0)), pl.BlockSpec((1, 8, TM), lambda i: (i, 0, 0)),
                   pl.BlockSpec((TM, 128), lambda i: (i, 0)), pl.BlockSpec((N_EXPERTS, 128), lambda i: (0, 0))),
        scratch_shapes=[pltpu.VMEM((N_EXPERTS, TM), f32)],
        compiler_params=_cparams(("arbitrary",)),
        name="moe_router",
    )(hp, rwt, rb, consts["ut"], consts["lt"])


def _slots_kernel(ps_ref, ei_ref, pos_ref, d_ref):
    ei = ei_ref[0]
    slot = pos_ref[0]
    for e in range(N_EXPERTS):
        slot = slot + jnp.where(ei == e, ps_ref[e], 0)
    d_ref[...] = slot


def _slots(pstart, ei, pos, nt):
    return pl.pallas_call(
        _slots_kernel,
        out_shape=jax.ShapeDtypeStruct((8, nt * TM), jnp.int32),
        grid_spec=pltpu.PrefetchScalarGridSpec(
            num_scalar_prefetch=1,
            grid=(nt,),
            in_specs=[pl.BlockSpec((1, 8, TM), lambda i, ps: (i, 0, 0)), pl.BlockSpec((1, 8, TM), lambda i, ps: (i, 0, 0))],
            out_specs=pl.BlockSpec((8, TM), lambda i, ps: (0, i))),
        compiler_params=_cparams(("arbitrary",)),
        name="moe_slots",
    )(pstart, ei, pos)


def _sc_dispatch(hp, dest, n_slots):
    t, w = hp.shape
    wh = w // 2
    mesh = plsc.VectorSubcoreMesh(core_axis_name="core", subcore_axis_name="subcore", num_cores=SC_CORES,
                                  num_subcores=SC_SUBCORES)
    idx = [dest[kq:kq + 1] for kq in range(TOP_K)]
    half = jax.ShapeDtypeStruct((n_slots, wh), hp.dtype)

    @pl.kernel(out_type=(half, half), mesh=mesh, scratch_types=[])
    def scatter_rows(hp_hbm, *rest):
        idx_hbm, xs_hbm = rest[:TOP_K], rest[TOP_K:]
        for c in range(2):
            def body(x_vmem, *i_vmem, c=c):
                for iv in i_vmem:
                    pltpu.sync_copy(x_vmem, xs_hbm[c].at[iv.at[0]])

            pltpu.emit_pipeline(
                body,
                grid=(t // SC_WINDOW,),
                in_specs=[pl.BlockSpec((SC_WINDOW, wh), lambda i, c=c: (i, c))]
                + [pl.BlockSpec((1, SC_WINDOW), lambda i: (0, i))] * TOP_K,
                out_specs=[],
                core_axis_name=("core", "subcore"),
                dimension_semantics=(pltpu.PARALLEL,),
            )(hp_hbm, *idx_hbm)

    return scatter_rows(hp, *idx)


def _expert_kernel(be_ref, nb_ref, bv_ref, xa_ref, xb_ref, wgu_ref, wd_ref, ya_ref, yb_ref):
    i = pl.program_id(0)

    @pl.when(i < nb_ref[0])
    def _():
        live = lax.broadcasted_iota(jnp.int32, (MOE_BM, 1), 0) < bv_ref[i]
        la, ha = _unpack_bf16_pairs(jnp.where(live, xa_ref[...], jnp.uint32(0)))
        lb, hb = _unpack_bf16_pairs(jnp.where(live, xb_ref[...], jnp.uint32(0)))
        gu = (jnp.dot(la, wgu_ref[0, 0:256, :], preferred_element_type=f32)
              + jnp.dot(lb, wgu_ref[0, 256:512, :], preferred_element_type=f32)
              + jnp.dot(ha, wgu_ref[0, 512:768, :], preferred_element_type=f32)
              + jnp.dot(hb, wgu_ref[0, 768:1024, :], preferred_element_type=f32))
        gg, uu = gu[:, 0:EXPERT_FF], gu[:, EXPERT_FF:2 * EXPERT_FF]
        act = gg * _sigmoid(gg) * uu
        y = _mm(act, wd_ref[0])
        ya_ref[...] = _pack_bf16_pairs(y[:, 0:256], y[:, 512:768])
        yb_ref[...] = _pack_bf16_pairs(y[:, 256:512], y[:, 768:1024])

    @pl.when(i >= nb_ref[0])
    def _():
        ya_ref[...] = jnp.zeros_like(ya_ref)
        yb_ref[...] = jnp.zeros_like(yb_ref)


def _experts(block_expert, nb_used, block_valid, xs, wgu, wd):
    xa, xb = xs
    n_slots = xa.shape[0]
    half = jax.ShapeDtypeStruct((n_slots, xa.shape[1]), jnp.uint32)
    hspec = pl.BlockSpec((MOE_BM, xa.shape[1]), lambda i, be, nb, bv: (i, 0))
    return pl.pallas_call(
        _expert_kernel,
        out_shape=(half, half),
        grid_spec=pltpu.PrefetchScalarGridSpec(
            num_scalar_prefetch=3,
            grid=(n_slots // MOE_BM,),
            in_specs=[pl.BlockSpec((MOE_BM, xa.shape[1]), lambda i, be, nb, bv: (i, 0)),
                      pl.BlockSpec((MOE_BM, xb.shape[1]), lambda i, be, nb, bv: (i, 0)),
                      pl.BlockSpec((1,) + wgu.shape[1:], lambda i, be, nb, bv: (be[i], 0, 0)),
                      pl.BlockSpec((1,) + wd.shape[1:], lambda i, be, nb, bv: (be[i], 0, 0))],
            out_specs=(hspec, hspec)),
        compiler_params=_cparams(("arbitrary",)),
        name="moe_experts",
    )(block_expert, nb_used, block_valid, xa, xb, wgu, wd)


def _sc_gather(ys, dest):
    t = dest.shape[1]
    wh = ys[0].shape[1]
    mesh = plsc.VectorSubcoreMesh(core_axis_name="core", subcore_axis_name="subcore", num_cores=SC_CORES,
                                  num_subcores=SC_SUBCORES)
    idx = [dest[kq:kq + 1] for kq in range(TOP_K)]
    out = jax.ShapeDtypeStruct((t, wh), ys[0].dtype)

    @pl.kernel(out_type=(out,) * (2 * TOP_K), mesh=mesh, scratch_types=[])
    def gather_rows(ya_hbm, yb_hbm, *rest):
        idx_hbm, out_hbm = rest[:TOP_K], rest[TOP_K:]
        for c, y_hbm in enumerate((ya_hbm, yb_hbm)):
            for kq in range(TOP_K):
                def body(i_vmem, o_vmem, y_hbm=y_hbm):
                    pltpu.sync_copy(y_hbm.at[i_vmem.at[0]], o_vmem)

                pltpu.emit_pipeline(
                    body,
                    grid=(t // SC_WINDOW,),
                    in_specs=[pl.BlockSpec((1, SC_WINDOW), lambda i: (0, i))],
                    out_specs=[pl.BlockSpec((SC_WINDOW, wh), lambda i: (i, 0))],
                    core_axis_name=("core", "subcore"),
                    dimension_semantics=(pltpu.PARALLEL,),
                )(idx_hbm[kq], out_hbm[c * TOP_K + kq])

    outs = gather_rows(ys[0], ys[1], *idx)
    return outs[:TOP_K], outs[TOP_K:]


def _combine_kernel(*refs):
    ga, gb = refs[0:TOP_K], refs[TOP_K:2 * TOP_K]
    hp_ref, gt_ref, x1_ref, m_ref, wsgu_ref, wsd_ref, pg_ref, o_ref = refs[2 * TOP_K:]
    lo, hi = _unpack_bf16_pairs(hp_ref[...])
    gu = (jnp.dot(lo, wsgu_ref[0:512, :], preferred_element_type=f32)
          + jnp.dot(hi, wsgu_ref[512:1024, :], preferred_element_type=f32))
    gg, uu = gu[:, 0:EXPERT_FF], gu[:, EXPERT_FF:2 * EXPERT_FF]
    f = _mm(gg * _sigmoid(gg) * uu, wsd_ref[...])
    gt = gt_ref[...]
    parts = [jnp.zeros((TM, 256), f32) for _ in range(4)]
    for kq in range(TOP_K):
        g = gt[:, kq:kq + 1]
        la, ha = _unpack_bf16_pairs(ga[kq][...])
        lb, hb = _unpack_bf16_pairs(gb[kq][...])
        for q, v in enumerate((la, lb, ha, hb)):
            parts[q] = parts[q] + v.astype(f32) * g
    f = f + jnp.concatenate(parts, axis=1)
    m = m_ref[0]
    o_ref[...] = x1_ref[...] + m[5:6] * (_rms(f) * pg_ref[...])


def _combine(ga, gb, hp, gt, x1, mt, p, nt):
    d = x1.shape[1]
    full = lambda a: pl.BlockSpec(a.shape, lambda i: (0,) * a.ndim)
    ws = [p["sh_wgu"], p["sh_wd"], p["post_ffn_g"]]
    gspec = pl.BlockSpec((TM, ga[0].shape[1]), lambda i: (i, 0))
    return pl.pallas_call(
        _combine_kernel,
        out_shape=jax.ShapeDtypeStruct((nt * TM, d), f32),
        grid=(nt,),
        in_specs=[gspec] * (2 * TOP_K)
        + [pl.BlockSpec((TM, d // 2), lambda i: (i, 0)),
           pl.BlockSpec((TM, 128), lambda i: (i, 0)),
           pl.BlockSpec((TM, d), lambda i: (i, 0)),
           pl.BlockSpec((1, 8, d), lambda i: (i, 0, 0))] + [full(w) for w in ws],
        out_specs=pl.BlockSpec((TM, d), lambda i: (i, 0)),
        compiler_params=_cparams(("arbitrary",)),
        name="moe_combine",
    )(*ga, *gb, hp, gt, x1, mt, *ws)


def _moe(hp, x1, mt, consts, p, nt):
    t = nt * TM
    ei, pos, gt, cnt = _router(hp, p["router_wt"], p["router_b"], consts, nt)
    n_assign = t * TOP_K
    n_blocks = -(-(n_assign + N_EXPERTS * (MOE_BM - 1)) // MOE_BM)
    n_slots = n_blocks * MOE_BM
    counts = cnt[:, 0].astype(jnp.int32)
    padded = (counts + MOE_BM - 1) // MOE_BM * MOE_BM
    pend = jnp.cumsum(padded)
    pstart = pend - padded
    nb_used = (pend[-1:] // MOE_BM).astype(jnp.int32)
    block_row0 = jnp.arange(n_blocks, dtype=jnp.int32) * MOE_BM
    block_expert = jnp.minimum(jnp.sum((pend[None, :] <= block_row0[:, None]).astype(jnp.int32), axis=1),
                               N_EXPERTS - 1)
    run_end = (pstart + counts)[block_expert]
    block_valid = jnp.clip(run_end - block_row0, 0, MOE_BM).astype(jnp.int32)
    dest = _slots(pstart.astype(jnp.int32), ei, pos, nt)
    xs = _sc_dispatch(hp[:t], dest, n_slots)
    ys = _experts(block_expert, nb_used, block_valid, xs, p["wgu"], p["wd"])
    ga, gb = _sc_gather(ys, dest)
    return _combine(ga, gb, hp, gt, x1, mt, p, nt)


def _np_consts(l_lat, l_ctx):
    n = RWKV_WIDTH
    i = np.arange(n)
    bd = (i[:, None] // 64 == i[None, :] // 64).astype(np.float32)
    t_r, t_c = (i % 64)[:, None], (i % 64)[None, :]
    tri = np.zeros((2, 6, n, n), np.float32)
    for d in range(2):
        before = (t_c < t_r) if d == 0 else (t_c > t_r)
        tri[d, 0] = bd * before
        tri[d, 1] = bd * (before | (t_c == t_r))
        tri[d, 2] = bd * before * (t_r // 8 == t_c // 8)
        for lvl, blk in enumerate((8, 16, 32)):
            tri[d, 3 + lvl] = bd * before * (t_r // (2 * blk) == t_c // (2 * blk)) * (t_r // blk != t_c // blk)
    j = np.arange(CHUNK)
    csi = np.stack([(j[None, :] <= j[:, None]), (j[None, :] >= j[:, None])]).astype(np.float32)
    tt = np.arange(TM)
    ut = (tt[:, None] < tt[None, :]).astype(np.float32)
    ee = np.arange(N_EXPERTS)
    lt = (ee[None, :] < ee[:, None]).astype(np.float32)
    jj = np.arange(TM + 2 * POOL_HALO)[None, :]
    band = np.stack([((jj >= tt[:, None] + POOL_HALO - w // 2) & (jj <= tt[:, None] + POOL_HALO + w // 2 - 1))
                     for w in POOL_WINDOWS]).astype(np.float32)

    def counts(length):
        t = np.arange(length)[:, None]
        half = np.repeat(np.array(POOL_WINDOWS) // 2, 64)[None, :]
        return (np.clip(t + half, 0, length) - np.clip(t - half, 0, length)).astype(np.float32)

    cnt = np.concatenate([counts(l_lat), counts(l_ctx)], axis=0)
    return dict(bd=jnp.asarray(bd), eye=jnp.eye(n, dtype=f32), tri=jnp.asarray(tri), csi=jnp.asarray(csi),
                avg=jnp.asarray(bd / 64.0), ut=jnp.asarray(ut, dtype=bf16), lt=jnp.asarray(lt, dtype=bf16),
                band=jnp.asarray(band), cnt=jnp.asarray(cnt))


def _rope_tables(l_lat):
    rows = l_lat // GRID_W
    row = jnp.repeat(jnp.arange(rows, dtype=f32), GRID_W)
    col = jnp.tile(jnp.arange(GRID_W, dtype=f32), rows)
    n_freq = QK_ROPE // 4
    inv = ROPE_THETA ** (-jnp.arange(n_freq, dtype=f32) / n_freq)
    ang = jnp.concatenate([row[:, None] * inv, col[:, None] * inv], -1)
    cos = jnp.concatenate([jnp.cos(ang), jnp.ones((TM, 16), f32)], 0)
    sin = jnp.concatenate([jnp.sin(ang), jnp.zeros((TM, 16), f32)], 0)
    n = cos.shape[0]
    ct = jnp.concatenate([jnp.ones((n, QK_NOPE), f32), cos, cos, jnp.zeros((n, 32), f32)], 1)
    st = jnp.concatenate([jnp.zeros((n, QK_NOPE), f32), -sin, sin, jnp.zeros((n, 32), f32)], 1)
    qs = MLA_SCALE * LOG2E
    return jnp.stack([ct * qs, st * qs, ct, st])


def _layer_params(i, a):
    d = a["w_in"].shape[1]
    p = {}
    row = lambda v: v.reshape(1, -1).astype(f32)
    for name in ("pre_mix_g", "post_mix_g", "pre_ffn_g", "post_ffn_g", "mla_q_norm", "mla_kv_norm"):
        p[name] = row(a[name][i])
    w_in = a["w_in"][i]
    zc = lambda n: jnp.zeros((d, n), f32)
    p["w_in_p"] = jnp.concatenate(
        [w_in[:, 0:MLA_IN], zc(128 - QK_ROPE), w_in[:, MLA_IN:MLA_IN + POOL_WIDTH],
         w_in[:, MLA_IN + POOL_WIDTH:], zc(RWKV_PAD - RWKV_IN)], 1).astype(bf16)
    wq = a["mla_w_q_b"][i].reshape(Q_LORA, MLA_HEADS, QK_NOPE + QK_ROPE)
    zq = jnp.zeros((Q_LORA, MLA_HEADS, 32), f32)
    half = QK_ROPE // 2
    p["wq_p"] = jnp.concatenate([wq, zq], 2).reshape(Q_LORA, -1).astype(bf16)
    p["wq_s"] = jnp.concatenate([jnp.zeros_like(wq[:, :, :QK_NOPE]), wq[:, :, QK_NOPE + half:],
                                 wq[:, :, QK_NOPE:QK_NOPE + half], zq], 2).reshape(Q_LORA, -1).astype(bf16)
    wkv = a["mla_w_kv_b"][i].reshape(KV_LORA, MLA_HEADS, QK_NOPE + V_HEAD)
    p["wk_p"] = jnp.concatenate([wkv[:, :, :QK_NOPE], jnp.zeros((KV_LORA, MLA_HEADS, 64), f32)], 2
                                ).reshape(KV_LORA, -1).astype(bf16)
    p["wv_p"] = jnp.concatenate([wkv[:, :, QK_NOPE:], jnp.zeros((KV_LORA, MLA_HEADS, 64), f32)], 2
                                ).reshape(KV_LORA, -1).astype(bf16)
    e_p = np.zeros((128, MLA_HEADS * HEAD_PAD), np.float32)
    e_s = np.zeros_like(e_p)
    for h in range(MLA_HEADS):
        for j in range(QK_ROPE):
            e_p[j, h * HEAD_PAD + QK_NOPE + j] = 1.0
            e_s[(j + half) % QK_ROPE, h * HEAD_PAD + QK_NOPE + j] = 1.0
    p["e_p"], p["e_s"] = jnp.asarray(e_p, dtype=bf16), jnp.asarray(e_s, dtype=bf16)
    pw = a["pool_w"][i]
    p["pool_w_bd"] = jax.scipy.linalg.block_diag(*[pw[g] for g in range(pw.shape[0])]).astype(bf16)
    p["pool_scale"] = row(a["pool_scale"][i])
    p["mu_p"] = jnp.pad(a["rwkv_mu"][i], (0, RWKV_PAD - RWKV_IN)).reshape(1, -1)
    p["w0"] = a["rwkv_w0"][i].reshape(2, 1, RWKV_WIDTH)
    p["a0"] = a["rwkv_a0"][i].reshape(2, 1, RWKV_WIDTH)
    z32 = jnp.zeros((32, RWKV_WIDTH), f32)
    w2, a2 = a["rwkv_w2"][i], a["rwkv_a2"][i]
    p["w2_p"] = jnp.stack([jnp.concatenate([w2[0], z32, z32, z32]), jnp.concatenate([z32, w2[1], z32, z32])])
    p["a2_p"] = jnp.stack([jnp.concatenate([z32, z32, a2[0], z32]), jnp.concatenate([z32, z32, z32, a2[1]])])
    p["g2_p"] = jnp.concatenate([a["rwkv_g2"][i], jnp.zeros((64, RWKV_WIDTH), f32)])
    for name in ("k_k", "k_a", "r_k"):
        p[name] = row(a["rwkv_" + name][i])
    p["ln_w"], p["ln_b"] = row(a["rwkv_ln_w"][i]), row(a["rwkv_ln_b"][i])
    p["w_out"] = a["w_out"][i].astype(bf16)
    p["router_wt"] = a["router_w"][i].T.astype(bf16)
    p["router_b"] = a["router_bias"][i].reshape(-1, 1).astype(f32)
    p["wgu"] = jnp.concatenate([a["exp_w_gate"][i], a["exp_w_up"][i]], 2).astype(bf16)
    p["wd"] = a["exp_w_down"][i].astype(bf16)
    p["sh_wgu"] = jnp.concatenate([a["sh_w_gate"][i], a["sh_w_up"][i]], 1).astype(bf16)
    p["sh_wd"] = a["sh_w_down"][i].astype(bf16)
    return p


def kernel(x, c, ctx, c_ctx, ada_w, ada_b, pre_mix_g, post_mix_g, pre_ffn_g, post_ffn_g, w_in, w_out, mla_q_norm, mla_w_q_b, mla_kv_norm, mla_w_kv_b, pool_w, pool_scale, rwkv_mu, rwkv_w0, rwkv_w2, rwkv_a0, rwkv_a2, rwkv_g2, rwkv_k_k, rwkv_k_a, rwkv_r_k, rwkv_ln_w, rwkv_ln_b, router_w, router_bias, exp_w_gate, exp_w_up, exp_w_down, sh_w_gate, sh_w_up, sh_w_down):
    arrs = dict(pre_mix_g=pre_mix_g, post_mix_g=post_mix_g, pre_ffn_g=pre_ffn_g, post_ffn_g=post_ffn_g, w_in=w_in,
                w_out=w_out, mla_q_norm=mla_q_norm, mla_w_q_b=mla_w_q_b, mla_kv_norm=mla_kv_norm,
                mla_w_kv_b=mla_w_kv_b, pool_w=pool_w, pool_scale=pool_scale, rwkv_mu=rwkv_mu, rwkv_w0=rwkv_w0,
                rwkv_w2=rwkv_w2, rwkv_a0=rwkv_a0, rwkv_a2=rwkv_a2, rwkv_g2=rwkv_g2, rwkv_k_k=rwkv_k_k,
                rwkv_k_a=rwkv_k_a, rwkv_r_k=rwkv_r_k, rwkv_ln_w=rwkv_ln_w, rwkv_ln_b=rwkv_ln_b, router_w=router_w,
                router_bias=router_bias, exp_w_gate=exp_w_gate, exp_w_up=exp_w_up, exp_w_down=exp_w_down,
                sh_w_gate=sh_w_gate, sh_w_up=sh_w_up, sh_w_down=sh_w_down)
    b_sz, l_lat, d = x.shape
    l_ctx = ctx.shape[1]
    assert l_lat % TM == 0 and l_ctx % TM == 0 and l_lat % GRID_W == 0 and b_sz < 16
    lt, ct = l_lat // TM, l_ctx // TM
    nlat, nctx = b_sz * lt, b_sz * ct
    nall = nlat + nctx
    ncc, nlc = l_ctx // CHUNK, l_lat // CHUNK
    consts = _np_consts(l_lat, l_ctx)
    rope = _rope_tables(l_lat)
    rope_idx = lambda i: jnp.where(i < nlat, i % lt, lt)
    cnt_idx = lambda i: jnp.where(i < nlat, i % lt, lt + (i - nlat) % ct)
    chunk_idx = lambda i: (jnp.where(i < nlat, i % lt, nlc * CHUNK // TM + (i - nlat) % ct),
                           jnp.where(i < nlat, i // lt, (i - nlat) // ct))

    c_all = jnp.zeros((16, d), f32).at[:b_sz].set(c).at[b_sz].set(c_ctx)
    mods = _ada_mod(c_all, ada_w, ada_b)
    tile_row = np.concatenate([np.repeat(np.arange(b_sz), lt), np.full(nctx, b_sz)])
    xa = jnp.concatenate([x.reshape(b_sz * l_lat, d), ctx.reshape(b_sz * l_ctx, d)], 0)

    for i in range(DEPTH):
        last = i == DEPTH - 1
        p = _layer_params(i, arrs)
        mt = jnp.pad(mods[i][tile_row].reshape(nall, 6, d), ((0, 0), (0, 2), (0, 0)))
        nt = nlat if last else nall
        q, k, v, zp, zr = _inproj(xa, mt, rope, rope_idx, chunk_idx, ncc + nlc, b_sz, p, nall)
        segs_lat = [(l_lat, min(512, l_lat), lambda t: t // lt), (l_ctx, l_ctx if l_ctx <= 512 else TM,
                                                                   lambda t: nlat * TM // l_ctx + t // lt)]
        att = _attention(q, k, v, nlat, 0, segs_lat)
        if not last:
            segs_ctx = [(l_ctx, l_ctx if l_ctx <= 512 else TM, lambda t: nlat * TM // l_ctx + t // ct)]
            att = jnp.concatenate([att, _attention(q, k, v, nctx, nlat, segs_ctx)], 0)
        py = _pool(zp, consts["band"], consts["cnt"], cnt_idx, p["pool_w_bd"], p["pool_scale"], nt, nlat, lt, ct)
        y, bv, g = _rwkv(zr, consts, p, ncc, nlc)
        x1, hp = _outproj(att, py, y, bv, g, xa, mt, chunk_idx, consts, p, nt)
        xa = _moe(hp, x1, mt, consts, p, nt)
    return xa[:b_sz * l_lat].reshape(b_sz, l_lat, d)
```

```python
import functools

import numpy as np
import jax
import jax.numpy as jnp
from jax import lax
from jax.experimental import pallas as pl
from jax.experimental.pallas import tpu as pltpu
from jax.experimental.pallas import tpu_sc as plsc

f32 = jnp.float32
bf16 = jnp.bfloat16
HIGHEST = lax.Precision.HIGHEST

DEPTH = 2
GRID_W = 64
NORM_EPS = 1e-6
MLA_HEADS = 8
Q_LORA = 384
KV_LORA = 256
QK_NOPE = 64
QK_ROPE = 32
V_HEAD = 64
ROPE_THETA = 10000.0
MLA_SCALE = (QK_NOPE + QK_ROPE) ** -0.5
MLA_IN = Q_LORA + KV_LORA + QK_ROPE
HEAD_PAD = 128
POOL_WINDOWS = (2, 4, 8, 16)
POOL_WIDTH = 256
POOL_HALO = 8
RWKV_HEADS = 4
RWKV_HEAD = 64
RWKV_WIDTH = 256
RWKV_IN = 960
RWKV_PAD = 1024
RWKV_GN_EPS = 64e-5
CHUNK = 64
RWKV_BATCHES_PER_STEP = 2
N_EXPERTS = 64
TOP_K = 6
N_GROUPS = 8
TOPK_GROUPS = 4
ROUTED_SCALE = 2.5
EXPERT_FF = 256
TM = 256
MOE_BM = 512
SC_CORES, SC_SUBCORES = 2, 16
SC_WINDOW = 128
Z_COLS = 2048
VMEM_LIMIT = 48 * 1024 * 1024
LOG2E = 1.4426950408889634
EXP_M05 = 0.6065306597126334


def _cparams(sem, vmem=VMEM_LIMIT):
    return pltpu.CompilerParams(dimension_semantics=sem, vmem_limit_bytes=vmem)


def _mm(a, b):
    return jnp.dot(a.astype(bf16), b.astype(bf16), preferred_element_type=f32)


def _mm_nt(a, b):
    return lax.dot_general(a.astype(bf16), b.astype(bf16), (((1,), (1,)), ((), ())), preferred_element_type=f32)


def _mm_tn(a, b):
    return lax.dot_general(a.astype(bf16), b.astype(bf16), (((0,), (0,)), ((), ())), preferred_element_type=f32)


def _mmf(a, b):
    return jnp.dot(a, b, precision=HIGHEST, preferred_element_type=f32)


def _split_hi_lo(a):
    hi = a.astype(bf16)
    return hi, (a - hi.astype(f32)).astype(bf16)


def _mm_x01(a, w01):
    hi, lo = _split_hi_lo(a)
    w = w01.astype(bf16)
    return jnp.dot(hi, w, preferred_element_type=f32) + jnp.dot(lo, w, preferred_element_type=f32)


def _mm_01x(w01, a):
    hi, lo = _split_hi_lo(a)
    w = w01.astype(bf16)
    return jnp.dot(w, hi, preferred_element_type=f32) + jnp.dot(w, lo, preferred_element_type=f32)


def _rms(x):
    return x * lax.rsqrt(jnp.mean(x * x, axis=-1, keepdims=True) + NORM_EPS)


def _sigmoid(x):
    return 1.0 / (1.0 + jnp.exp(-x))


def _ada_kernel(c_ref, w_ref, b_ref, o_ref):
    c = c_ref[...]
    s = c * _sigmoid(c)
    o_ref[0] = _mm(s, w_ref[0]) + b_ref[0]


def _ada_mod(c_all, ada_w, ada_b):
    depth, d, n = ada_w.shape
    tn = 1024
    return pl.pallas_call(
        _ada_kernel,
        out_shape=jax.ShapeDtypeStruct((depth, 16, n), f32),
        grid=(depth, n // tn),
        in_specs=[pl.BlockSpec((16, d), lambda i, j: (0, 0)),
                  pl.BlockSpec((1, d, tn), lambda i, j: (i, 0, j)),
                  pl.BlockSpec((1, 1, tn), lambda i, j: (i, 0, j))],
        out_specs=pl.BlockSpec((1, 16, tn), lambda i, j: (i, 0, j)),
        compiler_params=_cparams(("arbitrary", "arbitrary")),
        name="ada_mod",
    )(c_all, ada_w, ada_b.reshape(depth, 1, n))


def _inproj_kernel(x_ref, m_ref, rope_ref, g_ref, win_ref, qg_ref, wq_ref, wqs_ref, kvg_ref, wk_ref, wv_ref,
                   e_ref, es_ref, q_ref, k_ref, v_ref, zp_ref, zr_ref):
    m = m_ref[0]
    h = _rms(x_ref[...]) * g_ref[...]
    h = h * (1.0 + m[1:2]) + m[0:1]
    z = _mm(h, win_ref[...])
    zp_ref[...] = z[:, 768:1024]
    for c4 in range(TM // CHUNK):
        zr_ref[c4, 0] = z[c4 * CHUNK:(c4 + 1) * CHUNK, 1024:2048]
    tile8 = lambda t: jnp.concatenate([t] * MLA_HEADS, axis=1)
    qn = (_rms(z[:, 0:Q_LORA]) * qg_ref[...]).astype(bf16)
    q = _mm(qn, wq_ref[...]) * tile8(rope_ref[0]) + _mm(qn, wqs_ref[...]) * tile8(rope_ref[1])
    q_ref[...] = q.astype(bf16)
    kvn = (_rms(z[:, Q_LORA:Q_LORA + KV_LORA]) * kvg_ref[...]).astype(bf16)
    kpe = z[:, 640:768].astype(bf16)
    k = _mm(kvn, wk_ref[...]) + _mm(kpe, e_ref[...]) * tile8(rope_ref[2]) + _mm(kpe, es_ref[...]) * tile8(rope_ref[3])
    k_ref[...] = k.astype(bf16)
    lane = lax.broadcasted_iota(jnp.int32, (1, MLA_HEADS * HEAD_PAD), 1)
    v_ref[...] = (_mm(kvn, wv_ref[...]) + jnp.where(lane % HEAD_PAD == V_HEAD, 1.0, 0.0)).astype(bf16)


def _inproj(xa, mt, rope, rope_idx, chunk_idx, npos, b_sz, p, nt):
    t_all, d = xa.shape
    cpt = TM // CHUNK
    full = lambda a: pl.BlockSpec(a.shape, lambda i: (0,) * a.ndim)
    ws = [p["pre_mix_g"], p["w_in_p"], p["mla_q_norm"], p["wq_p"], p["wq_s"], p["mla_kv_norm"], p["wk_p"], p["wv_p"],
          p["e_p"], p["e_s"]]
    return pl.pallas_call(
        _inproj_kernel,
        out_shape=(jax.ShapeDtypeStruct((t_all, MLA_HEADS * HEAD_PAD), bf16),
                   jax.ShapeDtypeStruct((t_all, MLA_HEADS * HEAD_PAD), bf16),
                   jax.ShapeDtypeStruct((t_all, MLA_HEADS * HEAD_PAD), bf16),
                   jax.ShapeDtypeStruct((t_all, POOL_WIDTH), f32),
                   jax.ShapeDtypeStruct((npos, b_sz, CHUNK, RWKV_PAD), f32)),
        grid=(nt,),
        in_specs=[pl.BlockSpec((TM, d), lambda i: (i, 0)),
                  pl.BlockSpec((1, 8, d), lambda i: (i, 0, 0)),
                  pl.BlockSpec((4, TM, HEAD_PAD), lambda i: (0, rope_idx(i), 0))] + [full(w) for w in ws],
        out_specs=(pl.BlockSpec((TM, 1024), lambda i: (i, 0)),
                   pl.BlockSpec((TM, 1024), lambda i: (i, 0)),
                   pl.BlockSpec((TM, 1024), lambda i: (i, 0)),
                   pl.BlockSpec((TM, POOL_WIDTH), lambda i: (i, 0)),
                   pl.BlockSpec((cpt, 1, CHUNK, RWKV_PAD), lambda i: chunk_idx(i) + (0, 0))),
        compiler_params=_cparams(("arbitrary",)),
        name="in_proj",
    )(xa, mt, rope, *ws)


def _attn_kernel(*refs):
    q_ref, kv, o_ref = refs[0], refs[1:-1], refs[-1]
    nt = (((1,), (1,)), ((), ()))
    outs = []
    for h in range(MLA_HEADS):
        hs = slice(h * HEAD_PAD, (h + 1) * HEAD_PAD)
        qh = q_ref[:, hs]
        scores = [lax.dot_general(qh, k_ref[:, hs], nt, preferred_element_type=f32) for k_ref in kv[0::2]]
        m = functools.reduce(jnp.maximum, [jnp.max(s, axis=-1, keepdims=True) for s in scores])
        acc = sum(jnp.dot(jnp.exp2((s - m).astype(bf16)), v_ref[:, hs], preferred_element_type=f32)
                  for s, v_ref in zip(scores, kv[1::2]))
        outs.append(acc[:, :V_HEAD] / acc[:, V_HEAD:V_HEAD + 1])
    for pr2 in range(MLA_HEADS // 2):
        o_ref[:, pr2 * 128:(pr2 + 1) * 128] = jnp.concatenate(outs[2 * pr2:2 * pr2 + 2], axis=1).astype(bf16)


def _attention(q, k, v, n_q_tiles, q_tile0, segs):
    in_specs = [pl.BlockSpec((TM, 1024), lambda i: (i + q_tile0, 0))]
    args = [q]
    for rows, bidx in segs:
        in_specs.append(pl.BlockSpec((rows, 1024), lambda i, bidx=bidx: (bidx(i), 0), pipeline_mode=pl.Buffered(1)))
        in_specs.append(pl.BlockSpec((rows, 1024), lambda i, bidx=bidx: (bidx(i), 0), pipeline_mode=pl.Buffered(1)))
        args += [k, v]
    return pl.pallas_call(
        _attn_kernel,
        out_shape=jax.ShapeDtypeStruct((n_q_tiles * TM, MLA_HEADS * V_HEAD), bf16),
        grid=(n_q_tiles,),
        in_specs=in_specs,
        out_specs=pl.BlockSpec((TM, 512), lambda i: (i, 0)),
        compiler_params=_cparams(("arbitrary",)),
        name="mla_attention",
    )(*args)


def _pool_kernel(z_ref, zp_ref, zn_ref, band_ref, cnt_ref, pw_ref, ps_ref, o_ref, *, nlat, lt, ct):
    i = pl.program_id(0)
    is_lat = i < nlat
    j = jnp.where(is_lat, i % lt, (i - nlat) % ct)
    n = jnp.where(is_lat, lt, ct)
    z = z_ref[...]
    prev = zp_ref[...] * jnp.where(j == 0, 0.0, 1.0)
    nxt = zn_ref[...] * jnp.where(j == n - 1, 0.0, 1.0)
    zh = jnp.concatenate([prev, z, nxt], axis=0)
    lane_grp = lax.broadcasted_iota(jnp.int32, (1, POOL_WIDTH), 1) // 64
    tot = jnp.zeros_like(z)
    for g in range(len(POOL_WINDOWS)):
        tot = tot + _mm_01x(band_ref[g], zh * jnp.where(lane_grp == g, 1.0, 0.0))
    diff = tot / cnt_ref[...] - z
    o_ref[...] = (_mm(diff, pw_ref[...]) * ps_ref[...]).astype(bf16)


def _pool(zp, band, cnt, cnt_idx, pw_bd, pscale, nt, nlat, lt, ct):
    t_all = zp.shape[0]
    nb8 = t_all // POOL_HALO
    r = TM // POOL_HALO
    return pl.pallas_call(
        functools.partial(_pool_kernel, nlat=nlat, lt=lt, ct=ct),
        out_shape=jax.ShapeDtypeStruct((nt * TM, POOL_WIDTH), bf16),
        grid=(nt,),
        in_specs=[pl.BlockSpec((TM, POOL_WIDTH), lambda i: (i, 0)),
                  pl.BlockSpec((POOL_HALO, POOL_WIDTH), lambda i: (jnp.maximum(i * r - 1, 0), 0)),
                  pl.BlockSpec((POOL_HALO, POOL_WIDTH), lambda i: (jnp.minimum((i + 1) * r, nb8 - 1), 0)),
                  pl.BlockSpec(band.shape, lambda i: (0, 0, 0)),
                  pl.BlockSpec((TM, POOL_WIDTH), lambda i: (cnt_idx(i), 0)),
                  pl.BlockSpec((POOL_WIDTH, POOL_WIDTH), lambda i: (0, 0)),
                  pl.BlockSpec((1, POOL_WIDTH), lambda i: (0, 0))],
        out_specs=pl.BlockSpec((TM, POOL_WIDTH), lambda i: (i, 0)),
        compiler_params=_cparams(("arbitrary",)),
        name="pool_mixer",
    )(zp, zp, zp, band, cnt, pw_bd, pscale)


def _rwkv_pos(d, s, *, ncc, nlc):
    in_ctx = s < ncc
    jc = jnp.where(d == 0, s, ncc - 1 - s)
    jl = jnp.where(d == 0, s - ncc, nlc - 1 - (s - ncc))
    pos = jnp.where(in_ctx, nlc + jc, jl)
    first = jnp.where(in_ctx, jc == 0, jl == 0)
    last = jnp.where(in_ctx, jc == ncc - 1, jl == nlc - 1)
    return pos, first, last


def _tri_inverse(lm, tri_ref, eye):
    n = -(lm * tri_ref[0, 2])
    n2 = _mm(n, n)
    n4 = _mm(n2, n2)
    t = _mm(_mm(eye + n, eye + n2), eye + n4)
    for lvl in range(3):
        t = t - _mm(_mm(t, lm * tri_ref[0, 3 + lvl]), t)
    return t


def _rwkv_kernel(z_ref, zp_ref, zn_ref, bd_ref, eye_ref, tri_ref, csi_ref, mu_ref, w0_ref, w2_ref, a0_ref, a2_ref,
                 g2_ref, kk_ref, ka_ref, rk_ref, y_ref, bv_ref, g_ref, s_ref, *, ncc, nlc):
    d, s = pl.program_id(0), pl.program_id(2)
    _, first, last = _rwkv_pos(d, s, ncc=ncc, nlc=nlc)

    @pl.when(s == 0)
    def _():
        s_ref[...] = jnp.zeros_like(s_ref)

    for nb in range(z_ref.shape[1]):
        prev_row = zp_ref[0, nb, 7:8, :] * jnp.where(first, 0.0, 1.0)
        next_row = zn_ref[0, nb, 0:1, :] * jnp.where(last, 0.0, 1.0)
        y, bv, g, st = _rwkv_chunk(z_ref[0, nb], prev_row, next_row, s_ref[nb], d, bd_ref, eye_ref, tri_ref, csi_ref,
                                   mu_ref, w0_ref, w2_ref, a0_ref, a2_ref, g2_ref, kk_ref, ka_ref, rk_ref)
        y_ref[0, 0, nb] = y
        bv_ref[0, 0, nb] = bv
        g_ref[0, 0, nb] = g
        s_ref[nb] = st


def _rwkv_chunk(z, prev_row, next_row, st, d, bd_ref, eye_ref, tri_ref, csi_ref, mu_ref, w0_ref, w2_ref, a0_ref,
                a2_ref, g2_ref, kk_ref, ka_ref, rk_ref):
    c = CHUNK
    row = lax.broadcasted_iota(jnp.int32, (c, 1), 0)
    zp = jnp.where(row == 0, prev_row, pltpu.roll(z, 1, 0))
    zn = jnp.where(row == c - 1, next_row, pltpu.roll(z, c - 1, 0))
    zs = z + mu_ref[...] * (0.5 * (zp + zn) - z)
    r, k, v = zs[:, 0:256], zs[:, 256:512], zs[:, 512:768]
    lora, gate_in = zs[:, 768:896], zs[:, 896:1024]
    bd = bd_ref[...]
    eye = eye_ref[...]

    g = _mm(_sigmoid(gate_in), g2_ref[...])
    e = EXP_M05 * _sigmoid(w0_ref[0] + _mm(jnp.tanh(lora), w2_ref[0]))
    a = _sigmoid(a0_ref[0] + _mm(lora, a2_ref[0]))
    kd = k * (1.0 + (a - 1.0) * ka_ref[...])
    kkr = k * kk_ref[...]
    kk = kkr / jnp.maximum(jnp.sqrt(_mm_x01(kkr * kkr, bd)), 1e-12)
    bv = _mm_x01(r * kd * rk_ref[...], bd) * v
    bb = kk * a

    cs = _mm_01x(csi_ref[0], e)
    tot = jnp.where(d == 0, cs[c - 1:c, :], cs[0:1, :])
    a_s = jnp.exp(e - cs) * kk
    b_s = bb * jnp.exp(cs)
    k_s = kd * jnp.exp(cs)
    r_s = r * jnp.exp(-cs)
    b_e = bb * jnp.exp(cs - tot)
    k_e = kd * jnp.exp(cs - tot)
    g_end = jnp.exp(-tot)

    rep4 = lambda t: jnp.concatenate([t] * RWKV_HEADS, axis=0)
    fold4 = lambda t: t[0:c] + t[c:2 * c] + t[2 * c:3 * c] + t[3 * c:4 * c]
    a4, r4, v4 = rep4(a_s) * bd, rep4(r_s) * bd, rep4(v) * bd
    b4, k4 = rep4(b_s), rep4(k_s)
    tri_s, tri_i = tri_ref[0, 0], tri_ref[0, 1]
    lm = _mm_nt(a4, b4) * tri_s
    akm = _mm_nt(a4, k4) * tri_s
    rbm = _mm_nt(r4, b4) * tri_i
    rkm = _mm_nt(r4, k4) * tri_i
    t = _tri_inverse(lm, tri_ref, eye)
    w4 = _mm(t, a4)
    u4 = _mm(t, _mm(akm, v4))
    q4 = r4 - _mm(rbm, w4)
    y4 = _mm(rkm, v4) - _mm(rbm, u4)
    w_all, u_all, q_all, y0 = fold4(w4), fold4(u4), fold4(q4), fold4(y4)
    g_bd = eye * g_end - bd * _mm_tn(w_all, b_e)
    h_bd = bd * (_mm_tn(v, k_e) - _mm_tn(u_all, b_e))
    return _mm_nt(q_all, st) + y0, bv, g, _mm(st, g_bd) + h_bd


def _rwkv(zr, consts, p, ncc, nlc):
    npos, b_sz = zr.shape[0], zr.shape[1]
    nbat = RWKV_BATCHES_PER_STEP
    kw = dict(ncc=ncc, nlc=nlc)
    pos = lambda d, s: _rwkv_pos(d, s, **kw)[0]
    full = lambda a: pl.BlockSpec(a.shape, lambda d, b, s: (0,) * a.ndim)
    by_dir = lambda a: pl.BlockSpec((1,) + a.shape[1:], lambda d, b, s: (d,) + (0,) * (a.ndim - 1))
    out = jax.ShapeDtypeStruct((2, npos, b_sz, CHUNK, RWKV_WIDTH), f32)
    ospec = pl.BlockSpec((1, 1, nbat, CHUNK, RWKV_WIDTH), lambda d, b, s: (d, pos(d, s), b, 0, 0))
    last8 = CHUNK // 8 - 1
    return pl.pallas_call(
        functools.partial(_rwkv_kernel, **kw),
        out_shape=(out, out, out),
        grid=(2, b_sz // nbat, npos),
        in_specs=[pl.BlockSpec((1, nbat, CHUNK, RWKV_PAD), lambda d, b, s: (pos(d, s), b, 0, 0)),
                  pl.BlockSpec((1, nbat, 8, RWKV_PAD), lambda d, b, s: (jnp.maximum(pos(d, s) - 1, 0), b, last8, 0)),
                  pl.BlockSpec((1, nbat, 8, RWKV_PAD), lambda d, b, s: (jnp.minimum(pos(d, s) + 1, npos - 1), b, 0, 0)),
                  full(consts["bd"]), full(consts["eye"]), by_dir(consts["tri"]), by_dir(consts["csi"]),
                  full(p["mu_p"]), by_dir(p["w0"]), by_dir(p["w2_p"]), by_dir(p["a0"]), by_dir(p["a2_p"]),
                  full(p["g2_p"]), full(p["k_k"]), full(p["k_a"]), full(p["r_k"])],
        out_specs=(ospec, ospec, ospec),
        scratch_shapes=[pltpu.VMEM((nbat, RWKV_WIDTH, RWKV_WIDTH), f32)],
        compiler_params=_cparams(("arbitrary", "arbitrary", "arbitrary")),
        name="rwkv7_chunked",
    )(zr, zr, zr, consts["bd"], consts["eye"], consts["tri"], consts["csi"], p["mu_p"], p["w0"], p["w2_p"], p["a0"],
      p["a2_p"], p["g2_p"], p["k_k"], p["k_a"], p["r_k"])


def _pack_bf16_pairs(lo, hi):
    lo_b = pltpu.bitcast(lo.astype(bf16).astype(f32), jnp.uint32)
    hi_b = pltpu.bitcast(hi.astype(bf16).astype(f32), jnp.uint32)
    return (hi_b & jnp.uint32(0xFFFF0000)) | (lo_b >> 16)


def _unpack_bf16_pairs(w):
    lo = pltpu.bitcast(w << 16, f32).astype(bf16)
    hi = pltpu.bitcast(w & jnp.uint32(0xFFFF0000), f32).astype(bf16)
    return lo, hi


def _outproj_kernel(a_ref, py_ref, y_ref, bv_ref, g_ref, x_ref, m_ref, avg_ref, lnw_ref, lnb_ref, wo_ref, pmg_ref,
                    pfg_ref, x1_ref, hp_ref):
    m = m_ref[0]
    rows = lambda ref, dd: jnp.concatenate([ref[dd, c4, 0] for c4 in range(TM // CHUNK)], axis=0)
    ysum = rows(y_ref, 0) + rows(y_ref, 1)
    avg = avg_ref[...]
    dev = ysum - _mm_x01(ysum, avg)
    var = _mm_x01(dev * dev, avg)
    yn = dev * lax.rsqrt(var + RWKV_GN_EPS) * lnw_ref[...] + lnb_ref[...]
    rw = (yn + rows(bv_ref, 0) + rows(bv_ref, 1)) * rows(g_ref, 0)
    o = (jnp.dot(a_ref[...], wo_ref[0:512, :], preferred_element_type=f32)
         + jnp.dot(py_ref[...], wo_ref[512:768, :], preferred_element_type=f32)
         + _mm(rw, wo_ref[768:1024, :]))
    x1 = x_ref[...] + m[2:3] * (_rms(o) * pmg_ref[...])
    x1_ref[...] = x1
    h = (_rms(x1) * pfg_ref[...]) * (1.0 + m[4:5]) + m[3:4]
    hp_ref[...] = _pack_bf16_pairs(h[:, 0:512], h[:, 512:1024])


def _outproj(att, py, y, bv, g, xa, mt, chunk_idx, consts, p, nt):
    d = xa.shape[1]
    cpt = TM // CHUNK
    cspec = lambda nd: pl.BlockSpec((nd, cpt, 1, CHUNK, RWKV_WIDTH), lambda i: (0,) + chunk_idx(i) + (0, 0))
    full = lambda a: pl.BlockSpec(a.shape, lambda i: (0,) * a.ndim)
    ws = [consts["avg"], p["ln_w"], p["ln_b"], p["w_out"], p["post_mix_g"], p["pre_ffn_g"]]
    return pl.pallas_call(
        _outproj_kernel,
        out_shape=(jax.ShapeDtypeStruct((nt * TM, d), f32), jax.ShapeDtypeStruct((nt * TM, d // 2), jnp.uint32)),
        grid=(nt,),
        in_specs=[pl.BlockSpec((TM, 512), lambda i: (i, 0)),
                  pl.BlockSpec((TM, POOL_WIDTH), lambda i: (i, 0)),
                  cspec(2), cspec(2), cspec(1),
                  pl.BlockSpec((TM, d), lambda i: (i, 0)),
                  pl.BlockSpec((1, 8, d), lambda i: (i, 0, 0))] + [full(w) for w in ws],
        out_specs=(pl.BlockSpec((TM, d), lambda i: (i, 0)), pl.BlockSpec((TM, d // 2), lambda i: (i, 0))),
        compiler_params=_cparams(("arbitrary",)),
        name="out_proj",
    )(att, py, y, bv, g, xa, mt, *ws)


def _router_kernel(hp_ref, rw_ref, rb_ref, ut_ref, lt_ref, ei_ref, pos_ref, gt_ref, cnt_ref, run_ref):
    i = pl.program_id(0)

    @pl.when(i == 0)
    def _():
        run_ref[...] = jnp.zeros_like(run_ref)

    tm = hp_ref.shape[0]
    ne, ng = N_EXPERTS, N_GROUPS
    pg = ne // ng
    lo, hi = _unpack_bf16_pairs(hp_ref[...])
    logits = (lax.dot_general(rw_ref[:, 0:512], lo, (((1,), (1,)), ((), ())), preferred_element_type=f32)
              + lax.dot_general(rw_ref[:, 512:1024], hi, (((1,), (1,)), ((), ())), preferred_element_type=f32))
    scores = _sigmoid(logits)
    sel = scores + rb_ref[...]
    neg = -jnp.inf

    s3 = sel.reshape(ng, pg, tm)
    io = lax.broadcasted_iota(jnp.int32, (ng, pg, tm), 1)
    m1 = jnp.max(s3, axis=1, keepdims=True)
    i1 = jnp.min(jnp.where(s3 == m1, io, pg), axis=1, keepdims=True)
    m2 = jnp.max(jnp.where(io == i1, neg, s3), axis=1, keepdims=True)
    gs = (m1 + m2).reshape(ng, tm)
    gi = lax.broadcasted_iota(jnp.int32, (ng, tm), 0)
    grank = jnp.zeros((ng, tm), f32)
    for j in range(ng):
        rj = gs[j:j + 1, :]
        grank = grank + jnp.where((rj > gs) | ((rj == gs) & (j < gi)), 1.0, 0.0)
    gsel = jnp.where(grank < TOPK_GROUPS, 1.0, 0.0)
    gsel3 = jnp.broadcast_to(gsel.reshape(ng, 1, tm), (ng, pg, tm)).reshape(ne, tm)
    msk = jnp.where(gsel3 > 0.5, sel, neg)
    ei = lax.broadcasted_iota(jnp.int32, (ne, tm), 0)
    erank = jnp.zeros((ne, tm), f32)
    for j in range(ne):
        rj = msk[j:j + 1, :]
        erank = erank + jnp.where((rj > msk) | ((rj == msk) & (j < ei)), 1.0, 0.0)
    chosen = erank < TOP_K
    chf = jnp.where(chosen, 1.0, 0.0)
    graw = jnp.where(chosen, scores, 0.0)
    gate = graw / jnp.sum(graw, axis=0, keepdims=True) * ROUTED_SCALE

    pos = run_ref[...] + _mm(chf, ut_ref[...])
    tot = jnp.sum(chf, axis=1, keepdims=True)
    run_new = run_ref[...] + tot
    run_ref[...] = run_new
    cnt_ref[...] = run_new[:, 0:128]
    rk = _mm(lt_ref[...], chf)
    eif = ei.astype(f32)
    rows_e, rows_p, rows_g = [], [], []
    for kq in range(TOP_K):
        mk = chosen & (rk == float(kq))
        rows_e.append(jnp.sum(jnp.where(mk, eif, 0.0), axis=0, keepdims=True))
        rows_p.append(jnp.sum(jnp.where(mk, pos, 0.0), axis=0, keepdims=True))
        rows_g.append(jnp.sum(jnp.where(mk, gate, 0.0), axis=0, keepdims=True))
    zrow = jnp.zeros((8 - TOP_K, tm), f32)
    ei_ref[0] = jnp.concatenate(rows_e + [zrow], axis=0).astype(jnp.int32)
    pos_ref[0] = jnp.concatenate(rows_p + [zrow], axis=0).astype(jnp.int32)
    gpad = jnp.concatenate(rows_g + [jnp.zeros((128 - TOP_K, tm), f32)], axis=0)
    gt_ref[...] = gpad.T


def _router(hp, rwt, rb, consts, nt):
    full = lambda a: pl.BlockSpec(a.shape, lambda i: (0,) * a.ndim)
    return pl.pallas_call(
        _router_kernel,
        out_shape=(jax.ShapeDtypeStruct((nt, 8, TM), jnp.int32), jax.ShapeDtypeStruct((nt, 8, TM), jnp.int32),
                   jax.ShapeDtypeStruct((nt * TM, 128), f32), jax.ShapeDtypeStruct((N_EXPERTS, 128), f32)),
        grid=(nt,),
        in_specs=[pl.BlockSpec((TM, 512), lambda i: (i, 0)), full(rwt), full(rb), full(consts["ut"]),
                  full(consts["lt"])],
        out_specs=(pl.BlockSpec((1, 8, TM), lambda i: (i, 0, 0)), pl.BlockSpec((1, 8, TM), lambda i: (i, 0, 0)),
                   pl.BlockSpec((TM, 128), lambda i: (i, 0)), pl.BlockSpec((N_EXPERTS, 128), lambda i: (0, 0))),
        scratch_shapes=[pltpu.VMEM((N_EXPERTS, TM), f32)],
        compiler_params=_cparams(("arbitrary",)),
        name="moe_router",
    )(hp, rwt, rb, consts["ut"], consts["lt"])


def _slots_kernel(ps_ref, ei_ref, pos_ref, d_ref):
    ei = ei_ref[0]
    slot = pos_ref[0]
    for e in range(N_EXPERTS):
        slot = slot + jnp.where(ei == e, ps_ref[e], 0)
    d_ref[...] = slot


def _slots(pstart, ei, pos, nt):
    return pl.pallas_call(
        _slots_kernel,
        out_shape=jax.ShapeDtypeStruct((8, nt * TM), jnp.int32),
        grid_spec=pltpu.PrefetchScalarGridSpec(
            num_scalar_prefetch=1,
            grid=(nt,),
            in_specs=[pl.BlockSpec((1, 8, TM), lambda i, ps: (i, 0, 0)), pl.BlockSpec((1, 8, TM), lambda i, ps: (i, 0, 0))],
            out_specs=pl.BlockSpec((8, TM), lambda i, ps: (0, i))),
        compiler_params=_cparams(("arbitrary",)),
        name="moe_slots",
    )(pstart, ei, pos)


def _sc_dispatch(hp, dest, n_slots):
    t, w = hp.shape
    wh = w // 2
    mesh = plsc.VectorSubcoreMesh(core_axis_name="core", subcore_axis_name="subcore", num_cores=SC_CORES,
                                  num_subcores=SC_SUBCORES)
    idx = [dest[kq:kq + 1] for kq in range(TOP_K)]
    half = jax.ShapeDtypeStruct((n_slots, wh), hp.dtype)

    @pl.kernel(out_type=(half, half), mesh=mesh, scratch_types=[])
    def scatter_rows(hp_hbm, *rest):
        idx_hbm, xs_hbm = rest[:TOP_K], rest[TOP_K:]
        for c in range(2):
            def body(x_vmem, *i_vmem, c=c):
                for iv in i_vmem:
                    pltpu.sync_copy(x_vmem, xs_hbm[c].at[iv.at[0]])

            pltpu.emit_pipeline(
                body,
                grid=(t // SC_WINDOW,),
                in_specs=[pl.BlockSpec((SC_WINDOW, wh), lambda i, c=c: (i, c))]
                + [pl.BlockSpec((1, SC_WINDOW), lambda i: (0, i))] * TOP_K,
                out_specs=[],
                core_axis_name=("core", "subcore"),
                dimension_semantics=(pltpu.PARALLEL,),
            )(hp_hbm, *idx_hbm)

    return scatter_rows(hp, *idx)


def _expert_kernel(be_ref, nb_ref, bv_ref, xa_ref, xb_ref, wgu_ref, wd_ref, ya_ref, yb_ref):
    i = pl.program_id(0)

    @pl.when(i < nb_ref[0])
    def _():
        live = lax.broadcasted_iota(jnp.int32, (MOE_BM, 1), 0) < bv_ref[i]
        la, ha = _unpack_bf16_pairs(jnp.where(live, xa_ref[...], jnp.uint32(0)))
        lb, hb = _unpack_bf16_pairs(jnp.where(live, xb_ref[...], jnp.uint32(0)))
        gu = (jnp.dot(la, wgu_ref[0, 0:256, :], preferred_element_type=f32)
              + jnp.dot(lb, wgu_ref[0, 256:512, :], preferred_element_type=f32)
              + jnp.dot(ha, wgu_ref[0, 512:768, :], preferred_element_type=f32)
              + jnp.dot(hb, wgu_ref[0, 768:1024, :], preferred_element_type=f32))
        gg, uu = gu[:, 0:EXPERT_FF], gu[:, EXPERT_FF:2 * EXPERT_FF]
        act = gg * _sigmoid(gg) * uu
        y = _mm(act, wd_ref[0])
        ya_ref[...] = _pack_bf16_pairs(y[:, 0:256], y[:, 512:768])
        yb_ref[...] = _pack_bf16_pairs(y[:, 256:512], y[:, 768:1024])

    @pl.when(i >= nb_ref[0])
    def _():
        ya_ref[...] = jnp.zeros_like(ya_ref)
        yb_ref[...] = jnp.zeros_like(yb_ref)


def _experts(block_expert, nb_used, block_valid, xs, wgu, wd):
    xa, xb = xs
    n_slots = xa.shape[0]
    half = jax.ShapeDtypeStruct((n_slots, xa.shape[1]), jnp.uint32)
    hspec = pl.BlockSpec((MOE_BM, xa.shape[1]), lambda i, be, nb, bv: (i, 0))
    return pl.pallas_call(
        _expert_kernel,
        out_shape=(half, half),
        grid_spec=pltpu.PrefetchScalarGridSpec(
            num_scalar_prefetch=3,
            grid=(n_slots // MOE_BM,),
            in_specs=[pl.BlockSpec((MOE_BM, xa.shape[1]), lambda i, be, nb, bv: (i, 0)),
                      pl.BlockSpec((MOE_BM, xb.shape[1]), lambda i, be, nb, bv: (i, 0)),
                      pl.BlockSpec((1,) + wgu.shape[1:], lambda i, be, nb, bv: (be[i], 0, 0)),
                      pl.BlockSpec((1,) + wd.shape[1:], lambda i, be, nb, bv: (be[i], 0, 0))],
            out_specs=(hspec, hspec)),
        compiler_params=_cparams(("arbitrary",)),
        name="moe_experts",
    )(block_expert, nb_used, block_valid, xa, xb, wgu, wd)


def _sc_gather(ys, dest):
    t = dest.shape[1]
    wh = ys[0].shape[1]
    mesh = plsc.VectorSubcoreMesh(core_axis_name="core", subcore_axis_name="subcore", num_cores=SC_CORES,
                                  num_subcores=SC_SUBCORES)
    idx = [dest[kq:kq + 1] for kq in range(TOP_K)]
    out = jax.ShapeDtypeStruct((t, wh), ys[0].dtype)

    @pl.kernel(out_type=(out,) * (2 * TOP_K), mesh=mesh, scratch_types=[])
    def gather_rows(ya_hbm, yb_hbm, *rest):
        idx_hbm, out_hbm = rest[:TOP_K], rest[TOP_K:]
        for c, y_hbm in enumerate((ya_hbm, yb_hbm)):
            for kq in range(TOP_K):
                def body(i_vmem, o_vmem, y_hbm=y_hbm):
                    pltpu.sync_copy(y_hbm.at[i_vmem.at[0]], o_vmem)

                pltpu.emit_pipeline(
                    body,
                    grid=(t // SC_WINDOW,),
                    in_specs=[pl.BlockSpec((1, SC_WINDOW), lambda i: (0, i))],
                    out_specs=[pl.BlockSpec((SC_WINDOW, wh), lambda i: (i, 0))],
                    core_axis_name=("core", "subcore"),
                    dimension_semantics=(pltpu.PARALLEL,),
                )(idx_hbm[kq], out_hbm[c * TOP_K + kq])

    outs = gather_rows(ys[0], ys[1], *idx)
    return outs[:TOP_K], outs[TOP_K:]


def _combine_kernel(*refs):
    ga, gb = refs[0:TOP_K], refs[TOP_K:2 * TOP_K]
    hp_ref, gt_ref, x1_ref, m_ref, wsgu_ref, wsd_ref, pg_ref, o_ref = refs[2 * TOP_K:]
    lo, hi = _unpack_bf16_pairs(hp_ref[...])
    gu = (jnp.dot(lo, wsgu_ref[0:512, :], preferred_element_type=f32)
          + jnp.dot(hi, wsgu_ref[512:1024, :], preferred_element_type=f32))
    gg, uu = gu[:, 0:EXPERT_FF], gu[:, EXPERT_FF:2 * EXPERT_FF]
    f = _mm(gg * _sigmoid(gg) * uu, wsd_ref[...])
    gt = gt_ref[...]
    parts = [jnp.zeros((TM, 256), f32) for _ in range(4)]
    for kq in range(TOP_K):
        g = gt[:, kq:kq + 1]
        la, ha = _unpack_bf16_pairs(ga[kq][...])
        lb, hb = _unpack_bf16_pairs(gb[kq][...])
        for q, v in enumerate((la, lb, ha, hb)):
            parts[q] = parts[q] + v.astype(f32) * g
    f = f + jnp.concatenate(parts, axis=1)
    m = m_ref[0]
    o_ref[...] = x1_ref[...] + m[5:6] * (_rms(f) * pg_ref[...])


def _combine(ga, gb, hp, gt, x1, mt, p, nt):
    d = x1.shape[1]
    full = lambda a: pl.BlockSpec(a.shape, lambda i: (0,) * a.ndim)
    ws = [p["sh_wgu"], p["sh_wd"], p["post_ffn_g"]]
    gspec = pl.BlockSpec((TM, ga[0].shape[1]), lambda i: (i, 0))
    return pl.pallas_call(
        _combine_kernel,
        out_shape=jax.ShapeDtypeStruct((nt * TM, d), f32),
        grid=(nt,),
        in_specs=[gspec] * (2 * TOP_K)
        + [pl.BlockSpec((TM, d // 2), lambda i: (i, 0)),
           pl.BlockSpec((TM, 128), lambda i: (i, 0)),
           pl.BlockSpec((TM, d), lambda i: (i, 0)),
           pl.BlockSpec((1, 8, d), lambda i: (i, 0, 0))] + [full(w) for w in ws],
        out_specs=pl.BlockSpec((TM, d), lambda i: (i, 0)),
        compiler_params=_cparams(("arbitrary",)),
        name="moe_combine",
    )(*ga, *gb, hp, gt, x1, mt, *ws)


def _moe(hp, x1, mt, consts, p, nt):
    t = nt * TM
    ei, pos, gt, cnt = _router(hp, p["router_wt"], p["router_b"], consts, nt)
    n_assign = t * TOP_K
    n_blocks = -(-(n_assign + N_EXPERTS * (MOE_BM - 1)) // MOE_BM)
    n_slots = n_blocks * MOE_BM
    counts = cnt[:, 0].astype(jnp.int32)
    padded = (counts + MOE_BM - 1) // MOE_BM * MOE_BM
    pend = jnp.cumsum(padded)
    pstart = pend - padded
    nb_used = (pend[-1:] // MOE_BM).astype(jnp.int32)
    block_row0 = jnp.arange(n_blocks, dtype=jnp.int32) * MOE_BM
    block_expert = jnp.minimum(jnp.sum((pend[None, :] <= block_row0[:, None]).astype(jnp.int32), axis=1),
                               N_EXPERTS - 1)
    run_end = (pstart + counts)[block_expert]
    block_valid = jnp.clip(run_end - block_row0, 0, MOE_BM).astype(jnp.int32)
    dest = _slots(pstart.astype(jnp.int32), ei, pos, nt)
    xs = _sc_dispatch(hp[:t], dest, n_slots)
    ys = _experts(block_expert, nb_used, block_valid, xs, p["wgu"], p["wd"])
    ga, gb = _sc_gather(ys, dest)
    return _combine(ga, gb, hp, gt, x1, mt, p, nt)


def _np_consts(l_lat, l_ctx):
    n = RWKV_WIDTH
    i = np.arange(n)
    bd = (i[:, None] // 64 == i[None, :] // 64).astype(np.float32)
    t_r, t_c = (i % 64)[:, None], (i % 64)[None, :]
    tri = np.zeros((2, 6, n, n), np.float32)
    for d in range(2):
        before = (t_c < t_r) if d == 0 else (t_c > t_r)
        tri[d, 0] = bd * before
        tri[d, 1] = bd * (before | (t_c == t_r))
        tri[d, 2] = bd * before * (t_r // 8 == t_c // 8)
        for lvl, blk in enumerate((8, 16, 32)):
            tri[d, 3 + lvl] = bd * before * (t_r // (2 * blk) == t_c // (2 * blk)) * (t_r // blk != t_c // blk)
    j = np.arange(CHUNK)
    csi = np.stack([(j[None, :] <= j[:, None]), (j[None, :] >= j[:, None])]).astype(np.float32)
    tt = np.arange(TM)
    ut = (tt[:, None] < tt[None, :]).astype(np.float32)
    ee = np.arange(N_EXPERTS)
    lt = (ee[None, :] < ee[:, None]).astype(np.float32)
    jj = np.arange(TM + 2 * POOL_HALO)[None, :]
    band = np.stack([((jj >= tt[:, None] + POOL_HALO - w // 2) & (jj <= tt[:, None] + POOL_HALO + w // 2 - 1))
                     for w in POOL_WINDOWS]).astype(np.float32)

    def counts(length):
        t = np.arange(length)[:, None]
        half = np.repeat(np.array(POOL_WINDOWS) // 2, 64)[None, :]
        return (np.clip(t + half, 0, length) - np.clip(t - half, 0, length)).astype(np.float32)

    cnt = np.concatenate([counts(l_lat), counts(l_ctx)], axis=0)
    return dict(bd=jnp.asarray(bd), eye=jnp.eye(n, dtype=f32), tri=jnp.asarray(tri), csi=jnp.asarray(csi),
                avg=jnp.asarray(bd / 64.0), ut=jnp.asarray(ut, dtype=bf16), lt=jnp.asarray(lt, dtype=bf16),
                band=jnp.asarray(band), cnt=jnp.asarray(cnt))


def _rope_tables(l_lat):
    rows = l_lat // GRID_W
    row = jnp.repeat(jnp.arange(rows, dtype=f32), GRID_W)
    col = jnp.tile(jnp.arange(GRID_W, dtype=f32), rows)
    n_freq = QK_ROPE // 4
    inv = ROPE_THETA ** (-jnp.arange(n_freq, dtype=f32) / n_freq)
    ang = jnp.concatenate([row[:, None] * inv, col[:, None] * inv], -1)
    cos = jnp.concatenate([jnp.cos(ang), jnp.ones((TM, 16), f32)], 0)
    sin = jnp.concatenate([jnp.sin(ang), jnp.zeros((TM, 16), f32)], 0)
    n = cos.shape[0]
    ct = jnp.concatenate([jnp.ones((n, QK_NOPE), f32), cos, cos, jnp.zeros((n, 32), f32)], 1)
    st = jnp.concatenate([jnp.zeros((n, QK_NOPE), f32), -sin, sin, jnp.zeros((n, 32), f32)], 1)
    qs = MLA_SCALE * LOG2E
    return jnp.stack([ct * qs, st * qs, ct, st])


def _layer_params(i, a):
    d = a["w_in"].shape[1]
    p = {}
    row = lambda v: v.reshape(1, -1).astype(f32)
    for name in ("pre_mix_g", "post_mix_g", "pre_ffn_g", "post_ffn_g", "mla_q_norm", "mla_kv_norm"):
        p[name] = row(a[name][i])
    w_in = a["w_in"][i]
    zc = lambda n: jnp.zeros((d, n), f32)
    p["w_in_p"] = jnp.concatenate(
        [w_in[:, 0:MLA_IN], zc(128 - QK_ROPE), w_in[:, MLA_IN:MLA_IN + POOL_WIDTH],
         w_in[:, MLA_IN + POOL_WIDTH:], zc(RWKV_PAD - RWKV_IN)], 1).astype(bf16)
    wq = a["mla_w_q_b"][i].reshape(Q_LORA, MLA_HEADS, QK_NOPE + QK_ROPE)
    zq = jnp.zeros((Q_LORA, MLA_HEADS, 32), f32)
    half = QK_ROPE // 2
    p["wq_p"] = jnp.concatenate([wq, zq], 2).reshape(Q_LORA, -1).astype(bf16)
    p["wq_s"] = jnp.concatenate([jnp.zeros_like(wq[:, :, :QK_NOPE]), wq[:, :, QK_NOPE + half:],
                                 wq[:, :, QK_NOPE:QK_NOPE + half], zq], 2).reshape(Q_LORA, -1).astype(bf16)
    wkv = a["mla_w_kv_b"][i].reshape(KV_LORA, MLA_HEADS, QK_NOPE + V_HEAD)
    p["wk_p"] = jnp.concatenate([wkv[:, :, :QK_NOPE], jnp.zeros((KV_LORA, MLA_HEADS, 64), f32)], 2
                                ).reshape(KV_LORA, -1).astype(bf16)
    p["wv_p"] = jnp.concatenate([wkv[:, :, QK_NOPE:], jnp.zeros((KV_LORA, MLA_HEADS, 64), f32)], 2
                                ).reshape(KV_LORA, -1).astype(bf16)
    e_p = np.zeros((128, MLA_HEADS * HEAD_PAD), np.float32)
    e_s = np.zeros_like(e_p)
    for h in range(MLA_HEADS):
        for j in range(QK_ROPE):
            e_p[j, h * HEAD_PAD + QK_NOPE + j] = 1.0
            e_s[(j + half) % QK_ROPE, h * HEAD_PAD + QK_NOPE + j] = 1.0
    p["e_p"], p["e_s"] = jnp.asarray(e_p, dtype=bf16), jnp.asarray(e_s, dtype=bf16)
    pw = a["pool_w"][i]
    p["pool_w_bd"] = jax.scipy.linalg.block_diag(*[pw[g] for g in range(pw.shape[0])]).astype(bf16)
    p["pool_scale"] = row(a["pool_scale"][i])
    p["mu_p"] = jnp.pad(a["rwkv_mu"][i], (0, RWKV_PAD - RWKV_IN)).reshape(1, -1)
    p["w0"] = a["rwkv_w0"][i].reshape(2, 1, RWKV_WIDTH)
    p["a0"] = a["rwkv_a0"][i].reshape(2, 1, RWKV_WIDTH)
    z32 = jnp.zeros((32, RWKV_WIDTH), f32)
    w2, a2 = a["rwkv_w2"][i], a["rwkv_a2"][i]
    p["w2_p"] = jnp.stack([jnp.concatenate([w2[0], z32, z32, z32]), jnp.concatenate([z32, w2[1], z32, z32])])
    p["a2_p"] = jnp.stack([jnp.concatenate([z32, z32, a2[0], z32]), jnp.concatenate([z32, z32, z32, a2[1]])])
    p["g2_p"] = jnp.concatenate([a["rwkv_g2"][i], jnp.zeros((64, RWKV_WIDTH), f32)])
    for name in ("k_k", "k_a", "r_k"):
        p[name] = row(a["rwkv_" + name][i])
    p["ln_w"], p["ln_b"] = row(a["rwkv_ln_w"][i]), row(a["rwkv_ln_b"][i])
    p["w_out"] = a["w_out"][i].astype(bf16)
    p["router_wt"] = a["router_w"][i].T.astype(bf16)
    p["router_b"] = a["router_bias"][i].reshape(-1, 1).astype(f32)
    p["wgu"] = jnp.concatenate([a["exp_w_gate"][i], a["exp_w_up"][i]], 2).astype(bf16)
    p["wd"] = a["exp_w_down"][i].astype(bf16)
    p["sh_wgu"] = jnp.concatenate([a["sh_w_gate"][i], a["sh_w_up"][i]], 1).astype(bf16)
    p["sh_wd"] = a["sh_w_down"][i].astype(bf16)
    return p


def kernel(x, c, ctx, c_ctx, ada_w, ada_b, pre_mix_g, post_mix_g, pre_ffn_g, post_ffn_g, w_in, w_out, mla_q_norm, mla_w_q_b, mla_kv_norm, mla_w_kv_b, pool_w, pool_scale, rwkv_mu, rwkv_w0, rwkv_w2, rwkv_a0, rwkv_a2, rwkv_g2, rwkv_k_k, rwkv_k_a, rwkv_r_k, rwkv_ln_w, rwkv_ln_b, router_w, router_bias, exp_w_gate, exp_w_up, exp_w_down, sh_w_gate, sh_w_up, sh_w_down):
    arrs = dict(pre_mix_g=pre_mix_g, post_mix_g=post_mix_g, pre_ffn_g=pre_ffn_g, post_ffn_g=post_ffn_g, w_in=w_in,
                w_out=w_out, mla_q_norm=mla_q_norm, mla_w_q_b=mla_w_q_b, mla_kv_norm=mla_kv_norm,
                mla_w_kv_b=mla_w_kv_b, pool_w=pool_w, pool_scale=pool_scale, rwkv_mu=rwkv_mu, rwkv_w0=rwkv_w0,
                rwkv_w2=rwkv_w2, rwkv_a0=rwkv_a0, rwkv_a2=rwkv_a2, rwkv_g2=rwkv_g2, rwkv_k_k=rwkv_k_k,
                rwkv_k_a=rwkv_k_a, rwkv_r_k=rwkv_r_k, rwkv_ln_w=rwkv_ln_w, rwkv_ln_b=rwkv_ln_b, router_w=router_w,
                router_bias=router_bias, exp_w_gate=exp_w_gate, exp_w_up=exp_w_up, exp_w_down=exp_w_down,
                sh_w_gate=sh_w_gate, sh_w_up=sh_w_up, sh_w_down=sh_w_down)
    b_sz, l_lat, d = x.shape
    l_ctx = ctx.shape[1]
    assert l_lat % TM == 0 and l_ctx % TM == 0 and l_lat % GRID_W == 0 and b_sz < 16
    lt, ct = l_lat // TM, l_ctx // TM
    nlat, nctx = b_sz * lt, b_sz * ct
    nall = nlat + nctx
    ncc, nlc = l_ctx // CHUNK, l_lat // CHUNK
    consts = _np_consts(l_lat, l_ctx)
    rope = _rope_tables(l_lat)
    rope_idx = lambda i: jnp.where(i < nlat, i % lt, lt)
    cnt_idx = lambda i: jnp.where(i < nlat, i % lt, lt + (i - nlat) % ct)
    chunk_idx = lambda i: (jnp.where(i < nlat, i % lt, nlc * CHUNK // TM + (i - nlat) % ct),
                           jnp.where(i < nlat, i // lt, (i - nlat) // ct))

    c_all = jnp.zeros((16, d), f32).at[:b_sz].set(c).at[b_sz].set(c_ctx)
    mods = _ada_mod(c_all, ada_w, ada_b)
    tile_row = np.concatenate([np.repeat(np.arange(b_sz), lt), np.full(nctx, b_sz)])
    xa = jnp.concatenate([x.reshape(b_sz * l_lat, d), ctx.reshape(b_sz * l_ctx, d)], 0)

    for i in range(DEPTH):
        last = i == DEPTH - 1
        p = _layer_params(i, arrs)
        mt = jnp.pad(mods[i][tile_row].reshape(nall, 6, d), ((0, 0), (0, 2), (0, 0)))
        nt = nlat if last else nall
        q, k, v, zp, zr = _inproj(xa, mt, rope, rope_idx, chunk_idx, ncc + nlc, b_sz, p, nall)
        ctx_blk0 = nlat * TM // l_ctx
        att = _attention(q, k, v, nlat, 0, [(l_lat, lambda t: t // lt), (l_ctx, lambda t: ctx_blk0 + t // lt)])
        if not last:
            att = jnp.concatenate([att, _attention(q, k, v, nctx, nlat, [(l_ctx, lambda t: ctx_blk0 + t // ct)])], 0)
        py = _pool(zp, consts["band"], consts["cnt"], cnt_idx, p["pool_w_bd"], p["pool_scale"], nt, nlat, lt, ct)
        y, bv, g = _rwkv(zr, consts, p, ncc, nlc)
        x1, hp = _outproj(att, py, y, bv, g, xa, mt, chunk_idx, consts, p, nt)
        xa = _moe(hp, x1, mt, consts, p, nt)
    return xa[:b_sz * l_lat].reshape(b_sz, l_lat, d)
```

```python
import functools
import math

import numpy as np
import jax
import jax.numpy as jnp
from jax import lax
from jax.experimental import pallas as pl
from jax.experimental.pallas import tpu as pltpu
from jax.experimental.pallas import tpu_sc as plsc

f32 = jnp.float32
bf16 = jnp.bfloat16
HIGHEST = lax.Precision.HIGHEST

DEPTH = 2
GRID_W = 64
NORM_EPS = 1e-6
MLA_HEADS = 8
Q_LORA = 384
KV_LORA = 256
QK_NOPE = 64
QK_ROPE = 32
V_HEAD = 64
ROPE_THETA = 10000.0
MLA_SCALE = (QK_NOPE + QK_ROPE) ** -0.5
MLA_IN = Q_LORA + KV_LORA + QK_ROPE
HEAD_PAD = 128
POOL_WINDOWS = (2, 4, 8, 16)
POOL_WIDTH = 256
POOL_HALO = 8
RWKV_HEADS = 4
RWKV_HEAD = 64
RWKV_WIDTH = 256
RWKV_IN = 960
RWKV_PAD = 1024
RWKV_GN_EPS = 64e-5
CHUNK = 64
RWKV_BATCHES_PER_STEP = 4
N_EXPERTS = 64
TOP_K = 6
N_GROUPS = 8
TOPK_GROUPS = 4
ROUTED_SCALE = 2.5
EXPERT_FF = 256
TM = 256
MOE_BM = 512
SC_CORES, SC_SUBCORES = 2, 16
SC_WINDOW = 128
Z_COLS = 2048
VMEM_LIMIT = 48 * 1024 * 1024
LOG2E = 1.4426950408889634
EXP_M05 = 0.6065306597126334


def _cparams(sem, vmem=VMEM_LIMIT):
    return pltpu.CompilerParams(dimension_semantics=sem, vmem_limit_bytes=vmem)


def _mm(a, b):
    return jnp.dot(a.astype(bf16), b.astype(bf16), preferred_element_type=f32)


def _mm_nt(a, b):
    return lax.dot_general(a.astype(bf16), b.astype(bf16), (((1,), (1,)), ((), ())), preferred_element_type=f32)


def _mm_tn(a, b):
    return lax.dot_general(a.astype(bf16), b.astype(bf16), (((0,), (0,)), ((), ())), preferred_element_type=f32)


def _mmf(a, b):
    return jnp.dot(a, b, precision=HIGHEST, preferred_element_type=f32)


def _split_hi_lo(a):
    hi = a.astype(bf16)
    return hi, (a - hi.astype(f32)).astype(bf16)


def _mm_x01(a, w01):
    hi, lo = _split_hi_lo(a)
    w = w01.astype(bf16)
    return jnp.dot(hi, w, preferred_element_type=f32) + jnp.dot(lo, w, preferred_element_type=f32)


def _mm_01x(w01, a):
    hi, lo = _split_hi_lo(a)
    w = w01.astype(bf16)
    return jnp.dot(w, hi, preferred_element_type=f32) + jnp.dot(w, lo, preferred_element_type=f32)


def _rms(x):
    return x * lax.rsqrt(jnp.mean(x * x, axis=-1, keepdims=True) + NORM_EPS)


def _sigmoid(x):
    return 1.0 / (1.0 + jnp.exp(-x))


def _ada_kernel(c_ref, w_ref, b_ref, o_ref):
    c = c_ref[...]
    s = c * _sigmoid(c)
    o_ref[0] = _mm(s, w_ref[0]) + b_ref[0]


def _ada_mod(c_all, ada_w, ada_b):
    depth, d, n = ada_w.shape
    tn = 1024
    return pl.pallas_call(
        _ada_kernel,
        out_shape=jax.ShapeDtypeStruct((depth, 16, n), f32),
        grid=(depth, n // tn),
        in_specs=[pl.BlockSpec((16, d), lambda i, j: (0, 0)),
                  pl.BlockSpec((1, d, tn), lambda i, j: (i, 0, j)),
                  pl.BlockSpec((1, 1, tn), lambda i, j: (i, 0, j))],
        out_specs=pl.BlockSpec((1, 16, tn), lambda i, j: (i, 0, j)),
        compiler_params=_cparams(("arbitrary", "arbitrary")),
        name="ada_mod",
    )(c_all, ada_w, ada_b.reshape(depth, 1, n))


def _inproj_kernel(x_ref, m_ref, rope_ref, g_ref, win_ref, qg_ref, wq_ref, wqs_ref, kvg_ref, wk_ref, wv_ref,
                   e_ref, es_ref, q_ref, k_ref, v_ref, zp_ref, zr_ref):
    m = m_ref[0]
    h = _rms(x_ref[...]) * g_ref[...]
    h = h * (1.0 + m[1:2]) + m[0:1]
    z = _mm(h, win_ref[...])
    zp_ref[...] = z[:, 768:1024]
    for c4 in range(TM // CHUNK):
        zr_ref[c4, 0] = z[c4 * CHUNK:(c4 + 1) * CHUNK, 1024:2048]
    tile8 = lambda t: jnp.concatenate([t] * MLA_HEADS, axis=1)
    qn = (_rms(z[:, 0:Q_LORA]) * qg_ref[...]).astype(bf16)
    q = _mm(qn, wq_ref[...]) * tile8(rope_ref[0]) + _mm(qn, wqs_ref[...]) * tile8(rope_ref[1])
    q_ref[...] = q.astype(bf16)
    kvn = (_rms(z[:, Q_LORA:Q_LORA + KV_LORA]) * kvg_ref[...]).astype(bf16)
    kpe = z[:, 640:768].astype(bf16)
    k = _mm(kvn, wk_ref[...]) + _mm(kpe, e_ref[...]) * tile8(rope_ref[2]) + _mm(kpe, es_ref[...]) * tile8(rope_ref[3])
    k_ref[...] = k.astype(bf16)
    lane = lax.broadcasted_iota(jnp.int32, (1, MLA_HEADS * HEAD_PAD), 1)
    v_ref[...] = (_mm(kvn, wv_ref[...]) + jnp.where(lane % HEAD_PAD == V_HEAD, 1.0, 0.0)).astype(bf16)


def _inproj(xa, mt, rope, rope_idx, chunk_idx, npos, b_sz, p, nt):
    t_all, d = xa.shape
    cpt = TM // CHUNK
    full = lambda a: pl.BlockSpec(a.shape, lambda i: (0,) * a.ndim)
    ws = [p["pre_mix_g"], p["w_in_p"], p["mla_q_norm"], p["wq_p"], p["wq_s"], p["mla_kv_norm"], p["wk_p"], p["wv_p"],
          p["e_p"], p["e_s"]]
    return pl.pallas_call(
        _inproj_kernel,
        out_shape=(jax.ShapeDtypeStruct((t_all, MLA_HEADS * HEAD_PAD), bf16),
                   jax.ShapeDtypeStruct((t_all, MLA_HEADS * HEAD_PAD), bf16),
                   jax.ShapeDtypeStruct((t_all, MLA_HEADS * HEAD_PAD), bf16),
                   jax.ShapeDtypeStruct((t_all, POOL_WIDTH), f32),
                   jax.ShapeDtypeStruct((npos, b_sz, CHUNK, RWKV_PAD), f32)),
        grid=(nt,),
        in_specs=[pl.BlockSpec((TM, d), lambda i: (i, 0)),
                  pl.BlockSpec((1, 8, d), lambda i: (i, 0, 0)),
                  pl.BlockSpec((4, TM, HEAD_PAD), lambda i: (0, rope_idx(i), 0))] + [full(w) for w in ws],
        out_specs=(pl.BlockSpec((TM, 1024), lambda i: (i, 0)),
                   pl.BlockSpec((TM, 1024), lambda i: (i, 0)),
                   pl.BlockSpec((TM, 1024), lambda i: (i, 0)),
                   pl.BlockSpec((TM, POOL_WIDTH), lambda i: (i, 0)),
                   pl.BlockSpec((cpt, 1, CHUNK, RWKV_PAD), lambda i: chunk_idx(i) + (0, 0))),
        compiler_params=_cparams(("arbitrary",)),
        name="in_proj",
    )(xa, mt, rope, *ws)


def _attn_kernel(*refs):
    q_ref, kv, o_ref = refs[0], refs[1:-1], refs[-1]
    nt = (((1,), (1,)), ((), ()))
    outs = []
    for h in range(MLA_HEADS):
        hs = slice(h * HEAD_PAD, (h + 1) * HEAD_PAD)
        qh = q_ref[:, hs]
        scores = [lax.dot_general(qh, k_ref[:, hs], nt, preferred_element_type=f32) for k_ref in kv[0::2]]
        m = functools.reduce(jnp.maximum, [jnp.max(s, axis=-1, keepdims=True) for s in scores])
        acc = sum(jnp.dot(jnp.exp2((s - m).astype(bf16)), v_ref[:, hs], preferred_element_type=f32)
                  for s, v_ref in zip(scores, kv[1::2]))
        outs.append(acc[:, :V_HEAD] / acc[:, V_HEAD:V_HEAD + 1])
    for pr2 in range(MLA_HEADS // 2):
        o_ref[:, pr2 * 128:(pr2 + 1) * 128] = jnp.concatenate(outs[2 * pr2:2 * pr2 + 2], axis=1).astype(bf16)


def _attention(q, k, v, n_q_tiles, q_tile0, segs):
    in_specs = [pl.BlockSpec((TM, 1024), lambda i: (i + q_tile0, 0))]
    args = [q]
    for rows, bidx in segs:
        in_specs.append(pl.BlockSpec((rows, 1024), lambda i, bidx=bidx: (bidx(i), 0), pipeline_mode=pl.Buffered(1)))
        in_specs.append(pl.BlockSpec((rows, 1024), lambda i, bidx=bidx: (bidx(i), 0), pipeline_mode=pl.Buffered(1)))
        args += [k, v]
    return pl.pallas_call(
        _attn_kernel,
        out_shape=jax.ShapeDtypeStruct((n_q_tiles * TM, MLA_HEADS * V_HEAD), bf16),
        grid=(n_q_tiles,),
        in_specs=in_specs,
        out_specs=pl.BlockSpec((TM, 512), lambda i: (i, 0)),
        compiler_params=_cparams(("arbitrary",)),
        name="mla_attention",
    )(*args)


def _pool_kernel(z_ref, zp_ref, zn_ref, band_ref, cnt_ref, pw_ref, ps_ref, o_ref, *, nlat, lt, ct):
    i = pl.program_id(0)
    is_lat = i < nlat
    j = jnp.where(is_lat, i % lt, (i - nlat) % ct)
    n = jnp.where(is_lat, lt, ct)
    z = z_ref[...]
    prev = zp_ref[...] * jnp.where(j == 0, 0.0, 1.0)
    nxt = zn_ref[...] * jnp.where(j == n - 1, 0.0, 1.0)
    zh = jnp.concatenate([prev, z, nxt], axis=0)
    lane_grp = lax.broadcasted_iota(jnp.int32, (1, POOL_WIDTH), 1) // 64
    tot = jnp.zeros_like(z)
    for g in range(len(POOL_WINDOWS)):
        tot = tot + _mm_01x(band_ref[g], zh * jnp.where(lane_grp == g, 1.0, 0.0))
    diff = tot / cnt_ref[...] - z
    o_ref[...] = (_mm(diff, pw_ref[...]) * ps_ref[...]).astype(bf16)


def _pool(zp, band, cnt, cnt_idx, pw_bd, pscale, nt, nlat, lt, ct):
    t_all = zp.shape[0]
    nb8 = t_all // POOL_HALO
    r = TM // POOL_HALO
    return pl.pallas_call(
        functools.partial(_pool_kernel, nlat=nlat, lt=lt, ct=ct),
        out_shape=jax.ShapeDtypeStruct((nt * TM, POOL_WIDTH), bf16),
        grid=(nt,),
        in_specs=[pl.BlockSpec((TM, POOL_WIDTH), lambda i: (i, 0)),
                  pl.BlockSpec((POOL_HALO, POOL_WIDTH), lambda i: (jnp.maximum(i * r - 1, 0), 0)),
                  pl.BlockSpec((POOL_HALO, POOL_WIDTH), lambda i: (jnp.minimum((i + 1) * r, nb8 - 1), 0)),
                  pl.BlockSpec(band.shape, lambda i: (0, 0, 0)),
                  pl.BlockSpec((TM, POOL_WIDTH), lambda i: (cnt_idx(i), 0)),
                  pl.BlockSpec((POOL_WIDTH, POOL_WIDTH), lambda i: (0, 0)),
                  pl.BlockSpec((1, POOL_WIDTH), lambda i: (0, 0))],
        out_specs=pl.BlockSpec((TM, POOL_WIDTH), lambda i: (i, 0)),
        compiler_params=_cparams(("arbitrary",)),
        name="pool_mixer",
    )(zp, zp, zp, band, cnt, pw_bd, pscale)


def _rwkv_pos(d, s, *, ncc, nlc):
    in_ctx = s < ncc
    jc = jnp.where(d == 0, s, ncc - 1 - s)
    jl = jnp.where(d == 0, s - ncc, nlc - 1 - (s - ncc))
    pos = jnp.where(in_ctx, nlc + jc, jl)
    first = jnp.where(in_ctx, jc == 0, jl == 0)
    last = jnp.where(in_ctx, jc == ncc - 1, jl == nlc - 1)
    return pos, first, last


def _each(f, *lists):
    return [f(*xs) for xs in zip(*lists)]


def _mmb(a, b):
    return jnp.dot(a, b, preferred_element_type=f32)


def _tri_inverse(lmb, tri_ref, eye_b):
    cast = lambda xs: _each(lambda x: x.astype(bf16), xs)
    n = _each(lambda l: l * tri_ref[0, 2], lmb)
    n2 = cast(_each(_mmb, n, n))
    n4 = cast(_each(_mmb, n2, n2))
    t = _each(lambda a, b: _mmb((eye_b + a), (eye_b + b)), n, n2)
    t = _each(lambda a, b: _mmb(a.astype(bf16), eye_b + b), t, n4)
    for lvl in range(3):
        tb = cast(t)
        x = cast(_each(lambda a, l: _mmb(a, l * tri_ref[0, 3 + lvl]), tb, lmb))
        t = _each(lambda a, xx, ab: a + _mmb(xx, ab), t, x, tb)
    return t


def _rwkv_kernel(z_ref, zp_ref, zn_ref, bd_ref, eye_ref, tri_ref, csi_ref, mu_ref, w0_ref, w2_ref, a0_ref, a2_ref,
                 g2_ref, kk_ref, ka_ref, rk_ref, y_ref, bv_ref, g_ref, s_ref, *, ncc, nlc):
    d, s = pl.program_id(0), pl.program_id(2)
    _, first, last = _rwkv_pos(d, s, ncc=ncc, nlc=nlc)

    @pl.when(s == 0)
    def _():
        s_ref[...] = jnp.zeros_like(s_ref)

    c = CHUNK
    nbs = list(range(z_ref.shape[1]))
    keep_prev, keep_next = jnp.where(first, 0.0, 1.0), jnp.where(last, 0.0, 1.0)
    row = lax.broadcasted_iota(jnp.int32, (c, 1), 0)
    bd = bd_ref[...]
    bd_b = bd.astype(bf16)
    eye = eye_ref[...]
    eye_b = eye.astype(bf16)
    cast = lambda xs: _each(lambda x: x.astype(bf16), xs)

    def shifted(nb):
        z = z_ref[0, nb]
        zp = jnp.where(row == 0, zp_ref[0, nb, 7:8, :] * keep_prev, pltpu.roll(z, 1, 0))
        zn = jnp.where(row == c - 1, zn_ref[0, nb, 0:1, :] * keep_next, pltpu.roll(z, c - 1, 0))
        return z + mu_ref[...] * (0.5 * (zp + zn) - z)

    zs = _each(shifted, nbs)
    r, k, v = (_each(lambda z, o=o: z[:, o:o + 256], zs) for o in (0, 256, 512))
    lora = _each(lambda z: z[:, 768:896], zs)
    g = _each(lambda z: _mm(_sigmoid(z[:, 896:1024]), g2_ref[...]), zs)
    e = _each(lambda x: EXP_M05 * _sigmoid(w0_ref[0] + _mm(jnp.tanh(x), w2_ref[0])), lora)
    a = _each(lambda x: _sigmoid(a0_ref[0] + _mm(x, a2_ref[0])), lora)
    kd = _each(lambda kx, ax: kx * (1.0 + (ax - 1.0) * ka_ref[...]), k, a)
    kkr = _each(lambda kx: kx * kk_ref[...], k)
    kk = _each(lambda x: x / jnp.maximum(jnp.sqrt(_mm_x01(x * x, bd)), 1e-12), kkr)
    bv = _each(lambda rx, kx, vx: _mm_x01(rx * kx * rk_ref[...], bd) * vx, r, kd, v)
    bb = _each(lambda x, ax: x * ax, kk, a)

    cs = _each(lambda x: _mm_01x(csi_ref[0], x), e)
    tot = _each(lambda x: jnp.where(d == 0, x[c - 1:c, :], x[0:1, :]), cs)
    rep4 = lambda t: jnp.concatenate([t] * RWKV_HEADS, axis=0)
    fold4 = lambda t: t[0:c] + t[c:2 * c] + t[2 * c:3 * c] + t[3 * c:4 * c]
    head_rows = lambda t: rep4(t.astype(bf16)) * bd_b
    a4 = _each(lambda ex, cx, kx: head_rows(jnp.exp(ex - cx) * kx), e, cs, kk)
    r_s = _each(lambda rx, cx: rx * jnp.exp(-cx), r, cs)
    r4 = _each(head_rows, r_s)
    v4 = _each(head_rows, v)
    grow = _each(jnp.exp, cs)
    b4 = _each(lambda x, gx: rep4((x * gx).astype(bf16)), bb, grow)
    k4 = _each(lambda x, gx: rep4((x * gx).astype(bf16)), kd, grow)
    to_end = _each(lambda cx, tx: jnp.exp(cx - tx), cs, tot)
    b_e = _each(lambda x, gx: (x * gx).astype(bf16), bb, to_end)
    k_e = _each(lambda x, gx: (x * gx).astype(bf16), kd, to_end)
    g_end = _each(lambda tx: jnp.exp(-tx), tot)

    ntb = lambda x, y: lax.dot_general(x, y, (((1,), (1,)), ((), ())), preferred_element_type=f32).astype(bf16)
    tnb = lambda x, y: lax.dot_general(x, y, (((0,), (0,)), ((), ())), preferred_element_type=f32)
    lmb = _each(lambda x, y: ntb(x, y) * tri_ref[0, 0], a4, b4)
    akm = _each(lambda x, y: ntb(x, y) * tri_ref[0, 0], a4, k4)
    rbn = _each(lambda x, y: ntb(x, y) * tri_ref[0, 1], r4, b4)
    rkm = _each(lambda x, y: ntb(x, y) * tri_ref[0, 6], r4, k4)
    t = cast(_tri_inverse(lmb, tri_ref, eye_b))
    w4 = cast(_each(_mmb, t, a4))
    u4 = cast(_each(lambda tx, ax, vx: _mmb(tx, _mmb(ax, vx).astype(bf16)), t, akm, v4))
    q_all = _each(lambda rx, bx, wx: rx + fold4(_mmb(bx, wx)), r_s, rbn, w4)
    y0 = _each(lambda kx, vx, bx, ux: fold4(_mmb(kx, vx) + _mmb(bx, ux)), rkm, v4, rbn, u4)
    w_all, u_all = _each(fold4, w4), _each(fold4, u4)
    g_bd = _each(lambda gx, wx, bx: eye * gx - bd * tnb(wx, bx), g_end, w_all, b_e)
    h_bd = _each(lambda vx, kx, ux, bx: bd * (tnb(vx.astype(bf16), kx) - tnb(ux, bx)), v, k_e, u_all, b_e)
    for nb in nbs:
        st = s_ref[nb]
        y_ref[0, 0, nb] = _mm_nt(q_all[nb], st) + y0[nb]
        bv_ref[0, 0, nb] = bv[nb]
        g_ref[0, 0, nb] = g[nb]
        s_ref[nb] = _mm(st, g_bd[nb]) + h_bd[nb]


def _rwkv(zr, consts, p, ncc, nlc):
    npos, b_sz = zr.shape[0], zr.shape[1]
    nbat = math.gcd(b_sz, RWKV_BATCHES_PER_STEP)
    kw = dict(ncc=ncc, nlc=nlc)
    pos = lambda d, s: _rwkv_pos(d, s, **kw)[0]
    full = lambda a: pl.BlockSpec(a.shape, lambda d, b, s: (0,) * a.ndim)
    by_dir = lambda a: pl.BlockSpec((1,) + a.shape[1:], lambda d, b, s: (d,) + (0,) * (a.ndim - 1))
    out = jax.ShapeDtypeStruct((2, npos, b_sz, CHUNK, RWKV_WIDTH), f32)
    ospec = pl.BlockSpec((1, 1, nbat, CHUNK, RWKV_WIDTH), lambda d, b, s: (d, pos(d, s), b, 0, 0))
    last8 = CHUNK // 8 - 1
    return pl.pallas_call(
        functools.partial(_rwkv_kernel, **kw),
        out_shape=(out, out, out),
        grid=(2, b_sz // nbat, npos),
        in_specs=[pl.BlockSpec((1, nbat, CHUNK, RWKV_PAD), lambda d, b, s: (pos(d, s), b, 0, 0)),
                  pl.BlockSpec((1, nbat, 8, RWKV_PAD), lambda d, b, s: (jnp.maximum(pos(d, s) - 1, 0), b, last8, 0)),
                  pl.BlockSpec((1, nbat, 8, RWKV_PAD), lambda d, b, s: (jnp.minimum(pos(d, s) + 1, npos - 1), b, 0, 0)),
                  full(consts["bd"]), full(consts["eye"]), by_dir(consts["tri"]), by_dir(consts["csi"]),
                  full(p["mu_p"]), by_dir(p["w0"]), by_dir(p["w2_p"]), by_dir(p["a0"]), by_dir(p["a2_p"]),
                  full(p["g2_p"]), full(p["k_k"]), full(p["k_a"]), full(p["r_k"])],
        out_specs=(ospec, ospec, ospec),
        scratch_shapes=[pltpu.VMEM((nbat, RWKV_WIDTH, RWKV_WIDTH), f32)],
        compiler_params=_cparams(("arbitrary", "arbitrary", "arbitrary")),
        name="rwkv7_chunked",
    )(zr, zr, zr, consts["bd"], consts["eye"], consts["tri"], consts["csi"], p["mu_p"], p["w0"], p["w2_p"], p["a0"],
      p["a2_p"], p["g2_p"], p["k_k"], p["k_a"], p["r_k"])


def _pack_bf16_pairs(lo, hi):
    lo_b = pltpu.bitcast(lo.astype(bf16).astype(f32), jnp.uint32)
    hi_b = pltpu.bitcast(hi.astype(bf16).astype(f32), jnp.uint32)
    return (hi_b & jnp.uint32(0xFFFF0000)) | (lo_b >> 16)


def _unpack_bf16_pairs(w):
    lo = pltpu.bitcast(w << 16, f32).astype(bf16)
    hi = pltpu.bitcast(w & jnp.uint32(0xFFFF0000), f32).astype(bf16)
    return lo, hi


def _outproj_kernel(a_ref, py_ref, y_ref, bv_ref, g_ref, x_ref, m_ref, avg_ref, lnw_ref, lnb_ref, wo_ref, pmg_ref,
                    pfg_ref, x1_ref, hp_ref):
    m = m_ref[0]
    rows = lambda ref, dd: jnp.concatenate([ref[dd, c4, 0] for c4 in range(TM // CHUNK)], axis=0)
    ysum = rows(y_ref, 0) + rows(y_ref, 1)
    avg = avg_ref[...]
    dev = ysum - _mm_x01(ysum, avg)
    var = _mm_x01(dev * dev, avg)
    yn = dev * lax.rsqrt(var + RWKV_GN_EPS) * lnw_ref[...] + lnb_ref[...]
    rw = (yn + rows(bv_ref, 0) + rows(bv_ref, 1)) * rows(g_ref, 0)
    o = (jnp.dot(a_ref[...], wo_ref[0:512, :], preferred_element_type=f32)
         + jnp.dot(py_ref[...], wo_ref[512:768, :], preferred_element_type=f32)
         + _mm(rw, wo_ref[768:1024, :]))
    x1 = x_ref[...] + m[2:3] * (_rms(o) * pmg_ref[...])
    x1_ref[...] = x1
    h = (_rms(x1) * pfg_ref[...]) * (1.0 + m[4:5]) + m[3:4]
    hp_ref[...] = _pack_bf16_pairs(h[:, 0:512], h[:, 512:1024])


def _outproj(att, py, y, bv, g, xa, mt, chunk_idx, consts, p, nt):
    d = xa.shape[1]
    cpt = TM // CHUNK
    cspec = lambda nd: pl.BlockSpec((nd, cpt, 1, CHUNK, RWKV_WIDTH), lambda i: (0,) + chunk_idx(i) + (0, 0))
    full = lambda a: pl.BlockSpec(a.shape, lambda i: (0,) * a.ndim)
    ws = [consts["avg"], p["ln_w"], p["ln_b"], p["w_out"], p["post_mix_g"], p["pre_ffn_g"]]
    return pl.pallas_call(
        _outproj_kernel,
        out_shape=(jax.ShapeDtypeStruct((nt * TM, d), f32), jax.ShapeDtypeStruct((nt * TM, d // 2), jnp.uint32)),
        grid=(nt,),
        in_specs=[pl.BlockSpec((TM, 512), lambda i: (i, 0)),
                  pl.BlockSpec((TM, POOL_WIDTH), lambda i: (i, 0)),
                  cspec(2), cspec(2), cspec(1),
                  pl.BlockSpec((TM, d), lambda i: (i, 0)),
                  pl.BlockSpec((1, 8, d), lambda i: (i, 0, 0))] + [full(w) for w in ws],
        out_specs=(pl.BlockSpec((TM, d), lambda i: (i, 0)), pl.BlockSpec((TM, d // 2), lambda i: (i, 0))),
        compiler_params=_cparams(("arbitrary",)),
        name="out_proj",
    )(att, py, y, bv, g, xa, mt, *ws)


def _router_kernel(hp_ref, rw_ref, rb_ref, ut_ref, lt_ref, ei_ref, pos_ref, gt_ref, cnt_ref, run_ref):
    i = pl.program_id(0)

    @pl.when(i == 0)
    def _():
        run_ref[...] = jnp.zeros_like(run_ref)

    tm = hp_ref.shape[0]
    ne, ng = N_EXPERTS, N_GROUPS
    pg = ne // ng
    lo, hi = _unpack_bf16_pairs(hp_ref[...])
    logits = (lax.dot_general(rw_ref[:, 0:512], lo, (((1,), (1,)), ((), ())), preferred_element_type=f32)
              + lax.dot_general(rw_ref[:, 512:1024], hi, (((1,), (1,)), ((), ())), preferred_element_type=f32))
    scores = _sigmoid(logits)
    sel = scores + rb_ref[...]
    neg = -jnp.inf

    s3 = sel.reshape(ng, pg, tm)
    io = lax.broadcasted_iota(jnp.int32, (ng, pg, tm), 1)
    m1 = jnp.max(s3, axis=1, keepdims=True)
    i1 = jnp.min(jnp.where(s3 == m1, io, pg), axis=1, keepdims=True)
    m2 = jnp.max(jnp.where(io == i1, neg, s3), axis=1, keepdims=True)
    gs = (m1 + m2).reshape(ng, tm)
    gi = lax.broadcasted_iota(jnp.int32, (ng, tm), 0)
    grank = jnp.zeros((ng, tm), f32)
    for j in range(ng):
        rj = gs[j:j + 1, :]
        grank = grank + jnp.where((rj > gs) | ((rj == gs) & (j < gi)), 1.0, 0.0)
    gsel = jnp.where(grank < TOPK_GROUPS, 1.0, 0.0)
    gsel3 = jnp.broadcast_to(gsel.reshape(ng, 1, tm), (ng, pg, tm)).reshape(ne, tm)
    msk = jnp.where(gsel3 > 0.5, sel, neg)
    ei = lax.broadcasted_iota(jnp.int32, (ne, tm), 0)
    erank = jnp.zeros((ne, tm), f32)
    for j in range(ne):
        rj = msk[j:j + 1, :]
        erank = erank + jnp.where((rj > msk) | ((rj == msk) & (j < ei)), 1.0, 0.0)
    chosen = erank < TOP_K
    chf = jnp.where(chosen, 1.0, 0.0)
    graw = jnp.where(chosen, scores, 0.0)
    gate = graw / jnp.sum(graw, axis=0, keepdims=True) * ROUTED_SCALE

    pos = run_ref[...] + _mm(chf, ut_ref[...])
    tot = jnp.sum(chf, axis=1, keepdims=True)
    run_new = run_ref[...] + tot
    run_ref[...] = run_new
    cnt_ref[...] = run_new[:, 0:128]
    rk = _mm(lt_ref[...], chf)
    eif = ei.astype(f32)
    rows_e, rows_p, rows_g = [], [], []
    for kq in range(TOP_K):
        mk = chosen & (rk == float(kq))
        rows_e.append(jnp.sum(jnp.where(mk, eif, 0.0), axis=0, keepdims=True))
        rows_p.append(jnp.sum(jnp.where(mk, pos, 0.0), axis=0, keepdims=True))
        rows_g.append(jnp.sum(jnp.where(mk, gate, 0.0), axis=0, keepdims=True))
    zrow = jnp.zeros((8 - TOP_K, tm), f32)
    ei_ref[0] = jnp.concatenate(rows_e + [zrow], axis=0).astype(jnp.int32)
    pos_ref[0] = jnp.concatenate(rows_p + [zrow], axis=0).astype(jnp.int32)
    gpad = jnp.concatenate(rows_g + [jnp.zeros((128 - TOP_K, tm), f32)], axis=0)
    gt_ref[...] = gpad.T


def _router(hp, rwt, rb, consts, nt):
    full = lambda a: pl.BlockSpec(a.shape, lambda i: (0,) * a.ndim)
    return pl.pallas_call(
        _router_kernel,
        out_shape=(jax.ShapeDtypeStruct((nt, 8, TM), jnp.int32), jax.ShapeDtypeStruct((nt, 8, TM), jnp.int32),
                   jax.ShapeDtypeStruct((nt * TM, 128), f32), jax.ShapeDtypeStruct((N_EXPERTS, 128), f32)),
        grid=(nt,),
        in_specs=[pl.BlockSpec((TM, 512), lambda i: (i, 0)), full(rwt), full(rb), full(consts["ut"]),
                  full(consts["lt"])],
        out_specs=(pl.BlockSpec((1, 8, TM), lambda i: (i, 0, 0)), pl.BlockSpec((1, 8, TM), lambda i: (i, 0, 0)),
                   pl.BlockSpec((TM, 128), lambda i: (i, 0)), pl.BlockSpec((N_EXPERTS, 128), lambda i: (0, 0))),
        scratch_shapes=[pltpu.VMEM((N_EXPERTS, TM), f32)],
        compiler_params=_cparams(("arbitrary",)),
        name="moe_router",
    )(hp, rwt, rb, consts["ut"], consts["lt"])


def _slots_kernel(ps_ref, ei_ref, pos_ref, d_ref):
    ei = ei_ref[0]
    slot = pos_ref[0]
    for e in range(N_EXPERTS):
        slot = slot + jnp.where(ei == e, ps_ref[e], 0)
    d_ref[...] = slot


def _slots(pstart, ei, pos, nt):
    return pl.pallas_call(
        _slots_kernel,
        out_shape=jax.ShapeDtypeStruct((8, nt * TM), jnp.int32),
        grid_spec=pltpu.PrefetchScalarGridSpec(
            num_scalar_prefetch=1,
            grid=(nt,),
            in_specs=[pl.BlockSpec((1, 8, TM), lambda i, ps: (i, 0, 0)), pl.BlockSpec((1, 8, TM), lambda i, ps: (i, 0, 0))],
            out_specs=pl.BlockSpec((8, TM), lambda i, ps: (0, i))),
        compiler_params=_cparams(("arbitrary",)),
        name="moe_slots",
    )(pstart, ei, pos)


def _sc_dispatch(hp, dest, n_slots):
    t, w = hp.shape
    wh = w // 2
    mesh = plsc.VectorSubcoreMesh(core_axis_name="core", subcore_axis_name="subcore", num_cores=SC_CORES,
                                  num_subcores=SC_SUBCORES)
    idx = [dest[kq:kq + 1] for kq in range(TOP_K)]
    half = jax.ShapeDtypeStruct((n_slots, wh), hp.dtype)

    @pl.kernel(out_type=(half, half), mesh=mesh, scratch_types=[])
    def scatter_rows(hp_hbm, *rest):
        idx_hbm, xs_hbm = rest[:TOP_K], rest[TOP_K:]
        for c in range(2):
            def body(x_vmem, *i_vmem, c=c):
                for iv in i_vmem:
                    pltpu.sync_copy(x_vmem, xs_hbm[c].at[iv.at[0]])

            pltpu.emit_pipeline(
                body,
                grid=(t // SC_WINDOW,),
                in_specs=[pl.BlockSpec((SC_WINDOW, wh), lambda i, c=c: (i, c))]
                + [pl.BlockSpec((1, SC_WINDOW), lambda i: (0, i))] * TOP_K,
                out_specs=[],
                core_axis_name=("core", "subcore"),
                dimension_semantics=(pltpu.PARALLEL,),
            )(hp_hbm, *idx_hbm)

    return scatter_rows(hp, *idx)


def _expert_kernel(be_ref, nb_ref, bv_ref, xa_ref, xb_ref, wgu_ref, wd_ref, ya_ref, yb_ref):
    i = pl.program_id(0)

    @pl.when(i < nb_ref[0])
    def _():
        live = lax.broadcasted_iota(jnp.int32, (MOE_BM, 1), 0) < bv_ref[i]
        la, ha = _unpack_bf16_pairs(jnp.where(live, xa_ref[...], jnp.uint32(0)))
        lb, hb = _unpack_bf16_pairs(jnp.where(live, xb_ref[...], jnp.uint32(0)))
        gu = (jnp.dot(la, wgu_ref[0, 0:256, :], preferred_element_type=f32)
              + jnp.dot(lb, wgu_ref[0, 256:512, :], preferred_element_type=f32)
              + jnp.dot(ha, wgu_ref[0, 512:768, :], preferred_element_type=f32)
              + jnp.dot(hb, wgu_ref[0, 768:1024, :], preferred_element_type=f32))
        gg, uu = gu[:, 0:EXPERT_FF], gu[:, EXPERT_FF:2 * EXPERT_FF]
        act = gg * _sigmoid(gg) * uu
        y = _mm(act, wd_ref[0])
        ya_ref[...] = _pack_bf16_pairs(y[:, 0:256], y[:, 512:768])
        yb_ref[...] = _pack_bf16_pairs(y[:, 256:512], y[:, 768:1024])

    @pl.when(i >= nb_ref[0])
    def _():
        ya_ref[...] = jnp.zeros_like(ya_ref)
        yb_ref[...] = jnp.zeros_like(yb_ref)


def _experts(block_expert, nb_used, block_valid, xs, wgu, wd):
    xa, xb = xs
    n_slots = xa.shape[0]
    half = jax.ShapeDtypeStruct((n_slots, xa.shape[1]), jnp.uint32)
    hspec = pl.BlockSpec((MOE_BM, xa.shape[1]), lambda i, be, nb, bv: (i, 0))
    return pl.pallas_call(
        _expert_kernel,
        out_shape=(half, half),
        grid_spec=pltpu.PrefetchScalarGridSpec(
            num_scalar_prefetch=3,
            grid=(n_slots // MOE_BM,),
            in_specs=[pl.BlockSpec((MOE_BM, xa.shape[1]), lambda i, be, nb, bv: (i, 0)),
                      pl.BlockSpec((MOE_BM, xb.shape[1]), lambda i, be, nb, bv: (i, 0)),
                      pl.BlockSpec((1,) + wgu.shape[1:], lambda i, be, nb, bv: (be[i], 0, 0)),
                      pl.BlockSpec((1,) + wd.shape[1:], lambda i, be, nb, bv: (be[i], 0, 0))],
            out_specs=(hspec, hspec)),
        compiler_params=_cparams(("arbitrary",)),
        name="moe_experts",
    )(block_expert, nb_used, block_valid, xa, xb, wgu, wd)


def _sc_gather(ys, dest):
    t = dest.shape[1]
    wh = ys[0].shape[1]
    mesh = plsc.VectorSubcoreMesh(core_axis_name="core", subcore_axis_name="subcore", num_cores=SC_CORES,
                                  num_subcores=SC_SUBCORES)
    idx = [dest[kq:kq + 1] for kq in range(TOP_K)]
    out = jax.ShapeDtypeStruct((t, wh), ys[0].dtype)

    @pl.kernel(out_type=(out,) * (2 * TOP_K), mesh=mesh, scratch_types=[])
    def gather_rows(ya_hbm, yb_hbm, *rest):
        idx_hbm, out_hbm = rest[:TOP_K], rest[TOP_K:]
        for c, y_hbm in enumerate((ya_hbm, yb_hbm)):
            for kq in range(TOP_K):
                def body(i_vmem, o_vmem, y_hbm=y_hbm):
                    pltpu.sync_copy(y_hbm.at[i_vmem.at[0]], o_vmem)

                pltpu.emit_pipeline(
                    body,
                    grid=(t // SC_WINDOW,),
                    in_specs=[pl.BlockSpec((1, SC_WINDOW), lambda i: (0, i))],
                    out_specs=[pl.BlockSpec((SC_WINDOW, wh), lambda i: (i, 0))],
                    core_axis_name=("core", "subcore"),
                    dimension_semantics=(pltpu.PARALLEL,),
                )(idx_hbm[kq], out_hbm[c * TOP_K + kq])

    outs = gather_rows(ys[0], ys[1], *idx)
    return outs[:TOP_K], outs[TOP_K:]


def _combine_kernel(*refs):
    ga, gb = refs[0:TOP_K], refs[TOP_K:2 * TOP_K]
    hp_ref, gt_ref, x1_ref, m_ref, wsgu_ref, wsd_ref, pg_ref, o_ref = refs[2 * TOP_K:]
    lo, hi = _unpack_bf16_pairs(hp_ref[...])
    gu = (jnp.dot(lo, wsgu_ref[0:512, :], preferred_element_type=f32)
          + jnp.dot(hi, wsgu_ref[512:1024, :], preferred_element_type=f32))
    gg, uu = gu[:, 0:EXPERT_FF], gu[:, EXPERT_FF:2 * EXPERT_FF]
    f = _mm(gg * _sigmoid(gg) * uu, wsd_ref[...])
    gt = gt_ref[...]
    parts = [jnp.zeros((TM, 256), f32) for _ in range(4)]
    for kq in range(TOP_K):
        g = gt[:, kq:kq + 1]
        la, ha = _unpack_bf16_pairs(ga[kq][...])
        lb, hb = _unpack_bf16_pairs(gb[kq][...])
        for q, v in enumerate((la, lb, ha, hb)):
            parts[q] = parts[q] + v.astype(f32) * g
    f = f + jnp.concatenate(parts, axis=1)
    m = m_ref[0]
    o_ref[...] = x1_ref[...] + m[5:6] * (_rms(f) * pg_ref[...])


def _combine(ga, gb, hp, gt, x1, mt, p, nt):
    d = x1.shape[1]
    full = lambda a: pl.BlockSpec(a.shape, lambda i: (0,) * a.ndim)
    ws = [p["sh_wgu"], p["sh_wd"], p["post_ffn_g"]]
    gspec = pl.BlockSpec((TM, ga[0].shape[1]), lambda i: (i, 0))
    return pl.pallas_call(
        _combine_kernel,
        out_shape=jax.ShapeDtypeStruct((nt * TM, d), f32),
        grid=(nt,),
        in_specs=[gspec] * (2 * TOP_K)
        + [pl.BlockSpec((TM, d // 2), lambda i: (i, 0)),
           pl.BlockSpec((TM, 128), lambda i: (i, 0)),
           pl.BlockSpec((TM, d), lambda i: (i, 0)),
           pl.BlockSpec((1, 8, d), lambda i: (i, 0, 0))] + [full(w) for w in ws],
        out_specs=pl.BlockSpec((TM, d), lambda i: (i, 0)),
        compiler_params=_cparams(("arbitrary",)),
        name="moe_combine",
    )(*ga, *gb, hp, gt, x1, mt, *ws)


def _moe(hp, x1, mt, consts, p, nt):
    t = nt * TM
    ei, pos, gt, cnt = _router(hp, p["router_wt"], p["router_b"], consts, nt)
    n_assign = t * TOP_K
    n_blocks = -(-(n_assign + N_EXPERTS * (MOE_BM - 1)) // MOE_BM)
    n_slots = n_blocks * MOE_BM
    counts = cnt[:, 0].astype(jnp.int32)
    padded = (counts + MOE_BM - 1) // MOE_BM * MOE_BM
    pend = jnp.cumsum(padded)
    pstart = pend - padded
    nb_used = (pend[-1:] // MOE_BM).astype(jnp.int32)
    block_row0 = jnp.arange(n_blocks, dtype=jnp.int32) * MOE_BM
    block_expert = jnp.minimum(jnp.sum((pend[None, :] <= block_row0[:, None]).astype(jnp.int32), axis=1),
                               N_EXPERTS - 1)
    run_end = (pstart + counts)[block_expert]
    block_valid = jnp.clip(run_end - block_row0, 0, MOE_BM).astype(jnp.int32)
    dest = _slots(pstart.astype(jnp.int32), ei, pos, nt)
    xs = _sc_dispatch(hp[:t], dest, n_slots)
    ys = _experts(block_expert, nb_used, block_valid, xs, p["wgu"], p["wd"])
    ga, gb = _sc_gather(ys, dest)
    return _combine(ga, gb, hp, gt, x1, mt, p, nt)


def _np_consts(l_lat, l_ctx):
    n = RWKV_WIDTH
    i = np.arange(n)
    bd = (i[:, None] // 64 == i[None, :] // 64).astype(np.float32)
    t_r, t_c = (i % 64)[:, None], (i % 64)[None, :]
    tri = np.zeros((2, 7, n, n), np.float32)
    for d in range(2):
        before = (t_c < t_r) if d == 0 else (t_c > t_r)
        tri[d, 0] = bd * before
        tri[d, 6] = bd * (before | (t_c == t_r))
        tri[d, 1] = -tri[d, 6]
        tri[d, 2] = -(bd * before * (t_r // 8 == t_c // 8))
        for lvl, blk in enumerate((8, 16, 32)):
            tri[d, 3 + lvl] = -(bd * before * (t_r // (2 * blk) == t_c // (2 * blk)) * (t_r // blk != t_c // blk))
    j = np.arange(CHUNK)
    csi = np.stack([(j[None, :] <= j[:, None]), (j[None, :] >= j[:, None])]).astype(np.float32)
    tt = np.arange(TM)
    ut = (tt[:, None] < tt[None, :]).astype(np.float32)
    ee = np.arange(N_EXPERTS)
    lt = (ee[None, :] < ee[:, None]).astype(np.float32)
    jj = np.arange(TM + 2 * POOL_HALO)[None, :]
    band = np.stack([((jj >= tt[:, None] + POOL_HALO - w // 2) & (jj <= tt[:, None] + POOL_HALO + w // 2 - 1))
                     for w in POOL_WINDOWS]).astype(np.float32)

    def counts(length):
        t = np.arange(length)[:, None]
        half = np.repeat(np.array(POOL_WINDOWS) // 2, 64)[None, :]
        return (np.clip(t + half, 0, length) - np.clip(t - half, 0, length)).astype(np.float32)

    cnt = np.concatenate([counts(l_lat), counts(l_ctx)], axis=0)
    return dict(bd=jnp.asarray(bd), eye=jnp.eye(n, dtype=f32), tri=jnp.asarray(tri, dtype=bf16), csi=jnp.asarray(csi),
                avg=jnp.asarray(bd / 64.0), ut=jnp.asarray(ut, dtype=bf16), lt=jnp.asarray(lt, dtype=bf16),
                band=jnp.asarray(band), cnt=jnp.asarray(cnt))


def _rope_tables(l_lat):
    rows = l_lat // GRID_W
    row = jnp.repeat(jnp.arange(rows, dtype=f32), GRID_W)
    col = jnp.tile(jnp.arange(GRID_W, dtype=f32), rows)
    n_freq = QK_ROPE // 4
    inv = ROPE_THETA ** (-jnp.arange(n_freq, dtype=f32) / n_freq)
    ang = jnp.concatenate([row[:, None] * inv, col[:, None] * inv], -1)
    cos = jnp.concatenate([jnp.cos(ang), jnp.ones((TM, 16), f32)], 0)
    sin = jnp.concatenate([jnp.sin(ang), jnp.zeros((TM, 16), f32)], 0)
    n = cos.shape[0]
    ct = jnp.concatenate([jnp.ones((n, QK_NOPE), f32), cos, cos, jnp.zeros((n, 32), f32)], 1)
    st = jnp.concatenate([jnp.zeros((n, QK_NOPE), f32), -sin, sin, jnp.zeros((n, 32), f32)], 1)
    qs = MLA_SCALE * LOG2E
    return jnp.stack([ct * qs, st * qs, ct, st])


def _layer_params(i, a):
    d = a["w_in"].shape[1]
    p = {}
    row = lambda v: v.reshape(1, -1).astype(f32)
    for name in ("pre_mix_g", "post_mix_g", "pre_ffn_g", "post_ffn_g", "mla_q_norm", "mla_kv_norm"):
        p[name] = row(a[name][i])
    w_in = a["w_in"][i]
    zc = lambda n: jnp.zeros((d, n), f32)
    p["w_in_p"] = jnp.concatenate(
        [w_in[:, 0:MLA_IN], zc(128 - QK_ROPE), w_in[:, MLA_IN:MLA_IN + POOL_WIDTH],
         w_in[:, MLA_IN + POOL_WIDTH:], zc(RWKV_PAD - RWKV_IN)], 1).astype(bf16)
    wq = a["mla_w_q_b"][i].reshape(Q_LORA, MLA_HEADS, QK_NOPE + QK_ROPE)
    zq = jnp.zeros((Q_LORA, MLA_HEADS, 32), f32)
    half = QK_ROPE // 2
    p["wq_p"] = jnp.concatenate([wq, zq], 2).reshape(Q_LORA, -1).astype(bf16)
    p["wq_s"] = jnp.concatenate([jnp.zeros_like(wq[:, :, :QK_NOPE]), wq[:, :, QK_NOPE + half:],
                                 wq[:, :, QK_NOPE:QK_NOPE + half], zq], 2).reshape(Q_LORA, -1).astype(bf16)
    wkv = a["mla_w_kv_b"][i].reshape(KV_LORA, MLA_HEADS, QK_NOPE + V_HEAD)
    p["wk_p"] = jnp.concatenate([wkv[:, :, :QK_NOPE], jnp.zeros((KV_LORA, MLA_HEADS, 64), f32)], 2
                                ).reshape(KV_LORA, -1).astype(bf16)
    p["wv_p"] = jnp.concatenate([wkv[:, :, QK_NOPE:], jnp.zeros((KV_LORA, MLA_HEADS, 64), f32)], 2
                                ).reshape(KV_LORA, -1).astype(bf16)
    e_p = np.zeros((128, MLA_HEADS * HEAD_PAD), np.float32)
    e_s = np.zeros_like(e_p)
    for h in range(MLA_HEADS):
        for j in range(QK_ROPE):
            e_p[j, h * HEAD_PAD + QK_NOPE + j] = 1.0
            e_s[(j + half) % QK_ROPE, h * HEAD_PAD + QK_NOPE + j] = 1.0
    p["e_p"], p["e_s"] = jnp.asarray(e_p, dtype=bf16), jnp.asarray(e_s, dtype=bf16)
    pw = a["pool_w"][i]
    p["pool_w_bd"] = jax.scipy.linalg.block_diag(*[pw[g] for g in range(pw.shape[0])]).astype(bf16)
    p["pool_scale"] = row(a["pool_scale"][i])
    p["mu_p"] = jnp.pad(a["rwkv_mu"][i], (0, RWKV_PAD - RWKV_IN)).reshape(1, -1)
    p["w0"] = a["rwkv_w0"][i].reshape(2, 1, RWKV_WIDTH)
    p["a0"] = a["rwkv_a0"][i].reshape(2, 1, RWKV_WIDTH)
    z32 = jnp.zeros((32, RWKV_WIDTH), f32)
    w2, a2 = a["rwkv_w2"][i], a["rwkv_a2"][i]
    p["w2_p"] = jnp.stack([jnp.concatenate([w2[0], z32, z32, z32]), jnp.concatenate([z32, w2[1], z32, z32])])
    p["a2_p"] = jnp.stack([jnp.concatenate([z32, z32, a2[0], z32]), jnp.concatenate([z32, z32, z32, a2[1]])])
    p["g2_p"] = jnp.concatenate([a["rwkv_g2"][i], jnp.zeros((64, RWKV_WIDTH), f32)])
    for name in ("k_k", "k_a", "r_k"):
        p[name] = row(a["rwkv_" + name][i])
    p["ln_w"], p["ln_b"] = row(a["rwkv_ln_w"][i]), row(a["rwkv_ln_b"][i])
    p["w_out"] = a["w_out"][i].astype(bf16)
    p["router_wt"] = a["router_w"][i].T.astype(bf16)
    p["router_b"] = a["router_bias"][i].reshape(-1, 1).astype(f32)
    p["wgu"] = jnp.concatenate([a["exp_w_gate"][i], a["exp_w_up"][i]], 2).astype(bf16)
    p["wd"] = a["exp_w_down"][i].astype(bf16)
    p["sh_wgu"] = jnp.concatenate([a["sh_w_gate"][i], a["sh_w_up"][i]], 1).astype(bf16)
    p["sh_wd"] = a["sh_w_down"][i].astype(bf16)
    return p


def kernel(x, c, ctx, c_ctx, ada_w, ada_b, pre_mix_g, post_mix_g, pre_ffn_g, post_ffn_g, w_in, w_out, mla_q_norm, mla_w_q_b, mla_kv_norm, mla_w_kv_b, pool_w, pool_scale, rwkv_mu, rwkv_w0, rwkv_w2, rwkv_a0, rwkv_a2, rwkv_g2, rwkv_k_k, rwkv_k_a, rwkv_r_k, rwkv_ln_w, rwkv_ln_b, router_w, router_bias, exp_w_gate, exp_w_up, exp_w_down, sh_w_gate, sh_w_up, sh_w_down):
    arrs = dict(pre_mix_g=pre_mix_g, post_mix_g=post_mix_g, pre_ffn_g=pre_ffn_g, post_ffn_g=post_ffn_g, w_in=w_in,
                w_out=w_out, mla_q_norm=mla_q_norm, mla_w_q_b=mla_w_q_b, mla_kv_norm=mla_kv_norm,
                mla_w_kv_b=mla_w_kv_b, pool_w=pool_w, pool_scale=pool_scale, rwkv_mu=rwkv_mu, rwkv_w0=rwkv_w0,
                rwkv_w2=rwkv_w2, rwkv_a0=rwkv_a0, rwkv_a2=rwkv_a2, rwkv_g2=rwkv_g2, rwkv_k_k=rwkv_k_k,
                rwkv_k_a=rwkv_k_a, rwkv_r_k=rwkv_r_k, rwkv_ln_w=rwkv_ln_w, rwkv_ln_b=rwkv_ln_b, router_w=router_w,
                router_bias=router_bias, exp_w_gate=exp_w_gate, exp_w_up=exp_w_up, exp_w_down=exp_w_down,
                sh_w_gate=sh_w_gate, sh_w_up=sh_w_up, sh_w_down=sh_w_down)
    b_sz, l_lat, d = x.shape
    l_ctx = ctx.shape[1]
    assert l_lat % TM == 0 and l_ctx % TM == 0 and l_lat % GRID_W == 0 and b_sz < 16
    lt, ct = l_lat // TM, l_ctx // TM
    nlat, nctx = b_sz * lt, b_sz * ct
    nall = nlat + nctx
    ncc, nlc = l_ctx // CHUNK, l_lat // CHUNK
    consts = _np_consts(l_lat, l_ctx)
    rope = _rope_tables(l_lat)
    rope_idx = lambda i: jnp.where(i < nlat, i % lt, lt)
    cnt_idx = lambda i: jnp.where(i < nlat, i % lt, lt + (i - nlat) % ct)
    chunk_idx = lambda i: (jnp.where(i < nlat, i % lt, nlc * CHUNK // TM + (i - nlat) % ct),
                           jnp.where(i < nlat, i // lt, (i - nlat) // ct))

    c_all = jnp.zeros((16, d), f32).at[:b_sz].set(c).at[b_sz].set(c_ctx)
    mods = _ada_mod(c_all, ada_w, ada_b)
    tile_row = np.concatenate([np.repeat(np.arange(b_sz), lt), np.full(nctx, b_sz)])
    xa = jnp.concatenate([x.reshape(b_sz * l_lat, d), ctx.reshape(b_sz * l_ctx, d)], 0)

    for i in range(DEPTH):
        last = i == DEPTH - 1
        p = _layer_params(i, arrs)
        mt = jnp.pad(mods[i][tile_row].reshape(nall, 6, d), ((0, 0), (0, 2), (0, 0)))
        nt = nlat if last else nall
        q, k, v, zp, zr = _inproj(xa, mt, rope, rope_idx, chunk_idx, ncc + nlc, b_sz, p, nall)
        ctx_blk0 = nlat * TM // l_ctx
        att = _attention(q, k, v, nlat, 0, [(l_lat, lambda t: t // lt), (l_ctx, lambda t: ctx_blk0 + t // lt)])
        if not last:
            att = jnp.concatenate([att, _attention(q, k, v, nctx, nlat, [(l_ctx, lambda t: ctx_blk0 + t // ct)])], 0)
        py = _pool(zp, consts["band"], consts["cnt"], cnt_idx, p["pool_w_bd"], p["pool_scale"], nt, nlat, lt, ct)
        y, bv, g = _rwkv(zr, consts, p, ncc, nlc)
        x1, hp = _outproj(att, py, y, bv, g, xa, mt, chunk_idx, consts, p, nt)
        xa = _moe(hp, x1, mt, consts, p, nt)
    return xa[:b_sz * l_lat].reshape(b_sz, l_lat, d)
```

```python
import functools
import math

import numpy as np
import jax
import jax.numpy as jnp
from jax import lax
from jax.experimental import pallas as pl
from jax.experimental.pallas import tpu as pltpu
from jax.experimental.pallas import tpu_sc as plsc

f32 = jnp.float32
bf16 = jnp.bfloat16
HIGHEST = lax.Precision.HIGHEST

DEPTH = 2
GRID_W = 64
NORM_EPS = 1e-6
MLA_HEADS = 8
Q_LORA = 384
KV_LORA = 256
QK_NOPE = 64
QK_ROPE = 32
V_HEAD = 64
ROPE_THETA = 10000.0
MLA_SCALE = (QK_NOPE + QK_ROPE) ** -0.5
MLA_IN = Q_LORA + KV_LORA + QK_ROPE
HEAD_PAD = 128
POOL_WINDOWS = (2, 4, 8, 16)
POOL_WIDTH = 256
POOL_HALO = 8
RWKV_HEADS = 4
RWKV_HEAD = 64
RWKV_WIDTH = 256
RWKV_IN = 960
RWKV_PAD = 1024
RWKV_GN_EPS = 64e-5
CHUNK = 64
RWKV_BATCHES_PER_STEP = 4
N_EXPERTS = 64
TOP_K = 6
N_GROUPS = 8
TOPK_GROUPS = 4
ROUTED_SCALE = 2.5
EXPERT_FF = 256
TM = 256
MOE_BM = 512
SC_CORES, SC_SUBCORES = 2, 16
SC_WINDOW = 128
Z_COLS = 2048
VMEM_LIMIT = 48 * 1024 * 1024
LOG2E = 1.4426950408889634
EXP_M05 = 0.6065306597126334


def _cparams(sem, vmem=VMEM_LIMIT):
    return pltpu.CompilerParams(dimension_semantics=sem, vmem_limit_bytes=vmem)


def _mm(a, b):
    return jnp.dot(a.astype(bf16), b.astype(bf16), preferred_element_type=f32)


def _mm_nt(a, b):
    return lax.dot_general(a.astype(bf16), b.astype(bf16), (((1,), (1,)), ((), ())), preferred_element_type=f32)


def _mm_tn(a, b):
    return lax.dot_general(a.astype(bf16), b.astype(bf16), (((0,), (0,)), ((), ())), preferred_element_type=f32)


def _mmf(a, b):
    return jnp.dot(a, b, precision=HIGHEST, preferred_element_type=f32)


def _split_hi_lo(a):
    hi = a.astype(bf16)
    return hi, (a - hi.astype(f32)).astype(bf16)


def _mm_x01(a, w01):
    hi, lo = _split_hi_lo(a)
    w = w01.astype(bf16)
    return jnp.dot(hi, w, preferred_element_type=f32) + jnp.dot(lo, w, preferred_element_type=f32)


def _mm_01x(w01, a):
    hi, lo = _split_hi_lo(a)
    w = w01.astype(bf16)
    return jnp.dot(w, hi, preferred_element_type=f32) + jnp.dot(w, lo, preferred_element_type=f32)


def _rms(x):
    return x * lax.rsqrt(jnp.mean(x * x, axis=-1, keepdims=True) + NORM_EPS)


def _sigmoid(x):
    return 1.0 / (1.0 + jnp.exp(-x))


def _ada_kernel(c_ref, w_ref, b_ref, o_ref):
    c = c_ref[...]
    s = c * _sigmoid(c)
    o_ref[0] = _mm(s, w_ref[0]) + b_ref[0]


def _ada_mod(c_all, ada_w, ada_b):
    depth, d, n = ada_w.shape
    tn = 1024
    return pl.pallas_call(
        _ada_kernel,
        out_shape=jax.ShapeDtypeStruct((depth, 16, n), f32),
        grid=(depth, n // tn),
        in_specs=[pl.BlockSpec((16, d), lambda i, j: (0, 0)),
                  pl.BlockSpec((1, d, tn), lambda i, j: (i, 0, j)),
                  pl.BlockSpec((1, 1, tn), lambda i, j: (i, 0, j))],
        out_specs=pl.BlockSpec((1, 16, tn), lambda i, j: (i, 0, j)),
        compiler_params=_cparams(("arbitrary", "arbitrary")),
        name="ada_mod",
    )(c_all, ada_w, ada_b.reshape(depth, 1, n))


def _inproj_kernel(x_ref, m_ref, rope_ref, g_ref, win_ref, qg_ref, wq_ref, wqs_ref, kvg_ref, wk_ref, wv_ref,
                   e_ref, es_ref, q_ref, k_ref, v_ref, zp_ref, zr_ref):
    m = m_ref[0]
    h = _rms(x_ref[...]) * g_ref[...]
    h = h * (1.0 + m[1:2]) + m[0:1]
    z = _mm(h, win_ref[...])
    zp_ref[...] = z[:, 768:1024]
    for c4 in range(TM // CHUNK):
        zr_ref[c4, 0] = z[c4 * CHUNK:(c4 + 1) * CHUNK, 1024:2048]
    tile8 = lambda t: jnp.concatenate([t] * MLA_HEADS, axis=1)
    qn = (_rms(z[:, 0:Q_LORA]) * qg_ref[...]).astype(bf16)
    q = _mm(qn, wq_ref[...]) * tile8(rope_ref[0]) + _mm(qn, wqs_ref[...]) * tile8(rope_ref[1])
    q_ref[...] = q.astype(bf16)
    kvn = (_rms(z[:, Q_LORA:Q_LORA + KV_LORA]) * kvg_ref[...]).astype(bf16)
    kpe = z[:, 640:768].astype(bf16)
    k = _mm(kvn, wk_ref[...]) + _mm(kpe, e_ref[...]) * tile8(rope_ref[2]) + _mm(kpe, es_ref[...]) * tile8(rope_ref[3])
    k_ref[...] = k.astype(bf16)
    lane = lax.broadcasted_iota(jnp.int32, (1, MLA_HEADS * HEAD_PAD), 1)
    v_ref[...] = (_mm(kvn, wv_ref[...]) + jnp.where(lane % HEAD_PAD == V_HEAD, 1.0, 0.0)).astype(bf16)


def _inproj(xa, mt, rope, rope_idx, chunk_idx, npos, b_sz, p, nt):
    t_all, d = xa.shape
    cpt = TM // CHUNK
    full = lambda a: pl.BlockSpec(a.shape, lambda i: (0,) * a.ndim)
    ws = [p["pre_mix_g"], p["w_in_p"], p["mla_q_norm"], p["wq_p"], p["wq_s"], p["mla_kv_norm"], p["wk_p"], p["wv_p"],
          p["e_p"], p["e_s"]]
    return pl.pallas_call(
        _inproj_kernel,
        out_shape=(jax.ShapeDtypeStruct((t_all, MLA_HEADS * HEAD_PAD), bf16),
                   jax.ShapeDtypeStruct((t_all, MLA_HEADS * HEAD_PAD), bf16),
                   jax.ShapeDtypeStruct((t_all, MLA_HEADS * HEAD_PAD), bf16),
                   jax.ShapeDtypeStruct((t_all, POOL_WIDTH), f32),
                   jax.ShapeDtypeStruct((npos, b_sz, CHUNK, RWKV_PAD), f32)),
        grid=(nt,),
        in_specs=[pl.BlockSpec((TM, d), lambda i: (i, 0)),
                  pl.BlockSpec((1, 8, d), lambda i: (i, 0, 0)),
                  pl.BlockSpec((4, TM, HEAD_PAD), lambda i: (0, rope_idx(i), 0))] + [full(w) for w in ws],
        out_specs=(pl.BlockSpec((TM, 1024), lambda i: (i, 0)),
                   pl.BlockSpec((TM, 1024), lambda i: (i, 0)),
                   pl.BlockSpec((TM, 1024), lambda i: (i, 0)),
                   pl.BlockSpec((TM, POOL_WIDTH), lambda i: (i, 0)),
                   pl.BlockSpec((cpt, 1, CHUNK, RWKV_PAD), lambda i: chunk_idx(i) + (0, 0))),
        compiler_params=_cparams(("arbitrary",)),
        name="in_proj",
    )(xa, mt, rope, *ws)


def _attn_kernel(*refs):
    q_ref, kv, o_ref = refs[0], refs[1:-1], refs[-1]
    nt = (((1,), (1,)), ((), ()))
    for pr2 in range(MLA_HEADS // 2):
        hss = [slice(h * HEAD_PAD, (h + 1) * HEAD_PAD) for h in (2 * pr2, 2 * pr2 + 1)]
        scores = [[lax.dot_general(q_ref[:, hs], k_ref[:, hs], nt, preferred_element_type=f32) for k_ref in kv[0::2]]
                  for hs in hss]
        ms = [functools.reduce(jnp.maximum, [jnp.max(s, axis=-1, keepdims=True) for s in sc]) for sc in scores]
        prs = [[jnp.exp2((s - m).astype(bf16)) for s in sc] for sc, m in zip(scores, ms)]
        accs = [sum(jnp.dot(pr, v_ref[:, hs], preferred_element_type=f32) for pr, v_ref in zip(prh, kv[1::2]))
                for prh, hs in zip(prs, hss)]
        outs = [acc[:, :V_HEAD] / acc[:, V_HEAD:V_HEAD + 1] for acc in accs]
        o_ref[:, pr2 * 128:(pr2 + 1) * 128] = jnp.concatenate(outs, axis=1).astype(bf16)


def _attention(q, k, v, n_q_tiles, q_tile0, segs):
    in_specs = [pl.BlockSpec((TM, 1024), lambda i: (i + q_tile0, 0))]
    args = [q]
    for rows, bidx in segs:
        in_specs.append(pl.BlockSpec((rows, 1024), lambda i, bidx=bidx: (bidx(i), 0), pipeline_mode=pl.Buffered(1)))
        in_specs.append(pl.BlockSpec((rows, 1024), lambda i, bidx=bidx: (bidx(i), 0), pipeline_mode=pl.Buffered(1)))
        args += [k, v]
    return pl.pallas_call(
        _attn_kernel,
        out_shape=jax.ShapeDtypeStruct((n_q_tiles * TM, MLA_HEADS * V_HEAD), bf16),
        grid=(n_q_tiles,),
        in_specs=in_specs,
        out_specs=pl.BlockSpec((TM, 512), lambda i: (i, 0)),
        compiler_params=_cparams(("arbitrary",)),
        name="mla_attention",
    )(*args)


def _pool_kernel(z_ref, zp_ref, zn_ref, band_ref, cnt_ref, pw_ref, ps_ref, o_ref, *, nlat, lt, ct):
    i = pl.program_id(0)
    is_lat = i < nlat
    j = jnp.where(is_lat, i % lt, (i - nlat) % ct)
    n = jnp.where(is_lat, lt, ct)
    z = z_ref[...]
    prev = zp_ref[...] * jnp.where(j == 0, 0.0, 1.0)
    nxt = zn_ref[...] * jnp.where(j == n - 1, 0.0, 1.0)
    zh = jnp.concatenate([prev, z, nxt], axis=0)
    lane_grp = lax.broadcasted_iota(jnp.int32, (1, POOL_WIDTH), 1) // 64
    tot = jnp.zeros_like(z)
    for g in range(len(POOL_WINDOWS)):
        tot = tot + _mm_01x(band_ref[g], zh * jnp.where(lane_grp == g, 1.0, 0.0))
    diff = tot / cnt_ref[...] - z
    o_ref[...] = (_mm(diff, pw_ref[...]) * ps_ref[...]).astype(bf16)


def _pool(zp, band, cnt, cnt_idx, pw_bd, pscale, nt, nlat, lt, ct):
    t_all = zp.shape[0]
    nb8 = t_all // POOL_HALO
    r = TM // POOL_HALO
    return pl.pallas_call(
        functools.partial(_pool_kernel, nlat=nlat, lt=lt, ct=ct),
        out_shape=jax.ShapeDtypeStruct((nt * TM, POOL_WIDTH), bf16),
        grid=(nt,),
        in_specs=[pl.BlockSpec((TM, POOL_WIDTH), lambda i: (i, 0)),
                  pl.BlockSpec((POOL_HALO, POOL_WIDTH), lambda i: (jnp.maximum(i * r - 1, 0), 0)),
                  pl.BlockSpec((POOL_HALO, POOL_WIDTH), lambda i: (jnp.minimum((i + 1) * r, nb8 - 1), 0)),
                  pl.BlockSpec(band.shape, lambda i: (0, 0, 0)),
                  pl.BlockSpec((TM, POOL_WIDTH), lambda i: (cnt_idx(i), 0)),
                  pl.BlockSpec((POOL_WIDTH, POOL_WIDTH), lambda i: (0, 0)),
                  pl.BlockSpec((1, POOL_WIDTH), lambda i: (0, 0))],
        out_specs=pl.BlockSpec((TM, POOL_WIDTH), lambda i: (i, 0)),
        compiler_params=_cparams(("arbitrary",)),
        name="pool_mixer",
    )(zp, zp, zp, band, cnt, pw_bd, pscale)


def _rwkv_pos(d, s, *, ncc, nlc):
    in_ctx = s < ncc
    jc = jnp.where(d == 0, s, ncc - 1 - s)
    jl = jnp.where(d == 0, s - ncc, nlc - 1 - (s - ncc))
    pos = jnp.where(in_ctx, nlc + jc, jl)
    first = jnp.where(in_ctx, jc == 0, jl == 0)
    last = jnp.where(in_ctx, jc == ncc - 1, jl == nlc - 1)
    return pos, first, last


def _each(f, *lists):
    return [f(*xs) for xs in zip(*lists)]


def _mmb(a, b):
    return jnp.dot(a, b, preferred_element_type=f32)


def _tri_inverse(lmb, tri_ref, eye_b):
    cast = lambda xs: _each(lambda x: x.astype(bf16), xs)
    n = _each(lambda l: l * tri_ref[0, 2], lmb)
    n2 = cast(_each(_mmb, n, n))
    n4 = cast(_each(_mmb, n2, n2))
    t = _each(lambda a, b: _mmb((eye_b + a), (eye_b + b)), n, n2)
    t = _each(lambda a, b: _mmb(a.astype(bf16), eye_b + b), t, n4)
    for lvl in range(3):
        tb = cast(t)
        x = cast(_each(lambda a, l: _mmb(a, l * tri_ref[0, 3 + lvl]), tb, lmb))
        t = _each(lambda a, xx, ab: a + _mmb(xx, ab), t, x, tb)
    return t


def _rwkv_kernel(z_ref, zp_ref, zn_ref, bd_ref, eye_ref, tri_ref, csi_ref, mu_ref, w0_ref, w2_ref, a0_ref, a2_ref,
                 g2_ref, kk_ref, ka_ref, rk_ref, y_ref, bv_ref, g_ref, s_ref, *, ncc, nlc):
    d, s = pl.program_id(0), pl.program_id(2)
    _, first, last = _rwkv_pos(d, s, ncc=ncc, nlc=nlc)

    @pl.when(s == 0)
    def _():
        s_ref[...] = jnp.zeros_like(s_ref)

    c = CHUNK
    nbs = list(range(z_ref.shape[1]))
    keep_prev, keep_next = jnp.where(first, 0.0, 1.0), jnp.where(last, 0.0, 1.0)
    row = lax.broadcasted_iota(jnp.int32, (c, 1), 0)
    bd = bd_ref[...]
    bd_b = bd.astype(bf16)
    eye = eye_ref[...]
    eye_b = eye.astype(bf16)
    cast = lambda xs: _each(lambda x: x.astype(bf16), xs)

    def shifted(nb):
        z = z_ref[0, nb]
        zp = jnp.where(row == 0, zp_ref[0, nb, 7:8, :] * keep_prev, pltpu.roll(z, 1, 0))
        zn = jnp.where(row == c - 1, zn_ref[0, nb, 0:1, :] * keep_next, pltpu.roll(z, c - 1, 0))
        return z + mu_ref[...] * (0.5 * (zp + zn) - z)

    zs = _each(shifted, nbs)
    r, k, v = (_each(lambda z, o=o: z[:, o:o + 256], zs) for o in (0, 256, 512))
    lora = _each(lambda z: z[:, 768:896], zs)
    g = _each(lambda z: _mm(_sigmoid(z[:, 896:1024]), g2_ref[...]), zs)
    e = _each(lambda x: EXP_M05 * _sigmoid(w0_ref[0] + _mm(jnp.tanh(x), w2_ref[0])), lora)
    a = _each(lambda x: _sigmoid(a0_ref[0] + _mm(x, a2_ref[0])), lora)
    kd = _each(lambda kx, ax: kx * (1.0 + (ax - 1.0) * ka_ref[...]), k, a)
    kkr = _each(lambda kx: kx * kk_ref[...], k)
    kk = _each(lambda x: x / jnp.maximum(jnp.sqrt(_mm_x01(x * x, bd)), 1e-12), kkr)
    bv = _each(lambda rx, kx, vx: _mm_x01(rx * kx * rk_ref[...], bd) * vx, r, kd, v)
    bb = _each(lambda x, ax: x * ax, kk, a)

    cs = _each(lambda x: _mm_01x(csi_ref[0], x), e)
    tot = _each(lambda x: jnp.where(d == 0, x[c - 1:c, :], x[0:1, :]), cs)
    rep4 = lambda t: jnp.concatenate([t] * RWKV_HEADS, axis=0)
    fold4 = lambda t: t[0:c] + t[c:2 * c] + t[2 * c:3 * c] + t[3 * c:4 * c]
    head_rows = lambda t: rep4(t.astype(bf16)) * bd_b
    a4 = _each(lambda ex, cx, kx: head_rows(jnp.exp(ex - cx) * kx), e, cs, kk)
    r_s = _each(lambda rx, cx: rx * jnp.exp(-cx), r, cs)
    r4 = _each(head_rows, r_s)
    v4 = _each(head_rows, v)
    grow = _each(jnp.exp, cs)
    b4 = _each(lambda x, gx: rep4((x * gx).astype(bf16)), bb, grow)
    k4 = _each(lambda x, gx: rep4((x * gx).astype(bf16)), kd, grow)
    to_end = _each(lambda cx, tx: jnp.exp(cx - tx), cs, tot)
    b_e = _each(lambda x, gx: (x * gx).astype(bf16), bb, to_end)
    k_e = _each(lambda x, gx: (x * gx).astype(bf16), kd, to_end)
    g_end = _each(lambda tx: jnp.exp(-tx), tot)

    ntb = lambda x, y: lax.dot_general(x, y, (((1,), (1,)), ((), ())), preferred_element_type=f32).astype(bf16)
    tnb = lambda x, y: lax.dot_general(x, y, (((0,), (0,)), ((), ())), preferred_element_type=f32)
    lmb = _each(lambda x, y: ntb(x, y) * tri_ref[0, 0], a4, b4)
    akm = _each(lambda x, y: ntb(x, y) * tri_ref[0, 0], a4, k4)
    rbn = _each(lambda x, y: ntb(x, y) * tri_ref[0, 1], r4, b4)
    rkm = _each(lambda x, y: ntb(x, y) * tri_ref[0, 6], r4, k4)
    t = cast(_tri_inverse(lmb, tri_ref, eye_b))
    w4 = cast(_each(_mmb, t, a4))
    u4 = cast(_each(lambda tx, ax, vx: _mmb(tx, _mmb(ax, vx).astype(bf16)), t, akm, v4))
    q_all = _each(lambda rx, bx, wx: rx + fold4(_mmb(bx, wx)), r_s, rbn, w4)
    y0 = _each(lambda kx, vx, bx, ux: fold4(_mmb(kx, vx) + _mmb(bx, ux)), rkm, v4, rbn, u4)
    w_all, u_all = _each(fold4, w4), _each(fold4, u4)
    g_bd = _each(lambda gx, wx, bx: eye * gx - bd * tnb(wx, bx), g_end, w_all, b_e)
    h_bd = _each(lambda vx, kx, ux, bx: bd * (tnb(vx.astype(bf16), kx) - tnb(ux, bx)), v, k_e, u_all, b_e)
    for nb in nbs:
        st = s_ref[nb]
        y_ref[0, 0, nb] = _mm_nt(q_all[nb], st) + y0[nb]
        bv_ref[0, 0, nb] = bv[nb]
        g_ref[0, 0, nb] = g[nb]
        s_ref[nb] = _mm(st, g_bd[nb]) + h_bd[nb]


def _rwkv(zr, consts, p, ncc, nlc):
    npos, b_sz = zr.shape[0], zr.shape[1]
    nbat = math.gcd(b_sz, RWKV_BATCHES_PER_STEP)
    kw = dict(ncc=ncc, nlc=nlc)
    pos = lambda d, s: _rwkv_pos(d, s, **kw)[0]
    full = lambda a: pl.BlockSpec(a.shape, lambda d, b, s: (0,) * a.ndim)
    by_dir = lambda a: pl.BlockSpec((1,) + a.shape[1:], lambda d, b, s: (d,) + (0,) * (a.ndim - 1))
    out = jax.ShapeDtypeStruct((2, npos, b_sz, CHUNK, RWKV_WIDTH), f32)
    ospec = pl.BlockSpec((1, 1, nbat, CHUNK, RWKV_WIDTH), lambda d, b, s: (d, pos(d, s), b, 0, 0))
    last8 = CHUNK // 8 - 1
    return pl.pallas_call(
        functools.partial(_rwkv_kernel, **kw),
        out_shape=(out, out, out),
        grid=(2, b_sz // nbat, npos),
        in_specs=[pl.BlockSpec((1, nbat, CHUNK, RWKV_PAD), lambda d, b, s: (pos(d, s), b, 0, 0)),
                  pl.BlockSpec((1, nbat, 8, RWKV_PAD), lambda d, b, s: (jnp.maximum(pos(d, s) - 1, 0), b, last8, 0)),
                  pl.BlockSpec((1, nbat, 8, RWKV_PAD), lambda d, b, s: (jnp.minimum(pos(d, s) + 1, npos - 1), b, 0, 0)),
                  full(consts["bd"]), full(consts["eye"]), by_dir(consts["tri"]), by_dir(consts["csi"]),
                  full(p["mu_p"]), by_dir(p["w0"]), by_dir(p["w2_p"]), by_dir(p["a0"]), by_dir(p["a2_p"]),
                  full(p["g2_p"]), full(p["k_k"]), full(p["k_a"]), full(p["r_k"])],
        out_specs=(ospec, ospec, ospec),
        scratch_shapes=[pltpu.VMEM((nbat, RWKV_WIDTH, RWKV_WIDTH), f32)],
        compiler_params=_cparams(("arbitrary", "arbitrary", "arbitrary")),
        name="rwkv7_chunked",
    )(zr, zr, zr, consts["bd"], consts["eye"], consts["tri"], consts["csi"], p["mu_p"], p["w0"], p["w2_p"], p["a0"],
      p["a2_p"], p["g2_p"], p["k_k"], p["k_a"], p["r_k"])


def _pack_bf16_pairs(lo, hi):
    lo_b = pltpu.bitcast(lo.astype(bf16).astype(f32), jnp.uint32)
    hi_b = pltpu.bitcast(hi.astype(bf16).astype(f32), jnp.uint32)
    return (hi_b & jnp.uint32(0xFFFF0000)) | (lo_b >> 16)


def _unpack_bf16_pairs(w):
    lo = pltpu.bitcast(w << 16, f32).astype(bf16)
    hi = pltpu.bitcast(w & jnp.uint32(0xFFFF0000), f32).astype(bf16)
    return lo, hi


def _outproj_kernel(a_ref, py_ref, y_ref, bv_ref, g_ref, x_ref, m_ref, avg_ref, lnw_ref, lnb_ref, wo_ref, pmg_ref,
                    pfg_ref, x1_ref, hp_ref):
    m = m_ref[0]
    rows = lambda ref, dd: jnp.concatenate([ref[dd, c4, 0] for c4 in range(TM // CHUNK)], axis=0)
    ysum = rows(y_ref, 0) + rows(y_ref, 1)
    avg = avg_ref[...]
    dev = ysum - _mm_x01(ysum, avg)
    var = _mm_x01(dev * dev, avg)
    yn = dev * lax.rsqrt(var + RWKV_GN_EPS) * lnw_ref[...] + lnb_ref[...]
    rw = (yn + rows(bv_ref, 0) + rows(bv_ref, 1)) * rows(g_ref, 0)
    o = (jnp.dot(a_ref[...], wo_ref[0:512, :], preferred_element_type=f32)
         + jnp.dot(py_ref[...], wo_ref[512:768, :], preferred_element_type=f32)
         + _mm(rw, wo_ref[768:1024, :]))
    x1 = x_ref[...] + m[2:3] * (_rms(o) * pmg_ref[...])
    x1_ref[...] = x1
    h = (_rms(x1) * pfg_ref[...]) * (1.0 + m[4:5]) + m[3:4]
    hp_ref[...] = _pack_bf16_pairs(h[:, 0:512], h[:, 512:1024])


def _outproj(att, py, y, bv, g, xa, mt, chunk_idx, consts, p, nt):
    d = xa.shape[1]
    cpt = TM // CHUNK
    cspec = lambda nd: pl.BlockSpec((nd, cpt, 1, CHUNK, RWKV_WIDTH), lambda i: (0,) + chunk_idx(i) + (0, 0))
    full = lambda a: pl.BlockSpec(a.shape, lambda i: (0,) * a.ndim)
    ws = [consts["avg"], p["ln_w"], p["ln_b"], p["w_out"], p["post_mix_g"], p["pre_ffn_g"]]
    return pl.pallas_call(
        _outproj_kernel,
        out_shape=(jax.ShapeDtypeStruct((nt * TM, d), f32), jax.ShapeDtypeStruct((nt * TM, d // 2), jnp.uint32)),
        grid=(nt,),
        in_specs=[pl.BlockSpec((TM, 512), lambda i: (i, 0)),
                  pl.BlockSpec((TM, POOL_WIDTH), lambda i: (i, 0)),
                  cspec(2), cspec(2), cspec(1),
                  pl.BlockSpec((TM, d), lambda i: (i, 0)),
                  pl.BlockSpec((1, 8, d), lambda i: (i, 0, 0))] + [full(w) for w in ws],
        out_specs=(pl.BlockSpec((TM, d), lambda i: (i, 0)), pl.BlockSpec((TM, d // 2), lambda i: (i, 0))),
        compiler_params=_cparams(("arbitrary",)),
        name="out_proj",
    )(att, py, y, bv, g, xa, mt, *ws)


def _router_kernel(hp_ref, rw_ref, rb_ref, ut_ref, lt_ref, ei_ref, pos_ref, gt_ref, cnt_ref, run_ref):
    i = pl.program_id(0)

    @pl.when(i == 0)
    def _():
        run_ref[...] = jnp.zeros_like(run_ref)

    tm = hp_ref.shape[0]
    ne, ng = N_EXPERTS, N_GROUPS
    pg = ne // ng
    lo, hi = _unpack_bf16_pairs(hp_ref[...])
    logits = (lax.dot_general(rw_ref[:, 0:512], lo, (((1,), (1,)), ((), ())), preferred_element_type=f32)
              + lax.dot_general(rw_ref[:, 512:1024], hi, (((1,), (1,)), ((), ())), preferred_element_type=f32))
    scores = _sigmoid(logits)
    sel = scores + rb_ref[...]
    neg = -jnp.inf

    s3 = sel.reshape(ng, pg, tm)
    io = lax.broadcasted_iota(jnp.int32, (ng, pg, tm), 1)
    m1 = jnp.max(s3, axis=1, keepdims=True)
    i1 = jnp.min(jnp.where(s3 == m1, io, pg), axis=1, keepdims=True)
    m2 = jnp.max(jnp.where(io == i1, neg, s3), axis=1, keepdims=True)
    gs = (m1 + m2).reshape(ng, tm)
    gi = lax.broadcasted_iota(jnp.int32, (ng, tm), 0)
    grank = jnp.zeros((ng, tm), f32)
    for j in range(ng):
        rj = gs[j:j + 1, :]
        grank = grank + jnp.where((rj > gs) | ((rj == gs) & (j < gi)), 1.0, 0.0)
    gsel = jnp.where(grank < TOPK_GROUPS, 1.0, 0.0)
    gsel3 = jnp.broadcast_to(gsel.reshape(ng, 1, tm), (ng, pg, tm)).reshape(ne, tm)
    msk = jnp.where(gsel3 > 0.5, sel, neg)
    ei = lax.broadcasted_iota(jnp.int32, (ne, tm), 0)
    erank = jnp.zeros((ne, tm), f32)
    for j in range(ne):
        rj = msk[j:j + 1, :]
        erank = erank + jnp.where((rj > msk) | ((rj == msk) & (j < ei)), 1.0, 0.0)
    chosen = erank < TOP_K
    chf = jnp.where(chosen, 1.0, 0.0)
    graw = jnp.where(chosen, scores, 0.0)
    gate = graw / jnp.sum(graw, axis=0, keepdims=True) * ROUTED_SCALE

    pos = run_ref[...] + _mm(chf, ut_ref[...])
    tot = jnp.sum(chf, axis=1, keepdims=True)
    run_new = run_ref[...] + tot
    run_ref[...] = run_new
    cnt_ref[...] = run_new[:, 0:128]
    rk = _mm(lt_ref[...], chf)
    eif = ei.astype(f32)
    rows_e, rows_p, rows_g = [], [], []
    for kq in range(TOP_K):
        mk = chosen & (rk == float(kq))
        rows_e.append(jnp.sum(jnp.where(mk, eif, 0.0), axis=0, keepdims=True))
        rows_p.append(jnp.sum(jnp.where(mk, pos, 0.0), axis=0, keepdims=True))
        rows_g.append(jnp.sum(jnp.where(mk, gate, 0.0), axis=0, keepdims=True))
    zrow = jnp.zeros((8 - TOP_K, tm), f32)
    ei_ref[0] = jnp.concatenate(rows_e + [zrow], axis=0).astype(jnp.int32)
    pos_ref[0] = jnp.concatenate(rows_p + [zrow], axis=0).astype(jnp.int32)
    gpad = jnp.concatenate(rows_g + [jnp.zeros((128 - TOP_K, tm), f32)], axis=0)
    gt_ref[...] = gpad.T


def _router(hp, rwt, rb, consts, nt):
    full = lambda a: pl.BlockSpec(a.shape, lambda i: (0,) * a.ndim)
    return pl.pallas_call(
        _router_kernel,
        out_shape=(jax.ShapeDtypeStruct((nt, 8, TM), jnp.int32), jax.ShapeDtypeStruct((nt, 8, TM), jnp.int32),
                   jax.ShapeDtypeStruct((nt * TM, 128), f32), jax.ShapeDtypeStruct((N_EXPERTS, 128), f32)),
        grid=(nt,),
        in_specs=[pl.BlockSpec((TM, 512), lambda i: (i, 0)), full(rwt), full(rb), full(consts["ut"]),
                  full(consts["lt"])],
        out_specs=(pl.BlockSpec((1, 8, TM), lambda i: (i, 0, 0)), pl.BlockSpec((1, 8, TM), lambda i: (i, 0, 0)),
                   pl.BlockSpec((TM, 128), lambda i: (i, 0)), pl.BlockSpec((N_EXPERTS, 128), lambda i: (0, 0))),
        scratch_shapes=[pltpu.VMEM((N_EXPERTS, TM), f32)],
        compiler_params=_cparams(("arbitrary",)),
        name="moe_router",
    )(hp, rwt, rb, consts["ut"], consts["lt"])


def _slots_kernel(ps_ref, ei_ref, pos_ref, d_ref):
    ei = ei_ref[0]
    slot = pos_ref[0]
    for e in range(N_EXPERTS):
        slot = slot + jnp.where(ei == e, ps_ref[e], 0)
    d_ref[...] = slot


def _slots(pstart, ei, pos, nt):
    return pl.pallas_call(
        _slots_kernel,
        out_shape=jax.ShapeDtypeStruct((8, nt * TM), jnp.int32),
        grid_spec=pltpu.PrefetchScalarGridSpec(
            num_scalar_prefetch=1,
            grid=(nt,),
            in_specs=[pl.BlockSpec((1, 8, TM), lambda i, ps: (i, 0, 0)), pl.BlockSpec((1, 8, TM), lambda i, ps: (i, 0, 0))],
            out_specs=pl.BlockSpec((8, TM), lambda i, ps: (0, i))),
        compiler_params=_cparams(("arbitrary",)),
        name="moe_slots",
    )(pstart, ei, pos)


def _sc_dispatch(hp, dest, n_slots):
    t, w = hp.shape
    wh = w // 2
    mesh = plsc.VectorSubcoreMesh(core_axis_name="core", subcore_axis_name="subcore", num_cores=SC_CORES,
                                  num_subcores=SC_SUBCORES)
    idx = [dest[kq:kq + 1] for kq in range(TOP_K)]
    half = jax.ShapeDtypeStruct((n_slots, wh), hp.dtype)

    @pl.kernel(out_type=(half, half), mesh=mesh, scratch_types=[])
    def scatter_rows(hp_hbm, *rest):
        idx_hbm, xs_hbm = rest[:TOP_K], rest[TOP_K:]
        for c in range(2):
            def body(x_vmem, *i_vmem, c=c):
                for iv in i_vmem:
                    pltpu.sync_copy(x_vmem, xs_hbm[c].at[iv.at[0]])

            pltpu.emit_pipeline(
                body,
                grid=(t // SC_WINDOW,),
                in_specs=[pl.BlockSpec((SC_WINDOW, wh), lambda i, c=c: (i, c))]
                + [pl.BlockSpec((1, SC_WINDOW), lambda i: (0, i))] * TOP_K,
                out_specs=[],
                core_axis_name=("core", "subcore"),
                dimension_semantics=(pltpu.PARALLEL,),
            )(hp_hbm, *idx_hbm)

    return scatter_rows(hp, *idx)


def _expert_kernel(be_ref, nb_ref, bv_ref, xa_ref, xb_ref, wg_ref, wu_ref, wd_ref, ya_ref, yb_ref):
    i = pl.program_id(0)

    @pl.when(i < nb_ref[0])
    def _():
        live = lax.broadcasted_iota(jnp.int32, (MOE_BM, 1), 0) < bv_ref[i]
        la, ha = _unpack_bf16_pairs(jnp.where(live, xa_ref[...], jnp.uint32(0)))
        lb, hb = _unpack_bf16_pairs(jnp.where(live, xb_ref[...], jnp.uint32(0)))

        def proj(w_ref):
            return sum(jnp.dot(x, w_ref[0, 0, q * 256:(q + 1) * 256, :].astype(bf16), preferred_element_type=f32)
                       for q, x in enumerate((la, lb, ha, hb)))

        gg, uu = proj(wg_ref), proj(wu_ref)
        act = gg * _sigmoid(gg) * uu
        y = _mm(act, wd_ref[0, 0])
        ya_ref[...] = _pack_bf16_pairs(y[:, 0:256], y[:, 512:768])
        yb_ref[...] = _pack_bf16_pairs(y[:, 256:512], y[:, 768:1024])

    @pl.when(i >= nb_ref[0])
    def _():
        ya_ref[...] = jnp.zeros_like(ya_ref)
        yb_ref[...] = jnp.zeros_like(yb_ref)


def _experts(block_expert, nb_used, block_valid, xs, layer, w_gate, w_up, w_down):
    xa, xb = xs
    n_slots = xa.shape[0]
    half = jax.ShapeDtypeStruct((n_slots, xa.shape[1]), jnp.uint32)
    hspec = pl.BlockSpec((MOE_BM, xa.shape[1]), lambda i, be, nb, bv: (i, 0))
    wspec = lambda w: pl.BlockSpec((1, 1) + w.shape[2:], lambda i, be, nb, bv: (layer, be[i], 0, 0))
    return pl.pallas_call(
        _expert_kernel,
        out_shape=(half, half),
        grid_spec=pltpu.PrefetchScalarGridSpec(
            num_scalar_prefetch=3,
            grid=(n_slots // MOE_BM,),
            in_specs=[pl.BlockSpec((MOE_BM, xa.shape[1]), lambda i, be, nb, bv: (i, 0)),
                      pl.BlockSpec((MOE_BM, xb.shape[1]), lambda i, be, nb, bv: (i, 0)),
                      wspec(w_gate), wspec(w_up), wspec(w_down)],
            out_specs=(hspec, hspec)),
        compiler_params=_cparams(("arbitrary",)),
        name="moe_experts",
    )(block_expert, nb_used, block_valid, xa, xb, w_gate, w_up, w_down)


def _sc_gather(ys, dest):
    t = dest.shape[1]
    wh = ys[0].shape[1]
    mesh = plsc.VectorSubcoreMesh(core_axis_name="core", subcore_axis_name="subcore", num_cores=SC_CORES,
                                  num_subcores=SC_SUBCORES)
    idx = [dest[kq:kq + 1] for kq in range(TOP_K)]
    out = jax.ShapeDtypeStruct((t, wh), ys[0].dtype)

    @pl.kernel(out_type=(out,) * (2 * TOP_K), mesh=mesh, scratch_types=[])
    def gather_rows(ya_hbm, yb_hbm, *rest):
        idx_hbm, out_hbm = rest[:TOP_K], rest[TOP_K:]
        for c, y_hbm in enumerate((ya_hbm, yb_hbm)):
            for kq in range(TOP_K):
                def body(i_vmem, o_vmem, y_hbm=y_hbm):
                    pltpu.sync_copy(y_hbm.at[i_vmem.at[0]], o_vmem)

                pltpu.emit_pipeline(
                    body,
                    grid=(t // SC_WINDOW,),
                    in_specs=[pl.BlockSpec((1, SC_WINDOW), lambda i: (0, i))],
                    out_specs=[pl.BlockSpec((SC_WINDOW, wh), lambda i: (i, 0))],
                    core_axis_name=("core", "subcore"),
                    dimension_semantics=(pltpu.PARALLEL,),
                )(idx_hbm[kq], out_hbm[c * TOP_K + kq])

    outs = gather_rows(ys[0], ys[1], *idx)
    return outs[:TOP_K], outs[TOP_K:]


def _combine_kernel(*refs):
    ga, gb = refs[0:TOP_K], refs[TOP_K:2 * TOP_K]
    hp_ref, gt_ref, x1_ref, m_ref, wsgu_ref, wsd_ref, pg_ref, o_ref = refs[2 * TOP_K:]
    lo, hi = _unpack_bf16_pairs(hp_ref[...])
    gu = (jnp.dot(lo, wsgu_ref[0:512, :], preferred_element_type=f32)
          + jnp.dot(hi, wsgu_ref[512:1024, :], preferred_element_type=f32))
    gg, uu = gu[:, 0:EXPERT_FF], gu[:, EXPERT_FF:2 * EXPERT_FF]
    f = _mm(gg * _sigmoid(gg) * uu, wsd_ref[...])
    gt = gt_ref[...]
    parts = [jnp.zeros((TM, 256), f32) for _ in range(4)]
    for kq in range(TOP_K):
        g = gt[:, kq:kq + 1]
        la, ha = _unpack_bf16_pairs(ga[kq][...])
        lb, hb = _unpack_bf16_pairs(gb[kq][...])
        for q, v in enumerate((la, lb, ha, hb)):
            parts[q] = parts[q] + v.astype(f32) * g
    f = f + jnp.concatenate(parts, axis=1)
    m = m_ref[0]
    o_ref[...] = x1_ref[...] + m[5:6] * (_rms(f) * pg_ref[...])


def _combine(ga, gb, hp, gt, x1, mt, p, nt):
    d = x1.shape[1]
    full = lambda a: pl.BlockSpec(a.shape, lambda i: (0,) * a.ndim)
    ws = [p["sh_wgu"], p["sh_wd"], p["post_ffn_g"]]
    gspec = pl.BlockSpec((TM, ga[0].shape[1]), lambda i: (i, 0))
    return pl.pallas_call(
        _combine_kernel,
        out_shape=jax.ShapeDtypeStruct((nt * TM, d), f32),
        grid=(nt,),
        in_specs=[gspec] * (2 * TOP_K)
        + [pl.BlockSpec((TM, d // 2), lambda i: (i, 0)),
           pl.BlockSpec((TM, 128), lambda i: (i, 0)),
           pl.BlockSpec((TM, d), lambda i: (i, 0)),
           pl.BlockSpec((1, 8, d), lambda i: (i, 0, 0))] + [full(w) for w in ws],
        out_specs=pl.BlockSpec((TM, d), lambda i: (i, 0)),
        compiler_params=_cparams(("arbitrary",)),
        name="moe_combine",
    )(*ga, *gb, hp, gt, x1, mt, *ws)


def _moe(hp, x1, mt, consts, p, nt):
    t = nt * TM
    ei, pos, gt, cnt = _router(hp, p["router_wt"], p["router_b"], consts, nt)
    n_assign = t * TOP_K
    n_blocks = -(-(n_assign + N_EXPERTS * (MOE_BM - 1)) // MOE_BM)
    n_slots = n_blocks * MOE_BM
    counts = cnt[:, 0].astype(jnp.int32)
    padded = (counts + MOE_BM - 1) // MOE_BM * MOE_BM
    pend = jnp.cumsum(padded)
    pstart = pend - padded
    nb_used = (pend[-1:] // MOE_BM).astype(jnp.int32)
    block_row0 = jnp.arange(n_blocks, dtype=jnp.int32) * MOE_BM
    block_expert = jnp.minimum(jnp.sum((pend[None, :] <= block_row0[:, None]).astype(jnp.int32), axis=1),
                               N_EXPERTS - 1)
    run_end = (pstart + counts)[block_expert]
    block_valid = jnp.clip(run_end - block_row0, 0, MOE_BM).astype(jnp.int32)
    dest = _slots(pstart.astype(jnp.int32), ei, pos, nt)
    xs = _sc_dispatch(hp, dest, n_slots)
    ys = _experts(block_expert, nb_used, block_valid, xs, p["layer"], *p["experts"])
    ga, gb = _sc_gather(ys, dest)
    return _combine(ga, gb, hp, gt, x1, mt, p, nt)


def _np_consts(l_lat, l_ctx):
    n = RWKV_WIDTH
    i = np.arange(n)
    bd = (i[:, None] // 64 == i[None, :] // 64).astype(np.float32)
    t_r, t_c = (i % 64)[:, None], (i % 64)[None, :]
    tri = np.zeros((2, 7, n, n), np.float32)
    for d in range(2):
        before = (t_c < t_r) if d == 0 else (t_c > t_r)
        tri[d, 0] = bd * before
        tri[d, 6] = bd * (before | (t_c == t_r))
        tri[d, 1] = -tri[d, 6]
        tri[d, 2] = -(bd * before * (t_r // 8 == t_c // 8))
        for lvl, blk in enumerate((8, 16, 32)):
            tri[d, 3 + lvl] = -(bd * before * (t_r // (2 * blk) == t_c // (2 * blk)) * (t_r // blk != t_c // blk))
    j = np.arange(CHUNK)
    csi = np.stack([(j[None, :] <= j[:, None]), (j[None, :] >= j[:, None])]).astype(np.float32)
    tt = np.arange(TM)
    ut = (tt[:, None] < tt[None, :]).astype(np.float32)
    ee = np.arange(N_EXPERTS)
    lt = (ee[None, :] < ee[:, None]).astype(np.float32)
    jj = np.arange(TM + 2 * POOL_HALO)[None, :]
    band = np.stack([((jj >= tt[:, None] + POOL_HALO - w // 2) & (jj <= tt[:, None] + POOL_HALO + w // 2 - 1))
                     for w in POOL_WINDOWS]).astype(np.float32)

    def counts(length):
        t = np.arange(length)[:, None]
        half = np.repeat(np.array(POOL_WINDOWS) // 2, 64)[None, :]
        return (np.clip(t + half, 0, length) - np.clip(t - half, 0, length)).astype(np.float32)

    cnt = np.concatenate([counts(l_lat), counts(l_ctx)], axis=0)
    return dict(bd=jnp.asarray(bd), eye=jnp.eye(n, dtype=f32), tri=jnp.asarray(tri, dtype=bf16), csi=jnp.asarray(csi),
                avg=jnp.asarray(bd / 64.0), ut=jnp.asarray(ut, dtype=bf16), lt=jnp.asarray(lt, dtype=bf16),
                band=jnp.asarray(band), cnt=jnp.asarray(cnt))


def _rope_tables(l_lat):
    rows = l_lat // GRID_W
    row = jnp.repeat(jnp.arange(rows, dtype=f32), GRID_W)
    col = jnp.tile(jnp.arange(GRID_W, dtype=f32), rows)
    n_freq = QK_ROPE // 4
    inv = ROPE_THETA ** (-jnp.arange(n_freq, dtype=f32) / n_freq)
    ang = jnp.concatenate([row[:, None] * inv, col[:, None] * inv], -1)
    cos = jnp.concatenate([jnp.cos(ang), jnp.ones((TM, 16), f32)], 0)
    sin = jnp.concatenate([jnp.sin(ang), jnp.zeros((TM, 16), f32)], 0)
    n = cos.shape[0]
    ct = jnp.concatenate([jnp.ones((n, QK_NOPE), f32), cos, cos, jnp.zeros((n, 32), f32)], 1)
    st = jnp.concatenate([jnp.zeros((n, QK_NOPE), f32), -sin, sin, jnp.zeros((n, 32), f32)], 1)
    qs = MLA_SCALE * LOG2E
    return jnp.stack([ct * qs, st * qs, ct, st])


def _layer_params(i, a):
    d = a["w_in"].shape[1]
    p = {}
    row = lambda v: v.reshape(1, -1).astype(f32)
    for name in ("pre_mix_g", "post_mix_g", "pre_ffn_g", "post_ffn_g", "mla_q_norm", "mla_kv_norm"):
        p[name] = row(a[name][i])
    w_in = a["w_in"][i]
    zc = lambda n: jnp.zeros((d, n), f32)
    p["w_in_p"] = jnp.concatenate(
        [w_in[:, 0:MLA_IN], zc(128 - QK_ROPE), w_in[:, MLA_IN:MLA_IN + POOL_WIDTH],
         w_in[:, MLA_IN + POOL_WIDTH:], zc(RWKV_PAD - RWKV_IN)], 1).astype(bf16)
    wq = a["mla_w_q_b"][i].reshape(Q_LORA, MLA_HEADS, QK_NOPE + QK_ROPE)
    zq = jnp.zeros((Q_LORA, MLA_HEADS, 32), f32)
    half = QK_ROPE // 2
    p["wq_p"] = jnp.concatenate([wq, zq], 2).reshape(Q_LORA, -1).astype(bf16)
    p["wq_s"] = jnp.concatenate([jnp.zeros_like(wq[:, :, :QK_NOPE]), wq[:, :, QK_NOPE + half:],
                                 wq[:, :, QK_NOPE:QK_NOPE + half], zq], 2).reshape(Q_LORA, -1).astype(bf16)
    wkv = a["mla_w_kv_b"][i].reshape(KV_LORA, MLA_HEADS, QK_NOPE + V_HEAD)
    p["wk_p"] = jnp.concatenate([wkv[:, :, :QK_NOPE], jnp.zeros((KV_LORA, MLA_HEADS, 64), f32)], 2
                                ).reshape(KV_LORA, -1).astype(bf16)
    p["wv_p"] = jnp.concatenate([wkv[:, :, QK_NOPE:], jnp.zeros((KV_LORA, MLA_HEADS, 64), f32)], 2
                                ).reshape(KV_LORA, -1).astype(bf16)
    e_p = np.zeros((128, MLA_HEADS * HEAD_PAD), np.float32)
    e_s = np.zeros_like(e_p)
    for h in range(MLA_HEADS):
        for j in range(QK_ROPE):
            e_p[j, h * HEAD_PAD + QK_NOPE + j] = 1.0
            e_s[(j + half) % QK_ROPE, h * HEAD_PAD + QK_NOPE + j] = 1.0
    p["e_p"], p["e_s"] = jnp.asarray(e_p, dtype=bf16), jnp.asarray(e_s, dtype=bf16)
    pw = a["pool_w"][i]
    p["pool_w_bd"] = jax.scipy.linalg.block_diag(*[pw[g] for g in range(pw.shape[0])]).astype(bf16)
    p["pool_scale"] = row(a["pool_scale"][i])
    p["mu_p"] = jnp.pad(a["rwkv_mu"][i], (0, RWKV_PAD - RWKV_IN)).reshape(1, -1)
    p["w0"] = a["rwkv_w0"][i].reshape(2, 1, RWKV_WIDTH)
    p["a0"] = a["rwkv_a0"][i].reshape(2, 1, RWKV_WIDTH)
    z32 = jnp.zeros((32, RWKV_WIDTH), f32)
    w2, a2 = a["rwkv_w2"][i], a["rwkv_a2"][i]
    p["w2_p"] = jnp.stack([jnp.concatenate([w2[0], z32, z32, z32]), jnp.concatenate([z32, w2[1], z32, z32])])
    p["a2_p"] = jnp.stack([jnp.concatenate([z32, z32, a2[0], z32]), jnp.concatenate([z32, z32, z32, a2[1]])])
    p["g2_p"] = jnp.concatenate([a["rwkv_g2"][i], jnp.zeros((64, RWKV_WIDTH), f32)])
    for name in ("k_k", "k_a", "r_k"):
        p[name] = row(a["rwkv_" + name][i])
    p["ln_w"], p["ln_b"] = row(a["rwkv_ln_w"][i]), row(a["rwkv_ln_b"][i])
    p["w_out"] = a["w_out"][i].astype(bf16)
    p["router_wt"] = a["router_w"][i].T.astype(bf16)
    p["router_b"] = a["router_bias"][i].reshape(-1, 1).astype(f32)
    p["layer"] = i
    p["experts"] = (a["exp_w_gate"], a["exp_w_up"], a["exp_w_down"])
    p["sh_wgu"] = jnp.concatenate([a["sh_w_gate"][i], a["sh_w_up"][i]], 1).astype(bf16)
    p["sh_wd"] = a["sh_w_down"][i].astype(bf16)
    return p


def kernel(x, c, ctx, c_ctx, ada_w, ada_b, pre_mix_g, post_mix_g, pre_ffn_g, post_ffn_g, w_in, w_out, mla_q_norm, mla_w_q_b, mla_kv_norm, mla_w_kv_b, pool_w, pool_scale, rwkv_mu, rwkv_w0, rwkv_w2, rwkv_a0, rwkv_a2, rwkv_g2, rwkv_k_k, rwkv_k_a, rwkv_r_k, rwkv_ln_w, rwkv_ln_b, router_w, router_bias, exp_w_gate, exp_w_up, exp_w_down, sh_w_gate, sh_w_up, sh_w_down):
    arrs = dict(pre_mix_g=pre_mix_g, post_mix_g=post_mix_g, pre_ffn_g=pre_ffn_g, post_ffn_g=post_ffn_g, w_in=w_in,
                w_out=w_out, mla_q_norm=mla_q_norm, mla_w_q_b=mla_w_q_b, mla_kv_norm=mla_kv_norm,
                mla_w_kv_b=mla_w_kv_b, pool_w=pool_w, pool_scale=pool_scale, rwkv_mu=rwkv_mu, rwkv_w0=rwkv_w0,
                rwkv_w2=rwkv_w2, rwkv_a0=rwkv_a0, rwkv_a2=rwkv_a2, rwkv_g2=rwkv_g2, rwkv_k_k=rwkv_k_k,
                rwkv_k_a=rwkv_k_a, rwkv_r_k=rwkv_r_k, rwkv_ln_w=rwkv_ln_w, rwkv_ln_b=rwkv_ln_b, router_w=router_w,
                router_bias=router_bias, exp_w_gate=exp_w_gate, exp_w_up=exp_w_up, exp_w_down=exp_w_down,
                sh_w_gate=sh_w_gate, sh_w_up=sh_w_up, sh_w_down=sh_w_down)
    b_sz, l_lat, d = x.shape
    l_ctx = ctx.shape[1]
    assert l_lat % TM == 0 and l_ctx % TM == 0 and l_lat % GRID_W == 0 and b_sz < 16
    lt, ct = l_lat // TM, l_ctx // TM
    nlat, nctx = b_sz * lt, b_sz * ct
    nall = nlat + nctx
    ncc, nlc = l_ctx // CHUNK, l_lat // CHUNK
    consts = _np_consts(l_lat, l_ctx)
    rope = _rope_tables(l_lat)
    rope_idx = lambda i: jnp.where(i < nlat, i % lt, lt)
    cnt_idx = lambda i: jnp.where(i < nlat, i % lt, lt + (i - nlat) % ct)
    chunk_idx = lambda i: (jnp.where(i < nlat, i % lt, nlc * CHUNK // TM + (i - nlat) % ct),
                           jnp.where(i < nlat, i // lt, (i - nlat) // ct))

    c_all = jnp.zeros((16, d), f32).at[:b_sz].set(c).at[b_sz].set(c_ctx)
    mods = _ada_mod(c_all, ada_w, ada_b)
    tile_row = np.concatenate([np.repeat(np.arange(b_sz), lt), np.full(nctx, b_sz)])
    xa = jnp.concatenate([x.reshape(b_sz * l_lat, d), ctx.reshape(b_sz * l_ctx, d)], 0)

    for i in range(DEPTH):
        last = i == DEPTH - 1
        p = _layer_params(i, arrs)
        mt = jnp.pad(mods[i][tile_row].reshape(nall, 6, d), ((0, 0), (0, 2), (0, 0)))
        nt = nlat if last else nall
        q, k, v, zp, zr = _inproj(xa, mt, rope, rope_idx, chunk_idx, ncc + nlc, b_sz, p, nall)
        ctx_blk0 = nlat * TM // l_ctx
        att = _attention(q, k, v, nlat, 0, [(l_lat, lambda t: t // lt), (l_ctx, lambda t: ctx_blk0 + t // lt)])
        if not last:
            att = jnp.concatenate([att, _attention(q, k, v, nctx, nlat, [(l_ctx, lambda t: ctx_blk0 + t // ct)])], 0)
        py = _pool(zp, consts["band"], consts["cnt"], cnt_idx, p["pool_w_bd"], p["pool_scale"], nt, nlat, lt, ct)
        y, bv, g = _rwkv(zr, consts, p, ncc, nlc)
        x1, hp = _outproj(att, py, y, bv, g, xa, mt, chunk_idx, consts, p, nt)
        xa = _moe(hp, x1, mt, consts, p, nt)
    return xa[:b_sz * l_lat].reshape(b_sz, l_lat, d)
```

```python
import functools
import math

import numpy as np
import jax
import jax.numpy as jnp
from jax import lax
from jax.experimental import pallas as pl
from jax.experimental.pallas import tpu as pltpu
from jax.experimental.pallas import tpu_sc as plsc

f32 = jnp.float32
bf16 = jnp.bfloat16
HIGHEST = lax.Precision.HIGHEST

DEPTH = 2
GRID_W = 64
NORM_EPS = 1e-6
MLA_HEADS = 8
Q_LORA = 384
KV_LORA = 256
QK_NOPE = 64
QK_ROPE = 32
V_HEAD = 64
ROPE_THETA = 10000.0
MLA_SCALE = (QK_NOPE + QK_ROPE) ** -0.5
MLA_IN = Q_LORA + KV_LORA + QK_ROPE
HEAD_PAD = 128
POOL_WINDOWS = (2, 4, 8, 16)
POOL_WIDTH = 256
POOL_HALO = 8
RWKV_HEADS = 4
RWKV_HEAD = 64
RWKV_WIDTH = 256
RWKV_IN = 960
RWKV_PAD = 1024
RWKV_GN_EPS = 64e-5
CHUNK = 64
RWKV_BATCHES_PER_STEP = 4
N_EXPERTS = 64
TOP_K = 6
N_GROUPS = 8
TOPK_GROUPS = 4
ROUTED_SCALE = 2.5
EXPERT_FF = 256
TM = 256
ATTN_TQ = 512
MOE_BM = 512
SC_CORES, SC_SUBCORES = 2, 16
SC_WINDOW = 128
Z_COLS = 2048
VMEM_LIMIT = 48 * 1024 * 1024
LOG2E = 1.4426950408889634
EXP_M05 = 0.6065306597126334


def _cparams(sem, vmem=VMEM_LIMIT):
    return pltpu.CompilerParams(dimension_semantics=sem, vmem_limit_bytes=vmem)


def _mm(a, b):
    return jnp.dot(a.astype(bf16), b.astype(bf16), preferred_element_type=f32)


def _mm_nt(a, b):
    return lax.dot_general(a.astype(bf16), b.astype(bf16), (((1,), (1,)), ((), ())), preferred_element_type=f32)


def _mm_tn(a, b):
    return lax.dot_general(a.astype(bf16), b.astype(bf16), (((0,), (0,)), ((), ())), preferred_element_type=f32)


def _mmf(a, b):
    return jnp.dot(a, b, precision=HIGHEST, preferred_element_type=f32)


def _split_hi_lo(a):
    hi = a.astype(bf16)
    return hi, (a - hi.astype(f32)).astype(bf16)


def _mm_x01(a, w01):
    hi, lo = _split_hi_lo(a)
    w = w01.astype(bf16)
    return jnp.dot(hi, w, preferred_element_type=f32) + jnp.dot(lo, w, preferred_element_type=f32)


def _mm_01x(w01, a):
    hi, lo = _split_hi_lo(a)
    w = w01.astype(bf16)
    return jnp.dot(w, hi, preferred_element_type=f32) + jnp.dot(w, lo, preferred_element_type=f32)


def _rms(x):
    return x * lax.rsqrt(jnp.mean(x * x, axis=-1, keepdims=True) + NORM_EPS)


def _sigmoid(x):
    return 1.0 / (1.0 + jnp.exp(-x))


def _ada_kernel(c_ref, w_ref, b_ref, o_ref):
    c = c_ref[...]
    s = c * _sigmoid(c)
    o_ref[0] = _mm(s, w_ref[0]) + b_ref[0]


def _ada_mod(c_all, ada_w, ada_b):
    depth, d, n = ada_w.shape
    tn = 1024
    return pl.pallas_call(
        _ada_kernel,
        out_shape=jax.ShapeDtypeStruct((depth, 16, n), f32),
        grid=(depth, n // tn),
        in_specs=[pl.BlockSpec((16, d), lambda i, j: (0, 0)),
                  pl.BlockSpec((1, d, tn), lambda i, j: (i, 0, j)),
                  pl.BlockSpec((1, 1, tn), lambda i, j: (i, 0, j))],
        out_specs=pl.BlockSpec((1, 16, tn), lambda i, j: (i, 0, j)),
        compiler_params=_cparams(("arbitrary", "arbitrary")),
        name="ada_mod",
    )(c_all, ada_w, ada_b.reshape(depth, 1, n))


def _part_specs(parts, cols):
    specs, start = [], 0
    for a in parts:
        n = a.shape[0] // TM
        specs.append(pl.BlockSpec((TM, cols), lambda i, s=start, n=n: (jnp.clip(i - s, 0, n - 1), 0)))
        start += n
    return specs


def _part_tile(refs, part_tiles):
    i = pl.program_id(0)
    x, start = refs[0][...], part_tiles[0]
    for ref, n in zip(refs[1:], part_tiles[1:]):
        x = jnp.where(i >= start, ref[...], x)
        start += n
    return x


def _inproj_kernel(*refs, x_tiles):
    nx = len(x_tiles)
    (m_ref, rope_ref, g_ref, win_ref, qg_ref, wq_ref, wqs_ref, kvg_ref, wk_ref, wv_ref, e_ref, es_ref, q_ref, k_ref,
     v_ref, zp_ref, zr_ref) = refs[nx:]
    m = m_ref[0]
    h = _rms(_part_tile(refs[:nx], x_tiles)) * g_ref[...]
    h = h * (1.0 + m[1:2]) + m[0:1]
    z = _mm(h, win_ref[...])
    zp_ref[...] = z[:, 768:1024]
    for c4 in range(TM // CHUNK):
        zr_ref[c4, 0] = z[c4 * CHUNK:(c4 + 1) * CHUNK, 1024:2048]
    tile8 = lambda t: jnp.concatenate([t] * MLA_HEADS, axis=1)
    qn = (_rms(z[:, 0:Q_LORA]) * qg_ref[...]).astype(bf16)
    q = _mm(qn, wq_ref[...]) * tile8(rope_ref[0]) + _mm(qn, wqs_ref[...]) * tile8(rope_ref[1])
    q_ref[...] = q.astype(bf16)
    kvn = (_rms(z[:, Q_LORA:Q_LORA + KV_LORA]) * kvg_ref[...]).astype(bf16)
    kpe = z[:, 640:768].astype(bf16)
    k = _mm(kvn, wk_ref[...]) + _mm(kpe, e_ref[...]) * tile8(rope_ref[2]) + _mm(kpe, es_ref[...]) * tile8(rope_ref[3])
    k_ref[...] = k.astype(bf16)
    lane = lax.broadcasted_iota(jnp.int32, (1, MLA_HEADS * HEAD_PAD), 1)
    v_ref[...] = (_mm(kvn, wv_ref[...]) + jnp.where(lane % HEAD_PAD == V_HEAD, 1.0, 0.0)).astype(bf16)


def _inproj(x_parts, mt, rope, rope_idx, chunk_idx, npos, b_sz, p, nt):
    t_all, d = nt * TM, x_parts[0].shape[1]
    cpt = TM // CHUNK
    full = lambda a: pl.BlockSpec(a.shape, lambda i: (0,) * a.ndim)
    ws = [p["pre_mix_g"], p["w_in_p"], p["mla_q_norm"], p["wq_p"], p["wq_s"], p["mla_kv_norm"], p["wk_p"], p["wv_p"],
          p["e_p"], p["e_s"]]
    return pl.pallas_call(
        functools.partial(_inproj_kernel, x_tiles=tuple(a.shape[0] // TM for a in x_parts)),
        out_shape=(jax.ShapeDtypeStruct((t_all, MLA_HEADS * HEAD_PAD), bf16),
                   jax.ShapeDtypeStruct((t_all, MLA_HEADS * HEAD_PAD), bf16),
                   jax.ShapeDtypeStruct((t_all, MLA_HEADS * HEAD_PAD), bf16),
                   jax.ShapeDtypeStruct((t_all, POOL_WIDTH), f32),
                   jax.ShapeDtypeStruct((npos, b_sz, CHUNK, RWKV_PAD), f32)),
        grid=(nt,),
        in_specs=_part_specs(x_parts, d)
        + [pl.BlockSpec((1, 8, d), lambda i: (i, 0, 0)),
           pl.BlockSpec((4, TM, HEAD_PAD), lambda i: (0, rope_idx(i), 0))] + [full(w) for w in ws],
        out_specs=(pl.BlockSpec((TM, 1024), lambda i: (i, 0)),
                   pl.BlockSpec((TM, 1024), lambda i: (i, 0)),
                   pl.BlockSpec((TM, 1024), lambda i: (i, 0)),
                   pl.BlockSpec((TM, POOL_WIDTH), lambda i: (i, 0)),
                   pl.BlockSpec((cpt, 1, CHUNK, RWKV_PAD), lambda i: chunk_idx(i) + (0, 0))),
        compiler_params=_cparams(("arbitrary",)),
        name="in_proj",
    )(*x_parts, mt, rope, *ws)


def _attn_kernel(*refs):
    q_ref, kv, o_ref = refs[0], refs[1:-1], refs[-1]
    nt = (((1,), (1,)), ((), ()))
    outs = []
    for h in range(MLA_HEADS):
        hs = slice(h * HEAD_PAD, (h + 1) * HEAD_PAD)
        qh = q_ref[:, hs]
        scores = [lax.dot_general(qh, k_ref[:, hs], nt, preferred_element_type=f32) for k_ref in kv[0::2]]
        m = functools.reduce(jnp.maximum, [jnp.max(s, axis=-1, keepdims=True) for s in scores])
        acc = sum(jnp.dot(jnp.exp2((s - m).astype(bf16)), v_ref[:, hs], preferred_element_type=f32)
                  for s, v_ref in zip(scores, kv[1::2]))
        outs.append(acc[:, :V_HEAD] / acc[:, V_HEAD:V_HEAD + 1])
    for pr2 in range(MLA_HEADS // 2):
        o_ref[:, pr2 * 128:(pr2 + 1) * 128] = jnp.concatenate(outs[2 * pr2:2 * pr2 + 2], axis=1).astype(bf16)


def _attention(q, k, v, n_q_tiles, q_tile0, tq, segs):
    in_specs = [pl.BlockSpec((tq, 1024), lambda i: (i + q_tile0, 0))]
    args = [q]
    for rows, bidx in segs:
        in_specs.append(pl.BlockSpec((rows, 1024), lambda i, bidx=bidx: (bidx(i), 0), pipeline_mode=pl.Buffered(1)))
        in_specs.append(pl.BlockSpec((rows, 1024), lambda i, bidx=bidx: (bidx(i), 0), pipeline_mode=pl.Buffered(1)))
        args += [k, v]
    return pl.pallas_call(
        _attn_kernel,
        out_shape=jax.ShapeDtypeStruct((n_q_tiles * tq, MLA_HEADS * V_HEAD), bf16),
        grid=(n_q_tiles,),
        in_specs=in_specs,
        out_specs=pl.BlockSpec((tq, 512), lambda i: (i, 0)),
        compiler_params=_cparams(("arbitrary",)),
        name="mla_attention",
    )(*args)


def _pool_kernel(z_ref, zp_ref, zn_ref, band_ref, cnt_ref, pw_ref, ps_ref, o_ref, *, nlat, lt, ct):
    i = pl.program_id(0)
    is_lat = i < nlat
    j = jnp.where(is_lat, i % lt, (i - nlat) % ct)
    n = jnp.where(is_lat, lt, ct)
    z = z_ref[...]
    prev = zp_ref[...] * jnp.where(j == 0, 0.0, 1.0)
    nxt = zn_ref[...] * jnp.where(j == n - 1, 0.0, 1.0)
    zh = jnp.concatenate([prev, z, nxt], axis=0)
    lane_grp = lax.broadcasted_iota(jnp.int32, (1, POOL_WIDTH), 1) // 64
    tot = jnp.zeros_like(z)
    for g in range(len(POOL_WINDOWS)):
        tot = tot + _mm_01x(band_ref[g], zh * jnp.where(lane_grp == g, 1.0, 0.0))
    diff = tot / cnt_ref[...] - z
    o_ref[...] = (_mm(diff, pw_ref[...]) * ps_ref[...]).astype(bf16)


def _pool(zp, band, cnt, cnt_idx, pw_bd, pscale, nt, nlat, lt, ct):
    t_all = zp.shape[0]
    nb8 = t_all // POOL_HALO
    r = TM // POOL_HALO
    return pl.pallas_call(
        functools.partial(_pool_kernel, nlat=nlat, lt=lt, ct=ct),
        out_shape=jax.ShapeDtypeStruct((nt * TM, POOL_WIDTH), bf16),
        grid=(nt,),
        in_specs=[pl.BlockSpec((TM, POOL_WIDTH), lambda i: (i, 0)),
                  pl.BlockSpec((POOL_HALO, POOL_WIDTH), lambda i: (jnp.maximum(i * r - 1, 0), 0)),
                  pl.BlockSpec((POOL_HALO, POOL_WIDTH), lambda i: (jnp.minimum((i + 1) * r, nb8 - 1), 0)),
                  pl.BlockSpec(band.shape, lambda i: (0, 0, 0)),
                  pl.BlockSpec((TM, POOL_WIDTH), lambda i: (cnt_idx(i), 0)),
                  pl.BlockSpec((POOL_WIDTH, POOL_WIDTH), lambda i: (0, 0)),
                  pl.BlockSpec((1, POOL_WIDTH), lambda i: (0, 0))],
        out_specs=pl.BlockSpec((TM, POOL_WIDTH), lambda i: (i, 0)),
        compiler_params=_cparams(("arbitrary",)),
        name="pool_mixer",
    )(zp, zp, zp, band, cnt, pw_bd, pscale)


def _rwkv_pos(d, s, *, ncc, nlc):
    in_ctx = s < ncc
    jc = jnp.where(d == 0, s, ncc - 1 - s)
    jl = jnp.where(d == 0, s - ncc, nlc - 1 - (s - ncc))
    pos = jnp.where(in_ctx, nlc + jc, jl)
    first = jnp.where(in_ctx, jc == 0, jl == 0)
    last = jnp.where(in_ctx, jc == ncc - 1, jl == nlc - 1)
    return pos, first, last


def _each(f, *lists):
    return [f(*xs) for xs in zip(*lists)]


def _mmb(a, b):
    return jnp.dot(a, b, preferred_element_type=f32)


def _tri_inverse(lmb, tri_ref, eye_b):
    cast = lambda xs: _each(lambda x: x.astype(bf16), xs)
    n = _each(lambda l: l * tri_ref[0, 2], lmb)
    n2 = cast(_each(_mmb, n, n))
    n4 = cast(_each(_mmb, n2, n2))
    t = _each(lambda a, b: _mmb((eye_b + a), (eye_b + b)), n, n2)
    t = _each(lambda a, b: _mmb(a.astype(bf16), eye_b + b), t, n4)
    for lvl in range(3):
        tb = cast(t)
        x = cast(_each(lambda a, l: _mmb(a, l * tri_ref[0, 3 + lvl]), tb, lmb))
        t = _each(lambda a, xx, ab: a + _mmb(xx, ab), t, x, tb)
    return t


def _rwkv_kernel(z_ref, zp_ref, zn_ref, bd_ref, eye_ref, tri_ref, csi_ref, mu_ref, w0_ref, w2_ref, a0_ref, a2_ref,
                 g2_ref, kk_ref, ka_ref, rk_ref, y_ref, bv_ref, g_ref, s_ref, *, ncc, nlc):
    d, s = pl.program_id(0), pl.program_id(2)
    _, first, last = _rwkv_pos(d, s, ncc=ncc, nlc=nlc)

    @pl.when(s == 0)
    def _():
        s_ref[...] = jnp.zeros_like(s_ref)

    c = CHUNK
    nbs = list(range(z_ref.shape[1]))
    keep_prev, keep_next = jnp.where(first, 0.0, 1.0), jnp.where(last, 0.0, 1.0)
    row = lax.broadcasted_iota(jnp.int32, (c, 1), 0)
    bd = bd_ref[...]
    bd_b = bd.astype(bf16)
    eye = eye_ref[...]
    eye_b = eye.astype(bf16)
    cast = lambda xs: _each(lambda x: x.astype(bf16), xs)

    def shifted(nb):
        z = z_ref[0, nb]
        zp = jnp.where(row == 0, zp_ref[0, nb, 7:8, :] * keep_prev, pltpu.roll(z, 1, 0))
        zn = jnp.where(row == c - 1, zn_ref[0, nb, 0:1, :] * keep_next, pltpu.roll(z, c - 1, 0))
        return z + mu_ref[...] * (0.5 * (zp + zn) - z)

    zs = _each(shifted, nbs)
    r, k, v = (_each(lambda z, o=o: z[:, o:o + 256], zs) for o in (0, 256, 512))
    lora = _each(lambda z: z[:, 768:896], zs)
    g = _each(lambda z: _mm(_sigmoid(z[:, 896:1024]), g2_ref[...]), zs)
    e = _each(lambda x: EXP_M05 * _sigmoid(w0_ref[0] + _mm(jnp.tanh(x), w2_ref[0])), lora)
    a = _each(lambda x: _sigmoid(a0_ref[0] + _mm(x, a2_ref[0])), lora)
    kd = _each(lambda kx, ax: kx * (1.0 + (ax - 1.0) * ka_ref[...]), k, a)
    kkr = _each(lambda kx: kx * kk_ref[...], k)
    kk = _each(lambda x: x / jnp.maximum(jnp.sqrt(_mm_x01(x * x, bd)), 1e-12), kkr)
    bv = _each(lambda rx, kx, vx: _mm_x01(rx * kx * rk_ref[...], bd) * vx, r, kd, v)
    bb = _each(lambda x, ax: x * ax, kk, a)

    cs = _each(lambda x: _mm_01x(csi_ref[0], x), e)
    tot = _each(lambda x: jnp.where(d == 0, x[c - 1:c, :], x[0:1, :]), cs)
    rep4 = lambda t: jnp.concatenate([t] * RWKV_HEADS, axis=0)
    fold4 = lambda t: t[0:c] + t[c:2 * c] + t[2 * c:3 * c] + t[3 * c:4 * c]
    head_rows = lambda t: rep4(t.astype(bf16)) * bd_b
    a4 = _each(lambda ex, cx, kx: head_rows(jnp.exp(ex - cx) * kx), e, cs, kk)
    r_s = _each(lambda rx, cx: rx * jnp.exp(-cx), r, cs)
    r4 = _each(head_rows, r_s)
    v4 = _each(head_rows, v)
    grow = _each(jnp.exp, cs)
    b4 = _each(lambda x, gx: rep4((x * gx).astype(bf16)), bb, grow)
    k4 = _each(lambda x, gx: rep4((x * gx).astype(bf16)), kd, grow)
    to_end = _each(lambda cx, tx: jnp.exp(cx - tx), cs, tot)
    b_e = _each(lambda x, gx: (x * gx).astype(bf16), bb, to_end)
    k_e = _each(lambda x, gx: (x * gx).astype(bf16), kd, to_end)
    g_end = _each(lambda tx: jnp.exp(-tx), tot)

    ntb = lambda x, y: lax.dot_general(x, y, (((1,), (1,)), ((), ())), preferred_element_type=f32).astype(bf16)
    tnb = lambda x, y: lax.dot_general(x, y, (((0,), (0,)), ((), ())), preferred_element_type=f32)
    lmb = _each(lambda x, y: ntb(x, y) * tri_ref[0, 0], a4, b4)
    akm = _each(lambda x, y: ntb(x, y) * tri_ref[0, 0], a4, k4)
    rbn = _each(lambda x, y: ntb(x, y) * tri_ref[0, 1], r4, b4)
    rkm = _each(lambda x, y: ntb(x, y) * tri_ref[0, 6], r4, k4)
    t = cast(_tri_inverse(lmb, tri_ref, eye_b))
    w4 = cast(_each(_mmb, t, a4))
    u4 = cast(_each(lambda tx, ax, vx: _mmb(tx, _mmb(ax, vx).astype(bf16)), t, akm, v4))
    q_all = _each(lambda rx, bx, wx: rx + fold4(_mmb(bx, wx)), r_s, rbn, w4)
    y0 = _each(lambda kx, vx, bx, ux: fold4(_mmb(kx, vx) + _mmb(bx, ux)), rkm, v4, rbn, u4)
    w_all, u_all = _each(fold4, w4), _each(fold4, u4)
    g_bd = _each(lambda gx, wx, bx: eye * gx - bd * tnb(wx, bx), g_end, w_all, b_e)
    h_bd = _each(lambda vx, kx, ux, bx: bd * (tnb(vx.astype(bf16), kx) - tnb(ux, bx)), v, k_e, u_all, b_e)
    for nb in nbs:
        st = s_ref[nb]
        y_ref[0, 0, nb] = _mm_nt(q_all[nb], st) + y0[nb]
        bv_ref[0, 0, nb] = bv[nb]
        g_ref[0, 0, nb] = g[nb]
        s_ref[nb] = _mm(st, g_bd[nb]) + h_bd[nb]


def _rwkv(zr, consts, p, ncc, nlc):
    npos, b_sz = zr.shape[0], zr.shape[1]
    nbat = math.gcd(b_sz, RWKV_BATCHES_PER_STEP)
    kw = dict(ncc=ncc, nlc=nlc)
    pos = lambda d, s: _rwkv_pos(d, s, **kw)[0]
    full = lambda a: pl.BlockSpec(a.shape, lambda d, b, s: (0,) * a.ndim)
    by_dir = lambda a: pl.BlockSpec((1,) + a.shape[1:], lambda d, b, s: (d,) + (0,) * (a.ndim - 1))
    out = jax.ShapeDtypeStruct((2, npos, b_sz, CHUNK, RWKV_WIDTH), f32)
    ospec = pl.BlockSpec((1, 1, nbat, CHUNK, RWKV_WIDTH), lambda d, b, s: (d, pos(d, s), b, 0, 0))
    last8 = CHUNK // 8 - 1
    return pl.pallas_call(
        functools.partial(_rwkv_kernel, **kw),
        out_shape=(out, out, out),
        grid=(2, b_sz // nbat, npos),
        in_specs=[pl.BlockSpec((1, nbat, CHUNK, RWKV_PAD), lambda d, b, s: (pos(d, s), b, 0, 0)),
                  pl.BlockSpec((1, nbat, 8, RWKV_PAD), lambda d, b, s: (jnp.maximum(pos(d, s) - 1, 0), b, last8, 0)),
                  pl.BlockSpec((1, nbat, 8, RWKV_PAD), lambda d, b, s: (jnp.minimum(pos(d, s) + 1, npos - 1), b, 0, 0)),
                  full(consts["bd"]), full(consts["eye"]), by_dir(consts["tri"]), by_dir(consts["csi"]),
                  full(p["mu_p"]), by_dir(p["w0"]), by_dir(p["w2_p"]), by_dir(p["a0"]), by_dir(p["a2_p"]),
                  full(p["g2_p"]), full(p["k_k"]), full(p["k_a"]), full(p["r_k"])],
        out_specs=(ospec, ospec, ospec),
        scratch_shapes=[pltpu.VMEM((nbat, RWKV_WIDTH, RWKV_WIDTH), f32)],
        compiler_params=_cparams(("arbitrary", "arbitrary", "arbitrary")),
        name="rwkv7_chunked",
    )(zr, zr, zr, consts["bd"], consts["eye"], consts["tri"], consts["csi"], p["mu_p"], p["w0"], p["w2_p"], p["a0"],
      p["a2_p"], p["g2_p"], p["k_k"], p["k_a"], p["r_k"])


def _pack_bf16_pairs(lo, hi):
    lo_b = pltpu.bitcast(lo.astype(bf16).astype(f32), jnp.uint32)
    hi_b = pltpu.bitcast(hi.astype(bf16).astype(f32), jnp.uint32)
    return (hi_b & jnp.uint32(0xFFFF0000)) | (lo_b >> 16)


def _unpack_bf16_pairs(w):
    lo = pltpu.bitcast(w << 16, f32).astype(bf16)
    hi = pltpu.bitcast(w & jnp.uint32(0xFFFF0000), f32).astype(bf16)
    return lo, hi


def _outproj_kernel(*refs, a_tiles, x_tiles):
    na, nx = len(a_tiles), len(x_tiles)
    a_refs, (py_ref, y_ref, bv_ref, g_ref) = refs[:na], refs[na:na + 4]
    x_refs = refs[na + 4:na + 4 + nx]
    m_ref, avg_ref, lnw_ref, lnb_ref, wo_ref, pmg_ref, pfg_ref, x1_ref, hp_ref = refs[na + 4 + nx:]
    m = m_ref[0]
    rows = lambda ref, dd: jnp.concatenate([ref[dd, c4, 0] for c4 in range(TM // CHUNK)], axis=0)
    ysum = rows(y_ref, 0) + rows(y_ref, 1)
    avg = avg_ref[...]
    dev = ysum - _mm_x01(ysum, avg)
    var = _mm_x01(dev * dev, avg)
    yn = dev * lax.rsqrt(var + RWKV_GN_EPS) * lnw_ref[...] + lnb_ref[...]
    rw = (yn + rows(bv_ref, 0) + rows(bv_ref, 1)) * rows(g_ref, 0)
    o = (jnp.dot(_part_tile(a_refs, a_tiles), wo_ref[0:512, :], preferred_element_type=f32)
         + jnp.dot(py_ref[...], wo_ref[512:768, :], preferred_element_type=f32)
         + _mm(rw, wo_ref[768:1024, :]))
    x1 = _part_tile(x_refs, x_tiles) + m[2:3] * (_rms(o) * pmg_ref[...])
    x1_ref[...] = x1
    h = (_rms(x1) * pfg_ref[...]) * (1.0 + m[4:5]) + m[3:4]
    hp_ref[...] = _pack_bf16_pairs(h[:, 0:512], h[:, 512:1024])


def _outproj(att_parts, py, y, bv, g, x_parts, mt, chunk_idx, consts, p, nt):
    d = x_parts[0].shape[1]
    tiles = lambda parts: tuple(a.shape[0] // TM for a in parts)
    cpt = TM // CHUNK
    cspec = lambda nd: pl.BlockSpec((nd, cpt, 1, CHUNK, RWKV_WIDTH), lambda i: (0,) + chunk_idx(i) + (0, 0))
    full = lambda a: pl.BlockSpec(a.shape, lambda i: (0,) * a.ndim)
    ws = [consts["avg"], p["ln_w"], p["ln_b"], p["w_out"], p["post_mix_g"], p["pre_ffn_g"]]
    return pl.pallas_call(
        functools.partial(_outproj_kernel, a_tiles=tiles(att_parts), x_tiles=tiles(x_parts)),
        out_shape=(jax.ShapeDtypeStruct((nt * TM, d), f32), jax.ShapeDtypeStruct((nt * TM, d // 2), jnp.uint32)),
        grid=(nt,),
        in_specs=_part_specs(att_parts, 512)
        + [pl.BlockSpec((TM, POOL_WIDTH), lambda i: (i, 0)), cspec(2), cspec(2), cspec(1)]
        + _part_specs(x_parts, d)
        + [pl.BlockSpec((1, 8, d), lambda i: (i, 0, 0))] + [full(w) for w in ws],
        out_specs=(pl.BlockSpec((TM, d), lambda i: (i, 0)), pl.BlockSpec((TM, d // 2), lambda i: (i, 0))),
        compiler_params=_cparams(("arbitrary",)),
        name="out_proj",
    )(*att_parts, py, y, bv, g, *x_parts, mt, *ws)


def _router_kernel(hp_ref, rw_ref, rb_ref, ut_ref, lt_ref, ei_ref, pos_ref, gt_ref, cnt_ref, run_ref):
    i = pl.program_id(0)

    @pl.when(i == 0)
    def _():
        run_ref[...] = jnp.zeros_like(run_ref)

    tm = hp_ref.shape[0]
    ne, ng = N_EXPERTS, N_GROUPS
    pg = ne // ng
    lo, hi = _unpack_bf16_pairs(hp_ref[...])
    logits = (lax.dot_general(rw_ref[:, 0:512], lo, (((1,), (1,)), ((), ())), preferred_element_type=f32)
              + lax.dot_general(rw_ref[:, 512:1024], hi, (((1,), (1,)), ((), ())), preferred_element_type=f32))
    scores = _sigmoid(logits)
    sel = scores + rb_ref[...]
    neg = -jnp.inf

    s3 = sel.reshape(ng, pg, tm)
    io = lax.broadcasted_iota(jnp.int32, (ng, pg, tm), 1)
    m1 = jnp.max(s3, axis=1, keepdims=True)
    i1 = jnp.min(jnp.where(s3 == m1, io, pg), axis=1, keepdims=True)
    m2 = jnp.max(jnp.where(io == i1, neg, s3), axis=1, keepdims=True)
    gs = (m1 + m2).reshape(ng, tm)
    gi = lax.broadcasted_iota(jnp.int32, (ng, tm), 0)
    grank = jnp.zeros((ng, tm), f32)
    for j in range(ng):
        rj = gs[j:j + 1, :]
        grank = grank + jnp.where((rj > gs) | ((rj == gs) & (j < gi)), 1.0, 0.0)
    gsel = jnp.where(grank < TOPK_GROUPS, 1.0, 0.0)
    gsel3 = jnp.broadcast_to(gsel.reshape(ng, 1, tm), (ng, pg, tm)).reshape(ne, tm)
    msk = jnp.where(gsel3 > 0.5, sel, neg)
    ei = lax.broadcasted_iota(jnp.int32, (ne, tm), 0)
    erank = jnp.zeros((ne, tm), f32)
    for j in range(ne):
        rj = msk[j:j + 1, :]
        erank = erank + jnp.where((rj > msk) | ((rj == msk) & (j < ei)), 1.0, 0.0)
    chosen = erank < TOP_K
    chf = jnp.where(chosen, 1.0, 0.0)
    graw = jnp.where(chosen, scores, 0.0)
    gate = graw / jnp.sum(graw, axis=0, keepdims=True) * ROUTED_SCALE

    pos = run_ref[...] + _mm(chf, ut_ref[...])
    tot = jnp.sum(chf, axis=1, keepdims=True)
    run_new = run_ref[...] + tot
    run_ref[...] = run_new
    cnt_ref[...] = run_new[:, 0:128]
    rk = _mm(lt_ref[...], chf)
    eif = ei.astype(f32)
    rows_e, rows_p, rows_g = [], [], []
    for kq in range(TOP_K):
        mk = chosen & (rk == float(kq))
        rows_e.append(jnp.sum(jnp.where(mk, eif, 0.0), axis=0, keepdims=True))
        rows_p.append(jnp.sum(jnp.where(mk, pos, 0.0), axis=0, keepdims=True))
        rows_g.append(jnp.sum(jnp.where(mk, gate, 0.0), axis=0, keepdims=True))
    zrow = jnp.zeros((8 - TOP_K, tm), f32)
    ei_ref[0] = jnp.concatenate(rows_e + [zrow], axis=0).astype(jnp.int32)
    pos_ref[0] = jnp.concatenate(rows_p + [zrow], axis=0).astype(jnp.int32)
    gpad = jnp.concatenate(rows_g + [jnp.zeros((128 - TOP_K, tm), f32)], axis=0)
    gt_ref[...] = gpad.T


def _router(hp, rwt, rb, consts, nt):
    full = lambda a: pl.BlockSpec(a.shape, lambda i: (0,) * a.ndim)
    return pl.pallas_call(
        _router_kernel,
        out_shape=(jax.ShapeDtypeStruct((nt, 8, TM), jnp.int32), jax.ShapeDtypeStruct((nt, 8, TM), jnp.int32),
                   jax.ShapeDtypeStruct((nt * TM, 128), f32), jax.ShapeDtypeStruct((N_EXPERTS, 128), f32)),
        grid=(nt,),
        in_specs=[pl.BlockSpec((TM, 512), lambda i: (i, 0)), full(rwt), full(rb), full(consts["ut"]),
                  full(consts["lt"])],
        out_specs=(pl.BlockSpec((1, 8, TM), lambda i: (i, 0, 0)), pl.BlockSpec((1, 8, TM), lambda i: (i, 0, 0)),
                   pl.BlockSpec((TM, 128), lambda i: (i, 0)), pl.BlockSpec((N_EXPERTS, 128), lambda i: (0, 0))),
        scratch_shapes=[pltpu.VMEM((N_EXPERTS, TM), f32)],
        compiler_params=_cparams(("arbitrary",)),
        name="moe_router",
    )(hp, rwt, rb, consts["ut"], consts["lt"])


def _slots_kernel(ps_ref, ei_ref, pos_ref, d_ref):
    ei = ei_ref[0]
    slot = pos_ref[0]
    for e in range(N_EXPERTS):
        slot = slot + jnp.where(ei == e, ps_ref[e], 0)
    d_ref[...] = slot


def _slots(pstart, ei, pos, nt):
    return pl.pallas_call(
        _slots_kernel,
        out_shape=jax.ShapeDtypeStruct((8, nt * TM), jnp.int32),
        grid_spec=pltpu.PrefetchScalarGridSpec(
            num_scalar_prefetch=1,
            grid=(nt,),
            in_specs=[pl.BlockSpec((1, 8, TM), lambda i, ps: (i, 0, 0)), pl.BlockSpec((1, 8, TM), lambda i, ps: (i, 0, 0))],
            out_specs=pl.BlockSpec((8, TM), lambda i, ps: (0, i))),
        compiler_params=_cparams(("arbitrary",)),
        name="moe_slots",
    )(pstart, ei, pos)


def _sc_dispatch(hp, dest, n_slots):
    t, w = hp.shape
    wh = w // 2
    mesh = plsc.VectorSubcoreMesh(core_axis_name="core", subcore_axis_name="subcore", num_cores=SC_CORES,
                                  num_subcores=SC_SUBCORES)
    idx = [dest[kq:kq + 1] for kq in range(TOP_K)]
    half = jax.ShapeDtypeStruct((n_slots, wh), hp.dtype)

    @pl.kernel(out_type=(half, half), mesh=mesh, scratch_types=[])
    def scatter_rows(hp_hbm, *rest):
        idx_hbm, xs_hbm = rest[:TOP_K], rest[TOP_K:]
        for c in range(2):
            def body(x_vmem, *i_vmem, c=c):
                for iv in i_vmem:
                    pltpu.sync_copy(x_vmem, xs_hbm[c].at[iv.at[0]])

            pltpu.emit_pipeline(
                body,
                grid=(t // SC_WINDOW,),
                in_specs=[pl.BlockSpec((SC_WINDOW, wh), lambda i, c=c: (i, c))]
                + [pl.BlockSpec((1, SC_WINDOW), lambda i: (0, i))] * TOP_K,
                out_specs=[],
                core_axis_name=("core", "subcore"),
                dimension_semantics=(pltpu.PARALLEL,),
            )(hp_hbm, *idx_hbm)

    return scatter_rows(hp, *idx)


def _expert_kernel(be_ref, nb_ref, bv_ref, xa_ref, xb_ref, wg_ref, wu_ref, wd_ref, ya_ref, yb_ref):
    i = pl.program_id(0)

    @pl.when(i < nb_ref[0])
    def _():
        live = lax.broadcasted_iota(jnp.int32, (MOE_BM, 1), 0) < bv_ref[i]
        la, ha = _unpack_bf16_pairs(jnp.where(live, xa_ref[...], jnp.uint32(0)))
        lb, hb = _unpack_bf16_pairs(jnp.where(live, xb_ref[...], jnp.uint32(0)))

        def proj(w_ref):
            return sum(jnp.dot(x, w_ref[0, 0, q * 256:(q + 1) * 256, :].astype(bf16), preferred_element_type=f32)
                       for q, x in enumerate((la, lb, ha, hb)))

        gg, uu = proj(wg_ref), proj(wu_ref)
        act = gg * _sigmoid(gg) * uu
        y = _mm(act, wd_ref[0, 0])
        ya_ref[...] = _pack_bf16_pairs(y[:, 0:256], y[:, 512:768])
        yb_ref[...] = _pack_bf16_pairs(y[:, 256:512], y[:, 768:1024])

    @pl.when(i >= nb_ref[0])
    def _():
        ya_ref[...] = jnp.zeros_like(ya_ref)
        yb_ref[...] = jnp.zeros_like(yb_ref)


def _experts(block_expert, nb_used, block_valid, xs, layer, w_gate, w_up, w_down):
    xa, xb = xs
    n_slots = xa.shape[0]
    half = jax.ShapeDtypeStruct((n_slots, xa.shape[1]), jnp.uint32)
    hspec = pl.BlockSpec((MOE_BM, xa.shape[1]), lambda i, be, nb, bv: (i, 0))
    wspec = lambda w: pl.BlockSpec((1, 1) + w.shape[2:], lambda i, be, nb, bv: (layer, be[i], 0, 0))
    return pl.pallas_call(
        _expert_kernel,
        out_shape=(half, half),
        grid_spec=pltpu.PrefetchScalarGridSpec(
            num_scalar_prefetch=3,
            grid=(n_slots // MOE_BM,),
            in_specs=[pl.BlockSpec((MOE_BM, xa.shape[1]), lambda i, be, nb, bv: (i, 0)),
                      pl.BlockSpec((MOE_BM, xb.shape[1]), lambda i, be, nb, bv: (i, 0)),
                      wspec(w_gate), wspec(w_up), wspec(w_down)],
            out_specs=(hspec, hspec)),
        compiler_params=_cparams(("arbitrary",)),
        name="moe_experts",
    )(block_expert, nb_used, block_valid, xa, xb, w_gate, w_up, w_down)


def _sc_gather(ys, dest):
    t = dest.shape[1]
    wh = ys[0].shape[1]
    mesh = plsc.VectorSubcoreMesh(core_axis_name="core", subcore_axis_name="subcore", num_cores=SC_CORES,
                                  num_subcores=SC_SUBCORES)
    idx = [dest[kq:kq + 1] for kq in range(TOP_K)]
    out = jax.ShapeDtypeStruct((t, wh), ys[0].dtype)

    @pl.kernel(out_type=(out,) * (2 * TOP_K), mesh=mesh, scratch_types=[])
    def gather_rows(ya_hbm, yb_hbm, *rest):
        idx_hbm, out_hbm = rest[:TOP_K], rest[TOP_K:]
        for c, y_hbm in enumerate((ya_hbm, yb_hbm)):
            for kq in range(TOP_K):
                def body(i_vmem, o_vmem, y_hbm=y_hbm):
                    pltpu.sync_copy(y_hbm.at[i_vmem.at[0]], o_vmem)

                pltpu.emit_pipeline(
                    body,
                    grid=(t // SC_WINDOW,),
                    in_specs=[pl.BlockSpec((1, SC_WINDOW), lambda i: (0, i))],
                    out_specs=[pl.BlockSpec((SC_WINDOW, wh), lambda i: (i, 0))],
                    core_axis_name=("core", "subcore"),
                    dimension_semantics=(pltpu.PARALLEL,),
                )(idx_hbm[kq], out_hbm[c * TOP_K + kq])

    outs = gather_rows(ys[0], ys[1], *idx)
    return outs[:TOP_K], outs[TOP_K:]


def _combine_kernel(*refs):
    ga, gb = refs[0:TOP_K], refs[TOP_K:2 * TOP_K]
    hp_ref, gt_ref, x1_ref, m_ref, wsgu_ref, wsd_ref, pg_ref, o_ref = refs[2 * TOP_K:]
    lo, hi = _unpack_bf16_pairs(hp_ref[...])
    gu = (jnp.dot(lo, wsgu_ref[0:512, :], preferred_element_type=f32)
          + jnp.dot(hi, wsgu_ref[512:1024, :], preferred_element_type=f32))
    gg, uu = gu[:, 0:EXPERT_FF], gu[:, EXPERT_FF:2 * EXPERT_FF]
    f = _mm(gg * _sigmoid(gg) * uu, wsd_ref[...])
    gt = gt_ref[...]
    parts = [jnp.zeros((TM, 256), f32) for _ in range(4)]
    for kq in range(TOP_K):
        g = gt[:, kq:kq + 1]
        la, ha = _unpack_bf16_pairs(ga[kq][...])
        lb, hb = _unpack_bf16_pairs(gb[kq][...])
        for q, v in enumerate((la, lb, ha, hb)):
            parts[q] = parts[q] + v.astype(f32) * g
    f = f + jnp.concatenate(parts, axis=1)
    m = m_ref[0]
    o_ref[...] = x1_ref[...] + m[5:6] * (_rms(f) * pg_ref[...])


def _combine(ga, gb, hp, gt, x1, mt, p, nt):
    d = x1.shape[1]
    full = lambda a: pl.BlockSpec(a.shape, lambda i: (0,) * a.ndim)
    ws = [p["sh_wgu"], p["sh_wd"], p["post_ffn_g"]]
    gspec = pl.BlockSpec((TM, ga[0].shape[1]), lambda i: (i, 0))
    return pl.pallas_call(
        _combine_kernel,
        out_shape=jax.ShapeDtypeStruct((nt * TM, d), f32),
        grid=(nt,),
        in_specs=[gspec] * (2 * TOP_K)
        + [pl.BlockSpec((TM, d // 2), lambda i: (i, 0)),
           pl.BlockSpec((TM, 128), lambda i: (i, 0)),
           pl.BlockSpec((TM, d), lambda i: (i, 0)),
           pl.BlockSpec((1, 8, d), lambda i: (i, 0, 0))] + [full(w) for w in ws],
        out_specs=pl.BlockSpec((TM, d), lambda i: (i, 0)),
        compiler_params=_cparams(("arbitrary",)),
        name="moe_combine",
    )(*ga, *gb, hp, gt, x1, mt, *ws)


def _moe(hp, x1, mt, consts, p, nt):
    t = nt * TM
    ei, pos, gt, cnt = _router(hp, p["router_wt"], p["router_b"], consts, nt)
    n_assign = t * TOP_K
    n_blocks = -(-(n_assign + N_EXPERTS * (MOE_BM - 1)) // MOE_BM)
    n_slots = n_blocks * MOE_BM
    counts = cnt[:, 0].astype(jnp.int32)
    padded = (counts + MOE_BM - 1) // MOE_BM * MOE_BM
    pend = jnp.cumsum(padded)
    pstart = pend - padded
    nb_used = (pend[-1:] // MOE_BM).astype(jnp.int32)
    block_row0 = jnp.arange(n_blocks, dtype=jnp.int32) * MOE_BM
    block_expert = jnp.minimum(jnp.sum((pend[None, :] <= block_row0[:, None]).astype(jnp.int32), axis=1),
                               N_EXPERTS - 1)
    run_end = (pstart + counts)[block_expert]
    block_valid = jnp.clip(run_end - block_row0, 0, MOE_BM).astype(jnp.int32)
    dest = _slots(pstart.astype(jnp.int32), ei, pos, nt)
    xs = _sc_dispatch(hp, dest, n_slots)
    ys = _experts(block_expert, nb_used, block_valid, xs, p["layer"], *p["experts"])
    ga, gb = _sc_gather(ys, dest)
    return _combine(ga, gb, hp, gt, x1, mt, p, nt)


def _np_consts(l_lat, l_ctx):
    n = RWKV_WIDTH
    i = np.arange(n)
    bd = (i[:, None] // 64 == i[None, :] // 64).astype(np.float32)
    t_r, t_c = (i % 64)[:, None], (i % 64)[None, :]
    tri = np.zeros((2, 7, n, n), np.float32)
    for d in range(2):
        before = (t_c < t_r) if d == 0 else (t_c > t_r)
        tri[d, 0] = bd * before
        tri[d, 6] = bd * (before | (t_c == t_r))
        tri[d, 1] = -tri[d, 6]
        tri[d, 2] = -(bd * before * (t_r // 8 == t_c // 8))
        for lvl, blk in enumerate((8, 16, 32)):
            tri[d, 3 + lvl] = -(bd * before * (t_r // (2 * blk) == t_c // (2 * blk)) * (t_r // blk != t_c // blk))
    j = np.arange(CHUNK)
    csi = np.stack([(j[None, :] <= j[:, None]), (j[None, :] >= j[:, None])]).astype(np.float32)
    tt = np.arange(TM)
    ut = (tt[:, None] < tt[None, :]).astype(np.float32)
    ee = np.arange(N_EXPERTS)
    lt = (ee[None, :] < ee[:, None]).astype(np.float32)
    jj = np.arange(TM + 2 * POOL_HALO)[None, :]
    band = np.stack([((jj >= tt[:, None] + POOL_HALO - w // 2) & (jj <= tt[:, None] + POOL_HALO + w // 2 - 1))
                     for w in POOL_WINDOWS]).astype(np.float32)

    def counts(length):
        t = np.arange(length)[:, None]
        half = np.repeat(np.array(POOL_WINDOWS) // 2, 64)[None, :]
        return (np.clip(t + half, 0, length) - np.clip(t - half, 0, length)).astype(np.float32)

    cnt = np.concatenate([counts(l_lat), counts(l_ctx)], axis=0)
    return dict(bd=jnp.asarray(bd), eye=jnp.eye(n, dtype=f32), tri=jnp.asarray(tri, dtype=bf16), csi=jnp.asarray(csi),
                avg=jnp.asarray(bd / 64.0), ut=jnp.asarray(ut, dtype=bf16), lt=jnp.asarray(lt, dtype=bf16),
                band=jnp.asarray(band), cnt=jnp.asarray(cnt))


def _rope_tables(l_lat):
    rows = l_lat // GRID_W
    row = jnp.repeat(jnp.arange(rows, dtype=f32), GRID_W)
    col = jnp.tile(jnp.arange(GRID_W, dtype=f32), rows)
    n_freq = QK_ROPE // 4
    inv = ROPE_THETA ** (-jnp.arange(n_freq, dtype=f32) / n_freq)
    ang = jnp.concatenate([row[:, None] * inv, col[:, None] * inv], -1)
    cos = jnp.concatenate([jnp.cos(ang), jnp.ones((TM, 16), f32)], 0)
    sin = jnp.concatenate([jnp.sin(ang), jnp.zeros((TM, 16), f32)], 0)
    n = cos.shape[0]
    ct = jnp.concatenate([jnp.ones((n, QK_NOPE), f32), cos, cos, jnp.zeros((n, 32), f32)], 1)
    st = jnp.concatenate([jnp.zeros((n, QK_NOPE), f32), -sin, sin, jnp.zeros((n, 32), f32)], 1)
    qs = MLA_SCALE * LOG2E
    return jnp.stack([ct * qs, st * qs, ct, st])


def _layer_params(i, a):
    d = a["w_in"].shape[1]
    p = {}
    row = lambda v: v.reshape(1, -1).astype(f32)
    for name in ("pre_mix_g", "post_mix_g", "pre_ffn_g", "post_ffn_g", "mla_q_norm", "mla_kv_norm"):
        p[name] = row(a[name][i])
    w_in = a["w_in"][i]
    zc = lambda n: jnp.zeros((d, n), f32)
    p["w_in_p"] = jnp.concatenate(
        [w_in[:, 0:MLA_IN], zc(128 - QK_ROPE), w_in[:, MLA_IN:MLA_IN + POOL_WIDTH],
         w_in[:, MLA_IN + POOL_WIDTH:], zc(RWKV_PAD - RWKV_IN)], 1).astype(bf16)
    wq = a["mla_w_q_b"][i].reshape(Q_LORA, MLA_HEADS, QK_NOPE + QK_ROPE)
    zq = jnp.zeros((Q_LORA, MLA_HEADS, 32), f32)
    half = QK_ROPE // 2
    p["wq_p"] = jnp.concatenate([wq, zq], 2).reshape(Q_LORA, -1).astype(bf16)
    p["wq_s"] = jnp.concatenate([jnp.zeros_like(wq[:, :, :QK_NOPE]), wq[:, :, QK_NOPE + half:],
                                 wq[:, :, QK_NOPE:QK_NOPE + half], zq], 2).reshape(Q_LORA, -1).astype(bf16)
    wkv = a["mla_w_kv_b"][i].reshape(KV_LORA, MLA_HEADS, QK_NOPE + V_HEAD)
    p["wk_p"] = jnp.concatenate([wkv[:, :, :QK_NOPE], jnp.zeros((KV_LORA, MLA_HEADS, 64), f32)], 2
                                ).reshape(KV_LORA, -1).astype(bf16)
    p["wv_p"] = jnp.concatenate([wkv[:, :, QK_NOPE:], jnp.zeros((KV_LORA, MLA_HEADS, 64), f32)], 2
                                ).reshape(KV_LORA, -1).astype(bf16)
    e_p = np.zeros((128, MLA_HEADS * HEAD_PAD), np.float32)
    e_s = np.zeros_like(e_p)
    for h in range(MLA_HEADS):
        for j in range(QK_ROPE):
            e_p[j, h * HEAD_PAD + QK_NOPE + j] = 1.0
            e_s[(j + half) % QK_ROPE, h * HEAD_PAD + QK_NOPE + j] = 1.0
    p["e_p"], p["e_s"] = jnp.asarray(e_p, dtype=bf16), jnp.asarray(e_s, dtype=bf16)
    pw = a["pool_w"][i]
    p["pool_w_bd"] = jax.scipy.linalg.block_diag(*[pw[g] for g in range(pw.shape[0])]).astype(bf16)
    p["pool_scale"] = row(a["pool_scale"][i])
    p["mu_p"] = jnp.pad(a["rwkv_mu"][i], (0, RWKV_PAD - RWKV_IN)).reshape(1, -1)
    p["w0"] = a["rwkv_w0"][i].reshape(2, 1, RWKV_WIDTH)
    p["a0"] = a["rwkv_a0"][i].reshape(2, 1, RWKV_WIDTH)
    z32 = jnp.zeros((32, RWKV_WIDTH), f32)
    w2, a2 = a["rwkv_w2"][i], a["rwkv_a2"][i]
    p["w2_p"] = jnp.stack([jnp.concatenate([w2[0], z32, z32, z32]), jnp.concatenate([z32, w2[1], z32, z32])])
    p["a2_p"] = jnp.stack([jnp.concatenate([z32, z32, a2[0], z32]), jnp.concatenate([z32, z32, z32, a2[1]])])
    p["g2_p"] = jnp.concatenate([a["rwkv_g2"][i], jnp.zeros((64, RWKV_WIDTH), f32)])
    for name in ("k_k", "k_a", "r_k"):
        p[name] = row(a["rwkv_" + name][i])
    p["ln_w"], p["ln_b"] = row(a["rwkv_ln_w"][i]), row(a["rwkv_ln_b"][i])
    p["w_out"] = a["w_out"][i].astype(bf16)
    p["router_wt"] = a["router_w"][i].T.astype(bf16)
    p["router_b"] = a["router_bias"][i].reshape(-1, 1).astype(f32)
    p["layer"] = i
    p["experts"] = (a["exp_w_gate"], a["exp_w_up"], a["exp_w_down"])
    p["sh_wgu"] = jnp.concatenate([a["sh_w_gate"][i], a["sh_w_up"][i]], 1).astype(bf16)
    p["sh_wd"] = a["sh_w_down"][i].astype(bf16)
    return p


def kernel(x, c, ctx, c_ctx, ada_w, ada_b, pre_mix_g, post_mix_g, pre_ffn_g, post_ffn_g, w_in, w_out, mla_q_norm, mla_w_q_b, mla_kv_norm, mla_w_kv_b, pool_w, pool_scale, rwkv_mu, rwkv_w0, rwkv_w2, rwkv_a0, rwkv_a2, rwkv_g2, rwkv_k_k, rwkv_k_a, rwkv_r_k, rwkv_ln_w, rwkv_ln_b, router_w, router_bias, exp_w_gate, exp_w_up, exp_w_down, sh_w_gate, sh_w_up, sh_w_down):
    arrs = dict(pre_mix_g=pre_mix_g, post_mix_g=post_mix_g, pre_ffn_g=pre_ffn_g, post_ffn_g=post_ffn_g, w_in=w_in,
                w_out=w_out, mla_q_norm=mla_q_norm, mla_w_q_b=mla_w_q_b, mla_kv_norm=mla_kv_norm,
                mla_w_kv_b=mla_w_kv_b, pool_w=pool_w, pool_scale=pool_scale, rwkv_mu=rwkv_mu, rwkv_w0=rwkv_w0,
                rwkv_w2=rwkv_w2, rwkv_a0=rwkv_a0, rwkv_a2=rwkv_a2, rwkv_g2=rwkv_g2, rwkv_k_k=rwkv_k_k,
                rwkv_k_a=rwkv_k_a, rwkv_r_k=rwkv_r_k, rwkv_ln_w=rwkv_ln_w, rwkv_ln_b=rwkv_ln_b, router_w=router_w,
                router_bias=router_bias, exp_w_gate=exp_w_gate, exp_w_up=exp_w_up, exp_w_down=exp_w_down,
                sh_w_gate=sh_w_gate, sh_w_up=sh_w_up, sh_w_down=sh_w_down)
    b_sz, l_lat, d = x.shape
    l_ctx = ctx.shape[1]
    assert l_lat % TM == 0 and l_ctx % TM == 0 and l_lat % GRID_W == 0 and b_sz < 16
    lt, ct = l_lat // TM, l_ctx // TM
    nlat, nctx = b_sz * lt, b_sz * ct
    nall = nlat + nctx
    ncc, nlc = l_ctx // CHUNK, l_lat // CHUNK
    consts = _np_consts(l_lat, l_ctx)
    rope = _rope_tables(l_lat)
    rope_idx = lambda i: jnp.where(i < nlat, i % lt, lt)
    cnt_idx = lambda i: jnp.where(i < nlat, i % lt, lt + (i - nlat) % ct)
    chunk_idx = lambda i: (jnp.where(i < nlat, i % lt, nlc * CHUNK // TM + (i - nlat) % ct),
                           jnp.where(i < nlat, i // lt, (i - nlat) // ct))

    c_all = jnp.zeros((16, d), f32).at[:b_sz].set(c).at[b_sz].set(c_ctx)
    mods = _ada_mod(c_all, ada_w, ada_b)
    tile_row = np.concatenate([np.repeat(np.arange(b_sz), lt), np.full(nctx, b_sz)])
    x_parts = [x.reshape(b_sz * l_lat, d), ctx.reshape(b_sz * l_ctx, d)]

    for i in range(DEPTH):
        last = i == DEPTH - 1
        p = _layer_params(i, arrs)
        mt = jnp.pad(mods[i][tile_row].reshape(nall, 6, d), ((0, 0), (0, 2), (0, 0)))
        nt = nlat if last else nall
        q, k, v, zp, zr = _inproj(x_parts, mt, rope, rope_idx, chunk_idx, ncc + nlc, b_sz, p, nall)
        ctx_blk0 = nlat * TM // l_ctx
        tq = math.gcd(l_lat, ATTN_TQ)
        qt = l_lat // tq
        att = [_attention(q, k, v, b_sz * qt, 0, tq, [(l_lat, lambda t: t // qt), (l_ctx, lambda t: ctx_blk0 + t // qt)])]
        if not last:
            att.append(_attention(q, k, v, nctx, nlat, TM, [(l_ctx, lambda t: ctx_blk0 + t // ct)]))
        py = _pool(zp, consts["band"], consts["cnt"], cnt_idx, p["pool_w_bd"], p["pool_scale"], nt, nlat, lt, ct)
        y, bv, g = _rwkv(zr, consts, p, ncc, nlc)
        x1, hp = _outproj(att, py, y, bv, g, x_parts, mt, chunk_idx, consts, p, nt)
        x_parts = [_moe(hp, x1, mt, consts, p, nt)]
    return x_parts[0][:b_sz * l_lat].reshape(b_sz, l_lat, d)
```

```python
import functools
import math

import numpy as np
import jax
import jax.numpy as jnp
from jax import lax
from jax.experimental import pallas as pl
from jax.experimental.pallas import tpu as pltpu
from jax.experimental.pallas import tpu_sc as plsc

f32 = jnp.float32
bf16 = jnp.bfloat16
HIGHEST = lax.Precision.HIGHEST

DEPTH = 2
GRID_W = 64
NORM_EPS = 1e-6
MLA_HEADS = 8
Q_LORA = 384
KV_LORA = 256
QK_NOPE = 64
QK_ROPE = 32
V_HEAD = 64
ROPE_THETA = 10000.0
MLA_SCALE = (QK_NOPE + QK_ROPE) ** -0.5
MLA_IN = Q_LORA + KV_LORA + QK_ROPE
HEAD_PAD = 128
POOL_WINDOWS = (2, 4, 8, 16)
POOL_WIDTH = 256
POOL_HALO = 8
RWKV_HEADS = 4
RWKV_HEAD = 64
RWKV_WIDTH = 256
RWKV_IN = 960
RWKV_PAD = 1024
RWKV_GN_EPS = 64e-5
CHUNK = 64
RWKV_BATCHES_PER_STEP = 4
N_EXPERTS = 64
TOP_K = 6
N_GROUPS = 8
TOPK_GROUPS = 4
ROUTED_SCALE = 2.5
EXPERT_FF = 256
TM = 256
ATTN_TQ = 512
MOE_BM = 512
SC_CORES, SC_SUBCORES = 2, 16
SC_WINDOW = 128
Z_COLS = 2048
VMEM_LIMIT = 48 * 1024 * 1024
LOG2E = 1.4426950408889634
EXP_M05 = 0.6065306597126334


def _cparams(sem, vmem=VMEM_LIMIT):
    return pltpu.CompilerParams(dimension_semantics=sem, vmem_limit_bytes=vmem)


def _mm(a, b):
    return jnp.dot(a.astype(bf16), b.astype(bf16), preferred_element_type=f32)


def _mm_nt(a, b):
    return lax.dot_general(a.astype(bf16), b.astype(bf16), (((1,), (1,)), ((), ())), preferred_element_type=f32)


def _mm_tn(a, b):
    return lax.dot_general(a.astype(bf16), b.astype(bf16), (((0,), (0,)), ((), ())), preferred_element_type=f32)


def _mmf(a, b):
    return jnp.dot(a, b, precision=HIGHEST, preferred_element_type=f32)


def _split_hi_lo(a):
    hi = a.astype(bf16)
    return hi, (a - hi.astype(f32)).astype(bf16)


def _mm_x01(a, w01):
    hi, lo = _split_hi_lo(a)
    w = w01.astype(bf16)
    return jnp.dot(hi, w, preferred_element_type=f32) + jnp.dot(lo, w, preferred_element_type=f32)


def _mm_01x(w01, a):
    hi, lo = _split_hi_lo(a)
    w = w01.astype(bf16)
    return jnp.dot(w, hi, preferred_element_type=f32) + jnp.dot(w, lo, preferred_element_type=f32)


def _rms(x):
    return x * lax.rsqrt(jnp.mean(x * x, axis=-1, keepdims=True) + NORM_EPS)


def _sigmoid(x):
    return 1.0 / (1.0 + jnp.exp(-x))


def _ada_kernel(c_ref, w_ref, b_ref, o_ref):
    c = c_ref[...]
    s = c * _sigmoid(c)
    o_ref[0] = _mm(s, w_ref[0]) + b_ref[0]


def _ada_mod(c_all, ada_w, ada_b):
    depth, d, n = ada_w.shape
    tn = 1024
    return pl.pallas_call(
        _ada_kernel,
        out_shape=jax.ShapeDtypeStruct((depth, 16, n), f32),
        grid=(depth, n // tn),
        in_specs=[pl.BlockSpec((16, d), lambda i, j: (0, 0)),
                  pl.BlockSpec((1, d, tn), lambda i, j: (i, 0, j)),
                  pl.BlockSpec((1, 1, tn), lambda i, j: (i, 0, j))],
        out_specs=pl.BlockSpec((1, 16, tn), lambda i, j: (i, 0, j)),
        compiler_params=_cparams(("arbitrary", "arbitrary")),
        name="ada_mod",
    )(c_all, ada_w, ada_b.reshape(depth, 1, n))


def _part_specs(parts, cols):
    specs, start = [], 0
    for a in parts:
        n = a.shape[0] // TM
        specs.append(pl.BlockSpec((TM, cols), lambda i, s=start, n=n: (jnp.clip(i - s, 0, n - 1), 0)))
        start += n
    return specs


def _part_tile(refs, part_tiles):
    i = pl.program_id(0)
    x, start = refs[0][...], part_tiles[0]
    for ref, n in zip(refs[1:], part_tiles[1:]):
        x = jnp.where(i >= start, ref[...], x)
        start += n
    return x


def _inproj_kernel(*refs, x_tiles):
    nx = len(x_tiles)
    (m_ref, rope_ref, g_ref, win_ref, qg_ref, wq_ref, wqs_ref, kvg_ref, wk_ref, wv_ref, e_ref, es_ref, q_ref, k_ref,
     v_ref, zp_ref, zr_ref) = refs[nx:]
    m = m_ref[0]
    h = _rms(_part_tile(refs[:nx], x_tiles)) * g_ref[...]
    h = h * (1.0 + m[1:2]) + m[0:1]
    z = _mm(h, win_ref[...])
    zp_ref[...] = z[:, 768:1024]
    for c4 in range(TM // CHUNK):
        zr_ref[c4, 0] = z[c4 * CHUNK:(c4 + 1) * CHUNK, 1024:2048]
    tile8 = lambda t: jnp.concatenate([t] * MLA_HEADS, axis=1)
    qn = (_rms(z[:, 0:Q_LORA]) * qg_ref[...]).astype(bf16)
    q = _mm(qn, wq_ref[...]) * tile8(rope_ref[0]) + _mm(qn, wqs_ref[...]) * tile8(rope_ref[1])
    q_ref[...] = q.astype(bf16)
    kvn = (_rms(z[:, Q_LORA:Q_LORA + KV_LORA]) * kvg_ref[...]).astype(bf16)
    kpe = z[:, 640:768].astype(bf16)
    k = _mm(kvn, wk_ref[...]) + _mm(kpe, e_ref[...]) * tile8(rope_ref[2]) + _mm(kpe, es_ref[...]) * tile8(rope_ref[3])
    k_ref[...] = k.astype(bf16)
    lane = lax.broadcasted_iota(jnp.int32, (1, MLA_HEADS * HEAD_PAD), 1)
    v_ref[...] = (_mm(kvn, wv_ref[...]) + jnp.where(lane % HEAD_PAD == V_HEAD, 1.0, 0.0)).astype(bf16)


def _inproj(x_parts, mt, rope, rope_idx, chunk_idx, npos, b_sz, p, nt):
    t_all, d = nt * TM, x_parts[0].shape[1]
    cpt = TM // CHUNK
    full = lambda a: pl.BlockSpec(a.shape, lambda i: (0,) * a.ndim)
    ws = [p["pre_mix_g"], p["w_in_p"], p["mla_q_norm"], p["wq_p"], p["wq_s"], p["mla_kv_norm"], p["wk_p"], p["wv_p"],
          p["e_p"], p["e_s"]]
    return pl.pallas_call(
        functools.partial(_inproj_kernel, x_tiles=tuple(a.shape[0] // TM for a in x_parts)),
        out_shape=(jax.ShapeDtypeStruct((t_all, MLA_HEADS * HEAD_PAD), bf16),
                   jax.ShapeDtypeStruct((t_all, MLA_HEADS * HEAD_PAD), bf16),
                   jax.ShapeDtypeStruct((t_all, MLA_HEADS * HEAD_PAD), bf16),
                   jax.ShapeDtypeStruct((t_all, POOL_WIDTH), f32),
                   jax.ShapeDtypeStruct((npos, b_sz, CHUNK, RWKV_PAD), f32)),
        grid=(nt,),
        in_specs=_part_specs(x_parts, d)
        + [pl.BlockSpec((1, 8, d), lambda i: (i, 0, 0)),
           pl.BlockSpec((4, TM, HEAD_PAD), lambda i: (0, rope_idx(i), 0))] + [full(w) for w in ws],
        out_specs=(pl.BlockSpec((TM, 1024), lambda i: (i, 0)),
                   pl.BlockSpec((TM, 1024), lambda i: (i, 0)),
                   pl.BlockSpec((TM, 1024), lambda i: (i, 0)),
                   pl.BlockSpec((TM, POOL_WIDTH), lambda i: (i, 0)),
                   pl.BlockSpec((cpt, 1, CHUNK, RWKV_PAD), lambda i: chunk_idx(i) + (0, 0))),
        compiler_params=_cparams(("arbitrary",)),
        name="in_proj",
    )(*x_parts, mt, rope, *ws)


def _attn_kernel(*refs):
    q_ref, kv, o_ref = refs[0], refs[1:-1], refs[-1]
    nt = (((1,), (1,)), ((), ()))
    outs = []
    for h in range(MLA_HEADS):
        hs = slice(h * HEAD_PAD, (h + 1) * HEAD_PAD)
        qh = q_ref[:, hs]
        scores = [lax.dot_general(qh, k_ref[:, hs], nt, preferred_element_type=f32) for k_ref in kv[0::2]]
        m = functools.reduce(jnp.maximum, [jnp.max(s, axis=-1, keepdims=True) for s in scores])
        acc = sum(jnp.dot(jnp.exp2((s - m).astype(bf16)), v_ref[:, hs], preferred_element_type=f32)
                  for s, v_ref in zip(scores, kv[1::2]))
        outs.append(acc[:, :V_HEAD] / acc[:, V_HEAD:V_HEAD + 1])
    for pr2 in range(MLA_HEADS // 2):
        o_ref[:, pr2 * 128:(pr2 + 1) * 128] = jnp.concatenate(outs[2 * pr2:2 * pr2 + 2], axis=1).astype(bf16)


def _attention(q, k, v, n_q_tiles, q_tile0, tq, segs):
    in_specs = [pl.BlockSpec((tq, 1024), lambda i: (i + q_tile0, 0))]
    args = [q]
    for rows, bidx in segs:
        in_specs.append(pl.BlockSpec((rows, 1024), lambda i, bidx=bidx: (bidx(i), 0), pipeline_mode=pl.Buffered(1)))
        in_specs.append(pl.BlockSpec((rows, 1024), lambda i, bidx=bidx: (bidx(i), 0), pipeline_mode=pl.Buffered(1)))
        args += [k, v]
    return pl.pallas_call(
        _attn_kernel,
        out_shape=jax.ShapeDtypeStruct((n_q_tiles * tq, MLA_HEADS * V_HEAD), bf16),
        grid=(n_q_tiles,),
        in_specs=in_specs,
        out_specs=pl.BlockSpec((tq, 512), lambda i: (i, 0)),
        compiler_params=_cparams(("arbitrary",)),
        name="mla_attention",
    )(*args)


def _pool_kernel(z_ref, zp_ref, zn_ref, band_ref, cnt_ref, pw_ref, ps_ref, o_ref, *, nlat, lt, ct):
    i = pl.program_id(0)
    is_lat = i < nlat
    j = jnp.where(is_lat, i % lt, (i - nlat) % ct)
    n = jnp.where(is_lat, lt, ct)
    z = z_ref[...]
    prev = zp_ref[...] * jnp.where(j == 0, 0.0, 1.0)
    nxt = zn_ref[...] * jnp.where(j == n - 1, 0.0, 1.0)
    zh = jnp.concatenate([prev, z, nxt], axis=0)
    lane_grp = lax.broadcasted_iota(jnp.int32, (1, POOL_WIDTH), 1) // 64
    tot = jnp.zeros_like(z)
    for g in range(len(POOL_WINDOWS)):
        tot = tot + _mm_01x(band_ref[g], zh * jnp.where(lane_grp == g, 1.0, 0.0))
    diff = tot / cnt_ref[...] - z
    o_ref[...] = (_mm(diff, pw_ref[...]) * ps_ref[...]).astype(bf16)


def _pool(zp, band, cnt, cnt_idx, pw_bd, pscale, nt, nlat, lt, ct):
    t_all = zp.shape[0]
    nb8 = t_all // POOL_HALO
    r = TM // POOL_HALO
    return pl.pallas_call(
        functools.partial(_pool_kernel, nlat=nlat, lt=lt, ct=ct),
        out_shape=jax.ShapeDtypeStruct((nt * TM, POOL_WIDTH), bf16),
        grid=(nt,),
        in_specs=[pl.BlockSpec((TM, POOL_WIDTH), lambda i: (i, 0)),
                  pl.BlockSpec((POOL_HALO, POOL_WIDTH), lambda i: (jnp.maximum(i * r - 1, 0), 0)),
                  pl.BlockSpec((POOL_HALO, POOL_WIDTH), lambda i: (jnp.minimum((i + 1) * r, nb8 - 1), 0)),
                  pl.BlockSpec(band.shape, lambda i: (0, 0, 0)),
                  pl.BlockSpec((TM, POOL_WIDTH), lambda i: (cnt_idx(i), 0)),
                  pl.BlockSpec((POOL_WIDTH, POOL_WIDTH), lambda i: (0, 0)),
                  pl.BlockSpec((1, POOL_WIDTH), lambda i: (0, 0))],
        out_specs=pl.BlockSpec((TM, POOL_WIDTH), lambda i: (i, 0)),
        compiler_params=_cparams(("arbitrary",)),
        name="pool_mixer",
    )(zp, zp, zp, band, cnt, pw_bd, pscale)


def _rwkv_pos(d, s, *, ncc, nlc):
    in_ctx = s < ncc
    jc = jnp.where(d == 0, s, ncc - 1 - s)
    jl = jnp.where(d == 0, s - ncc, nlc - 1 - (s - ncc))
    pos = jnp.where(in_ctx, nlc + jc, jl)
    first = jnp.where(in_ctx, jc == 0, jl == 0)
    last = jnp.where(in_ctx, jc == ncc - 1, jl == nlc - 1)
    return pos, first, last


def _each(f, *lists):
    return [f(*xs) for xs in zip(*lists)]


def _mmb(a, b):
    return jnp.dot(a, b, preferred_element_type=f32)


def _tri_inverse(lmb, tri_ref, eye_b):
    cast = lambda xs: _each(lambda x: x.astype(bf16), xs)
    n = _each(lambda l: l * tri_ref[0, 2], lmb)
    n2 = cast(_each(_mmb, n, n))
    n4 = cast(_each(_mmb, n2, n2))
    t = _each(lambda a, b: _mmb((eye_b + a), (eye_b + b)), n, n2)
    t = _each(lambda a, b: _mmb(a.astype(bf16), eye_b + b), t, n4)
    for lvl in range(3):
        tb = cast(t)
        x = cast(_each(lambda a, l: _mmb(a, l * tri_ref[0, 3 + lvl]), tb, lmb))
        t = _each(lambda a, xx, ab: a + _mmb(xx, ab), t, x, tb)
    return t


def _rwkv_kernel(z_ref, zp_ref, zn_ref, bd_ref, eye_ref, tri_ref, csi_ref, mu_ref, w0_ref, w2_ref, a0_ref, a2_ref,
                 g2_ref, kk_ref, ka_ref, rk_ref, y_ref, bv_ref, g_ref, s_ref, *, ncc, nlc):
    d, s = pl.program_id(0), pl.program_id(2)
    _, first, last = _rwkv_pos(d, s, ncc=ncc, nlc=nlc)

    @pl.when(s == 0)
    def _():
        s_ref[...] = jnp.zeros_like(s_ref)

    c = CHUNK
    nbs = list(range(z_ref.shape[1]))
    keep_prev, keep_next = jnp.where(first, 0.0, 1.0), jnp.where(last, 0.0, 1.0)
    row = lax.broadcasted_iota(jnp.int32, (c, 1), 0)
    bd = bd_ref[...]
    bd_b = bd.astype(bf16)
    eye = eye_ref[...]
    eye_b = eye.astype(bf16)
    cast = lambda xs: _each(lambda x: x.astype(bf16), xs)

    def shifted(nb):
        z = z_ref[0, nb]
        zp = jnp.where(row == 0, zp_ref[0, nb, 7:8, :] * keep_prev, pltpu.roll(z, 1, 0))
        zn = jnp.where(row == c - 1, zn_ref[0, nb, 0:1, :] * keep_next, pltpu.roll(z, c - 1, 0))
        return z + mu_ref[...] * (0.5 * (zp + zn) - z)

    zs = _each(shifted, nbs)
    r, k, v = (_each(lambda z, o=o: z[:, o:o + 256], zs) for o in (0, 256, 512))
    lora = _each(lambda z: z[:, 768:896], zs)
    g = _each(lambda z: _mm(_sigmoid(z[:, 896:1024]), g2_ref[...]), zs)
    e = _each(lambda x: EXP_M05 * _sigmoid(w0_ref[0] + _mm(jnp.tanh(x), w2_ref[0])), lora)
    a = _each(lambda x: _sigmoid(a0_ref[0] + _mm(x, a2_ref[0])), lora)
    kd = _each(lambda kx, ax: kx * (1.0 + (ax - 1.0) * ka_ref[...]), k, a)
    kkr = _each(lambda kx: kx * kk_ref[...], k)
    kk = _each(lambda x: x / jnp.maximum(jnp.sqrt(_mm_x01(x * x, bd)), 1e-12), kkr)
    bv = _each(lambda rx, kx, vx: _mm_x01(rx * kx * rk_ref[...], bd) * vx, r, kd, v)
    bb = _each(lambda x, ax: x * ax, kk, a)

    cs = _each(lambda x: _mm_01x(csi_ref[0], x), e)
    tot = _each(lambda x: jnp.where(d == 0, x[c - 1:c, :], x[0:1, :]), cs)
    rep4 = lambda t: jnp.concatenate([t] * RWKV_HEADS, axis=0)
    fold4 = lambda t: t[0:c] + t[c:2 * c] + t[2 * c:3 * c] + t[3 * c:4 * c]
    head_rows = lambda t: rep4(t.astype(bf16)) * bd_b
    a4 = _each(lambda ex, cx, kx: head_rows(jnp.exp(ex - cx) * kx), e, cs, kk)
    r_s = _each(lambda rx, cx: rx * jnp.exp(-cx), r, cs)
    r4 = _each(head_rows, r_s)
    v4 = _each(head_rows, v)
    grow = _each(jnp.exp, cs)
    b4 = _each(lambda x, gx: rep4((x * gx).astype(bf16)), bb, grow)
    k4 = _each(lambda x, gx: rep4((x * gx).astype(bf16)), kd, grow)
    to_end = _each(lambda cx, tx: jnp.exp(cx - tx), cs, tot)
    b_e = _each(lambda x, gx: (x * gx).astype(bf16), bb, to_end)
    k_e = _each(lambda x, gx: (x * gx).astype(bf16), kd, to_end)
    g_end = _each(lambda tx: jnp.exp(-tx), tot)

    ntb = lambda x, y: lax.dot_general(x, y, (((1,), (1,)), ((), ())), preferred_element_type=f32).astype(bf16)
    tnb = lambda x, y: lax.dot_general(x, y, (((0,), (0,)), ((), ())), preferred_element_type=f32)
    lmb = _each(lambda x, y: ntb(x, y) * tri_ref[0, 0], a4, b4)
    akm = _each(lambda x, y: ntb(x, y) * tri_ref[0, 0], a4, k4)
    rbn = _each(lambda x, y: ntb(x, y) * tri_ref[0, 1], r4, b4)
    rkm = _each(lambda x, y: ntb(x, y) * tri_ref[0, 6], r4, k4)
    t = cast(_tri_inverse(lmb, tri_ref, eye_b))
    w4 = cast(_each(_mmb, t, a4))
    u4 = cast(_each(lambda tx, ax, vx: _mmb(tx, _mmb(ax, vx).astype(bf16)), t, akm, v4))
    q_all = _each(lambda rx, bx, wx: rx + fold4(_mmb(bx, wx)), r_s, rbn, w4)
    y0 = _each(lambda kx, vx, bx, ux: fold4(_mmb(kx, vx) + _mmb(bx, ux)), rkm, v4, rbn, u4)
    w_all, u_all = _each(fold4, w4), _each(fold4, u4)
    g_bd = _each(lambda gx, wx, bx: eye * gx - bd * tnb(wx, bx), g_end, w_all, b_e)
    h_bd = _each(lambda vx, kx, ux, bx: bd * (tnb(vx.astype(bf16), kx) - tnb(ux, bx)), v, k_e, u_all, b_e)
    for nb in nbs:
        st = s_ref[nb]
        y_ref[0, 0, nb] = _mm_nt(q_all[nb], st) + y0[nb]
        bv_ref[0, 0, nb] = bv[nb]
        g_ref[0, 0, nb] = g[nb]
        s_ref[nb] = _mm(st, g_bd[nb]) + h_bd[nb]


def _rwkv(zr, consts, p, ncc, nlc):
    npos, b_sz = zr.shape[0], zr.shape[1]
    nbat = math.gcd(b_sz, RWKV_BATCHES_PER_STEP)
    kw = dict(ncc=ncc, nlc=nlc)
    pos = lambda d, s: _rwkv_pos(d, s, **kw)[0]
    full = lambda a: pl.BlockSpec(a.shape, lambda d, b, s: (0,) * a.ndim)
    by_dir = lambda a: pl.BlockSpec((1,) + a.shape[1:], lambda d, b, s: (d,) + (0,) * (a.ndim - 1))
    out = jax.ShapeDtypeStruct((2, npos, b_sz, CHUNK, RWKV_WIDTH), f32)
    ospec = pl.BlockSpec((1, 1, nbat, CHUNK, RWKV_WIDTH), lambda d, b, s: (d, pos(d, s), b, 0, 0))
    last8 = CHUNK // 8 - 1
    return pl.pallas_call(
        functools.partial(_rwkv_kernel, **kw),
        out_shape=(out, out, out),
        grid=(2, b_sz // nbat, npos),
        in_specs=[pl.BlockSpec((1, nbat, CHUNK, RWKV_PAD), lambda d, b, s: (pos(d, s), b, 0, 0)),
                  pl.BlockSpec((1, nbat, 8, RWKV_PAD), lambda d, b, s: (jnp.maximum(pos(d, s) - 1, 0), b, last8, 0)),
                  pl.BlockSpec((1, nbat, 8, RWKV_PAD), lambda d, b, s: (jnp.minimum(pos(d, s) + 1, npos - 1), b, 0, 0)),
                  full(consts["bd"]), full(consts["eye"]), by_dir(consts["tri"]), by_dir(consts["csi"]),
                  full(p["mu_p"]), by_dir(p["w0"]), by_dir(p["w2_p"]), by_dir(p["a0"]), by_dir(p["a2_p"]),
                  full(p["g2_p"]), full(p["k_k"]), full(p["k_a"]), full(p["r_k"])],
        out_specs=(ospec, ospec, ospec),
        scratch_shapes=[pltpu.VMEM((nbat, RWKV_WIDTH, RWKV_WIDTH), f32)],
        compiler_params=_cparams(("arbitrary", "arbitrary", "arbitrary")),
        name="rwkv7_chunked",
    )(zr, zr, zr, consts["bd"], consts["eye"], consts["tri"], consts["csi"], p["mu_p"], p["w0"], p["w2_p"], p["a0"],
      p["a2_p"], p["g2_p"], p["k_k"], p["k_a"], p["r_k"])


def _pack_bf16_pairs(lo, hi):
    lo_b = pltpu.bitcast(lo.astype(bf16).astype(f32), jnp.uint32)
    hi_b = pltpu.bitcast(hi.astype(bf16).astype(f32), jnp.uint32)
    return (hi_b & jnp.uint32(0xFFFF0000)) | (lo_b >> 16)


def _unpack_bf16_pairs(w):
    lo = pltpu.bitcast(w << 16, f32).astype(bf16)
    hi = pltpu.bitcast(w & jnp.uint32(0xFFFF0000), f32).astype(bf16)
    return lo, hi


def _outproj_kernel(*refs, a_tiles, x_tiles):
    na, nx = len(a_tiles), len(x_tiles)
    a_refs, (py_ref, y_ref, bv_ref, g_ref) = refs[:na], refs[na:na + 4]
    x_refs = refs[na + 4:na + 4 + nx]
    m_ref, avg_ref, lnw_ref, lnb_ref, wo_ref, pmg_ref, pfg_ref, x1_ref, hp_ref = refs[na + 4 + nx:]
    m = m_ref[0]
    rows = lambda ref, dd: jnp.concatenate([ref[dd, c4, 0] for c4 in range(TM // CHUNK)], axis=0)
    ysum = rows(y_ref, 0) + rows(y_ref, 1)
    avg = avg_ref[...]
    dev = ysum - _mm_x01(ysum, avg)
    var = _mm_x01(dev * dev, avg)
    yn = dev * lax.rsqrt(var + RWKV_GN_EPS) * lnw_ref[...] + lnb_ref[...]
    rw = (yn + rows(bv_ref, 0) + rows(bv_ref, 1)) * rows(g_ref, 0)
    o = (jnp.dot(_part_tile(a_refs, a_tiles), wo_ref[0:512, :], preferred_element_type=f32)
         + jnp.dot(py_ref[...], wo_ref[512:768, :], preferred_element_type=f32)
         + _mm(rw, wo_ref[768:1024, :]))
    x1 = _part_tile(x_refs, x_tiles) + m[2:3] * (_rms(o) * pmg_ref[...])
    x1_ref[...] = x1
    h = (_rms(x1) * pfg_ref[...]) * (1.0 + m[4:5]) + m[3:4]
    hp_ref[...] = _pack_bf16_pairs(h[:, 0:512], h[:, 512:1024])


def _outproj(att_parts, py, y, bv, g, x_parts, mt, chunk_idx, consts, p, nt):
    d = x_parts[0].shape[1]
    tiles = lambda parts: tuple(a.shape[0] // TM for a in parts)
    cpt = TM // CHUNK
    cspec = lambda nd: pl.BlockSpec((nd, cpt, 1, CHUNK, RWKV_WIDTH), lambda i: (0,) + chunk_idx(i) + (0, 0))
    full = lambda a: pl.BlockSpec(a.shape, lambda i: (0,) * a.ndim)
    ws = [consts["avg"], p["ln_w"], p["ln_b"], p["w_out"], p["post_mix_g"], p["pre_ffn_g"]]
    return pl.pallas_call(
        functools.partial(_outproj_kernel, a_tiles=tiles(att_parts), x_tiles=tiles(x_parts)),
        out_shape=(jax.ShapeDtypeStruct((nt * TM, d), f32), jax.ShapeDtypeStruct((nt * TM, d // 2), jnp.uint32)),
        grid=(nt,),
        in_specs=_part_specs(att_parts, 512)
        + [pl.BlockSpec((TM, POOL_WIDTH), lambda i: (i, 0)), cspec(2), cspec(2), cspec(1)]
        + _part_specs(x_parts, d)
        + [pl.BlockSpec((1, 8, d), lambda i: (i, 0, 0))] + [full(w) for w in ws],
        out_specs=(pl.BlockSpec((TM, d), lambda i: (i, 0)), pl.BlockSpec((TM, d // 2), lambda i: (i, 0))),
        compiler_params=_cparams(("arbitrary",)),
        name="out_proj",
    )(*att_parts, py, y, bv, g, *x_parts, mt, *ws)


def _router_kernel(hp_ref, rw_ref, rb_ref, ut_ref, lt_ref, ei_ref, pos_ref, gt_ref, cnt_ref, run_ref):
    i = pl.program_id(0)

    @pl.when(i == 0)
    def _():
        run_ref[...] = jnp.zeros_like(run_ref)

    tm = hp_ref.shape[0]
    ne, ng = N_EXPERTS, N_GROUPS
    pg = ne // ng
    lo, hi = _unpack_bf16_pairs(hp_ref[...])
    logits = (lax.dot_general(rw_ref[:, 0:512], lo, (((1,), (1,)), ((), ())), preferred_element_type=f32)
              + lax.dot_general(rw_ref[:, 512:1024], hi, (((1,), (1,)), ((), ())), preferred_element_type=f32))
    scores = _sigmoid(logits)
    sel = scores + rb_ref[...]
    neg = -jnp.inf

    s3 = sel.reshape(ng, pg, tm)
    io = lax.broadcasted_iota(jnp.int32, (ng, pg, tm), 1)
    m1 = jnp.max(s3, axis=1, keepdims=True)
    i1 = jnp.min(jnp.where(s3 == m1, io, pg), axis=1, keepdims=True)
    m2 = jnp.max(jnp.where(io == i1, neg, s3), axis=1, keepdims=True)
    gs = (m1 + m2).reshape(ng, tm)
    gi = lax.broadcasted_iota(jnp.int32, (ng, tm), 0)
    grank = jnp.zeros((ng, tm), f32)
    for j in range(ng):
        rj = gs[j:j + 1, :]
        grank = grank + jnp.where((rj > gs) | ((rj == gs) & (j < gi)), 1.0, 0.0)
    gsel = jnp.where(grank < TOPK_GROUPS, 1.0, 0.0)
    gsel3 = jnp.broadcast_to(gsel.reshape(ng, 1, tm), (ng, pg, tm)).reshape(ne, tm)
    msk = jnp.where(gsel3 > 0.5, sel, neg)
    ei = lax.broadcasted_iota(jnp.int32, (ne, tm), 0)
    erank = jnp.zeros((ne, tm), f32)
    for j in range(ne):
        rj = msk[j:j + 1, :]
        erank = erank + jnp.where((rj > msk) | ((rj == msk) & (j < ei)), 1.0, 0.0)
    chosen = erank < TOP_K
    chf = jnp.where(chosen, 1.0, 0.0)
    graw = jnp.where(chosen, scores, 0.0)
    gate = graw / jnp.sum(graw, axis=0, keepdims=True) * ROUTED_SCALE

    pos = run_ref[...] + _mm(chf, ut_ref[...])
    tot = jnp.sum(chf, axis=1, keepdims=True)
    run_new = run_ref[...] + tot
    run_ref[...] = run_new
    cnt_ref[...] = run_new[:, 0:128]
    rk = _mm(lt_ref[...], chf)
    eif = ei.astype(f32)
    rows_e, rows_p, rows_g = [], [], []
    for kq in range(TOP_K):
        mk = chosen & (rk == float(kq))
        rows_e.append(jnp.sum(jnp.where(mk, eif, 0.0), axis=0, keepdims=True))
        rows_p.append(jnp.sum(jnp.where(mk, pos, 0.0), axis=0, keepdims=True))
        rows_g.append(jnp.sum(jnp.where(mk, gate, 0.0), axis=0, keepdims=True))
    zrow = jnp.zeros((8 - TOP_K, tm), f32)
    ei_ref[0] = jnp.concatenate(rows_e + [zrow], axis=0).astype(jnp.int32)
    pos_ref[0] = jnp.concatenate(rows_p + [zrow], axis=0).astype(jnp.int32)
    gpad = jnp.concatenate(rows_g + [jnp.zeros((128 - TOP_K, tm), f32)], axis=0)
    gt_ref[...] = gpad.T


def _router(hp, rwt, rb, consts, nt):
    full = lambda a: pl.BlockSpec(a.shape, lambda i: (0,) * a.ndim)
    return pl.pallas_call(
        _router_kernel,
        out_shape=(jax.ShapeDtypeStruct((nt, 8, TM), jnp.int32), jax.ShapeDtypeStruct((nt, 8, TM), jnp.int32),
                   jax.ShapeDtypeStruct((nt * TM, 128), f32), jax.ShapeDtypeStruct((N_EXPERTS, 128), f32)),
        grid=(nt,),
        in_specs=[pl.BlockSpec((TM, 512), lambda i: (i, 0)), full(rwt), full(rb), full(consts["ut"]),
                  full(consts["lt"])],
        out_specs=(pl.BlockSpec((1, 8, TM), lambda i: (i, 0, 0)), pl.BlockSpec((1, 8, TM), lambda i: (i, 0, 0)),
                   pl.BlockSpec((TM, 128), lambda i: (i, 0)), pl.BlockSpec((N_EXPERTS, 128), lambda i: (0, 0))),
        scratch_shapes=[pltpu.VMEM((N_EXPERTS, TM), f32)],
        compiler_params=_cparams(("arbitrary",)),
        name="moe_router",
    )(hp, rwt, rb, consts["ut"], consts["lt"])


def _slots_kernel(ps_ref, ei_ref, pos_ref, d_ref):
    ei = ei_ref[0]
    slot = pos_ref[0]
    for e in range(N_EXPERTS):
        slot = slot + jnp.where(ei == e, ps_ref[e], 0)
    d_ref[...] = slot


def _slots(pstart, ei, pos, nt):
    return pl.pallas_call(
        _slots_kernel,
        out_shape=jax.ShapeDtypeStruct((8, nt * TM), jnp.int32),
        grid_spec=pltpu.PrefetchScalarGridSpec(
            num_scalar_prefetch=1,
            grid=(nt,),
            in_specs=[pl.BlockSpec((1, 8, TM), lambda i, ps: (i, 0, 0)), pl.BlockSpec((1, 8, TM), lambda i, ps: (i, 0, 0))],
            out_specs=pl.BlockSpec((8, TM), lambda i, ps: (0, i))),
        compiler_params=_cparams(("arbitrary",)),
        name="moe_slots",
    )(pstart, ei, pos)


def _sc_dispatch(hp, dest, n_slots):
    t, w = hp.shape
    wh = w // 2
    mesh = plsc.VectorSubcoreMesh(core_axis_name="core", subcore_axis_name="subcore", num_cores=SC_CORES,
                                  num_subcores=SC_SUBCORES)
    idx = [dest[kq:kq + 1] for kq in range(TOP_K)]
    half = jax.ShapeDtypeStruct((n_slots, wh), hp.dtype)

    @pl.kernel(out_type=(half, half), mesh=mesh, scratch_types=[])
    def scatter_rows(hp_hbm, *rest):
        idx_hbm, xs_hbm = rest[:TOP_K], rest[TOP_K:]
        for c in range(2):
            def body(x_vmem, *i_vmem, c=c):
                for iv in i_vmem:
                    pltpu.sync_copy(x_vmem, xs_hbm[c].at[iv.at[0]])

            pltpu.emit_pipeline(
                body,
                grid=(t // SC_WINDOW,),
                in_specs=[pl.BlockSpec((SC_WINDOW, wh), lambda i, c=c: (i, c))]
                + [pl.BlockSpec((1, SC_WINDOW), lambda i: (0, i))] * TOP_K,
                out_specs=[],
                core_axis_name=("core", "subcore"),
                dimension_semantics=(pltpu.PARALLEL,),
            )(hp_hbm, *idx_hbm)

    return scatter_rows(hp, *idx)


def _expert_kernel(be_ref, nb_ref, bv_ref, xa_ref, xb_ref, wg_ref, wu_ref, wd_ref, ys_ref):
    i = pl.program_id(0)

    @pl.when(i < nb_ref[0])
    def _():
        live = lax.broadcasted_iota(jnp.int32, (MOE_BM, 1), 0) < bv_ref[i]
        la, ha = _unpack_bf16_pairs(jnp.where(live, xa_ref[...], jnp.uint32(0)))
        lb, hb = _unpack_bf16_pairs(jnp.where(live, xb_ref[...], jnp.uint32(0)))

        def proj(w_ref):
            return sum(jnp.dot(x, w_ref[0, 0, q * 256:(q + 1) * 256, :].astype(bf16), preferred_element_type=f32)
                       for q, x in enumerate((la, lb, ha, hb)))

        gg, uu = proj(wg_ref), proj(wu_ref)
        act = gg * _sigmoid(gg) * uu
        y = _mm(act, wd_ref[0, 0])
        ys_ref[0] = _pack_bf16_pairs(y[:, 0:256], y[:, 512:768])
        ys_ref[1] = _pack_bf16_pairs(y[:, 256:512], y[:, 768:1024])

    @pl.when(i >= nb_ref[0])
    def _():
        ys_ref[...] = jnp.zeros_like(ys_ref)


def _experts(block_expert, nb_used, block_valid, xs, layer, w_gate, w_up, w_down):
    xa, xb = xs
    n_slots = xa.shape[0]
    wspec = lambda w: pl.BlockSpec((1, 1) + w.shape[2:], lambda i, be, nb, bv: (layer, be[i], 0, 0))
    return pl.pallas_call(
        _expert_kernel,
        out_shape=jax.ShapeDtypeStruct((2, n_slots, xa.shape[1]), jnp.uint32),
        grid_spec=pltpu.PrefetchScalarGridSpec(
            num_scalar_prefetch=3,
            grid=(n_slots // MOE_BM,),
            in_specs=[pl.BlockSpec((MOE_BM, xa.shape[1]), lambda i, be, nb, bv: (i, 0)),
                      pl.BlockSpec((MOE_BM, xb.shape[1]), lambda i, be, nb, bv: (i, 0)),
                      wspec(w_gate), wspec(w_up), wspec(w_down)],
            out_specs=pl.BlockSpec((2, MOE_BM, xa.shape[1]), lambda i, be, nb, bv: (0, i, 0))),
        compiler_params=_cparams(("arbitrary",)),
        name="moe_experts",
    )(block_expert, nb_used, block_valid, xa, xb, w_gate, w_up, w_down)


def _sc_gather(ys, dest):
    n_slots, wh = ys.shape[1], ys.shape[2]
    t = dest.shape[1]
    mesh = plsc.VectorSubcoreMesh(core_axis_name="core", subcore_axis_name="subcore", num_cores=SC_CORES,
                                  num_subcores=SC_SUBCORES)
    idx = jnp.concatenate([dest[:TOP_K], dest[:TOP_K] + n_slots], 0).reshape(1, 2 * TOP_K * t)
    rows = 2 * TOP_K * t

    @pl.kernel(out_type=jax.ShapeDtypeStruct((rows, wh), ys.dtype), mesh=mesh, scratch_types=[])
    def gather_rows(y_hbm, idx_hbm, out_hbm):
        def body(i_vmem, o_vmem):
            pltpu.sync_copy(y_hbm.at[i_vmem.at[0]], o_vmem)

        pltpu.emit_pipeline(
            body,
            grid=(rows // SC_WINDOW,),
            in_specs=[pl.BlockSpec((1, SC_WINDOW), lambda i: (0, i))],
            out_specs=[pl.BlockSpec((SC_WINDOW, wh), lambda i: (i, 0))],
            core_axis_name=("core", "subcore"),
            dimension_semantics=(pltpu.PARALLEL,),
        )(idx_hbm, out_hbm)

    return gather_rows(ys.reshape(2 * n_slots, wh), idx)


def _combine_kernel(*refs):
    ga, gb = refs[0:TOP_K], refs[TOP_K:2 * TOP_K]
    hp_ref, gt_ref, x1_ref, m_ref, wsgu_ref, wsd_ref, pg_ref, o_ref = refs[2 * TOP_K:]
    lo, hi = _unpack_bf16_pairs(hp_ref[...])
    gu = (jnp.dot(lo, wsgu_ref[0:512, :], preferred_element_type=f32)
          + jnp.dot(hi, wsgu_ref[512:1024, :], preferred_element_type=f32))
    gg, uu = gu[:, 0:EXPERT_FF], gu[:, EXPERT_FF:2 * EXPERT_FF]
    f = _mm(gg * _sigmoid(gg) * uu, wsd_ref[...])
    gt = gt_ref[...]
    parts = [jnp.zeros((TM, 256), f32) for _ in range(4)]
    for kq in range(TOP_K):
        g = gt[:, kq:kq + 1]
        la, ha = _unpack_bf16_pairs(ga[kq][...])
        lb, hb = _unpack_bf16_pairs(gb[kq][...])
        for q, v in enumerate((la, lb, ha, hb)):
            parts[q] = parts[q] + v.astype(f32) * g
    f = f + jnp.concatenate(parts, axis=1)
    m = m_ref[0]
    o_ref[...] = x1_ref[...] + m[5:6] * (_rms(f) * pg_ref[...])


def _combine(gathered, hp, gt, x1, mt, p, nt):
    d = x1.shape[1]
    full = lambda a: pl.BlockSpec(a.shape, lambda i: (0,) * a.ndim)
    ws = [p["sh_wgu"], p["sh_wd"], p["post_ffn_g"]]
    gspecs = [pl.BlockSpec((TM, gathered.shape[1]), lambda i, j=j: (j * nt + i, 0)) for j in range(2 * TOP_K)]
    return pl.pallas_call(
        _combine_kernel,
        out_shape=jax.ShapeDtypeStruct((nt * TM, d), f32),
        grid=(nt,),
        in_specs=gspecs
        + [pl.BlockSpec((TM, d // 2), lambda i: (i, 0)),
           pl.BlockSpec((TM, 128), lambda i: (i, 0)),
           pl.BlockSpec((TM, d), lambda i: (i, 0)),
           pl.BlockSpec((1, 8, d), lambda i: (i, 0, 0))] + [full(w) for w in ws],
        out_specs=pl.BlockSpec((TM, d), lambda i: (i, 0)),
        compiler_params=_cparams(("arbitrary",)),
        name="moe_combine",
    )(*([gathered] * (2 * TOP_K)), hp, gt, x1, mt, *ws)


def _moe(hp, x1, mt, consts, p, nt):
    t = nt * TM
    ei, pos, gt, cnt = _router(hp, p["router_wt"], p["router_b"], consts, nt)
    n_assign = t * TOP_K
    n_blocks = -(-(n_assign + N_EXPERTS * (MOE_BM - 1)) // MOE_BM)
    n_slots = n_blocks * MOE_BM
    counts = cnt[:, 0].astype(jnp.int32)
    padded = (counts + MOE_BM - 1) // MOE_BM * MOE_BM
    pend = jnp.cumsum(padded)
    pstart = pend - padded
    nb_used = (pend[-1:] // MOE_BM).astype(jnp.int32)
    block_row0 = jnp.arange(n_blocks, dtype=jnp.int32) * MOE_BM
    block_expert = jnp.minimum(jnp.sum((pend[None, :] <= block_row0[:, None]).astype(jnp.int32), axis=1),
                               N_EXPERTS - 1)
    run_end = (pstart + counts)[block_expert]
    block_valid = jnp.clip(run_end - block_row0, 0, MOE_BM).astype(jnp.int32)
    dest = _slots(pstart.astype(jnp.int32), ei, pos, nt)
    xs = _sc_dispatch(hp, dest, n_slots)
    ys = _experts(block_expert, nb_used, block_valid, xs, p["layer"], *p["experts"])
    return _combine(_sc_gather(ys, dest), hp, gt, x1, mt, p, nt)


def _np_consts(l_lat, l_ctx):
    n = RWKV_WIDTH
    i = np.arange(n)
    bd = (i[:, None] // 64 == i[None, :] // 64).astype(np.float32)
    t_r, t_c = (i % 64)[:, None], (i % 64)[None, :]
    tri = np.zeros((2, 7, n, n), np.float32)
    for d in range(2):
        before = (t_c < t_r) if d == 0 else (t_c > t_r)
        tri[d, 0] = bd * before
        tri[d, 6] = bd * (before | (t_c == t_r))
        tri[d, 1] = -tri[d, 6]
        tri[d, 2] = -(bd * before * (t_r // 8 == t_c // 8))
        for lvl, blk in enumerate((8, 16, 32)):
            tri[d, 3 + lvl] = -(bd * before * (t_r // (2 * blk) == t_c // (2 * blk)) * (t_r // blk != t_c // blk))
    j = np.arange(CHUNK)
    csi = np.stack([(j[None, :] <= j[:, None]), (j[None, :] >= j[:, None])]).astype(np.float32)
    tt = np.arange(TM)
    ut = (tt[:, None] < tt[None, :]).astype(np.float32)
    ee = np.arange(N_EXPERTS)
    lt = (ee[None, :] < ee[:, None]).astype(np.float32)
    jj = np.arange(TM + 2 * POOL_HALO)[None, :]
    band = np.stack([((jj >= tt[:, None] + POOL_HALO - w // 2) & (jj <= tt[:, None] + POOL_HALO + w // 2 - 1))
                     for w in POOL_WINDOWS]).astype(np.float32)

    def counts(length):
        t = np.arange(length)[:, None]
        half = np.repeat(np.array(POOL_WINDOWS) // 2, 64)[None, :]
        return (np.clip(t + half, 0, length) - np.clip(t - half, 0, length)).astype(np.float32)

    cnt = np.concatenate([counts(l_lat), counts(l_ctx)], axis=0)
    return dict(bd=jnp.asarray(bd), eye=jnp.eye(n, dtype=f32), tri=jnp.asarray(tri, dtype=bf16), csi=jnp.asarray(csi),
                avg=jnp.asarray(bd / 64.0), ut=jnp.asarray(ut, dtype=bf16), lt=jnp.asarray(lt, dtype=bf16),
                band=jnp.asarray(band), cnt=jnp.asarray(cnt))


def _rope_tables(l_lat):
    rows = l_lat // GRID_W
    row = jnp.repeat(jnp.arange(rows, dtype=f32), GRID_W)
    col = jnp.tile(jnp.arange(GRID_W, dtype=f32), rows)
    n_freq = QK_ROPE // 4
    inv = ROPE_THETA ** (-jnp.arange(n_freq, dtype=f32) / n_freq)
    ang = jnp.concatenate([row[:, None] * inv, col[:, None] * inv], -1)
    cos = jnp.concatenate([jnp.cos(ang), jnp.ones((TM, 16), f32)], 0)
    sin = jnp.concatenate([jnp.sin(ang), jnp.zeros((TM, 16), f32)], 0)
    n = cos.shape[0]
    ct = jnp.concatenate([jnp.ones((n, QK_NOPE), f32), cos, cos, jnp.zeros((n, 32), f32)], 1)
    st = jnp.concatenate([jnp.zeros((n, QK_NOPE), f32), -sin, sin, jnp.zeros((n, 32), f32)], 1)
    qs = MLA_SCALE * LOG2E
    return jnp.stack([ct * qs, st * qs, ct, st])


def _layer_params(i, a):
    d = a["w_in"].shape[1]
    p = {}
    row = lambda v: v.reshape(1, -1).astype(f32)
    for name in ("pre_mix_g", "post_mix_g", "pre_ffn_g", "post_ffn_g", "mla_q_norm", "mla_kv_norm"):
        p[name] = row(a[name][i])
    w_in = a["w_in"][i]
    zc = lambda n: jnp.zeros((d, n), f32)
    p["w_in_p"] = jnp.concatenate(
        [w_in[:, 0:MLA_IN], zc(128 - QK_ROPE), w_in[:, MLA_IN:MLA_IN + POOL_WIDTH],
         w_in[:, MLA_IN + POOL_WIDTH:], zc(RWKV_PAD - RWKV_IN)], 1).astype(bf16)
    wq = a["mla_w_q_b"][i].reshape(Q_LORA, MLA_HEADS, QK_NOPE + QK_ROPE)
    zq = jnp.zeros((Q_LORA, MLA_HEADS, 32), f32)
    half = QK_ROPE // 2
    p["wq_p"] = jnp.concatenate([wq, zq], 2).reshape(Q_LORA, -1).astype(bf16)
    p["wq_s"] = jnp.concatenate([jnp.zeros_like(wq[:, :, :QK_NOPE]), wq[:, :, QK_NOPE + half:],
                                 wq[:, :, QK_NOPE:QK_NOPE + half], zq], 2).reshape(Q_LORA, -1).astype(bf16)
    wkv = a["mla_w_kv_b"][i].reshape(KV_LORA, MLA_HEADS, QK_NOPE + V_HEAD)
    p["wk_p"] = jnp.concatenate([wkv[:, :, :QK_NOPE], jnp.zeros((KV_LORA, MLA_HEADS, 64), f32)], 2
                                ).reshape(KV_LORA, -1).astype(bf16)
    p["wv_p"] = jnp.concatenate([wkv[:, :, QK_NOPE:], jnp.zeros((KV_LORA, MLA_HEADS, 64), f32)], 2
                                ).reshape(KV_LORA, -1).astype(bf16)
    e_p = np.zeros((128, MLA_HEADS * HEAD_PAD), np.float32)
    e_s = np.zeros_like(e_p)
    for h in range(MLA_HEADS):
        for j in range(QK_ROPE):
            e_p[j, h * HEAD_PAD + QK_NOPE + j] = 1.0
            e_s[(j + half) % QK_ROPE, h * HEAD_PAD + QK_NOPE + j] = 1.0
    p["e_p"], p["e_s"] = jnp.asarray(e_p, dtype=bf16), jnp.asarray(e_s, dtype=bf16)
    pw = a["pool_w"][i]
    p["pool_w_bd"] = jax.scipy.linalg.block_diag(*[pw[g] for g in range(pw.shape[0])]).astype(bf16)
    p["pool_scale"] = row(a["pool_scale"][i])
    p["mu_p"] = jnp.pad(a["rwkv_mu"][i], (0, RWKV_PAD - RWKV_IN)).reshape(1, -1)
    p["w0"] = a["rwkv_w0"][i].reshape(2, 1, RWKV_WIDTH)
    p["a0"] = a["rwkv_a0"][i].reshape(2, 1, RWKV_WIDTH)
    z32 = jnp.zeros((32, RWKV_WIDTH), f32)
    w2, a2 = a["rwkv_w2"][i], a["rwkv_a2"][i]
    p["w2_p"] = jnp.stack([jnp.concatenate([w2[0], z32, z32, z32]), jnp.concatenate([z32, w2[1], z32, z32])])
    p["a2_p"] = jnp.stack([jnp.concatenate([z32, z32, a2[0], z32]), jnp.concatenate([z32, z32, z32, a2[1]])])
    p["g2_p"] = jnp.concatenate([a["rwkv_g2"][i], jnp.zeros((64, RWKV_WIDTH), f32)])
    for name in ("k_k", "k_a", "r_k"):
        p[name] = row(a["rwkv_" + name][i])
    p["ln_w"], p["ln_b"] = row(a["rwkv_ln_w"][i]), row(a["rwkv_ln_b"][i])
    p["w_out"] = a["w_out"][i].astype(bf16)
    p["router_wt"] = a["router_w"][i].T.astype(bf16)
    p["router_b"] = a["router_bias"][i].reshape(-1, 1).astype(f32)
    p["layer"] = i
    p["experts"] = (a["exp_w_gate"], a["exp_w_up"], a["exp_w_down"])
    p["sh_wgu"] = jnp.concatenate([a["sh_w_gate"][i], a["sh_w_up"][i]], 1).astype(bf16)
    p["sh_wd"] = a["sh_w_down"][i].astype(bf16)
    return p


def kernel(x, c, ctx, c_ctx, ada_w, ada_b, pre_mix_g, post_mix_g, pre_ffn_g, post_ffn_g, w_in, w_out, mla_q_norm, mla_w_q_b, mla_kv_norm, mla_w_kv_b, pool_w, pool_scale, rwkv_mu, rwkv_w0, rwkv_w2, rwkv_a0, rwkv_a2, rwkv_g2, rwkv_k_k, rwkv_k_a, rwkv_r_k, rwkv_ln_w, rwkv_ln_b, router_w, router_bias, exp_w_gate, exp_w_up, exp_w_down, sh_w_gate, sh_w_up, sh_w_down):
    arrs = dict(pre_mix_g=pre_mix_g, post_mix_g=post_mix_g, pre_ffn_g=pre_ffn_g, post_ffn_g=post_ffn_g, w_in=w_in,
                w_out=w_out, mla_q_norm=mla_q_norm, mla_w_q_b=mla_w_q_b, mla_kv_norm=mla_kv_norm,
                mla_w_kv_b=mla_w_kv_b, pool_w=pool_w, pool_scale=pool_scale, rwkv_mu=rwkv_mu, rwkv_w0=rwkv_w0,
                rwkv_w2=rwkv_w2, rwkv_a0=rwkv_a0, rwkv_a2=rwkv_a2, rwkv_g2=rwkv_g2, rwkv_k_k=rwkv_k_k,
                rwkv_k_a=rwkv_k_a, rwkv_r_k=rwkv_r_k, rwkv_ln_w=rwkv_ln_w, rwkv_ln_b=rwkv_ln_b, router_w=router_w,
                router_bias=router_bias, exp_w_gate=exp_w_gate, exp_w_up=exp_w_up, exp_w_down=exp_w_down,
                sh_w_gate=sh_w_gate, sh_w_up=sh_w_up, sh_w_down=sh_w_down)
    b_sz, l_lat, d = x.shape
    l_ctx = ctx.shape[1]
    assert l_lat % TM == 0 and l_ctx % TM == 0 and l_lat % GRID_W == 0 and b_sz < 16
    lt, ct = l_lat // TM, l_ctx // TM
    nlat, nctx = b_sz * lt, b_sz * ct
    nall = nlat + nctx
    ncc, nlc = l_ctx // CHUNK, l_lat // CHUNK
    consts = _np_consts(l_lat, l_ctx)
    rope = _rope_tables(l_lat)
    rope_idx = lambda i: jnp.where(i < nlat, i % lt, lt)
    cnt_idx = lambda i: jnp.where(i < nlat, i % lt, lt + (i - nlat) % ct)
    chunk_idx = lambda i: (jnp.where(i < nlat, i % lt, nlc * CHUNK // TM + (i - nlat) % ct),
                           jnp.where(i < nlat, i // lt, (i - nlat) // ct))

    c_all = jnp.zeros((16, d), f32).at[:b_sz].set(c).at[b_sz].set(c_ctx)
    mods = _ada_mod(c_all, ada_w, ada_b)
    tile_row = np.concatenate([np.repeat(np.arange(b_sz), lt), np.full(nctx, b_sz)])
    x_parts = [x.reshape(b_sz * l_lat, d), ctx.reshape(b_sz * l_ctx, d)]

    for i in range(DEPTH):
        last = i == DEPTH - 1
        p = _layer_params(i, arrs)
        mt = jnp.pad(mods[i][tile_row].reshape(nall, 6, d), ((0, 0), (0, 2), (0, 0)))
        nt = nlat if last else nall
        q, k, v, zp, zr = _inproj(x_parts, mt, rope, rope_idx, chunk_idx, ncc + nlc, b_sz, p, nall)
        ctx_blk0 = nlat * TM // l_ctx
        tq = math.gcd(l_lat, ATTN_TQ)
        qt = l_lat // tq
        att = [_attention(q, k, v, b_sz * qt, 0, tq, [(l_lat, lambda t: t // qt), (l_ctx, lambda t: ctx_blk0 + t // qt)])]
        if not last:
            att.append(_attention(q, k, v, nctx, nlat, TM, [(l_ctx, lambda t: ctx_blk0 + t // ct)]))
        py = _pool(zp, consts["band"], consts["cnt"], cnt_idx, p["pool_w_bd"], p["pool_scale"], nt, nlat, lt, ct)
        y, bv, g = _rwkv(zr, consts, p, ncc, nlc)
        x1, hp = _outproj(att, py, y, bv, g, x_parts, mt, chunk_idx, consts, p, nt)
        x_parts = [_moe(hp, x1, mt, consts, p, nt)]
    return x_parts[0][:b_sz * l_lat].reshape(b_sz, l_lat, d)
```

```python
import functools
import math

import numpy as np
import jax
import jax.numpy as jnp
from jax import lax
from jax.experimental import pallas as pl
from jax.experimental.pallas import tpu as pltpu
from jax.experimental.pallas import tpu_sc as plsc

f32 = jnp.float32
bf16 = jnp.bfloat16
HIGHEST = lax.Precision.HIGHEST

DEPTH = 2
GRID_W = 64
NORM_EPS = 1e-6
MLA_HEADS = 8
Q_LORA = 384
KV_LORA = 256
QK_NOPE = 64
QK_ROPE = 32
V_HEAD = 64
ROPE_THETA = 10000.0
MLA_SCALE = (QK_NOPE + QK_ROPE) ** -0.5
MLA_IN = Q_LORA + KV_LORA + QK_ROPE
HEAD_PAD = 128
POOL_WINDOWS = (2, 4, 8, 16)
POOL_WIDTH = 256
POOL_HALO = 8
RWKV_HEADS = 4
RWKV_HEAD = 64
RWKV_WIDTH = 256
RWKV_IN = 960
RWKV_PAD = 1024
RWKV_GN_EPS = 64e-5
CHUNK = 64
RWKV_BATCHES_PER_STEP = 8
N_EXPERTS = 64
TOP_K = 6
N_GROUPS = 8
TOPK_GROUPS = 4
ROUTED_SCALE = 2.5
EXPERT_FF = 256
TM = 256
ATTN_TQ = 512
MOE_BM = 512
SC_CORES, SC_SUBCORES = 2, 16
SC_WINDOW = 128
Z_COLS = 2048
VMEM_LIMIT = 48 * 1024 * 1024
LOG2E = 1.4426950408889634
EXP_M05 = 0.6065306597126334


def _cparams(sem, vmem=VMEM_LIMIT):
    return pltpu.CompilerParams(dimension_semantics=sem, vmem_limit_bytes=vmem)


def _mm(a, b):
    return jnp.dot(a.astype(bf16), b.astype(bf16), preferred_element_type=f32)


def _mm_nt(a, b):
    return lax.dot_general(a.astype(bf16), b.astype(bf16), (((1,), (1,)), ((), ())), preferred_element_type=f32)


def _mm_tn(a, b):
    return lax.dot_general(a.astype(bf16), b.astype(bf16), (((0,), (0,)), ((), ())), preferred_element_type=f32)


def _mmf(a, b):
    return jnp.dot(a, b, precision=HIGHEST, preferred_element_type=f32)


def _split_hi_lo(a):
    hi = a.astype(bf16)
    return hi, (a - hi.astype(f32)).astype(bf16)


def _mm_x01(a, w01):
    hi, lo = _split_hi_lo(a)
    w = w01.astype(bf16)
    return jnp.dot(hi, w, preferred_element_type=f32) + jnp.dot(lo, w, preferred_element_type=f32)


def _mm_01x(w01, a):
    hi, lo = _split_hi_lo(a)
    w = w01.astype(bf16)
    return jnp.dot(w, hi, preferred_element_type=f32) + jnp.dot(w, lo, preferred_element_type=f32)


def _rms(x):
    return x * lax.rsqrt(jnp.mean(x * x, axis=-1, keepdims=True) + NORM_EPS)


def _sigmoid(x):
    return 1.0 / (1.0 + jnp.exp(-x))


def _ada_kernel(c_ref, w_ref, b_ref, o_ref):
    c = c_ref[...]
    s = c * _sigmoid(c)
    o_ref[0] = _mm(s, w_ref[0]) + b_ref[0]


def _ada_mod(c_all, ada_w, ada_b):
    depth, d, n = ada_w.shape
    tn = 1024
    return pl.pallas_call(
        _ada_kernel,
        out_shape=jax.ShapeDtypeStruct((depth, 16, n), f32),
        grid=(depth, n // tn),
        in_specs=[pl.BlockSpec((16, d), lambda i, j: (0, 0)),
                  pl.BlockSpec((1, d, tn), lambda i, j: (i, 0, j)),
                  pl.BlockSpec((1, 1, tn), lambda i, j: (i, 0, j))],
        out_specs=pl.BlockSpec((1, 16, tn), lambda i, j: (i, 0, j)),
        compiler_params=_cparams(("arbitrary", "arbitrary")),
        name="ada_mod",
    )(c_all, ada_w, ada_b.reshape(depth, 1, n))


def _part_specs(parts, cols):
    specs, start = [], 0
    for a in parts:
        n = a.shape[0] // TM
        specs.append(pl.BlockSpec((TM, cols), lambda i, s=start, n=n: (jnp.clip(i - s, 0, n - 1), 0)))
        start += n
    return specs


def _part_tile(refs, part_tiles):
    i = pl.program_id(0)
    x, start = refs[0][...], part_tiles[0]
    for ref, n in zip(refs[1:], part_tiles[1:]):
        x = jnp.where(i >= start, ref[...], x)
        start += n
    return x


def _inproj_kernel(*refs, x_tiles):
    nx = len(x_tiles)
    (m_ref, rope_ref, g_ref, win_ref, qg_ref, wq_ref, wqs_ref, kvg_ref, wk_ref, wv_ref, e_ref, es_ref, q_ref, k_ref,
     v_ref, zp_ref, zr_ref) = refs[nx:]
    m = m_ref[0]
    h = _rms(_part_tile(refs[:nx], x_tiles)) * g_ref[...]
    h = h * (1.0 + m[1:2]) + m[0:1]
    z = _mm(h, win_ref[...])
    zp_ref[...] = z[:, 768:1024]
    for c4 in range(TM // CHUNK):
        zr_ref[c4, 0] = z[c4 * CHUNK:(c4 + 1) * CHUNK, 1024:2048]
    tile8 = lambda t: jnp.concatenate([t] * MLA_HEADS, axis=1)
    qn = (_rms(z[:, 0:Q_LORA]) * qg_ref[...]).astype(bf16)
    q = _mm(qn, wq_ref[...]) * tile8(rope_ref[0]) + _mm(qn, wqs_ref[...]) * tile8(rope_ref[1])
    q_ref[...] = q.astype(bf16)
    kvn = (_rms(z[:, Q_LORA:Q_LORA + KV_LORA]) * kvg_ref[...]).astype(bf16)
    kpe = z[:, 640:768].astype(bf16)
    k = _mm(kvn, wk_ref[...]) + _mm(kpe, e_ref[...]) * tile8(rope_ref[2]) + _mm(kpe, es_ref[...]) * tile8(rope_ref[3])
    k_ref[...] = k.astype(bf16)
    lane = lax.broadcasted_iota(jnp.int32, (1, MLA_HEADS * HEAD_PAD), 1)
    v_ref[...] = (_mm(kvn, wv_ref[...]) + jnp.where(lane % HEAD_PAD == V_HEAD, 1.0, 0.0)).astype(bf16)


def _inproj(x_parts, mt, rope, rope_idx, chunk_idx, npos, b_sz, p, nt):
    t_all, d = nt * TM, x_parts[0].shape[1]
    cpt = TM // CHUNK
    full = lambda a: pl.BlockSpec(a.shape, lambda i: (0,) * a.ndim)
    ws = [p["pre_mix_g"], p["w_in_p"], p["mla_q_norm"], p["wq_p"], p["wq_s"], p["mla_kv_norm"], p["wk_p"], p["wv_p"],
          p["e_p"], p["e_s"]]
    return pl.pallas_call(
        functools.partial(_inproj_kernel, x_tiles=tuple(a.shape[0] // TM for a in x_parts)),
        out_shape=(jax.ShapeDtypeStruct((t_all, MLA_HEADS * HEAD_PAD), bf16),
                   jax.ShapeDtypeStruct((t_all, MLA_HEADS * HEAD_PAD), bf16),
                   jax.ShapeDtypeStruct((t_all, MLA_HEADS * HEAD_PAD), bf16),
                   jax.ShapeDtypeStruct((t_all, POOL_WIDTH), f32),
                   jax.ShapeDtypeStruct((npos, b_sz, CHUNK, RWKV_PAD), f32)),
        grid=(nt,),
        in_specs=_part_specs(x_parts, d)
        + [pl.BlockSpec((1, 8, d), lambda i: (i, 0, 0)),
           pl.BlockSpec((4, TM, HEAD_PAD), lambda i: (0, rope_idx(i), 0))] + [full(w) for w in ws],
        out_specs=(pl.BlockSpec((TM, 1024), lambda i: (i, 0)),
                   pl.BlockSpec((TM, 1024), lambda i: (i, 0)),
                   pl.BlockSpec((TM, 1024), lambda i: (i, 0)),
                   pl.BlockSpec((TM, POOL_WIDTH), lambda i: (i, 0)),
                   pl.BlockSpec((cpt, 1, CHUNK, RWKV_PAD), lambda i: chunk_idx(i) + (0, 0))),
        compiler_params=_cparams(("arbitrary",)),
        name="in_proj",
    )(*x_parts, mt, rope, *ws)


def _attn_kernel(*refs):
    q_ref, kv, o_ref = refs[0], refs[1:-1], refs[-1]
    nt = (((1,), (1,)), ((), ()))
    outs = []
    for h in range(MLA_HEADS):
        hs = slice(h * HEAD_PAD, (h + 1) * HEAD_PAD)
        qh = q_ref[:, hs]
        scores = [lax.dot_general(qh, k_ref[:, hs], nt, preferred_element_type=f32) for k_ref in kv[0::2]]
        m = functools.reduce(jnp.maximum, [jnp.max(s, axis=-1, keepdims=True) for s in scores])
        acc = sum(jnp.dot(jnp.exp2((s - m).astype(bf16)), v_ref[:, hs], preferred_element_type=f32)
                  for s, v_ref in zip(scores, kv[1::2]))
        outs.append(acc[:, :V_HEAD] / acc[:, V_HEAD:V_HEAD + 1])
    for pr2 in range(MLA_HEADS // 2):
        o_ref[:, pr2 * 128:(pr2 + 1) * 128] = jnp.concatenate(outs[2 * pr2:2 * pr2 + 2], axis=1).astype(bf16)


def _attention(q, k, v, n_q_tiles, q_tile0, tq, segs):
    in_specs = [pl.BlockSpec((tq, 1024), lambda i: (i + q_tile0, 0))]
    args = [q]
    for rows, bidx in segs:
        in_specs.append(pl.BlockSpec((rows, 1024), lambda i, bidx=bidx: (bidx(i), 0), pipeline_mode=pl.Buffered(1)))
        in_specs.append(pl.BlockSpec((rows, 1024), lambda i, bidx=bidx: (bidx(i), 0), pipeline_mode=pl.Buffered(1)))
        args += [k, v]
    return pl.pallas_call(
        _attn_kernel,
        out_shape=jax.ShapeDtypeStruct((n_q_tiles * tq, MLA_HEADS * V_HEAD), bf16),
        grid=(n_q_tiles,),
        in_specs=in_specs,
        out_specs=pl.BlockSpec((tq, 512), lambda i: (i, 0)),
        compiler_params=_cparams(("arbitrary",)),
        name="mla_attention",
    )(*args)


def _pool_kernel(z_ref, zp_ref, zn_ref, band_ref, cnt_ref, pw_ref, ps_ref, o_ref, *, nlat, lt, ct):
    i = pl.program_id(0)
    is_lat = i < nlat
    j = jnp.where(is_lat, i % lt, (i - nlat) % ct)
    n = jnp.where(is_lat, lt, ct)
    z = z_ref[...]
    prev = zp_ref[...] * jnp.where(j == 0, 0.0, 1.0)
    nxt = zn_ref[...] * jnp.where(j == n - 1, 0.0, 1.0)
    zh = jnp.concatenate([prev, z, nxt], axis=0)
    lane_grp = lax.broadcasted_iota(jnp.int32, (1, POOL_WIDTH), 1) // 64
    tot = jnp.zeros_like(z)
    for g in range(len(POOL_WINDOWS)):
        tot = tot + _mm_01x(band_ref[g], zh * jnp.where(lane_grp == g, 1.0, 0.0))
    diff = tot / cnt_ref[...] - z
    o_ref[...] = (_mm(diff, pw_ref[...]) * ps_ref[...]).astype(bf16)


def _pool(zp, band, cnt, cnt_idx, pw_bd, pscale, nt, nlat, lt, ct):
    t_all = zp.shape[0]
    nb8 = t_all // POOL_HALO
    r = TM // POOL_HALO
    return pl.pallas_call(
        functools.partial(_pool_kernel, nlat=nlat, lt=lt, ct=ct),
        out_shape=jax.ShapeDtypeStruct((nt * TM, POOL_WIDTH), bf16),
        grid=(nt,),
        in_specs=[pl.BlockSpec((TM, POOL_WIDTH), lambda i: (i, 0)),
                  pl.BlockSpec((POOL_HALO, POOL_WIDTH), lambda i: (jnp.maximum(i * r - 1, 0), 0)),
                  pl.BlockSpec((POOL_HALO, POOL_WIDTH), lambda i: (jnp.minimum((i + 1) * r, nb8 - 1), 0)),
                  pl.BlockSpec(band.shape, lambda i: (0, 0, 0)),
                  pl.BlockSpec((TM, POOL_WIDTH), lambda i: (cnt_idx(i), 0)),
                  pl.BlockSpec((POOL_WIDTH, POOL_WIDTH), lambda i: (0, 0)),
                  pl.BlockSpec((1, POOL_WIDTH), lambda i: (0, 0))],
        out_specs=pl.BlockSpec((TM, POOL_WIDTH), lambda i: (i, 0)),
        compiler_params=_cparams(("arbitrary",)),
        name="pool_mixer",
    )(zp, zp, zp, band, cnt, pw_bd, pscale)


def _rwkv_pos(d, s, *, ncc, nlc):
    in_ctx = s < ncc
    jc = jnp.where(d == 0, s, ncc - 1 - s)
    jl = jnp.where(d == 0, s - ncc, nlc - 1 - (s - ncc))
    pos = jnp.where(in_ctx, nlc + jc, jl)
    first = jnp.where(in_ctx, jc == 0, jl == 0)
    last = jnp.where(in_ctx, jc == ncc - 1, jl == nlc - 1)
    return pos, first, last


def _each(f, *lists):
    return [f(*xs) for xs in zip(*lists)]


def _mmb(a, b):
    return jnp.dot(a, b, preferred_element_type=f32)


def _tri_inverse(lmb, tri_ref, eye_b):
    cast = lambda xs: _each(lambda x: x.astype(bf16), xs)
    n = _each(lambda l: l * tri_ref[0, 2], lmb)
    n2 = cast(_each(_mmb, n, n))
    n4 = cast(_each(_mmb, n2, n2))
    t = _each(lambda a, b: _mmb((eye_b + a), (eye_b + b)), n, n2)
    t = _each(lambda a, b: _mmb(a.astype(bf16), eye_b + b), t, n4)
    for lvl in range(3):
        tb = cast(t)
        x = cast(_each(lambda a, l: _mmb(a, l * tri_ref[0, 3 + lvl]), tb, lmb))
        t = _each(lambda a, xx, ab: a + _mmb(xx, ab), t, x, tb)
    return t


def _rwkv_kernel(z_ref, zp_ref, zn_ref, bd_ref, eye_ref, tri_ref, csi_ref, mu_ref, w0_ref, w2_ref, a0_ref, a2_ref,
                 g2_ref, kk_ref, ka_ref, rk_ref, y_ref, bv_ref, g_ref, s_ref, *, ncc, nlc):
    d, s = pl.program_id(0), pl.program_id(2)
    _, first, last = _rwkv_pos(d, s, ncc=ncc, nlc=nlc)

    @pl.when(s == 0)
    def _():
        s_ref[...] = jnp.zeros_like(s_ref)

    c = CHUNK
    nbs = list(range(z_ref.shape[1]))
    keep_prev, keep_next = jnp.where(first, 0.0, 1.0), jnp.where(last, 0.0, 1.0)
    row = lax.broadcasted_iota(jnp.int32, (c, 1), 0)
    bd = bd_ref[...]
    bd_b = bd.astype(bf16)
    eye = eye_ref[...]
    eye_b = eye.astype(bf16)
    cast = lambda xs: _each(lambda x: x.astype(bf16), xs)

    def shifted(nb):
        z = z_ref[0, nb]
        zp = jnp.where(row == 0, zp_ref[0, nb, 7:8, :] * keep_prev, pltpu.roll(z, 1, 0))
        zn = jnp.where(row == c - 1, zn_ref[0, nb, 0:1, :] * keep_next, pltpu.roll(z, c - 1, 0))
        return z + mu_ref[...] * (0.5 * (zp + zn) - z)

    zs = _each(shifted, nbs)
    r, k, v = (_each(lambda z, o=o: z[:, o:o + 256], zs) for o in (0, 256, 512))
    lora = _each(lambda z: z[:, 768:896], zs)
    g = _each(lambda z: _mm(_sigmoid(z[:, 896:1024]), g2_ref[...]), zs)
    e = _each(lambda x: EXP_M05 * _sigmoid(w0_ref[0] + _mm(jnp.tanh(x), w2_ref[0])), lora)
    a = _each(lambda x: _sigmoid(a0_ref[0] + _mm(x, a2_ref[0])), lora)
    kd = _each(lambda kx, ax: kx * (1.0 + (ax - 1.0) * ka_ref[...]), k, a)
    kkr = _each(lambda kx: kx * kk_ref[...], k)
    kk = _each(lambda x: x / jnp.maximum(jnp.sqrt(_mm_x01(x * x, bd)), 1e-12), kkr)
    bv = _each(lambda rx, kx, vx: _mm_x01(rx * kx * rk_ref[...], bd) * vx, r, kd, v)
    bb = _each(lambda x, ax: x * ax, kk, a)

    cs = _each(lambda x: _mm_01x(csi_ref[0], x), e)
    tot = _each(lambda x: jnp.where(d == 0, x[c - 1:c, :], x[0:1, :]), cs)
    rep4 = lambda t: jnp.concatenate([t] * RWKV_HEADS, axis=0)
    fold4 = lambda t: t[0:c] + t[c:2 * c] + t[2 * c:3 * c] + t[3 * c:4 * c]
    head_rows = lambda t: rep4(t.astype(bf16)) * bd_b
    a4 = _each(lambda ex, cx, kx: head_rows(jnp.exp(ex - cx) * kx), e, cs, kk)
    r_s = _each(lambda rx, cx: rx * jnp.exp(-cx), r, cs)
    r4 = _each(head_rows, r_s)
    v4 = _each(head_rows, v)
    grow = _each(jnp.exp, cs)
    b4 = _each(lambda x, gx: rep4((x * gx).astype(bf16)), bb, grow)
    k4 = _each(lambda x, gx: rep4((x * gx).astype(bf16)), kd, grow)
    to_end = _each(lambda cx, tx: jnp.exp(cx - tx), cs, tot)
    b_e = _each(lambda x, gx: (x * gx).astype(bf16), bb, to_end)
    k_e = _each(lambda x, gx: (x * gx).astype(bf16), kd, to_end)
    g_end = _each(lambda tx: jnp.exp(-tx), tot)

    ntb = lambda x, y: lax.dot_general(x, y, (((1,), (1,)), ((), ())), preferred_element_type=f32).astype(bf16)
    tnb = lambda x, y: lax.dot_general(x, y, (((0,), (0,)), ((), ())), preferred_element_type=f32)
    lmb = _each(lambda x, y: ntb(x, y) * tri_ref[0, 0], a4, b4)
    akm = _each(lambda x, y: ntb(x, y) * tri_ref[0, 0], a4, k4)
    rbn = _each(lambda x, y: ntb(x, y) * tri_ref[0, 1], r4, b4)
    rkm = _each(lambda x, y: ntb(x, y) * tri_ref[0, 6], r4, k4)
    t = cast(_tri_inverse(lmb, tri_ref, eye_b))
    w4 = cast(_each(_mmb, t, a4))
    u4 = cast(_each(lambda tx, ax, vx: _mmb(tx, _mmb(ax, vx).astype(bf16)), t, akm, v4))
    q_all = _each(lambda rx, bx, wx: rx + fold4(_mmb(bx, wx)), r_s, rbn, w4)
    y0 = _each(lambda kx, vx, bx, ux: fold4(_mmb(kx, vx) + _mmb(bx, ux)), rkm, v4, rbn, u4)
    w_all, u_all = _each(fold4, w4), _each(fold4, u4)
    g_bd = _each(lambda gx, wx, bx: eye * gx - bd * tnb(wx, bx), g_end, w_all, b_e)
    h_bd = _each(lambda vx, kx, ux, bx: bd * (tnb(vx.astype(bf16), kx) - tnb(ux, bx)), v, k_e, u_all, b_e)
    for nb in nbs:
        st = s_ref[nb]
        y_ref[0, 0, nb] = _mm_nt(q_all[nb], st) + y0[nb]
        bv_ref[0, 0, nb] = bv[nb]
        g_ref[0, 0, nb] = g[nb]
        s_ref[nb] = _mm(st, g_bd[nb]) + h_bd[nb]


def _rwkv(zr, consts, p, ncc, nlc):
    npos, b_sz = zr.shape[0], zr.shape[1]
    nbat = math.gcd(b_sz, RWKV_BATCHES_PER_STEP)
    kw = dict(ncc=ncc, nlc=nlc)
    pos = lambda d, s: _rwkv_pos(d, s, **kw)[0]
    full = lambda a: pl.BlockSpec(a.shape, lambda d, b, s: (0,) * a.ndim)
    by_dir = lambda a: pl.BlockSpec((1,) + a.shape[1:], lambda d, b, s: (d,) + (0,) * (a.ndim - 1))
    out = jax.ShapeDtypeStruct((2, npos, b_sz, CHUNK, RWKV_WIDTH), f32)
    ospec = pl.BlockSpec((1, 1, nbat, CHUNK, RWKV_WIDTH), lambda d, b, s: (d, pos(d, s), b, 0, 0))
    last8 = CHUNK // 8 - 1
    return pl.pallas_call(
        functools.partial(_rwkv_kernel, **kw),
        out_shape=(out, out, out),
        grid=(2, b_sz // nbat, npos),
        in_specs=[pl.BlockSpec((1, nbat, CHUNK, RWKV_PAD), lambda d, b, s: (pos(d, s), b, 0, 0)),
                  pl.BlockSpec((1, nbat, 8, RWKV_PAD), lambda d, b, s: (jnp.maximum(pos(d, s) - 1, 0), b, last8, 0)),
                  pl.BlockSpec((1, nbat, 8, RWKV_PAD), lambda d, b, s: (jnp.minimum(pos(d, s) + 1, npos - 1), b, 0, 0)),
                  full(consts["bd"]), full(consts["eye"]), by_dir(consts["tri"]), by_dir(consts["csi"]),
                  full(p["mu_p"]), by_dir(p["w0"]), by_dir(p["w2_p"]), by_dir(p["a0"]), by_dir(p["a2_p"]),
                  full(p["g2_p"]), full(p["k_k"]), full(p["k_a"]), full(p["r_k"])],
        out_specs=(ospec, ospec, ospec),
        scratch_shapes=[pltpu.VMEM((nbat, RWKV_WIDTH, RWKV_WIDTH), f32)],
        compiler_params=_cparams(("arbitrary", "arbitrary", "arbitrary")),
        name="rwkv7_chunked",
    )(zr, zr, zr, consts["bd"], consts["eye"], consts["tri"], consts["csi"], p["mu_p"], p["w0"], p["w2_p"], p["a0"],
      p["a2_p"], p["g2_p"], p["k_k"], p["k_a"], p["r_k"])


def _pack_bf16_pairs(lo, hi):
    lo_b = pltpu.bitcast(lo.astype(bf16).astype(f32), jnp.uint32)
    hi_b = pltpu.bitcast(hi.astype(bf16).astype(f32), jnp.uint32)
    return (hi_b & jnp.uint32(0xFFFF0000)) | (lo_b >> 16)


def _unpack_bf16_pairs(w):
    lo = pltpu.bitcast(w << 16, f32).astype(bf16)
    hi = pltpu.bitcast(w & jnp.uint32(0xFFFF0000), f32).astype(bf16)
    return lo, hi


def _outproj_kernel(*refs, a_tiles, x_tiles):
    na, nx = len(a_tiles), len(x_tiles)
    a_refs, (py_ref, y_ref, bv_ref, g_ref) = refs[:na], refs[na:na + 4]
    x_refs = refs[na + 4:na + 4 + nx]
    m_ref, avg_ref, lnw_ref, lnb_ref, wo_ref, pmg_ref, pfg_ref, x1_ref, hp_ref = refs[na + 4 + nx:]
    m = m_ref[0]
    rows = lambda ref, dd: jnp.concatenate([ref[dd, c4, 0] for c4 in range(TM // CHUNK)], axis=0)
    ysum = rows(y_ref, 0) + rows(y_ref, 1)
    avg = avg_ref[...]
    dev = ysum - _mm_x01(ysum, avg)
    var = _mm_x01(dev * dev, avg)
    yn = dev * lax.rsqrt(var + RWKV_GN_EPS) * lnw_ref[...] + lnb_ref[...]
    rw = (yn + rows(bv_ref, 0) + rows(bv_ref, 1)) * rows(g_ref, 0)
    o = (jnp.dot(_part_tile(a_refs, a_tiles), wo_ref[0:512, :], preferred_element_type=f32)
         + jnp.dot(py_ref[...], wo_ref[512:768, :], preferred_element_type=f32)
         + _mm(rw, wo_ref[768:1024, :]))
    x1 = _part_tile(x_refs, x_tiles) + m[2:3] * (_rms(o) * pmg_ref[...])
    x1_ref[...] = x1
    h = (_rms(x1) * pfg_ref[...]) * (1.0 + m[4:5]) + m[3:4]
    hp_ref[...] = _pack_bf16_pairs(h[:, 0:512], h[:, 512:1024])


def _outproj(att_parts, py, y, bv, g, x_parts, mt, chunk_idx, consts, p, nt):
    d = x_parts[0].shape[1]
    tiles = lambda parts: tuple(a.shape[0] // TM for a in parts)
    cpt = TM // CHUNK
    cspec = lambda nd: pl.BlockSpec((nd, cpt, 1, CHUNK, RWKV_WIDTH), lambda i: (0,) + chunk_idx(i) + (0, 0))
    full = lambda a: pl.BlockSpec(a.shape, lambda i: (0,) * a.ndim)
    ws = [consts["avg"], p["ln_w"], p["ln_b"], p["w_out"], p["post_mix_g"], p["pre_ffn_g"]]
    return pl.pallas_call(
        functools.partial(_outproj_kernel, a_tiles=tiles(att_parts), x_tiles=tiles(x_parts)),
        out_shape=(jax.ShapeDtypeStruct((nt * TM, d), f32), jax.ShapeDtypeStruct((nt * TM, d // 2), jnp.uint32)),
        grid=(nt,),
        in_specs=_part_specs(att_parts, 512)
        + [pl.BlockSpec((TM, POOL_WIDTH), lambda i: (i, 0)), cspec(2), cspec(2), cspec(1)]
        + _part_specs(x_parts, d)
        + [pl.BlockSpec((1, 8, d), lambda i: (i, 0, 0))] + [full(w) for w in ws],
        out_specs=(pl.BlockSpec((TM, d), lambda i: (i, 0)), pl.BlockSpec((TM, d // 2), lambda i: (i, 0))),
        compiler_params=_cparams(("arbitrary",)),
        name="out_proj",
    )(*att_parts, py, y, bv, g, *x_parts, mt, *ws)


def _router_kernel(hp_ref, rw_ref, rb_ref, ut_ref, lt_ref, ei_ref, pos_ref, gt_ref, cnt_ref, run_ref):
    i = pl.program_id(0)

    @pl.when(i == 0)
    def _():
        run_ref[...] = jnp.zeros_like(run_ref)

    tm = hp_ref.shape[0]
    ne, ng = N_EXPERTS, N_GROUPS
    pg = ne // ng
    lo, hi = _unpack_bf16_pairs(hp_ref[...])
    logits = (lax.dot_general(rw_ref[:, 0:512], lo, (((1,), (1,)), ((), ())), preferred_element_type=f32)
              + lax.dot_general(rw_ref[:, 512:1024], hi, (((1,), (1,)), ((), ())), preferred_element_type=f32))
    scores = _sigmoid(logits)
    sel = scores + rb_ref[...]
    neg = -jnp.inf

    s3 = sel.reshape(ng, pg, tm)
    io = lax.broadcasted_iota(jnp.int32, (ng, pg, tm), 1)
    m1 = jnp.max(s3, axis=1, keepdims=True)
    i1 = jnp.min(jnp.where(s3 == m1, io, pg), axis=1, keepdims=True)
    m2 = jnp.max(jnp.where(io == i1, neg, s3), axis=1, keepdims=True)
    gs = (m1 + m2).reshape(ng, tm)
    gi = lax.broadcasted_iota(jnp.int32, (ng, tm), 0)
    grank = jnp.zeros((ng, tm), f32)
    for j in range(ng):
        rj = gs[j:j + 1, :]
        grank = grank + jnp.where((rj > gs) | ((rj == gs) & (j < gi)), 1.0, 0.0)
    gsel = jnp.where(grank < TOPK_GROUPS, 1.0, 0.0)
    gsel3 = jnp.broadcast_to(gsel.reshape(ng, 1, tm), (ng, pg, tm)).reshape(ne, tm)
    msk = jnp.where(gsel3 > 0.5, sel, neg)
    ei = lax.broadcasted_iota(jnp.int32, (ne, tm), 0)
    erank = jnp.zeros((ne, tm), f32)
    for j in range(ne):
        rj = msk[j:j + 1, :]
        erank = erank + jnp.where((rj > msk) | ((rj == msk) & (j < ei)), 1.0, 0.0)
    chosen = erank < TOP_K
    chf = jnp.where(chosen, 1.0, 0.0)
    graw = jnp.where(chosen, scores, 0.0)
    gate = graw / jnp.sum(graw, axis=0, keepdims=True) * ROUTED_SCALE

    pos = run_ref[...] + _mm(chf, ut_ref[...])
    tot = jnp.sum(chf, axis=1, keepdims=True)
    run_new = run_ref[...] + tot
    run_ref[...] = run_new
    cnt_ref[...] = run_new[:, 0:128]
    rk = _mm(lt_ref[...], chf)
    eif = ei.astype(f32)
    rows_e, rows_p, rows_g = [], [], []
    for kq in range(TOP_K):
        mk = chosen & (rk == float(kq))
        rows_e.append(jnp.sum(jnp.where(mk, eif, 0.0), axis=0, keepdims=True))
        rows_p.append(jnp.sum(jnp.where(mk, pos, 0.0), axis=0, keepdims=True))
        rows_g.append(jnp.sum(jnp.where(mk, gate, 0.0), axis=0, keepdims=True))
    zrow = jnp.zeros((8 - TOP_K, tm), f32)
    ei_ref[0] = jnp.concatenate(rows_e + [zrow], axis=0).astype(jnp.int32)
    pos_ref[0] = jnp.concatenate(rows_p + [zrow], axis=0).astype(jnp.int32)
    gpad = jnp.concatenate(rows_g + [jnp.zeros((128 - TOP_K, tm), f32)], axis=0)
    gt_ref[...] = gpad.T


def _router(hp, rwt, rb, consts, nt):
    full = lambda a: pl.BlockSpec(a.shape, lambda i: (0,) * a.ndim)
    return pl.pallas_call(
        _router_kernel,
        out_shape=(jax.ShapeDtypeStruct((nt, 8, TM), jnp.int32), jax.ShapeDtypeStruct((nt, 8, TM), jnp.int32),
                   jax.ShapeDtypeStruct((nt * TM, 128), f32), jax.ShapeDtypeStruct((N_EXPERTS, 128), f32)),
        grid=(nt,),
        in_specs=[pl.BlockSpec((TM, 512), lambda i: (i, 0)), full(rwt), full(rb), full(consts["ut"]),
                  full(consts["lt"])],
        out_specs=(pl.BlockSpec((1, 8, TM), lambda i: (i, 0, 0)), pl.BlockSpec((1, 8, TM), lambda i: (i, 0, 0)),
                   pl.BlockSpec((TM, 128), lambda i: (i, 0)), pl.BlockSpec((N_EXPERTS, 128), lambda i: (0, 0))),
        scratch_shapes=[pltpu.VMEM((N_EXPERTS, TM), f32)],
        compiler_params=_cparams(("arbitrary",)),
        name="moe_router",
    )(hp, rwt, rb, consts["ut"], consts["lt"])


def _slots_kernel(ps_ref, ei_ref, pos_ref, d_ref):
    ei = ei_ref[0]
    slot = pos_ref[0]
    for e in range(N_EXPERTS):
        slot = slot + jnp.where(ei == e, ps_ref[e], 0)
    d_ref[...] = slot


def _slots(pstart, ei, pos, nt):
    return pl.pallas_call(
        _slots_kernel,
        out_shape=jax.ShapeDtypeStruct((8, nt * TM), jnp.int32),
        grid_spec=pltpu.PrefetchScalarGridSpec(
            num_scalar_prefetch=1,
            grid=(nt,),
            in_specs=[pl.BlockSpec((1, 8, TM), lambda i, ps: (i, 0, 0)), pl.BlockSpec((1, 8, TM), lambda i, ps: (i, 0, 0))],
            out_specs=pl.BlockSpec((8, TM), lambda i, ps: (0, i))),
        compiler_params=_cparams(("arbitrary",)),
        name="moe_slots",
    )(pstart, ei, pos)


def _sc_dispatch(hp, dest, n_slots):
    t, w = hp.shape
    wh = w // 2
    mesh = plsc.VectorSubcoreMesh(core_axis_name="core", subcore_axis_name="subcore", num_cores=SC_CORES,
                                  num_subcores=SC_SUBCORES)
    idx = [dest[kq:kq + 1] for kq in range(TOP_K)]
    half = jax.ShapeDtypeStruct((n_slots, wh), hp.dtype)

    @pl.kernel(out_type=(half, half), mesh=mesh, scratch_types=[])
    def scatter_rows(hp_hbm, *rest):
        idx_hbm, xs_hbm = rest[:TOP_K], rest[TOP_K:]
        for c in range(2):
            def body(x_vmem, *i_vmem, c=c):
                for iv in i_vmem:
                    pltpu.sync_copy(x_vmem, xs_hbm[c].at[iv.at[0]])

            pltpu.emit_pipeline(
                body,
                grid=(t // SC_WINDOW,),
                in_specs=[pl.BlockSpec((SC_WINDOW, wh), lambda i, c=c: (i, c))]
                + [pl.BlockSpec((1, SC_WINDOW), lambda i: (0, i))] * TOP_K,
                out_specs=[],
                core_axis_name=("core", "subcore"),
                dimension_semantics=(pltpu.PARALLEL,),
            )(hp_hbm, *idx_hbm)

    return scatter_rows(hp, *idx)


def _expert_kernel(be_ref, nb_ref, bv_ref, xa_ref, xb_ref, wg_ref, wu_ref, wd_ref, ys_ref):
    i = pl.program_id(0)

    @pl.when(i < nb_ref[0])
    def _():
        live = lax.broadcasted_iota(jnp.int32, (MOE_BM, 1), 0) < bv_ref[i]
        la, ha = _unpack_bf16_pairs(jnp.where(live, xa_ref[...], jnp.uint32(0)))
        lb, hb = _unpack_bf16_pairs(jnp.where(live, xb_ref[...], jnp.uint32(0)))

        def proj(w_ref):
            return sum(jnp.dot(x, w_ref[0, 0, q * 256:(q + 1) * 256, :].astype(bf16), preferred_element_type=f32)
                       for q, x in enumerate((la, lb, ha, hb)))

        gg, uu = proj(wg_ref), proj(wu_ref)
        act = gg * _sigmoid(gg) * uu
        y = _mm(act, wd_ref[0, 0])
        ys_ref[0] = _pack_bf16_pairs(y[:, 0:256], y[:, 512:768])
        ys_ref[1] = _pack_bf16_pairs(y[:, 256:512], y[:, 768:1024])

    @pl.when(i >= nb_ref[0])
    def _():
        ys_ref[...] = jnp.zeros_like(ys_ref)


def _experts(block_expert, nb_used, block_valid, xs, layer, w_gate, w_up, w_down):
    xa, xb = xs
    n_slots = xa.shape[0]
    wspec = lambda w: pl.BlockSpec((1, 1) + w.shape[2:], lambda i, be, nb, bv: (layer, be[i], 0, 0))
    return pl.pallas_call(
        _expert_kernel,
        out_shape=jax.ShapeDtypeStruct((2, n_slots, xa.shape[1]), jnp.uint32),
        grid_spec=pltpu.PrefetchScalarGridSpec(
            num_scalar_prefetch=3,
            grid=(n_slots // MOE_BM,),
            in_specs=[pl.BlockSpec((MOE_BM, xa.shape[1]), lambda i, be, nb, bv: (i, 0)),
                      pl.BlockSpec((MOE_BM, xb.shape[1]), lambda i, be, nb, bv: (i, 0)),
                      wspec(w_gate), wspec(w_up), wspec(w_down)],
            out_specs=pl.BlockSpec((2, MOE_BM, xa.shape[1]), lambda i, be, nb, bv: (0, i, 0))),
        compiler_params=_cparams(("arbitrary",)),
        name="moe_experts",
    )(block_expert, nb_used, block_valid, xa, xb, w_gate, w_up, w_down)


def _sc_gather(ys, dest):
    n_slots, wh = ys.shape[1], ys.shape[2]
    t = dest.shape[1]
    mesh = plsc.VectorSubcoreMesh(core_axis_name="core", subcore_axis_name="subcore", num_cores=SC_CORES,
                                  num_subcores=SC_SUBCORES)
    idx = jnp.concatenate([dest[:TOP_K], dest[:TOP_K] + n_slots], 0).reshape(1, 2 * TOP_K * t)
    rows = 2 * TOP_K * t

    @pl.kernel(out_type=jax.ShapeDtypeStruct((rows, wh), ys.dtype), mesh=mesh, scratch_types=[])
    def gather_rows(y_hbm, idx_hbm, out_hbm):
        def body(i_vmem, o_vmem):
            pltpu.sync_copy(y_hbm.at[i_vmem.at[0]], o_vmem)

        pltpu.emit_pipeline(
            body,
            grid=(rows // SC_WINDOW,),
            in_specs=[pl.BlockSpec((1, SC_WINDOW), lambda i: (0, i))],
            out_specs=[pl.BlockSpec((SC_WINDOW, wh), lambda i: (i, 0))],
            core_axis_name=("core", "subcore"),
            dimension_semantics=(pltpu.PARALLEL,),
        )(idx_hbm, out_hbm)

    return gather_rows(ys.reshape(2 * n_slots, wh), idx)


def _combine_kernel(*refs):
    ga, gb = refs[0:TOP_K], refs[TOP_K:2 * TOP_K]
    hp_ref, gt_ref, x1_ref, m_ref, wsgu_ref, wsd_ref, pg_ref, o_ref = refs[2 * TOP_K:]
    lo, hi = _unpack_bf16_pairs(hp_ref[...])
    gu = (jnp.dot(lo, wsgu_ref[0:512, :], preferred_element_type=f32)
          + jnp.dot(hi, wsgu_ref[512:1024, :], preferred_element_type=f32))
    gg, uu = gu[:, 0:EXPERT_FF], gu[:, EXPERT_FF:2 * EXPERT_FF]
    f = _mm(gg * _sigmoid(gg) * uu, wsd_ref[...])
    gt = gt_ref[...]
    parts = [jnp.zeros((TM, 256), f32) for _ in range(4)]
    for kq in range(TOP_K):
        g = gt[:, kq:kq + 1]
        la, ha = _unpack_bf16_pairs(ga[kq][...])
        lb, hb = _unpack_bf16_pairs(gb[kq][...])
        for q, v in enumerate((la, lb, ha, hb)):
            parts[q] = parts[q] + v.astype(f32) * g
    f = f + jnp.concatenate(parts, axis=1)
    m = m_ref[0]
    o_ref[...] = x1_ref[...] + m[5:6] * (_rms(f) * pg_ref[...])


def _combine(gathered, hp, gt, x1, mt, p, nt):
    d = x1.shape[1]
    full = lambda a: pl.BlockSpec(a.shape, lambda i: (0,) * a.ndim)
    ws = [p["sh_wgu"], p["sh_wd"], p["post_ffn_g"]]
    gspecs = [pl.BlockSpec((TM, gathered.shape[1]), lambda i, j=j: (j * nt + i, 0)) for j in range(2 * TOP_K)]
    return pl.pallas_call(
        _combine_kernel,
        out_shape=jax.ShapeDtypeStruct((nt * TM, d), f32),
        grid=(nt,),
        in_specs=gspecs
        + [pl.BlockSpec((TM, d // 2), lambda i: (i, 0)),
           pl.BlockSpec((TM, 128), lambda i: (i, 0)),
           pl.BlockSpec((TM, d), lambda i: (i, 0)),
           pl.BlockSpec((1, 8, d), lambda i: (i, 0, 0))] + [full(w) for w in ws],
        out_specs=pl.BlockSpec((TM, d), lambda i: (i, 0)),
        compiler_params=_cparams(("arbitrary",)),
        name="moe_combine",
    )(*([gathered] * (2 * TOP_K)), hp, gt, x1, mt, *ws)


def _plan_kernel(cnt_ref, ps_ref, blk_ref):
    nblk = blk_ref.shape[1]
    cnt = cnt_ref[...]
    padded = jnp.floor((cnt + (MOE_BM - 1)) / MOE_BM) * MOE_BM
    row = lax.broadcasted_iota(jnp.int32, cnt.shape, 0)
    pend = padded
    for sh in (1, 2, 4, 8, 16, 32):
        pend = pend + jnp.where(row >= sh, pltpu.roll(pend, sh, 0), 0.0)
    pstart = pend - padded
    ps_ref[...] = pstart.astype(jnp.int32)
    row0 = (lax.broadcasted_iota(jnp.int32, (1, nblk), 1) * MOE_BM).astype(f32)
    be = jnp.sum(jnp.where(pend[:, 0:1] <= row0, 1.0, 0.0), axis=0, keepdims=True)
    be = jnp.minimum(be, N_EXPERTS - 1.0)
    mine = lax.broadcasted_iota(jnp.int32, (N_EXPERTS, nblk), 0).astype(f32) == be
    run_end = jnp.sum(jnp.where(mine, (pstart + cnt)[:, 0:1], 0.0), axis=0, keepdims=True)
    valid = jnp.clip(run_end - row0, 0.0, float(MOE_BM))
    used = jnp.broadcast_to(pend[N_EXPERTS - 1:N_EXPERTS, 0:1] / MOE_BM, (1, nblk))
    out_row = lax.broadcasted_iota(jnp.int32, (8, nblk), 0)
    blk_ref[...] = jnp.where(out_row == 0, be, jnp.where(out_row == 1, valid, jnp.where(out_row == 2, used, 0.0))
                             ).astype(jnp.int32)


def _plan(cnt, n_blocks):
    nblk = -(-n_blocks // 128) * 128
    return pl.pallas_call(
        _plan_kernel,
        out_shape=(jax.ShapeDtypeStruct((N_EXPERTS, 128), jnp.int32), jax.ShapeDtypeStruct((8, nblk), jnp.int32)),
        name="moe_plan",
    )(cnt)


def _moe(hp, x1, mt, consts, p, nt):
    t = nt * TM
    ei, pos, gt, cnt = _router(hp, p["router_wt"], p["router_b"], consts, nt)
    n_assign = t * TOP_K
    n_blocks = -(-(n_assign + N_EXPERTS * (MOE_BM - 1)) // MOE_BM)
    n_slots = n_blocks * MOE_BM
    ps, blk = _plan(cnt, n_blocks)
    pstart, block_expert, block_valid, nb_used = ps[:, 0], blk[0, :n_blocks], blk[1, :n_blocks], blk[2, :1]
    dest = _slots(pstart, ei, pos, nt)
    xs = _sc_dispatch(hp, dest, n_slots)
    ys = _experts(block_expert, nb_used, block_valid, xs, p["layer"], *p["experts"])
    return _combine(_sc_gather(ys, dest), hp, gt, x1, mt, p, nt)


def _np_consts(l_lat, l_ctx):
    n = RWKV_WIDTH
    i = np.arange(n)
    bd = (i[:, None] // 64 == i[None, :] // 64).astype(np.float32)
    t_r, t_c = (i % 64)[:, None], (i % 64)[None, :]
    tri = np.zeros((2, 7, n, n), np.float32)
    for d in range(2):
        before = (t_c < t_r) if d == 0 else (t_c > t_r)
        tri[d, 0] = bd * before
        tri[d, 6] = bd * (before | (t_c == t_r))
        tri[d, 1] = -tri[d, 6]
        tri[d, 2] = -(bd * before * (t_r // 8 == t_c // 8))
        for lvl, blk in enumerate((8, 16, 32)):
            tri[d, 3 + lvl] = -(bd * before * (t_r // (2 * blk) == t_c // (2 * blk)) * (t_r // blk != t_c // blk))
    j = np.arange(CHUNK)
    csi = np.stack([(j[None, :] <= j[:, None]), (j[None, :] >= j[:, None])]).astype(np.float32)
    tt = np.arange(TM)
    ut = (tt[:, None] < tt[None, :]).astype(np.float32)
    ee = np.arange(N_EXPERTS)
    lt = (ee[None, :] < ee[:, None]).astype(np.float32)
    jj = np.arange(TM + 2 * POOL_HALO)[None, :]
    band = np.stack([((jj >= tt[:, None] + POOL_HALO - w // 2) & (jj <= tt[:, None] + POOL_HALO + w // 2 - 1))
                     for w in POOL_WINDOWS]).astype(np.float32)

    def counts(length):
        t = np.arange(length)[:, None]
        half = np.repeat(np.array(POOL_WINDOWS) // 2, 64)[None, :]
        return (np.clip(t + half, 0, length) - np.clip(t - half, 0, length)).astype(np.float32)

    cnt = np.concatenate([counts(l_lat), counts(l_ctx)], axis=0)
    return dict(bd=jnp.asarray(bd), eye=jnp.eye(n, dtype=f32), tri=jnp.asarray(tri, dtype=bf16), csi=jnp.asarray(csi),
                avg=jnp.asarray(bd / 64.0), ut=jnp.asarray(ut, dtype=bf16), lt=jnp.asarray(lt, dtype=bf16),
                band=jnp.asarray(band), cnt=jnp.asarray(cnt))


def _rope_tables(l_lat):
    rows = l_lat // GRID_W
    row = jnp.repeat(jnp.arange(rows, dtype=f32), GRID_W)
    col = jnp.tile(jnp.arange(GRID_W, dtype=f32), rows)
    n_freq = QK_ROPE // 4
    inv = ROPE_THETA ** (-jnp.arange(n_freq, dtype=f32) / n_freq)
    ang = jnp.concatenate([row[:, None] * inv, col[:, None] * inv], -1)
    cos = jnp.concatenate([jnp.cos(ang), jnp.ones((TM, 16), f32)], 0)
    sin = jnp.concatenate([jnp.sin(ang), jnp.zeros((TM, 16), f32)], 0)
    n = cos.shape[0]
    ct = jnp.concatenate([jnp.ones((n, QK_NOPE), f32), cos, cos, jnp.zeros((n, 32), f32)], 1)
    st = jnp.concatenate([jnp.zeros((n, QK_NOPE), f32), -sin, sin, jnp.zeros((n, 32), f32)], 1)
    qs = MLA_SCALE * LOG2E
    return jnp.stack([ct * qs, st * qs, ct, st])


def _layer_params(i, a):
    d = a["w_in"].shape[1]
    p = {}
    row = lambda v: v.reshape(1, -1).astype(f32)
    for name in ("pre_mix_g", "post_mix_g", "pre_ffn_g", "post_ffn_g", "mla_q_norm", "mla_kv_norm"):
        p[name] = row(a[name][i])
    w_in = a["w_in"][i]
    zc = lambda n: jnp.zeros((d, n), f32)
    p["w_in_p"] = jnp.concatenate(
        [w_in[:, 0:MLA_IN], zc(128 - QK_ROPE), w_in[:, MLA_IN:MLA_IN + POOL_WIDTH],
         w_in[:, MLA_IN + POOL_WIDTH:], zc(RWKV_PAD - RWKV_IN)], 1).astype(bf16)
    wq = a["mla_w_q_b"][i].reshape(Q_LORA, MLA_HEADS, QK_NOPE + QK_ROPE)
    zq = jnp.zeros((Q_LORA, MLA_HEADS, 32), f32)
    half = QK_ROPE // 2
    p["wq_p"] = jnp.concatenate([wq, zq], 2).reshape(Q_LORA, -1).astype(bf16)
    p["wq_s"] = jnp.concatenate([jnp.zeros_like(wq[:, :, :QK_NOPE]), wq[:, :, QK_NOPE + half:],
                                 wq[:, :, QK_NOPE:QK_NOPE + half], zq], 2).reshape(Q_LORA, -1).astype(bf16)
    wkv = a["mla_w_kv_b"][i].reshape(KV_LORA, MLA_HEADS, QK_NOPE + V_HEAD)
    p["wk_p"] = jnp.concatenate([wkv[:, :, :QK_NOPE], jnp.zeros((KV_LORA, MLA_HEADS, 64), f32)], 2
                                ).reshape(KV_LORA, -1).astype(bf16)
    p["wv_p"] = jnp.concatenate([wkv[:, :, QK_NOPE:], jnp.zeros((KV_LORA, MLA_HEADS, 64), f32)], 2
                                ).reshape(KV_LORA, -1).astype(bf16)
    e_p = np.zeros((128, MLA_HEADS * HEAD_PAD), np.float32)
    e_s = np.zeros_like(e_p)
    for h in range(MLA_HEADS):
        for j in range(QK_ROPE):
            e_p[j, h * HEAD_PAD + QK_NOPE + j] = 1.0
            e_s[(j + half) % QK_ROPE, h * HEAD_PAD + QK_NOPE + j] = 1.0
    p["e_p"], p["e_s"] = jnp.asarray(e_p, dtype=bf16), jnp.asarray(e_s, dtype=bf16)
    pw = a["pool_w"][i]
    p["pool_w_bd"] = jax.scipy.linalg.block_diag(*[pw[g] for g in range(pw.shape[0])]).astype(bf16)
    p["pool_scale"] = row(a["pool_scale"][i])
    p["mu_p"] = jnp.pad(a["rwkv_mu"][i], (0, RWKV_PAD - RWKV_IN)).reshape(1, -1)
    p["w0"] = a["rwkv_w0"][i].reshape(2, 1, RWKV_WIDTH)
    p["a0"] = a["rwkv_a0"][i].reshape(2, 1, RWKV_WIDTH)
    z32 = jnp.zeros((32, RWKV_WIDTH), f32)
    w2, a2 = a["rwkv_w2"][i], a["rwkv_a2"][i]
    p["w2_p"] = jnp.stack([jnp.concatenate([w2[0], z32, z32, z32]), jnp.concatenate([z32, w2[1], z32, z32])])
    p["a2_p"] = jnp.stack([jnp.concatenate([z32, z32, a2[0], z32]), jnp.concatenate([z32, z32, z32, a2[1]])])
    p["g2_p"] = jnp.concatenate([a["rwkv_g2"][i], jnp.zeros((64, RWKV_WIDTH), f32)])
    for name in ("k_k", "k_a", "r_k"):
        p[name] = row(a["rwkv_" + name][i])
    p["ln_w"], p["ln_b"] = row(a["rwkv_ln_w"][i]), row(a["rwkv_ln_b"][i])
    p["w_out"] = a["w_out"][i].astype(bf16)
    p["router_wt"] = a["router_w"][i].T.astype(bf16)
    p["router_b"] = a["router_bias"][i].reshape(-1, 1).astype(f32)
    p["layer"] = i
    p["experts"] = (a["exp_w_gate"], a["exp_w_up"], a["exp_w_down"])
    p["sh_wgu"] = jnp.concatenate([a["sh_w_gate"][i], a["sh_w_up"][i]], 1).astype(bf16)
    p["sh_wd"] = a["sh_w_down"][i].astype(bf16)
    return p


def kernel(x, c, ctx, c_ctx, ada_w, ada_b, pre_mix_g, post_mix_g, pre_ffn_g, post_ffn_g, w_in, w_out, mla_q_norm, mla_w_q_b, mla_kv_norm, mla_w_kv_b, pool_w, pool_scale, rwkv_mu, rwkv_w0, rwkv_w2, rwkv_a0, rwkv_a2, rwkv_g2, rwkv_k_k, rwkv_k_a, rwkv_r_k, rwkv_ln_w, rwkv_ln_b, router_w, router_bias, exp_w_gate, exp_w_up, exp_w_down, sh_w_gate, sh_w_up, sh_w_down):
    arrs = dict(pre_mix_g=pre_mix_g, post_mix_g=post_mix_g, pre_ffn_g=pre_ffn_g, post_ffn_g=post_ffn_g, w_in=w_in,
                w_out=w_out, mla_q_norm=mla_q_norm, mla_w_q_b=mla_w_q_b, mla_kv_norm=mla_kv_norm,
                mla_w_kv_b=mla_w_kv_b, pool_w=pool_w, pool_scale=pool_scale, rwkv_mu=rwkv_mu, rwkv_w0=rwkv_w0,
                rwkv_w2=rwkv_w2, rwkv_a0=rwkv_a0, rwkv_a2=rwkv_a2, rwkv_g2=rwkv_g2, rwkv_k_k=rwkv_k_k,
                rwkv_k_a=rwkv_k_a, rwkv_r_k=rwkv_r_k, rwkv_ln_w=rwkv_ln_w, rwkv_ln_b=rwkv_ln_b, router_w=router_w,
                router_bias=router_bias, exp_w_gate=exp_w_gate, exp_w_up=exp_w_up, exp_w_down=exp_w_down,
                sh_w_gate=sh_w_gate, sh_w_up=sh_w_up, sh_w_down=sh_w_down)
    b_sz, l_lat, d = x.shape
    l_ctx = ctx.shape[1]
    assert l_lat % TM == 0 and l_ctx % TM == 0 and l_lat % GRID_W == 0 and b_sz < 16
    lt, ct = l_lat // TM, l_ctx // TM
    nlat, nctx = b_sz * lt, b_sz * ct
    nall = nlat + nctx
    ncc, nlc = l_ctx // CHUNK, l_lat // CHUNK
    consts = _np_consts(l_lat, l_ctx)
    rope = _rope_tables(l_lat)
    rope_idx = lambda i: jnp.where(i < nlat, i % lt, lt)
    cnt_idx = lambda i: jnp.where(i < nlat, i % lt, lt + (i - nlat) % ct)
    chunk_idx = lambda i: (jnp.where(i < nlat, i % lt, nlc * CHUNK // TM + (i - nlat) % ct),
                           jnp.where(i < nlat, i // lt, (i - nlat) // ct))

    c_all = jnp.zeros((16, d), f32).at[:b_sz].set(c).at[b_sz].set(c_ctx)
    mods = _ada_mod(c_all, ada_w, ada_b)
    tile_row = np.concatenate([np.repeat(np.arange(b_sz), lt), np.full(nctx, b_sz)])
    x_parts = [x.reshape(b_sz * l_lat, d), ctx.reshape(b_sz * l_ctx, d)]

    for i in range(DEPTH):
        last = i == DEPTH - 1
        p = _layer_params(i, arrs)
        mt = jnp.pad(mods[i][tile_row].reshape(nall, 6, d), ((0, 0), (0, 2), (0, 0)))
        nt = nlat if last else nall
        q, k, v, zp, zr = _inproj(x_parts, mt, rope, rope_idx, chunk_idx, ncc + nlc, b_sz, p, nall)
        ctx_blk0 = nlat * TM // l_ctx
        tq = math.gcd(l_lat, ATTN_TQ)
        qt = l_lat // tq
        att = [_attention(q, k, v, b_sz * qt, 0, tq, [(l_lat, lambda t: t // qt), (l_ctx, lambda t: ctx_blk0 + t // qt)])]
        if not last:
            att.append(_attention(q, k, v, nctx, nlat, TM, [(l_ctx, lambda t: ctx_blk0 + t // ct)]))
        py = _pool(zp, consts["band"], consts["cnt"], cnt_idx, p["pool_w_bd"], p["pool_scale"], nt, nlat, lt, ct)
        y, bv, g = _rwkv(zr, consts, p, ncc, nlc)
        x1, hp = _outproj(att, py, y, bv, g, x_parts, mt, chunk_idx, consts, p, nt)
        x_parts = [_moe(hp, x1, mt, consts, p, nt)]
    return x_parts[0][:b_sz * l_lat].reshape(b_sz, l_lat, d)
```
